```python
import jax, jax.numpy as jnp
from jax import lax
import numpy as np

D_MODEL = 1024
BATCH = 8
SEQ = 2048
DEPTH = 4

ATT_HEADS = 16
ATT_KV_HEADS = 4
ATT_GROUP = ATT_HEADS // ATT_KV_HEADS
ATT_HEAD_DIM = 64
WINDOW = 128
DN_HEADS = 8
DN_HEAD_DIM = 128
CONV_K = 4
CHUNK = 64
D_FF = 4 * D_MODEL
ATT_Q_W = ATT_HEADS * ATT_HEAD_DIM
ATT_KV_W = ATT_KV_HEADS * ATT_HEAD_DIM
DN_W = DN_HEADS * DN_HEAD_DIM
IN_SPLITS = (ATT_Q_W, ATT_KV_W, ATT_KV_W, DN_W, DN_W, DN_W, DN_W, DN_HEADS, DN_HEADS, D_MODEL, D_MODEL)
D_IN = ATT_Q_W + 2 * ATT_KV_W + 4 * DN_W + 2 * DN_HEADS + 2 * D_MODEL
ALPHA = (2 * DEPTH) ** 0.25
BETA_INIT = (8 * DEPTH) ** -0.25
LN_EPS = 1e-5
RMS_EPS = 1e-6
ADA_SCALE = 0.2

kernel_name = "hybrid_swa_sink_gdn_parallel_deepnorm_adaln"


def _split_in(p):
    outs = []
    off = 0
    for w in IN_SPLITS:
        outs.append(p[..., off:off + w])
        off += w
    return outs


def _layer_norm(x, g, b):
    xf = x.astype(jnp.float32)
    mu = jnp.mean(xf, axis=-1, keepdims=True)
    var = jnp.mean(jnp.square(xf - mu), axis=-1, keepdims=True)
    return ((xf - mu) * lax.rsqrt(var + LN_EPS) * g.astype(jnp.float32) + b.astype(jnp.float32)).astype(x.dtype)


def _l2norm(t):
    return t * lax.rsqrt(jnp.sum(jnp.square(t), axis=-1, keepdims=True) + RMS_EPS)


def _causal_conv_silu(x, w):
    s = x.shape[1]
    xp = jnp.pad(x, ((0, 0), (CONV_K - 1, 0), (0, 0)))
    y = sum(xp[:, j:j + s] * w[j] for j in range(CONV_K))
    return jax.nn.silu(y)


def _sliding_window_attention(q, k, v, sinks):
    b, s, _ = q.shape
    nb = s // WINDOW
    qb = q.reshape(b, nb, WINDOW, ATT_KV_HEADS, ATT_GROUP, ATT_HEAD_DIM)

    def band(t):
        tp = jnp.pad(t, ((0, 0), (WINDOW, 0), (0, 0)))
        tb = tp.reshape(b, nb + 1, WINDOW, ATT_KV_HEADS, ATT_HEAD_DIM)
        return jnp.concatenate([tb[:, :-1], tb[:, 1:]], axis=2)

    kb, vb = band(k), band(v)
    scores = jnp.einsum('bnqhgd,bnshd->bnhgqs', qb, kb).astype(jnp.float32) * (ATT_HEAD_DIM ** -0.5)
    qi = jnp.arange(WINDOW)[:, None]
    si = jnp.arange(2 * WINDOW)[None, :]
    diff = qi + WINDOW - si
    blk = jnp.arange(nb)[:, None, None]
    valid = (diff >= 0) & (diff < WINDOW) & (blk * WINDOW + si - WINDOW >= 0)
    scores = jnp.where(valid[None, :, None, None], scores, -jnp.inf)
    sink = sinks.astype(jnp.float32).reshape(1, 1, ATT_KV_HEADS, ATT_GROUP, 1, 1)
    m = jnp.maximum(jnp.max(scores, axis=-1, keepdims=True), sink)
    p = jnp.exp(scores - m)
    denom = jnp.sum(p, axis=-1, keepdims=True) + jnp.exp(sink - m)
    probs = (p / denom).astype(v.dtype)
    o = jnp.einsum('bnhgqs,bnshd->bnqhgd', probs, vb)
    return o.reshape(b, s, ATT_Q_W)


def _gated_delta_rule(q, k, v, beta, g):
    b, s, h, d = q.shape
    n = s // CHUNK

    def chunks(t):
        t = t.reshape((b, n, CHUNK, h) + t.shape[3:])
        return jnp.moveaxis(t, 3, 1)

    q, k, v, beta, g = (chunks(t) for t in (q, k, v, beta, g))
    g_cum = jnp.cumsum(g, axis=-1)
    causal = jnp.tril(jnp.ones((CHUNK, CHUNK), dtype=bool))
    strict = jnp.tril(jnp.ones((CHUNK, CHUNK), dtype=bool), -1)
    decay = jnp.exp(jnp.where(causal, g_cum[..., :, None] - g_cum[..., None, :], -jnp.inf))
    kb = k * beta[..., None]
    vb = v * beta[..., None]
    l_mat = jnp.where(strict, jnp.einsum('bhncd,bhnsd->bhncs', kb, k) * decay, 0.0)
    a_mat = l_mat + jnp.eye(CHUNK, dtype=l_mat.dtype)
    rhs = jnp.concatenate([vb, kb * jnp.exp(g_cum)[..., None]], axis=-1)
    sol = lax.linalg.triangular_solve(a_mat, rhs, left_side=True, lower=True, unit_diagonal=True)
    u, w = sol[..., :d], sol[..., d:]
    intra = jnp.einsum('bhncd,bhnsd->bhncs', q, k) * decay
    q_dec = q * jnp.exp(g_cum)[..., None]
    k_dec = k * jnp.exp(g_cum[..., -1:] - g_cum)[..., None]
    last = jnp.exp(g_cum[..., -1])
    xs = tuple(jnp.moveaxis(t, 2, 0) for t in (u, w, intra, q_dec, k_dec, last))

    def step(state, inp):
        u_c, w_c, intra_c, q_c, k_c, last_c = inp
        v_new = u_c - jnp.einsum('bhck,bhkv->bhcv', w_c, state)
        o_c = jnp.einsum('bhck,bhkv->bhcv', q_c, state) + jnp.einsum('bhcs,bhsv->bhcv', intra_c, v_new)
        state = state * last_c[..., None, None] + jnp.einsum('bhck,bhcv->bhkv', k_c, v_new)
        return state, o_c

    s0 = jnp.zeros((b, h, d, d), jnp.float32)
    _, o = lax.scan(step, s0, xs)
    return jnp.transpose(o, (1, 0, 3, 2, 4)).reshape(b, s, h, d)


def _gated_deltanet(dq, dk, dv, z, b_raw, a_raw, conv_w, a_log, dt_bias, norm_w):
    bsz, s, _ = dq.shape
    qkv = _causal_conv_silu(jnp.concatenate([dq, dk, dv], axis=-1), conv_w)
    shp = (bsz, s, DN_HEADS, DN_HEAD_DIM)
    q = _l2norm(qkv[..., :DN_W].reshape(shp).astype(jnp.float32)) * (DN_HEAD_DIM ** -0.5)
    k = _l2norm(qkv[..., DN_W:2 * DN_W].reshape(shp).astype(jnp.float32))
    v = qkv[..., 2 * DN_W:].reshape(shp).astype(jnp.float32)
    beta = jax.nn.sigmoid(b_raw.astype(jnp.float32))
    g = -jnp.exp(a_log.astype(jnp.float32)) * jax.nn.softplus(a_raw.astype(jnp.float32) + dt_bias.astype(jnp.float32))
    o = _gated_delta_rule(q, k, v, beta, g)
    o = o * lax.rsqrt(jnp.mean(jnp.square(o), axis=-1, keepdims=True) + RMS_EPS) * norm_w.astype(jnp.float32)
    o = o * jax.nn.silu(z.reshape(shp).astype(jnp.float32))
    return o.reshape(bsz, s, DN_W).astype(dq.dtype)


def _fwd_setup_inputs(seed: int = 0) -> dict:
    key = jax.random.key(seed)
    ks = jax.random.split(key, 24)
    nrm = jax.random.normal
    L, D = DEPTH, D_MODEL
    dt = jnp.exp(jax.random.uniform(ks[8], (L, DN_HEADS), minval=np.log(1e-3), maxval=np.log(1e-1)))
    return {
        "x": nrm(ks[0], (BATCH, SEQ, D), jnp.float32),
        "c": nrm(ks[1], (BATCH, D), jnp.float32),
        "w_ada": nrm(ks[2], (L, D, 6 * D), jnp.float32) * (ADA_SCALE * D ** -0.5),
        "b_ada": nrm(ks[3], (L, 6 * D), jnp.float32) * 0.01,
        "w_in": nrm(ks[4], (L, D, D_IN), jnp.float32) * D ** -0.5,
        "conv_w": nrm(ks[5], (L, CONV_K, 3 * DN_W), jnp.float32) * CONV_K ** -0.5,
        "a_log": jnp.log(jax.random.uniform(ks[6], (L, DN_HEADS), minval=1.0, maxval=16.0)),
        "dt_bias": dt + jnp.log(-jnp.expm1(-dt)),
        "sinks": nrm(ks[7], (L, ATT_HEADS), jnp.float32),
        "dn_norm_w": 1.0 + 0.02 * nrm(ks[9], (L, DN_HEAD_DIM), jnp.float32),
        "w_oa": nrm(ks[10], (L, ATT_Q_W, D), jnp.float32) * ATT_Q_W ** -0.5,
        "w_ob": nrm(ks[11], (L, DN_W, D), jnp.float32) * DN_W ** -0.5,
        "w_out": nrm(ks[12], (L, D, D), jnp.float32) * (BETA_INIT * D ** -0.5),
        "ln1_g": 1.0 + 0.02 * nrm(ks[13], (L, D), jnp.float32),
        "ln1_b": 0.02 * nrm(ks[14], (L, D), jnp.float32),
        "w_ff1": nrm(ks[15], (L, D, D_FF), jnp.float32) * D ** -0.5,
        "b_ff1": 0.02 * nrm(ks[16], (L, D_FF), jnp.float32),
        "w_ff2": nrm(ks[17], (L, D_FF, D), jnp.float32) * (BETA_INIT * D_FF ** -0.5),
        "b_ff2": 0.02 * nrm(ks[18], (L, D), jnp.float32),
        "ln2_g": 1.0 + 0.02 * nrm(ks[19], (L, D), jnp.float32),
        "ln2_b": 0.02 * nrm(ks[20], (L, D), jnp.float32),
    }


def _fwd_reference(x, c, w_ada, b_ada, w_in, conv_w, a_log, dt_bias, sinks, dn_norm_w, w_oa, w_ob, w_out,
              ln1_g, ln1_b, w_ff1, b_ff1, w_ff2, b_ff2, ln2_g, ln2_b):
    c_act = jax.nn.silu(c)
    for l in range(DEPTH):
        mod = c_act @ w_ada[l] + b_ada[l]
        sh1, sc1, gt1, sh2, sc2, gt2 = jnp.split(mod[:, None, :], 6, axis=-1)
        u = x * (1.0 + sc1) + sh1
        proj = u @ w_in[l]
        qa, ka, va, dq, dk, dv, z, b_raw, a_raw, g_a, g_b = _split_in(proj)
        y_a = _sliding_window_attention(qa, ka, va, sinks[l]) @ w_oa[l]
        y_b = _gated_deltanet(dq, dk, dv, z, b_raw, a_raw, conv_w[l], a_log[l], dt_bias[l], dn_norm_w[l]) @ w_ob[l]
        mixed = (jax.nn.sigmoid(g_a) * y_a + jax.nn.sigmoid(g_b) * y_b) @ w_out[l]
        x = _layer_norm(ALPHA * x + (1.0 + gt1) * mixed, ln1_g[l], ln1_b[l])
        u2 = x * (1.0 + sc2) + sh2
        h = jnp.square(jax.nn.relu(u2 @ w_ff1[l] + b_ff1[l]))
        x = _layer_norm(ALPHA * x + (1.0 + gt2) * (h @ w_ff2[l] + b_ff2[l]), ln2_g[l], ln2_b[l])
    return x


import jax as _jax
import jax.numpy as _jnp

TWIN_FORMAT = 'train_step'
FWD_PARAMS = ['x', 'c', 'w_ada', 'b_ada', 'w_in', 'conv_w', 'a_log', 'dt_bias', 'sinks', 'dn_norm_w', 'w_oa', 'w_ob', 'w_out', 'ln1_g', 'ln1_b', 'w_ff1', 'b_ff1', 'w_ff2', 'b_ff2', 'ln2_g', 'ln2_b']
TWIN_WEIGHTS = ['w_ada', 'b_ada', 'w_in', 'conv_w', 'a_log', 'dt_bias', 'sinks', 'dn_norm_w', 'w_oa', 'w_ob', 'w_out', 'ln1_g', 'ln1_b', 'w_ff1', 'b_ff1', 'w_ff2', 'b_ff2', 'ln2_g', 'ln2_b']
TWIN_DIFF_INPUT = 'x'
TWIN_INPUTS = ['x', 'c', 'w_ada', 'b_ada', 'w_in', 'conv_w', 'a_log', 'dt_bias', 'sinks', 'dn_norm_w', 'w_oa', 'w_ob', 'w_out', 'ln1_g', 'ln1_b', 'w_ff1', 'b_ff1', 'w_ff2', 'b_ff2', 'ln2_g', 'ln2_b', 'loss_target', 'm_w_ada', 'm_b_ada', 'm_w_in', 'm_conv_w', 'm_a_log', 'm_dt_bias', 'm_sinks', 'm_dn_norm_w', 'm_w_oa', 'm_w_ob', 'm_w_out', 'm_ln1_g', 'm_ln1_b', 'm_w_ff1', 'm_b_ff1', 'm_w_ff2', 'm_b_ff2', 'm_ln2_g', 'm_ln2_b', 'v_w_ada', 'v_b_ada', 'v_w_in', 'v_conv_w', 'v_a_log', 'v_dt_bias', 'v_sinks', 'v_dn_norm_w', 'v_w_oa', 'v_w_ob', 'v_w_out', 'v_ln1_g', 'v_ln1_b', 'v_w_ff1', 'v_b_ff1', 'v_w_ff2', 'v_b_ff2', 'v_ln2_g', 'v_ln2_b']
TWIN_OUTPUTS = ['loss', 'grad_x', 'grad_w_ada', 'grad_b_ada', 'grad_w_in', 'grad_conv_w', 'grad_a_log', 'grad_dt_bias', 'grad_sinks', 'grad_dn_norm_w', 'grad_w_oa', 'grad_w_ob', 'grad_w_out', 'grad_ln1_g', 'grad_ln1_b', 'grad_w_ff1', 'grad_b_ff1', 'grad_w_ff2', 'grad_b_ff2', 'grad_ln2_g', 'grad_ln2_b', 'delta_w_ada', 'delta_b_ada', 'delta_w_in', 'delta_conv_w', 'delta_a_log', 'delta_dt_bias', 'delta_sinks', 'delta_dn_norm_w', 'delta_w_oa', 'delta_w_ob', 'delta_w_out', 'delta_ln1_g', 'delta_ln1_b', 'delta_w_ff1', 'delta_b_ff1', 'delta_w_ff2', 'delta_b_ff2', 'delta_ln2_g', 'delta_ln2_b', 'new_m_w_ada', 'new_m_b_ada', 'new_m_w_in', 'new_m_conv_w', 'new_m_a_log', 'new_m_dt_bias', 'new_m_sinks', 'new_m_dn_norm_w', 'new_m_w_oa', 'new_m_w_ob', 'new_m_w_out', 'new_m_ln1_g', 'new_m_ln1_b', 'new_m_w_ff1', 'new_m_b_ff1', 'new_m_w_ff2', 'new_m_b_ff2', 'new_m_ln2_g', 'new_m_ln2_b', 'new_v_w_ada', 'new_v_b_ada', 'new_v_w_in', 'new_v_conv_w', 'new_v_a_log', 'new_v_dt_bias', 'new_v_sinks', 'new_v_dn_norm_w', 'new_v_w_oa', 'new_v_w_ob', 'new_v_w_out', 'new_v_ln1_g', 'new_v_ln1_b', 'new_v_w_ff1', 'new_v_b_ff1', 'new_v_w_ff2', 'new_v_b_ff2', 'new_v_ln2_g', 'new_v_ln2_b']
TWIN_LEAF_KINDS = {'loss': 'loss', 'grad_x': 'grad_x', 'grad_w_ada': 'grad_w', 'grad_b_ada': 'grad_w', 'grad_w_in': 'grad_w', 'grad_conv_w': 'grad_w', 'grad_a_log': 'grad_w', 'grad_dt_bias': 'grad_w', 'grad_sinks': 'grad_w', 'grad_dn_norm_w': 'grad_w', 'grad_w_oa': 'grad_w', 'grad_w_ob': 'grad_w', 'grad_w_out': 'grad_w', 'grad_ln1_g': 'grad_w', 'grad_ln1_b': 'grad_w', 'grad_w_ff1': 'grad_w', 'grad_b_ff1': 'grad_w', 'grad_w_ff2': 'grad_w', 'grad_b_ff2': 'grad_w', 'grad_ln2_g': 'grad_w', 'grad_ln2_b': 'grad_w', 'delta_w_ada': 'delta_w', 'delta_b_ada': 'delta_w', 'delta_w_in': 'delta_w', 'delta_conv_w': 'delta_w', 'delta_a_log': 'delta_w', 'delta_dt_bias': 'delta_w', 'delta_sinks': 'delta_w', 'delta_dn_norm_w': 'delta_w', 'delta_w_oa': 'delta_w', 'delta_w_ob': 'delta_w', 'delta_w_out': 'delta_w', 'delta_ln1_g': 'delta_w', 'delta_ln1_b': 'delta_w', 'delta_w_ff1': 'delta_w', 'delta_b_ff1': 'delta_w', 'delta_w_ff2': 'delta_w', 'delta_b_ff2': 'delta_w', 'delta_ln2_g': 'delta_w', 'delta_ln2_b': 'delta_w', 'new_m_w_ada': 'new_m', 'new_m_b_ada': 'new_m', 'new_m_w_in': 'new_m', 'new_m_conv_w': 'new_m', 'new_m_a_log': 'new_m', 'new_m_dt_bias': 'new_m', 'new_m_sinks': 'new_m', 'new_m_dn_norm_w': 'new_m', 'new_m_w_oa': 'new_m', 'new_m_w_ob': 'new_m', 'new_m_w_out': 'new_m', 'new_m_ln1_g': 'new_m', 'new_m_ln1_b': 'new_m', 'new_m_w_ff1': 'new_m', 'new_m_b_ff1': 'new_m', 'new_m_w_ff2': 'new_m', 'new_m_b_ff2': 'new_m', 'new_m_ln2_g': 'new_m', 'new_m_ln2_b': 'new_m', 'new_v_w_ada': 'new_v', 'new_v_b_ada': 'new_v', 'new_v_w_in': 'new_v', 'new_v_conv_w': 'new_v', 'new_v_a_log': 'new_v', 'new_v_dt_bias': 'new_v', 'new_v_sinks': 'new_v', 'new_v_dn_norm_w': 'new_v', 'new_v_w_oa': 'new_v', 'new_v_w_ob': 'new_v', 'new_v_w_out': 'new_v', 'new_v_ln1_g': 'new_v', 'new_v_ln1_b': 'new_v', 'new_v_w_ff1': 'new_v', 'new_v_b_ff1': 'new_v', 'new_v_w_ff2': 'new_v', 'new_v_b_ff2': 'new_v', 'new_v_ln2_g': 'new_v', 'new_v_ln2_b': 'new_v'}


def _forward(args):
    return _fwd_reference(*[args[k] for k in FWD_PARAMS])


def _output_shape():
    out = _jax.eval_shape(lambda: _forward(_fwd_setup_inputs(0)))
    return out.shape, out.dtype

N_MICROBATCH = 1
ADAM_LR = 0.001
ADAM_B1 = 0.9
ADAM_B2 = 0.999
ADAM_EPS = 1e-08
ADAM_WD = 0.01
ADAM_STEP = 10
PER_EXAMPLE_BATCH_AXIS = {'x': 0, 'c': 0, 'loss_target': 0}
SHARED_INPUTS = []
_WEIGHT_DTYPES = {'w_ada': _jnp.float32, 'b_ada': _jnp.float32, 'w_in': _jnp.float32, 'conv_w': _jnp.float32, 'a_log': _jnp.float32, 'dt_bias': _jnp.float32, 'sinks': _jnp.float32, 'dn_norm_w': _jnp.float32, 'w_oa': _jnp.float32, 'w_ob': _jnp.float32, 'w_out': _jnp.float32, 'ln1_g': _jnp.float32, 'ln1_b': _jnp.float32, 'w_ff1': _jnp.float32, 'b_ff1': _jnp.float32, 'w_ff2': _jnp.float32, 'b_ff2': _jnp.float32, 'ln2_g': _jnp.float32, 'ln2_b': _jnp.float32}
MOMENT_SCALE = {'w_ada': 2.358501e-02, 'b_ada': 4.525477e-02, 'w_in': 7.263026e-03, 'conv_w': 8.532264e-03, 'a_log': 3.511032e-02, 'dt_bias': 3.442868e-02, 'sinks': 2.929244e-03, 'dn_norm_w': 3.130028e-02, 'w_oa': 6.219395e-03, 'w_ob': 1.138127e-02, 'w_out': 3.090350e-02, 'ln1_g': 4.627141e-01, 'ln1_b': 2.887073e-01, 'w_ff1': 2.310837e-02, 'b_ff1': 3.301998e-02, 'w_ff2': 1.188902e-01, 'b_ff2': 1.684886e-01, 'ln2_g': 8.070816e+00, 'ln2_b': 1.902330e+00}


def _to_microbatches(a, axis):
    t = _jnp.moveaxis(a, axis, 0)
    t = t.reshape((N_MICROBATCH, t.shape[0] // N_MICROBATCH) + t.shape[1:])
    return _jnp.moveaxis(t, 1, axis + 1)


def setup_inputs(seed: int = 0) -> dict:
    inp = _fwd_setup_inputs(seed)
    key = _jax.random.fold_in(_jax.random.key(seed), 7919)
    shape, _ = _output_shape()
    out = dict(inp)
    out["loss_target"] = _jax.random.normal(_jax.random.fold_in(key, 0), shape, _jnp.float32)
    for i, name in enumerate(TWIN_WEIGHTS):
        w = inp[name].astype(_jnp.float32)
        if MOMENT_SCALE is None:
            s = _jnp.sqrt(_jnp.mean(_jnp.square(w)) + 1e-30)
        else:
            s = MOMENT_SCALE[name]
        km, kv = _jax.random.split(_jax.random.fold_in(key, i + 1))
        out[name] = w
        out["m_" + name] = s * _jax.random.normal(km, w.shape, _jnp.float32)
        out["v_" + name] = (s * s) * _jax.random.uniform(kv, w.shape, _jnp.float32, 0.5, 1.5)
    if N_MICROBATCH > 1:
        for name, axis in PER_EXAMPLE_BATCH_AXIS.items():
            out[name] = _to_microbatches(out[name], axis)
    return {'x': out['x'], 'c': out['c'], 'w_ada': out['w_ada'], 'b_ada': out['b_ada'], 'w_in': out['w_in'], 'conv_w': out['conv_w'], 'a_log': out['a_log'], 'dt_bias': out['dt_bias'], 'sinks': out['sinks'], 'dn_norm_w': out['dn_norm_w'], 'w_oa': out['w_oa'], 'w_ob': out['w_ob'], 'w_out': out['w_out'], 'ln1_g': out['ln1_g'], 'ln1_b': out['ln1_b'], 'w_ff1': out['w_ff1'], 'b_ff1': out['b_ff1'], 'w_ff2': out['w_ff2'], 'b_ff2': out['b_ff2'], 'ln2_g': out['ln2_g'], 'ln2_b': out['ln2_b'], 'loss_target': out['loss_target'], 'm_w_ada': out['m_w_ada'], 'm_b_ada': out['m_b_ada'], 'm_w_in': out['m_w_in'], 'm_conv_w': out['m_conv_w'], 'm_a_log': out['m_a_log'], 'm_dt_bias': out['m_dt_bias'], 'm_sinks': out['m_sinks'], 'm_dn_norm_w': out['m_dn_norm_w'], 'm_w_oa': out['m_w_oa'], 'm_w_ob': out['m_w_ob'], 'm_w_out': out['m_w_out'], 'm_ln1_g': out['m_ln1_g'], 'm_ln1_b': out['m_ln1_b'], 'm_w_ff1': out['m_w_ff1'], 'm_b_ff1': out['m_b_ff1'], 'm_w_ff2': out['m_w_ff2'], 'm_b_ff2': out['m_b_ff2'], 'm_ln2_g': out['m_ln2_g'], 'm_ln2_b': out['m_ln2_b'], 'v_w_ada': out['v_w_ada'], 'v_b_ada': out['v_b_ada'], 'v_w_in': out['v_w_in'], 'v_conv_w': out['v_conv_w'], 'v_a_log': out['v_a_log'], 'v_dt_bias': out['v_dt_bias'], 'v_sinks': out['v_sinks'], 'v_dn_norm_w': out['v_dn_norm_w'], 'v_w_oa': out['v_w_oa'], 'v_w_ob': out['v_w_ob'], 'v_w_out': out['v_w_out'], 'v_ln1_g': out['v_ln1_g'], 'v_ln1_b': out['v_ln1_b'], 'v_w_ff1': out['v_w_ff1'], 'v_b_ff1': out['v_b_ff1'], 'v_w_ff2': out['v_w_ff2'], 'v_b_ff2': out['v_b_ff2'], 'v_ln2_g': out['v_ln2_g'], 'v_ln2_b': out['v_ln2_b']}


def _loss(weights, diff, rest, loss_target):
    with _jax.named_scope("forward"):
        args = {**rest, TWIN_DIFF_INPUT: diff, **{k: w.astype(_WEIGHT_DTYPES[k]) for k, w in weights.items()}}
        y = _forward(args)
    with _jax.named_scope("loss_head"):
        err = _jnp.square(y.astype(_jnp.float32) - loss_target)
        return 0.5 * _jnp.sum(_jnp.mean(err, axis=-1)) if err.ndim else 0.5 * err


def _adamw(w, g, m, v):
    m = ADAM_B1 * m + (1.0 - ADAM_B1) * g
    v = ADAM_B2 * v + (1.0 - ADAM_B2) * _jnp.square(g)
    m_hat = m / (1.0 - ADAM_B1 ** ADAM_STEP)
    v_hat = v / (1.0 - ADAM_B2 ** ADAM_STEP)
    delta = -ADAM_LR * (m_hat / (_jnp.sqrt(v_hat) + ADAM_EPS) + ADAM_WD * w)
    return delta, m, v


def reference(x, c, w_ada, b_ada, w_in, conv_w, a_log, dt_bias, sinks, dn_norm_w, w_oa, w_ob, w_out, ln1_g, ln1_b, w_ff1, b_ff1, w_ff2, b_ff2, ln2_g, ln2_b, loss_target, m_w_ada, m_b_ada, m_w_in, m_conv_w, m_a_log, m_dt_bias, m_sinks, m_dn_norm_w, m_w_oa, m_w_ob, m_w_out, m_ln1_g, m_ln1_b, m_w_ff1, m_b_ff1, m_w_ff2, m_b_ff2, m_ln2_g, m_ln2_b, v_w_ada, v_b_ada, v_w_in, v_conv_w, v_a_log, v_dt_bias, v_sinks, v_dn_norm_w, v_w_oa, v_w_ob, v_w_out, v_ln1_g, v_ln1_b, v_w_ff1, v_b_ff1, v_w_ff2, v_b_ff2, v_ln2_g, v_ln2_b):
    given = dict(x=x, c=c, w_ada=w_ada, b_ada=b_ada, w_in=w_in, conv_w=conv_w, a_log=a_log, dt_bias=dt_bias, sinks=sinks, dn_norm_w=dn_norm_w, w_oa=w_oa, w_ob=w_ob, w_out=w_out, ln1_g=ln1_g, ln1_b=ln1_b, w_ff1=w_ff1, b_ff1=b_ff1, w_ff2=w_ff2, b_ff2=b_ff2, ln2_g=ln2_g, ln2_b=ln2_b, loss_target=loss_target, m_w_ada=m_w_ada, m_b_ada=m_b_ada, m_w_in=m_w_in, m_conv_w=m_conv_w, m_a_log=m_a_log, m_dt_bias=m_dt_bias, m_sinks=m_sinks, m_dn_norm_w=m_dn_norm_w, m_w_oa=m_w_oa, m_w_ob=m_w_ob, m_w_out=m_w_out, m_ln1_g=m_ln1_g, m_ln1_b=m_ln1_b, m_w_ff1=m_w_ff1, m_b_ff1=m_b_ff1, m_w_ff2=m_w_ff2, m_b_ff2=m_b_ff2, m_ln2_g=m_ln2_g, m_ln2_b=m_ln2_b, v_w_ada=v_w_ada, v_b_ada=v_b_ada, v_w_in=v_w_in, v_conv_w=v_conv_w, v_a_log=v_a_log, v_dt_bias=v_dt_bias, v_sinks=v_sinks, v_dn_norm_w=v_dn_norm_w, v_w_oa=v_w_oa, v_w_ob=v_w_ob, v_w_out=v_w_out, v_ln1_g=v_ln1_g, v_ln1_b=v_ln1_b, v_w_ff1=v_w_ff1, v_b_ff1=v_b_ff1, v_w_ff2=v_w_ff2, v_b_ff2=v_b_ff2, v_ln2_g=v_ln2_g, v_ln2_b=v_ln2_b)
    weights = {n: given[n] for n in TWIN_WEIGHTS}
    shared = {n: given[n] for n in SHARED_INPUTS}
    per_example = {n: given[n] for n in ['x', 'c']}
    grad_fn = _jax.value_and_grad(_loss, argnums=(0, 1))

    def one_microbatch(ex, loss_target):
        ex = dict(ex)
        diff = ex.pop(TWIN_DIFF_INPUT)
        return grad_fn(weights, diff, {**shared, **ex}, loss_target)

    if N_MICROBATCH == 1:
        loss, (grad_w, grad_x) = one_microbatch(per_example, given["loss_target"])
    else:
        def body(carry, xs):
            loss_sum, grad_sum = carry
            l_k, (gw_k, gx_k) = one_microbatch(xs[0], xs[1])
            with _jax.named_scope("update"):
                return (loss_sum + l_k, _jax.tree.map(_jnp.add, grad_sum, gw_k)), gx_k

        init = (_jnp.zeros((), _jnp.float32), _jax.tree.map(_jnp.zeros_like, weights))
        (loss, grad_w), grad_x = _jax.lax.scan(body, init, (per_example, given["loss_target"]))
    with _jax.named_scope("update"):
        delta_w, new_m, new_v = {}, {}, {}
        for n in TWIN_WEIGHTS:
            delta_w[n], new_m[n], new_v[n] = _adamw(weights[n], grad_w[n], given["m_" + n], given["v_" + n])
    return (loss, grad_x, *[grad_w[n] for n in TWIN_WEIGHTS], *[delta_w[n] for n in TWIN_WEIGHTS],
            *[new_m[n] for n in TWIN_WEIGHTS], *[new_v[n] for n in TWIN_WEIGHTS])
```

```python
import functools

import jax
import jax.numpy as jnp
from jax import lax
from jax.experimental import pallas as pl
from jax.experimental.pallas import tpu as pltpu

F32 = jnp.float32
BF16 = jnp.bfloat16

D = 1024
DEPTH = 4
N_DEV = 8
ATT_HEADS, ATT_KV, ATT_GROUP, ATT_HD, WINDOW = 16, 4, 4, 64, 128
DN_HEADS, DN_HD, CONV_K, CHUNK = 8, 128, 4, 64
D_FF = 4096
ADA_COLS = 6 * D // N_DEV
ALPHA = (2 * DEPTH) ** 0.25
LN_EPS = 1e-5
RMS_EPS = 1e-6
ADAM_LR, ADAM_B1, ADAM_B2, ADAM_EPS, ADAM_WD, ADAM_STEP = 0.001, 0.9, 0.999, 1e-08, 0.01, 10
AXES = ("x", "y", "c")
MESH_IDS = pl.DeviceIdType.MESH
VMEM_LIMIT_BYTES = 48 * 1024 * 1024
LANES = 128
HIGHEST = lax.Precision.HIGHEST

IN_GROUPS = (("q", 1024), ("k", 256), ("v", 256), ("dq", 1024), ("dk", 1024), ("dv", 1024), ("z", 1024),
             ("small", 16), ("ga", 1024), ("gb", 1024))


def _params(sem=None):
    return pltpu.CompilerParams(dimension_semantics=sem, vmem_limit_bytes=VMEM_LIMIT_BYTES)


def _tile(n, pref, unit):
    if n <= pref:
        return n
    t = (pref // unit) * unit
    while t > unit and n % t:
        t -= unit
    assert n % t == 0, (n, pref, unit)
    return t


def _dg(a, b, ca, cb):
    return lax.dot_general(a.astype(BF16), b.astype(BF16), (((ca,), (cb,)), ((), ())), preferred_element_type=F32)


@jax.custom_vjp
def bdot_nn(a, b):
    return _dg(a, b, 1, 0)


def _bdot_nn_fwd(a, b):
    return _dg(a, b, 1, 0), (a, b)


def _bdot_nn_bwd(res, g):
    a, b = res
    return _dg(g, b, 1, 1).astype(a.dtype), _dg(a, g, 0, 0).astype(b.dtype)


bdot_nn.defvjp(_bdot_nn_fwd, _bdot_nn_bwd)


@jax.custom_vjp
def bdot_nt(a, b):
    return _dg(a, b, 1, 1)


def _bdot_nt_fwd(a, b):
    return _dg(a, b, 1, 1), (a, b)


def _bdot_nt_bwd(res, g):
    a, b = res
    return _dg(g, b, 1, 0).astype(a.dtype), _dg(g, a, 0, 0).astype(b.dtype)


bdot_nt.defvjp(_bdot_nt_fwd, _bdot_nt_bwd)


@jax.custom_vjp
def bdot_tn(a, b):
    return _dg(a, b, 0, 0)


def _bdot_tn_fwd(a, b):
    return _dg(a, b, 0, 0), (a, b)


def _bdot_tn_bwd(res, g):
    a, b = res
    return _dg(b, g, 1, 1).astype(a.dtype), _dg(a, g, 1, 0).astype(b.dtype)


bdot_tn.defvjp(_bdot_tn_fwd, _bdot_tn_bwd)


def hdot(a, b):
    return jnp.dot(a, b, precision=HIGHEST, preferred_element_type=F32)


def matmul(a, b, mode, out_dtype, name, acc=None):
    if mode == "nn":
        (m, k), (k2, n) = a.shape, b.shape
    elif mode == "nt":
        (m, k), (n, k2) = a.shape, b.shape
    else:
        (k, m), (k2, n) = a.shape, b.shape
    assert k == k2, (a.shape, b.shape, mode)
    tm, tn = _tile(m, 512, LANES), _tile(n, 512, LANES)
    tk = _tile(k, 512 if mode == "tn" else 1024, LANES)
    nk = k // tk
    a_spec = pl.BlockSpec((tk, tm), lambda i, j, kk: (kk, i)) if mode == "tn" else pl.BlockSpec((tm, tk), lambda i, j, kk: (i, kk))
    b_spec = pl.BlockSpec((tn, tk), lambda i, j, kk: (j, kk)) if mode == "nt" else pl.BlockSpec((tk, tn), lambda i, j, kk: (kk, j))
    o_spec = pl.BlockSpec((tm, tn), lambda i, j, kk: (i, j))
    ca, cb = {"nn": (1, 0), "nt": (1, 1), "tn": (0, 0)}[mode]

    def body(*refs):
        if acc is None:
            a_ref, b_ref, o_ref, acc_ref = refs
        else:
            a_ref, b_ref, c_ref, o_ref, acc_ref = refs
        kk = pl.program_id(2)

        @pl.when(kk == 0)
        def _():
            acc_ref[...] = jnp.zeros_like(acc_ref) if acc is None else c_ref[...].astype(F32)

        acc_ref[...] += _dg(a_ref[...], b_ref[...], ca, cb)

        @pl.when(kk == nk - 1)
        def _():
            o_ref[...] = acc_ref[...].astype(out_dtype)

    ins, in_specs = [a, b], [a_spec, b_spec]
    if acc is not None:
        ins.append(acc)
        in_specs.append(o_spec)
    return pl.pallas_call(
        body, name=name, grid=(m // tm, n // tn, nk), in_specs=in_specs, out_specs=o_spec,
        out_shape=jax.ShapeDtypeStruct((m, n), out_dtype), scratch_shapes=[pltpu.VMEM((tm, tn), F32)],
        compiler_params=_params(("parallel", "parallel", "arbitrary")))(*ins)


def make_mm(name):
    @jax.custom_vjp
    def mm(a, w):
        return matmul(a, w, "nn", F32, name + "_fwd")

    def fwd(a, w):
        return mm(a, w), (a, w)

    def bwd(res, g):
        a, w = res
        return matmul(g, w, "nt", a.dtype, name + "_da"), matmul(a, g, "tn", w.dtype, name + "_dw")

    mm.defvjp(fwd, bwd)
    return mm


def make_in_proj(name):
    @jax.custom_vjp
    def in_proj(u, ws):
        return tuple(matmul(u, w, "nn", F32, f"{name}_fwd_{g}") for (g, _), w in zip(IN_GROUPS, ws))

    def fwd(u, ws):
        return in_proj(u, ws), (u, ws)

    def bwd(res, gs):
        u, ws = res
        du = None
        for idx, ((g, _), w, dy) in enumerate(zip(IN_GROUPS, ws, gs)):
            last = idx == len(ws) - 1
            du = matmul(dy, w, "nt", u.dtype if last else F32, f"{name}_du_{g}", acc=du)
        dws = tuple(matmul(u, dy, "tn", w.dtype, f"{name}_dw_{g}") for (g, _), w, dy in zip(IN_GROUPS, ws, gs))
        return du, dws

    in_proj.defvjp(fwd, bwd)
    return in_proj


def make_rowwise(fn, name, tile_kinds, param_kinds, out_kinds, nc=1, tm=256):
    n_t, n_p, n_o = len(tile_kinds), len(param_kinds), len(out_kinds)

    def width(a, kind):
        if kind == "h":
            return a.shape[2]
        return a.shape[1] if kind == "s" else a.shape[1] // nc

    def spec(kind, w, rows):
        if kind == "t":
            return pl.BlockSpec((rows, w), lambda j, i: (i, j))
        if kind == "h":
            return pl.BlockSpec((None, rows, w), lambda j, i: (j, i, 0))
        if kind == "c":
            return pl.BlockSpec((1, w), lambda j, i: (0, j))
        return pl.BlockSpec((1, w), lambda j, i: (0, 0))

    def full_shape(kind, w, s):
        return (s, w * nc) if kind == "t" else (nc, s, w)

    def plan(tiles, params):
        s = tiles[0].shape[0] if tile_kinds[0] == "t" else tiles[0].shape[1]
        rows = min(tm, s)
        t_w = [width(a, kd) for a, kd in zip(tiles, tile_kinds)]
        p_w = [width(a, kd) for a, kd in zip(params, param_kinds)]
        t_s = [jax.ShapeDtypeStruct((rows, w), a.dtype) for a, w in zip(tiles, t_w)]
        p_s = [jax.ShapeDtypeStruct((1, w), a.dtype) for a, w in zip(params, p_w)]
        o_s = jax.eval_shape(fn, *t_s, *p_s)
        return s, rows, t_w, p_w, o_s

    def fwd_call(*args):
        tiles, params = args[:n_t], args[n_t:]
        s, rows, t_w, p_w, o_s = plan(tiles, params)

        def body(*refs):
            ins, outs = refs[:n_t + n_p], refs[n_t + n_p:]
            res = fn(*[r[...] for r in ins])
            for o_ref, val in zip(outs, res):
                o_ref[...] = val

        in_specs = [spec(kd, w, rows) for kd, w in zip(tile_kinds, t_w)] + [spec(kd, w, rows) for kd, w in zip(param_kinds, p_w)]
        return pl.pallas_call(
            body, name=name + "_fwd", grid=(nc, s // rows), in_specs=in_specs,
            out_specs=[spec(kd, o.shape[1], rows) for kd, o in zip(out_kinds, o_s)],
            out_shape=[jax.ShapeDtypeStruct(full_shape(kd, o.shape[1], s), o.dtype) for kd, o in zip(out_kinds, o_s)],
            compiler_params=_params(("parallel", "parallel")))(*args)

    def bwd_call(args, douts):
        tiles, params = args[:n_t], args[n_t:]
        s, rows, t_w, p_w, o_s = plan(tiles, params)

        def body(*refs):
            ins = refs[:n_t + n_p]
            dos = refs[n_t + n_p:n_t + n_p + n_o]
            dts = refs[n_t + n_p + n_o:n_t + n_p + n_o + n_t]
            dps = refs[n_t + n_p + n_o + n_t:]
            j, i = pl.program_id(0), pl.program_id(1)
            _, vjp = jax.vjp(lambda *a: tuple(fn(*a)), *[r[...] for r in ins])
            grads = vjp(tuple(r[...] for r in dos))
            for r, g in zip(dts, grads[:n_t]):
                r[...] = g.astype(r.dtype)
            for r, g, kd in zip(dps, grads[n_t:], param_kinds):
                first = (i == 0) if kd == "c" else jnp.logical_and(i == 0, j == 0)

                @pl.when(first)
                def _(r=r):
                    r[...] = jnp.zeros_like(r)

                r[...] += g.astype(F32)

        in_specs = ([spec(kd, w, rows) for kd, w in zip(tile_kinds, t_w)] + [spec(kd, w, rows) for kd, w in zip(param_kinds, p_w)]
                    + [spec(kd, o.shape[1], rows) for kd, o in zip(out_kinds, o_s)])
        out_specs = [spec(kd, w, rows) for kd, w in zip(tile_kinds, t_w)] + [spec(kd, w, rows) for kd, w in zip(param_kinds, p_w)]
        out_shape = [jax.ShapeDtypeStruct(a.shape, a.dtype) for a in tiles] + [jax.ShapeDtypeStruct(a.shape, F32) for a in params]
        return pl.pallas_call(
            body, name=name + "_bwd", grid=(nc, s // rows), in_specs=in_specs, out_specs=out_specs, out_shape=out_shape,
            compiler_params=_params(("arbitrary", "arbitrary")))(*args, *douts)

    @jax.custom_vjp
    def op(*args):
        return tuple(fwd_call(*args))

    def op_fwd(*args):
        return op(*args), args

    def op_bwd(args, douts):
        return tuple(bwd_call(args, douts))

    op.defvjp(op_fwd, op_bwd)
    return op


def _sigmoid(x):
    return 1.0 / (1.0 + jnp.exp(-x))


def _silu(x):
    return x * _sigmoid(x)


def _softplus(x):
    return jnp.maximum(x, 0.0) + jnp.log(1.0 + jnp.exp(-jnp.abs(x)))


def _layer_norm(h, g, b):
    mu = jnp.mean(h, axis=-1, keepdims=True)
    var = jnp.mean(jnp.square(h - mu), axis=-1, keepdims=True)
    return (h - mu) * lax.rsqrt(var + LN_EPS) * g + b


def _modulate_fn(x, sc, sh):
    return ((x * (1.0 + sc) + sh).astype(BF16),)


def _gates_fn(x, a_vec, b_vec):
    lane = lax.broadcasted_iota(jnp.int32, x.shape, 1)
    beta = _sigmoid(x)
    g = -jnp.exp(a_vec) * _softplus(x + b_vec)
    return (jnp.where(lane < DN_HEADS, beta, jnp.where(lane < 2 * DN_HEADS, g, 0.0)),)


def _gdn_post_fn(o, z, nw):
    o = o * lax.rsqrt(jnp.mean(jnp.square(o), axis=-1, keepdims=True) + RMS_EPS) * nw
    return ((o * _silu(z)).astype(BF16),)


def _mix_fn(ga, gb, ya, yb):
    return ((_sigmoid(ga) * ya + _sigmoid(gb) * yb).astype(BF16),)


def _ln1_fn(x, mixed, gt, g, b):
    return (_layer_norm(ALPHA * x + (1.0 + gt) * mixed, g, b),)


def _ln2_fn(x, f, gt, bf, g, b):
    return (_layer_norm(ALPHA * x + (1.0 + gt) * (f + bf), g, b),)


def _relu2_fn(h, b):
    return (jnp.square(jnp.maximum(h + b, 0.0)).astype(BF16),)


def _swa_block(q4, kp, kc, vp, vc, sink, first):
    rows = ATT_GROUP * WINDOW
    q = q4.reshape(rows, ATT_HD)
    k = jnp.concatenate([kp, kc], axis=0)
    v = jnp.concatenate([vp, vc], axis=0)
    s = bdot_nt(q, k) * (ATT_HD ** -0.5)
    qi = lax.broadcasted_iota(jnp.int32, (rows, 2 * WINDOW), 0) & (WINDOW - 1)
    si = lax.broadcasted_iota(jnp.int32, (rows, 2 * WINDOW), 1)
    diff = qi + WINDOW - si
    valid = (diff >= 0) & (diff < WINDOW) & (si >= jnp.where(first, WINDOW, 0))
    s = jnp.where(valid, s, -jnp.inf)
    m = lax.stop_gradient(jnp.maximum(jnp.max(s, axis=-1, keepdims=True), sink))
    p = jnp.exp(s - m)
    denom = jnp.sum(p, axis=-1, keepdims=True) + jnp.exp(sink - m)
    o = bdot_nn(p / denom, v)
    return o.reshape(ATT_GROUP, WINDOW, ATT_HD).astype(BF16)


def make_swa(name):
    def specs():
        q_spec = pl.BlockSpec((ATT_GROUP, WINDOW, ATT_HD), lambda h, n: (h, n, 0))
        cur = pl.BlockSpec((None, WINDOW, ATT_HD), lambda h, n: (h, n, 0))
        prev = pl.BlockSpec((None, WINDOW, ATT_HD), lambda h, n: (h, jnp.maximum(n - 1, 0), 0))
        sink = pl.BlockSpec((None, ATT_GROUP * WINDOW, 1), lambda h, n: (h, 0, 0))
        return q_spec, cur, prev, sink

    def fwd_call(q, k, v, sink):
        s = q.shape[1]
        q_spec, cur, prev, sink_spec = specs()

        def body(q_ref, kp_ref, kc_ref, vp_ref, vc_ref, s_ref, o_ref):
            first = pl.program_id(1) == 0
            o_ref[...] = _swa_block(q_ref[...], kp_ref[...], kc_ref[...], vp_ref[...], vc_ref[...], s_ref[...], first)

        return pl.pallas_call(
            body, name=name + "_fwd", grid=(ATT_KV, s // WINDOW), in_specs=[q_spec, prev, cur, prev, cur, sink_spec],
            out_specs=q_spec, out_shape=jax.ShapeDtypeStruct(q.shape, BF16),
            compiler_params=_params(("parallel", "parallel")))(q, k, k, v, v, sink)

    def bwd_call(q, k, v, sink, do):
        s = q.shape[1]
        q_spec, cur, prev, sink_spec = specs()

        def body(q_ref, kp_ref, kc_ref, vp_ref, vc_ref, s_ref, do_ref, dq_ref, dkp_ref, dkc_ref, dvp_ref, dvc_ref, ds_ref):
            n = pl.program_id(1)
            first = n == 0
            _, vjp = jax.vjp(lambda *a: _swa_block(*a, first), q_ref[...], kp_ref[...], kc_ref[...], vp_ref[...], vc_ref[...], s_ref[...])
            dq, dkp, dkc, dvp, dvc, ds = vjp(do_ref[...])
            dq_ref[...] = dq
            dkp_ref[...] = dkp
            dkc_ref[...] = dkc
            dvp_ref[...] = dvp
            dvc_ref[...] = dvc

            @pl.when(first)
            def _():
                ds_ref[...] = jnp.zeros_like(ds_ref)

            ds_ref[...] += ds

        kv = jax.ShapeDtypeStruct(k.shape, F32)
        return pl.pallas_call(
            body, name=name + "_bwd", grid=(ATT_KV, s // WINDOW), in_specs=[q_spec, prev, cur, prev, cur, sink_spec, q_spec],
            out_specs=[q_spec, cur, cur, cur, cur, sink_spec],
            out_shape=[jax.ShapeDtypeStruct(q.shape, F32), kv, kv, kv, kv, jax.ShapeDtypeStruct(sink.shape, F32)],
            compiler_params=_params(("parallel", "arbitrary")))(q, k, k, v, v, sink, do)

    @jax.custom_vjp
    def swa(q, k, v, sink):
        return fwd_call(q, k, v, sink)

    def fwd(q, k, v, sink):
        return swa(q, k, v, sink), (q, k, v, sink)

    def bwd(res, do):
        q, k, v, sink = res
        dq, dkp, dkc, dvp, dvc, ds = bwd_call(q, k, v, sink, do)

        def fold(cur, prev):
            return cur + jnp.concatenate([prev[:, WINDOW:], jnp.zeros_like(prev[:, :WINDOW])], axis=1)

        return dq, fold(dkc, dkp), fold(dvc, dvp), ds

    swa.defvjp(fwd, bwd)
    return swa


def _shift_down(x, d, row):
    return x if d == 0 else jnp.where(row >= d, pltpu.roll(x, d, 0), 0.0)


def _shift_up(x, d, row):
    n = x.shape[0]
    return x if d == 0 else jnp.where(row < n - d, pltpu.roll(x, n - d, 0), 0.0)


def _conv_pre(x, w, row):
    return sum(w[j:j + 1, :] * _shift_down(x, CONV_K - 1 - j, row) for j in range(CONV_K))


def _prep_post(pre, kind):
    act = _silu(pre)
    if kind == "v":
        return act
    r = lax.rsqrt(jnp.sum(jnp.square(act), axis=-1, keepdims=True) + RMS_EPS)
    return act * r * (DN_HD ** -0.5 if kind == "q" else 1.0)


def make_gdn_prep(name, kind):
    def fwd_call(x, w):
        s = x.shape[0]

        def body(x_ref, w_ref, o_ref):
            row = lax.broadcasted_iota(jnp.int32, (s, DN_HD), 0)
            o_ref[...] = _prep_post(_conv_pre(x_ref[...], w_ref[...], row), kind)

        return pl.pallas_call(
            body, name=name + "_fwd", grid=(DN_HEADS,),
            in_specs=[pl.BlockSpec((s, DN_HD), lambda j: (0, j)), pl.BlockSpec((CONV_K, DN_HD), lambda j: (0, j))],
            out_specs=pl.BlockSpec((None, s, DN_HD), lambda j: (j, 0, 0)),
            out_shape=jax.ShapeDtypeStruct((DN_HEADS, s, DN_HD), F32), compiler_params=_params(("parallel",)))(x, w)

    def bwd_call(x, w, dy):
        s = x.shape[0]

        def body(x_ref, w_ref, dy_ref, dx_ref, dw_ref):
            row = lax.broadcasted_iota(jnp.int32, (s, DN_HD), 0)
            xv, wv = x_ref[...], w_ref[...]
            _, vjp = jax.vjp(lambda p: _prep_post(p, kind), _conv_pre(xv, wv, row))
            (dpre,) = vjp(dy_ref[...])
            dx_ref[...] = sum(wv[j:j + 1, :] * _shift_up(dpre, CONV_K - 1 - j, row) for j in range(CONV_K))
            for j in range(CONV_K):
                dw_ref[j:j + 1, :] = jnp.sum(dpre * _shift_down(xv, CONV_K - 1 - j, row), axis=0, keepdims=True)

        x_spec = pl.BlockSpec((s, DN_HD), lambda j: (0, j))
        w_spec = pl.BlockSpec((CONV_K, DN_HD), lambda j: (0, j))
        return pl.pallas_call(
            body, name=name + "_bwd", grid=(DN_HEADS,),
            in_specs=[x_spec, w_spec, pl.BlockSpec((None, s, DN_HD), lambda j: (j, 0, 0))], out_specs=[x_spec, w_spec],
            out_shape=[jax.ShapeDtypeStruct(x.shape, F32), jax.ShapeDtypeStruct(w.shape, F32)],
            compiler_params=_params(("parallel",)))(x, w, dy)

    @jax.custom_vjp
    def prep(x, w):
        return fwd_call(x, w)

    def fwd(x, w):
        return prep(x, w), (x, w)

    def bwd(res, dy):
        return tuple(bwd_call(*res, dy))

    prep.defvjp(fwd, bwd)
    return prep


def _chunk_fn(q, k, v, beta, g, state):
    c = CHUNK
    r = lax.broadcasted_iota(jnp.int32, (c, c), 0)
    cc = lax.broadcasted_iota(jnp.int32, (c, c), 1)
    eye = (r == cc).astype(F32)
    causal, strict = r >= cc, r > cc
    g_row = jnp.sum(g * eye, axis=0, keepdims=True)
    gc = jnp.sum(jnp.where(causal, g_row, 0.0), axis=1, keepdims=True)
    gc_row = jnp.sum(gc * eye, axis=0, keepdims=True)
    decay = jnp.exp(jnp.where(causal, gc - gc_row, -jnp.inf))
    kb, vb = k * beta, v * beta
    m = -jnp.where(strict, bdot_nt(kb, k) * decay, 0.0)
    t, p = eye + m, m
    for _ in range(5):
        p = hdot(p, p)
        t = t + hdot(t, p)
    eg = jnp.exp(gc)
    u = hdot(t, vb)
    w = hdot(t, kb * eg)
    intra = bdot_nt(q, k) * decay
    g_last = jnp.sum(g, axis=0, keepdims=True)
    v_new = u - bdot_nn(w, state)
    o = bdot_nn(q * eg, state) + bdot_nn(intra, v_new)
    new_state = state * jnp.exp(g_last) + bdot_tn(k * jnp.exp(g_last - gc), v_new)
    return o, new_state


def make_delta(name):
    def fwd_call(qh, kh, vh, beta, g):
        s = qh.shape[1]
        n_chunks = s // CHUNK
        x_spec = pl.BlockSpec((None, CHUNK, DN_HD), lambda h, n: (h, n, 0))
        s_spec = pl.BlockSpec((None, CHUNK, 1), lambda h, n: (h, n, 0))
        st_spec = pl.BlockSpec((None, None, DN_HD, DN_HD), lambda h, n: (h, n, 0, 0))

        def body(q_ref, k_ref, v_ref, b_ref, g_ref, o_ref, st_ref, state):
            @pl.when(pl.program_id(1) == 0)
            def _():
                state[...] = jnp.zeros_like(state)

            s_in = state[...]
            st_ref[...] = s_in
            o, s_new = _chunk_fn(q_ref[...], k_ref[...], v_ref[...], b_ref[...], g_ref[...], s_in)
            o_ref[...] = o
            state[...] = s_new

        return pl.pallas_call(
            body, name=name + "_fwd", grid=(DN_HEADS, n_chunks), in_specs=[x_spec, x_spec, x_spec, s_spec, s_spec],
            out_specs=[x_spec, st_spec],
            out_shape=[jax.ShapeDtypeStruct(qh.shape, F32), jax.ShapeDtypeStruct((DN_HEADS, n_chunks, DN_HD, DN_HD), F32)],
            scratch_shapes=[pltpu.VMEM((DN_HD, DN_HD), F32)], compiler_params=_params(("parallel", "arbitrary")))(qh, kh, vh, beta, g)

    def bwd_call(qh, kh, vh, beta, g, states, do):
        s = qh.shape[1]
        n_chunks = s // CHUNK
        x_spec = pl.BlockSpec((None, CHUNK, DN_HD), lambda h, n: (h, n_chunks - 1 - n, 0))
        s_spec = pl.BlockSpec((None, CHUNK, 1), lambda h, n: (h, n_chunks - 1 - n, 0))
        st_spec = pl.BlockSpec((None, None, DN_HD, DN_HD), lambda h, n: (h, n_chunks - 1 - n, 0, 0))

        def body(q_ref, k_ref, v_ref, b_ref, g_ref, st_ref, do_ref, dq_ref, dk_ref, dv_ref, db_ref, dg_ref, dstate):
            @pl.when(pl.program_id(1) == 0)
            def _():
                dstate[...] = jnp.zeros_like(dstate)

            _, vjp = jax.vjp(_chunk_fn, q_ref[...], k_ref[...], v_ref[...], b_ref[...], g_ref[...], st_ref[...])
            dq, dk, dv, db, dg, ds = vjp((do_ref[...], dstate[...]))
            dq_ref[...] = dq
            dk_ref[...] = dk
            dv_ref[...] = dv
            db_ref[...] = db
            dg_ref[...] = dg
            dstate[...] = ds

        big = jax.ShapeDtypeStruct(qh.shape, F32)
        small = jax.ShapeDtypeStruct(beta.shape, F32)
        return pl.pallas_call(
            body, name=name + "_bwd", grid=(DN_HEADS, n_chunks),
            in_specs=[x_spec, x_spec, x_spec, s_spec, s_spec, st_spec, x_spec], out_specs=[x_spec, x_spec, x_spec, s_spec, s_spec],
            out_shape=[big, big, big, small, small], scratch_shapes=[pltpu.VMEM((DN_HD, DN_HD), F32)],
            compiler_params=_params(("parallel", "arbitrary")))(qh, kh, vh, beta, g, states, do)

    @jax.custom_vjp
    def delta(qh, kh, vh, beta, g):
        return fwd_call(qh, kh, vh, beta, g)[0]

    def fwd(qh, kh, vh, beta, g):
        o, states = fwd_call(qh, kh, vh, beta, g)
        return o, (qh, kh, vh, beta, g, states)

    def bwd(res, do):
        return tuple(bwd_call(*res, do))

    delta.defvjp(fwd, bwd)
    return delta


def loss_head(y, target):
    s, d = y.shape
    tm = min(256, s)

    def body(y_ref, t_ref, l_ref, dy_ref):
        err = y_ref[...] - t_ref[...]
        dy_ref[...] = err * (1.0 / d)

        @pl.when(pl.program_id(0) == 0)
        def _():
            l_ref[...] = jnp.zeros_like(l_ref)

        l_ref[...] += 0.5 * jnp.sum(jnp.mean(jnp.square(err), axis=-1, keepdims=True), axis=0, keepdims=True)

    spec = pl.BlockSpec((tm, d), lambda i: (i, 0))
    return pl.pallas_call(
        body, name="loss_head", grid=(s // tm,), in_specs=[spec, spec], out_specs=[pl.BlockSpec((8, LANES), lambda i: (0, 0)), spec],
        out_shape=[jax.ShapeDtypeStruct((8, LANES), F32), jax.ShapeDtypeStruct(y.shape, F32)], compiler_params=_params(("arbitrary",)))(y, target)


def ada_project(c_all, w_ada, b_cols):
    tn = 256
    cols = w_ada.shape[2]

    def body(c_ref, w_ref, b_ref, ca_ref, o_ref):
        c_act = _silu(c_ref[...])
        ca_ref[...] = c_act
        o_ref[...] = _dg(c_act, w_ref[...], 1, 0) + b_ref[...]

    return pl.pallas_call(
        body, name="ada_project", grid=(DEPTH, cols // tn),
        in_specs=[pl.BlockSpec((N_DEV, D), lambda l, j: (0, 0)), pl.BlockSpec((None, D, tn), lambda l, j: (l, 0, j)),
                  pl.BlockSpec((None, 1, tn), lambda l, j: (l, 0, j))],
        out_specs=[pl.BlockSpec((N_DEV, D), lambda l, j: (0, 0)), pl.BlockSpec((None, N_DEV, tn), lambda l, j: (l, 0, j))],
        out_shape=[jax.ShapeDtypeStruct((N_DEV, D), F32), jax.ShapeDtypeStruct((DEPTH, N_DEV, cols), F32)],
        compiler_params=_params(("arbitrary", "arbitrary")))(c_all, w_ada, b_cols.reshape(DEPTH, 1, cols))


def sum_partials(parts, name):
    _, r, c = parts.shape
    tr = _tile(r, 256, 8)

    def body(p_ref, o_ref):
        total = p_ref[0].astype(F32)
        for dev in range(1, N_DEV):
            total = total + p_ref[dev].astype(F32)
        o_ref[...] = total

    return pl.pallas_call(
        body, name=name, grid=(r // tr,), in_specs=[pl.BlockSpec((N_DEV, tr, c), lambda i: (0, i, 0))],
        out_specs=pl.BlockSpec((tr, c), lambda i: (i, 0)), out_shape=jax.ShapeDtypeStruct((r, c), F32),
        compiler_params=_params(("parallel",)))(parts)


def _adamw(w, g, m, v):
    m = ADAM_B1 * m + (1.0 - ADAM_B1) * g
    v = ADAM_B2 * v + (1.0 - ADAM_B2) * jnp.square(g)
    m_hat = m / (1.0 - ADAM_B1 ** ADAM_STEP)
    v_hat = v / (1.0 - ADAM_B2 ** ADAM_STEP)
    return -ADAM_LR * (m_hat / (jnp.sqrt(v_hat) + ADAM_EPS) + ADAM_WD * w), m, v


def adamw(w, g, m, v, name):
    l, r, c = w.shape
    tr = _tile(r, 256, 8)

    def body(w_ref, g_ref, m_ref, v_ref, d_ref, nm_ref, nv_ref):
        d_ref[...], nm_ref[...], nv_ref[...] = _adamw(w_ref[...], g_ref[...], m_ref[...], v_ref[...])

    spec = pl.BlockSpec((None, tr, c), lambda a, i: (a, i, 0))
    shape = jax.ShapeDtypeStruct(w.shape, F32)
    return pl.pallas_call(body, name=name, grid=(l, r // tr), in_specs=[spec] * 4, out_specs=[spec] * 3, out_shape=[shape] * 3,
                          compiler_params=_params(("parallel", "parallel")))(w, g, m, v)


def adamw_ada(c_act_t, dmod, w, m, v):
    l, r, c = w.shape
    tr = 256

    def body(c_ref, d_ref, w_ref, m_ref, v_ref, g_ref, dl_ref, nm_ref, nv_ref):
        g = hdot(c_ref[...], d_ref[...])
        g_ref[...] = g
        dl_ref[...], nm_ref[...], nv_ref[...] = _adamw(w_ref[...], g, m_ref[...], v_ref[...])

    spec = pl.BlockSpec((None, tr, c), lambda a, i: (a, i, 0))
    shape = jax.ShapeDtypeStruct(w.shape, F32)
    return pl.pallas_call(
        body, name="adamw_ada", grid=(l, r // tr),
        in_specs=[pl.BlockSpec((tr, LANES), lambda a, i: (i, 0)), pl.BlockSpec((None, LANES, c), lambda a, i: (a, 0, 0)), spec, spec, spec],
        out_specs=[spec] * 4, out_shape=[shape] * 4, compiler_params=_params(("parallel", "parallel")))(c_act_t, dmod, w, m, v)


def _place():
    x, y, c = lax.axis_index("x"), lax.axis_index("y"), lax.axis_index("c")
    return x, y, c


def all_gather(shard, name, in_vmem):
    r, c_dim = shard.shape
    space = pltpu.VMEM if in_vmem else pl.ANY

    def body(x_ref, out_ref, send_sems, recv_sems, local_sem):
        x, y, c = _place()
        me, sibling = (x, y, c), (x, y, 1 - c)
        chips = [(1 - x, y), (x, 1 - y), (1 - x, 1 - y)]

        def rows(px, py, pc):
            return out_ref.at[4 * px + 2 * py + pc]

        def copy(k, block, to, src=None):
            return pltpu.make_async_remote_copy(
                src_ref=rows(*block) if src is None else src, dst_ref=rows(*block), send_sem=send_sems.at[k], recv_sem=recv_sems.at[k],
                device_id=to, device_id_type=MESH_IDS)

        mine = pltpu.make_async_copy(x_ref, rows(*me), local_sem)
        mine.start()
        first = [copy(0, me, sibling, src=x_ref)] + [copy(1 + j, me, (*chip, c), src=x_ref) for j, chip in enumerate(chips)]
        for cp in first:
            cp.start()
        passed = [copy(4 + j, (*chip, c), sibling) for j, chip in enumerate(chips)]
        for j, chip in enumerate(chips):
            copy(1 + j, (*chip, c), me).wait_recv()
            passed[j].start()
        copy(0, sibling, me).wait_recv()
        for j, chip in enumerate(chips):
            copy(4 + j, (*chip, 1 - c), me).wait_recv()
        for cp in first + passed:
            cp.wait_send()
        mine.wait()

    return pl.pallas_call(
        body, name=name, out_shape=jax.ShapeDtypeStruct((N_DEV, r, c_dim), shard.dtype),
        in_specs=[pl.BlockSpec(memory_space=space)], out_specs=pl.BlockSpec(memory_space=space),
        scratch_shapes=[pltpu.SemaphoreType.DMA((7,)), pltpu.SemaphoreType.DMA((7,)), pltpu.SemaphoreType.DMA],
        compiler_params=pltpu.CompilerParams(vmem_limit_bytes=VMEM_LIMIT_BYTES))(shard)


def scatter_to_owners(slabs, name):
    def body(in_ref, out_ref, send_sems, recv_sems, local_sem):
        x, y, c = _place()
        my_slab = 4 * x + 2 * y + c
        mine = pltpu.make_async_copy(in_ref.at[my_slab], out_ref.at[my_slab], local_sem)
        mine.start()
        copies = []
        for k in range(1, N_DEV):
            px, py, pc = x ^ (k >> 2), y ^ ((k >> 1) & 1), c ^ (k & 1)
            copies.append(pltpu.make_async_remote_copy(
                src_ref=in_ref.at[4 * px + 2 * py + pc], dst_ref=out_ref.at[my_slab], send_sem=send_sems.at[k - 1],
                recv_sem=recv_sems.at[k - 1], device_id=(px, py, pc), device_id_type=MESH_IDS))
        for cp in copies:
            cp.start()
        for cp in copies:
            cp.wait_recv()
        for cp in copies:
            cp.wait_send()
        mine.wait()

    return pl.pallas_call(
        body, name=name, out_shape=jax.ShapeDtypeStruct(slabs.shape, slabs.dtype),
        in_specs=[pl.BlockSpec(memory_space=pl.ANY)], out_specs=pl.BlockSpec(memory_space=pl.ANY),
        scratch_shapes=[pltpu.SemaphoreType.DMA((7,)), pltpu.SemaphoreType.DMA((7,)), pltpu.SemaphoreType.DMA],
        compiler_params=pltpu.CompilerParams(vmem_limit_bytes=VMEM_LIMIT_BYTES))(slabs)


BIG = ("w_in", "w_oa", "w_ob", "w_out", "w_ff1", "w_ff2")
SMALL = ("a_log", "dt_bias", "sinks", "dn_norm_w", "ln1_g", "ln1_b", "b_ff1", "b_ff2", "ln2_g", "ln2_b")
PAD_ROWS = 120


def _pack_rows(arrs, width, unit):
    flat = jnp.concatenate([a.reshape(-1) for a in arrs])
    rows = -(-flat.shape[0] // (width * unit)) * unit
    return jnp.pad(flat, (0, rows * width - flat.shape[0])).reshape(rows, width)


def _split_flat(flat, like):
    out, off = [], 0
    for a in like:
        n = 1
        for dim in a.shape:
            n *= dim
        out.append(flat[off:off + n].reshape(a.shape))
        off += n
    return out


def _unpack_weights(gathered):
    def take(off, rows):
        return gathered[:, off:off + rows]

    sh_in = D * (7696 // N_DEV) * DEPTH // D
    w_in = take(0, sh_in).reshape(N_DEV, DEPTH, D, 7696 // N_DEV)
    off = sh_in
    rowsh = []
    for _ in range(3):
        rowsh.append(take(off, DEPTH * D // N_DEV).reshape(N_DEV, DEPTH, D // N_DEV, D))
        off += DEPTH * D // N_DEV
    w_ff1 = take(off, DEPTH * D_FF // N_DEV).reshape(N_DEV, DEPTH, D, D_FF // N_DEV)
    off += DEPTH * D_FF // N_DEV
    w_ff2 = take(off, DEPTH * D_FF // N_DEV).reshape(N_DEV, DEPTH, D_FF // N_DEV, D)
    layers = []
    for l in range(DEPTH):
        full_in = w_in[:, l].transpose(1, 0, 2).reshape(D, 7696)
        groups, col = [], 0
        for name, width in IN_GROUPS:
            w = full_in[:, col:col + width]
            if width < LANES:
                w = jnp.pad(w, ((0, 0), (0, LANES - width)))
            groups.append(w)
            col += width
        layers.append(dict(
            w_in=tuple(groups), w_oa=rowsh[0][:, l].reshape(D, D), w_ob=rowsh[1][:, l].reshape(D, D), w_out=rowsh[2][:, l].reshape(D, D),
            w_ff1=w_ff1[:, l].transpose(1, 0, 2).reshape(D, D_FF), w_ff2=w_ff2[:, l].reshape(D_FF, D)))
    return layers


def _pack_grads(grads):
    w_in = jnp.stack([jnp.concatenate([g[:, :width] for (name, width), g in zip(IN_GROUPS, lay["w_in"])], axis=1) for lay in grads])
    parts = [w_in.reshape(DEPTH, D, N_DEV, 7696 // N_DEV).transpose(2, 0, 1, 3).reshape(N_DEV, -1, D)]
    for name in ("w_oa", "w_ob", "w_out"):
        w = jnp.stack([lay[name] for lay in grads])
        parts.append(w.reshape(DEPTH, N_DEV, D // N_DEV, D).transpose(1, 0, 2, 3).reshape(N_DEV, -1, D))
    w = jnp.stack([lay["w_ff1"] for lay in grads])
    parts.append(w.reshape(DEPTH, D, N_DEV, D_FF // N_DEV).transpose(2, 0, 1, 3).reshape(N_DEV, -1, D))
    w = jnp.stack([lay["w_ff2"] for lay in grads])
    parts.append(w.reshape(DEPTH, N_DEV, D_FF // N_DEV, D).transpose(1, 0, 2, 3).reshape(N_DEV, -1, D))
    parts.append(jnp.zeros((N_DEV, PAD_ROWS, D), BF16))
    return jnp.concatenate(parts, axis=1)


def _layer(ops, x, mod, sm, conv_w, w):
    s = x.shape[0]
    sh1, sc1, gt1, sh2, sc2, gt2 = (mod[:, i * D:(i + 1) * D] for i in range(6))
    (u,) = ops["modulate1"](x, sc1, sh1)
    q, k, v, dq, dk, dv, z, small, ga, gb = ops["in_proj"](u, w["w_in"])
    qh = q.reshape(s, ATT_HEADS, ATT_HD).transpose(1, 0, 2)
    kh = k.reshape(s, ATT_KV, ATT_HD).transpose(1, 0, 2)
    vh = v.reshape(s, ATT_KV, ATT_HD).transpose(1, 0, 2)
    sink = jnp.broadcast_to(sm["sinks"].reshape(ATT_KV, ATT_GROUP, 1, 1), (ATT_KV, ATT_GROUP, WINDOW, 1)).reshape(ATT_KV, ATT_GROUP * WINDOW, 1)
    attn = ops["swa"](qh, kh, vh, sink).transpose(1, 0, 2).reshape(s, ATT_HEADS * ATT_HD)
    y_a = ops["mm_oa"](attn, w["w_oa"])
    qn = ops["prep_q"](dq, conv_w[:, :D])
    kn = ops["prep_k"](dk, conv_w[:, D:2 * D])
    vn = ops["prep_v"](dv, conv_w[:, 2 * D:])
    a_vec = jnp.pad(sm["a_log"], ((0, 0), (DN_HEADS, LANES - 2 * DN_HEADS)))
    b_vec = jnp.pad(sm["dt_bias"], ((0, 0), (DN_HEADS, LANES - 2 * DN_HEADS)))
    (gates,) = ops["gates"](small, a_vec, b_vec)
    beta = gates[:, :DN_HEADS].T[:, :, None]
    g = gates[:, DN_HEADS:2 * DN_HEADS].T[:, :, None]
    o = ops["delta"](qn, kn, vn, beta, g)
    (og,) = ops["gdn_post"](o, z, sm["dn_norm_w"])
    y_b = ops["mm_ob"](og, w["w_ob"])
    (mix,) = ops["mix"](ga, gb, y_a, y_b)
    mixed = ops["mm_out"](mix, w["w_out"])
    (x1,) = ops["ln1"](x, mixed, gt1, sm["ln1_g"], sm["ln1_b"])
    (u2,) = ops["modulate2"](x1, sc2, sh2)
    (h,) = ops["relu2"](ops["mm_ff1"](u2, w["w_ff1"]), sm["b_ff1"])
    f = ops["mm_ff2"](h, w["w_ff2"])
    (x2,) = ops["ln2"](x1, f, gt2, sm["b_ff2"], sm["ln2_g"], sm["ln2_b"])
    return x2


def _make_ops(l):
    t = f"l{l}_"
    return dict(
        modulate1=make_rowwise(_modulate_fn, t + "modulate1", "t", "cc", "t"),
        modulate2=make_rowwise(_modulate_fn, t + "modulate2", "t", "cc", "t"),
        in_proj=make_in_proj(t + "in_proj"), swa=make_swa(t + "swa"),
        mm_oa=make_mm(t + "mm_oa"), mm_ob=make_mm(t + "mm_ob"), mm_out=make_mm(t + "mm_out"),
        mm_ff1=make_mm(t + "mm_ff1"), mm_ff2=make_mm(t + "mm_ff2"),
        prep_q=make_gdn_prep(t + "prep_q", "q"), prep_k=make_gdn_prep(t + "prep_k", "k"), prep_v=make_gdn_prep(t + "prep_v", "v"),
        gates=make_rowwise(_gates_fn, t + "gates", "t", "cc", "t"), delta=make_delta(t + "delta"),
        gdn_post=make_rowwise(_gdn_post_fn, t + "gdn_post", "ht", "s", "t", nc=DN_HEADS),
        mix=make_rowwise(_mix_fn, t + "mix", "tttt", "", "t"),
        ln1=make_rowwise(_ln1_fn, t + "ln1", "tt", "ccc", "t"), ln2=make_rowwise(_ln2_fn, t + "ln2", "tt", "cccc", "t"),
        relu2=make_rowwise(_relu2_fn, t + "relu2", "t", "c", "t", nc=4))


def kernel(x, c, w_ada, b_ada, w_in, conv_w, a_log, dt_bias, sinks, dn_norm_w, w_oa, w_ob, w_out, ln1_g, ln1_b, w_ff1, b_ff1, w_ff2, b_ff2, ln2_g, ln2_b, loss_target, m_w_ada, m_b_ada, m_w_in, m_conv_w, m_a_log, m_dt_bias, m_sinks, m_dn_norm_w, m_w_oa, m_w_ob, m_w_out, m_ln1_g, m_ln1_b, m_w_ff1, m_b_ff1, m_w_ff2, m_b_ff2, m_ln2_g, m_ln2_b, v_w_ada, v_b_ada, v_w_in, v_conv_w, v_a_log, v_dt_bias, v_sinks, v_dn_norm_w, v_w_oa, v_w_ob, v_w_out, v_ln1_g, v_ln1_b, v_w_ff1, v_b_ff1, v_w_ff2, v_b_ff2, v_ln2_g, v_ln2_b):
    given = dict(locals())
    me = 4 * lax.axis_index("x") + 2 * lax.axis_index("y") + lax.axis_index("c")
    conv_cols = conv_w.shape[2]

    gathered = all_gather(_pack_rows([c, conv_w], LANES, 8), "gather_c_conv", True).reshape(N_DEV, -1)
    c_all = gathered[:, :D]
    conv_full = gathered[:, D:D + DEPTH * CONV_K * conv_cols].reshape(N_DEV, DEPTH, CONV_K, conv_cols).transpose(1, 2, 0, 3).reshape(DEPTH, CONV_K, -1)

    b_cols = lax.dynamic_slice_in_dim(b_ada, me * ADA_COLS, ADA_COLS, axis=1)
    c_act_all, mod_cols = ada_project(c_all, w_ada, b_cols)
    mod_all = all_gather(mod_cols.reshape(-1, LANES), "gather_mod", True).reshape(N_DEV, DEPTH, N_DEV, ADA_COLS)
    mods = lax.dynamic_index_in_dim(mod_all, me, axis=2, keepdims=False).transpose(1, 0, 2).reshape(DEPTH, 6 * D)

    shard = jnp.concatenate([given[n].astype(BF16).reshape(-1, D) for n in BIG] + [jnp.zeros((PAD_ROWS, D), BF16)])
    weights = _unpack_weights(all_gather(shard, "gather_weights", False))

    small = {n: given[n] for n in SMALL}
    ops = [_make_ops(l) for l in range(DEPTH)]

    def forward(x0, mods, small, conv_full, weights):
        h = x0
        for l in range(DEPTH):
            h = _layer(ops[l], h, mods[l:l + 1], {n: a[l:l + 1] for n, a in small.items()}, conv_full[l], weights[l])
        return h

    y, vjp = jax.vjp(forward, x[0], mods, small, conv_full, weights)
    loss_tile, dy = loss_head(y, loss_target[0])
    dx, d_mods, d_small, d_conv, d_weights = vjp(dy)
    loss = lax.psum(loss_tile[0, 0], AXES)

    received = scatter_to_owners(_pack_grads(d_weights), "scatter_grads")
    flat = sum_partials(received, "sum_big").reshape(-1)
    grad = dict(zip(BIG, _split_flat(flat, [given[n] for n in BIG])))
    delta, new_m, new_v = {}, {}, {}
    for n in BIG:
        delta[n], new_m[n], new_v[n] = adamw(given[n], grad[n], given["m_" + n], given["v_" + n], "adamw_" + n)

    partial = [d_small[n] for n in SMALL] + [d_mods, d_conv]
    parts = all_gather(_pack_rows(partial, LANES, 8), "gather_small_grads", True)
    mods_at = sum(d_small[n].size for n in SMALL)
    d_mods_all = parts.reshape(N_DEV, -1)[:, mods_at:mods_at + DEPTH * 6 * D].reshape(N_DEV, DEPTH, 6 * D)
    total = _split_flat(sum_partials(parts, "sum_small").reshape(-1), partial)
    for n, g in zip(SMALL, total):
        grad[n] = g
    grad["b_ada"] = total[len(SMALL)]
    grad["conv_w"] = lax.dynamic_slice_in_dim(total[len(SMALL) + 1], me * conv_cols, conv_cols, axis=2)
    names = SMALL + ("b_ada", "conv_w")
    packed = [_pack_rows([src[p + n] for n in names], LANES, 8)[None] for src, p in ((given, ""), (grad, ""), (given, "m_"), (given, "v_"))]
    outs = adamw(*packed, "adamw_small")
    for res, o in zip((delta, new_m, new_v), outs):
        for n, a in zip(names, _split_flat(o.reshape(-1), [given[n] for n in names])):
            res[n] = a

    dmod_mine = lax.dynamic_slice_in_dim(d_mods_all, me * ADA_COLS, ADA_COLS, axis=2).transpose(1, 0, 2)
    pad = LANES - N_DEV
    grad["w_ada"], delta["w_ada"], new_m["w_ada"], new_v["w_ada"] = adamw_ada(
        jnp.pad(c_act_all.T, ((0, 0), (0, pad))), jnp.pad(dmod_mine, ((0, 0), (0, pad), (0, 0))), w_ada, m_w_ada, v_w_ada)

    order = ("w_ada", "b_ada", "w_in", "conv_w", "a_log", "dt_bias", "sinks", "dn_norm_w", "w_oa", "w_ob", "w_out", "ln1_g", "ln1_b",
             "w_ff1", "b_ff1", "w_ff2", "b_ff2", "ln2_g", "ln2_b")
    return (loss, dx[None], *[grad[n] for n in order], *[delta[n] for n in order], *[new_m[n] for n in order], *[new_v[n] for n in order])
```

```python
import functools

import jax
import jax.numpy as jnp
from jax import lax
from jax.experimental import pallas as pl
from jax.experimental.pallas import tpu as pltpu

F32 = jnp.float32
BF16 = jnp.bfloat16

D = 1024
DEPTH = 4
N_DEV = 8
ATT_HEADS, ATT_KV, ATT_GROUP, ATT_HD, WINDOW = 16, 4, 4, 64, 128
DN_HEADS, DN_HD, CONV_K, CHUNK = 8, 128, 4, 64
D_FF = 4096
ADA_COLS = 6 * D // N_DEV
ALPHA = (2 * DEPTH) ** 0.25
LN_EPS = 1e-5
RMS_EPS = 1e-6
ADAM_LR, ADAM_B1, ADAM_B2, ADAM_EPS, ADAM_WD, ADAM_STEP = 0.001, 0.9, 0.999, 1e-08, 0.01, 10
AXES = ("x", "y", "c")
MESH_IDS = pl.DeviceIdType.MESH
VMEM_LIMIT_BYTES = 48 * 1024 * 1024
LANES = 128
HIGHEST = lax.Precision.HIGHEST

IN_GROUPS = (("q", 1024), ("k", 256), ("v", 256), ("dq", 1024), ("dk", 1024), ("dv", 1024), ("z", 1024),
             ("small", 16), ("ga", 1024), ("gb", 1024))


def _params(sem=None):
    return pltpu.CompilerParams(dimension_semantics=sem, vmem_limit_bytes=VMEM_LIMIT_BYTES)


def _tile(n, pref, unit):
    if n <= pref:
        return n
    t = (pref // unit) * unit
    while t > unit and n % t:
        t -= unit
    assert n % t == 0, (n, pref, unit)
    return t


def _dg(a, b, ca, cb):
    return lax.dot_general(a.astype(BF16), b.astype(BF16), (((ca,), (cb,)), ((), ())), preferred_element_type=F32)


@jax.custom_vjp
def bdot_nn(a, b):
    return _dg(a, b, 1, 0)


def _bdot_nn_fwd(a, b):
    return _dg(a, b, 1, 0), (a, b)


def _bdot_nn_bwd(res, g):
    a, b = res
    return _dg(g, b, 1, 1).astype(a.dtype), _dg(a, g, 0, 0).astype(b.dtype)


bdot_nn.defvjp(_bdot_nn_fwd, _bdot_nn_bwd)


@jax.custom_vjp
def bdot_nt(a, b):
    return _dg(a, b, 1, 1)


def _bdot_nt_fwd(a, b):
    return _dg(a, b, 1, 1), (a, b)


def _bdot_nt_bwd(res, g):
    a, b = res
    return _dg(g, b, 1, 0).astype(a.dtype), _dg(g, a, 0, 0).astype(b.dtype)


bdot_nt.defvjp(_bdot_nt_fwd, _bdot_nt_bwd)


@jax.custom_vjp
def bdot_tn(a, b):
    return _dg(a, b, 0, 0)


def _bdot_tn_fwd(a, b):
    return _dg(a, b, 0, 0), (a, b)


def _bdot_tn_bwd(res, g):
    a, b = res
    return _dg(b, g, 1, 1).astype(a.dtype), _dg(a, g, 1, 0).astype(b.dtype)


bdot_tn.defvjp(_bdot_tn_fwd, _bdot_tn_bwd)


def _hdg(a, b, ca, cb):
    a_hi, b_hi = a.astype(BF16), b.astype(BF16)
    a_lo, b_lo = (a - a_hi.astype(F32)).astype(BF16), (b - b_hi.astype(F32)).astype(BF16)

    def dot(x, y):
        return lax.dot_general(x, y, (((ca,), (cb,)), ((), ())), preferred_element_type=F32)

    return dot(a_hi, b_hi) + (dot(a_hi, b_lo) + dot(a_lo, b_hi))


@jax.custom_vjp
def hdot(a, b):
    return _hdg(a, b, 1, 0)


def _hdot_fwd(a, b):
    return _hdg(a, b, 1, 0), (a, b)


def _hdot_bwd(res, g):
    a, b = res
    return _hdg(g, b, 1, 1), _hdg(a, g, 0, 0)


hdot.defvjp(_hdot_fwd, _hdot_bwd)


def matmul(a, b, mode, out_dtype, name, acc=None):
    if mode == "nn":
        (m, k), (k2, n) = a.shape, b.shape
    elif mode == "nt":
        (m, k), (n, k2) = a.shape, b.shape
    else:
        (k, m), (k2, n) = a.shape, b.shape
    assert k == k2, (a.shape, b.shape, mode)
    tm, tn, tk = _tile(m, 1024, LANES), _tile(n, 512, LANES), _tile(k, 1024, LANES)
    nk = k // tk
    a_spec = pl.BlockSpec((tk, tm), lambda i, j, kk: (kk, i)) if mode == "tn" else pl.BlockSpec((tm, tk), lambda i, j, kk: (i, kk))
    b_spec = pl.BlockSpec((tn, tk), lambda i, j, kk: (j, kk)) if mode == "nt" else pl.BlockSpec((tk, tn), lambda i, j, kk: (kk, j))
    o_spec = pl.BlockSpec((tm, tn), lambda i, j, kk: (i, j))
    ca, cb = {"nn": (1, 0), "nt": (1, 1), "tn": (0, 0)}[mode]

    def body(*refs):
        if acc is None:
            a_ref, b_ref, o_ref, acc_ref = refs
        else:
            a_ref, b_ref, c_ref, o_ref, acc_ref = refs
        kk = pl.program_id(2)

        @pl.when(kk == 0)
        def _():
            acc_ref[...] = jnp.zeros_like(acc_ref) if acc is None else c_ref[...].astype(F32)

        acc_ref[...] += _dg(a_ref[...], b_ref[...], ca, cb)

        @pl.when(kk == nk - 1)
        def _():
            o_ref[...] = acc_ref[...].astype(out_dtype)

    ins, in_specs = [a, b], [a_spec, b_spec]
    if acc is not None:
        ins.append(acc)
        in_specs.append(o_spec)
    return pl.pallas_call(
        body, name=name, grid=(m // tm, n // tn, nk), in_specs=in_specs, out_specs=o_spec,
        out_shape=jax.ShapeDtypeStruct((m, n), out_dtype), scratch_shapes=[pltpu.VMEM((tm, tn), F32)],
        compiler_params=_params(("parallel", "parallel", "arbitrary")))(*ins)


def make_mm(name):
    @jax.custom_vjp
    def mm(a, w):
        return matmul(a, w, "nn", F32, name + "_fwd")

    def fwd(a, w):
        return mm(a, w), (a, w)

    def bwd(res, g):
        a, w = res
        return matmul(g, w, "nt", a.dtype, name + "_da"), matmul(a, g, "tn", w.dtype, name + "_dw")

    mm.defvjp(fwd, bwd)
    return mm


def make_in_proj(name):
    @jax.custom_vjp
    def in_proj(u, ws):
        return tuple(matmul(u, w, "nn", F32, f"{name}_fwd_{g}") for (g, _), w in zip(IN_GROUPS, ws))

    def fwd(u, ws):
        return in_proj(u, ws), (u, ws)

    def bwd(res, gs):
        u, ws = res
        du = None
        for idx, ((g, _), w, dy) in enumerate(zip(IN_GROUPS, ws, gs)):
            last = idx == len(ws) - 1
            du = matmul(dy, w, "nt", u.dtype if last else F32, f"{name}_du_{g}", acc=du)
        dws = tuple(matmul(u, dy, "tn", w.dtype, f"{name}_dw_{g}") for (g, _), w, dy in zip(IN_GROUPS, ws, gs))
        return du, dws

    in_proj.defvjp(fwd, bwd)
    return in_proj


def make_rowwise(fn, name, tile_kinds, param_kinds, out_kinds, nc=1, tm=256):
    n_t, n_p, n_o = len(tile_kinds), len(param_kinds), len(out_kinds)

    def width(a, kind):
        if kind == "h":
            return a.shape[2]
        return a.shape[1] if kind == "s" else a.shape[1] // nc

    def spec(kind, w, rows):
        if kind == "t":
            return pl.BlockSpec((rows, w), lambda j, i: (i, j))
        if kind == "h":
            return pl.BlockSpec((None, rows, w), lambda j, i: (j, i, 0))
        if kind == "c":
            return pl.BlockSpec((1, w), lambda j, i: (0, j))
        return pl.BlockSpec((1, w), lambda j, i: (0, 0))

    def full_shape(kind, w, s):
        return (s, w * nc) if kind == "t" else (nc, s, w)

    def plan(tiles, params):
        s = tiles[0].shape[0] if tile_kinds[0] == "t" else tiles[0].shape[1]
        rows = min(tm, s)
        t_w = [width(a, kd) for a, kd in zip(tiles, tile_kinds)]
        p_w = [width(a, kd) for a, kd in zip(params, param_kinds)]
        t_s = [jax.ShapeDtypeStruct((rows, w), a.dtype) for a, w in zip(tiles, t_w)]
        p_s = [jax.ShapeDtypeStruct((1, w), a.dtype) for a, w in zip(params, p_w)]
        o_s = jax.eval_shape(fn, *t_s, *p_s)
        return s, rows, t_w, p_w, o_s

    def fwd_call(*args):
        tiles, params = args[:n_t], args[n_t:]
        s, rows, t_w, p_w, o_s = plan(tiles, params)

        def body(*refs):
            ins, outs = refs[:n_t + n_p], refs[n_t + n_p:]
            res = fn(*[r[...] for r in ins])
            for o_ref, val in zip(outs, res):
                o_ref[...] = val

        in_specs = [spec(kd, w, rows) for kd, w in zip(tile_kinds, t_w)] + [spec(kd, w, rows) for kd, w in zip(param_kinds, p_w)]
        return pl.pallas_call(
            body, name=name + "_fwd", grid=(nc, s // rows), in_specs=in_specs,
            out_specs=[spec(kd, o.shape[1], rows) for kd, o in zip(out_kinds, o_s)],
            out_shape=[jax.ShapeDtypeStruct(full_shape(kd, o.shape[1], s), o.dtype) for kd, o in zip(out_kinds, o_s)],
            compiler_params=_params(("parallel", "parallel")))(*args)

    def bwd_call(args, douts):
        tiles, params = args[:n_t], args[n_t:]
        s, rows, t_w, p_w, o_s = plan(tiles, params)

        def body(*refs):
            ins = refs[:n_t + n_p]
            dos = refs[n_t + n_p:n_t + n_p + n_o]
            dts = refs[n_t + n_p + n_o:n_t + n_p + n_o + n_t]
            dps = refs[n_t + n_p + n_o + n_t:]
            j, i = pl.program_id(0), pl.program_id(1)
            _, vjp = jax.vjp(lambda *a: tuple(fn(*a)), *[r[...] for r in ins])
            grads = vjp(tuple(r[...] for r in dos))
            for r, g in zip(dts, grads[:n_t]):
                r[...] = g.astype(r.dtype)
            for r, g, kd in zip(dps, grads[n_t:], param_kinds):
                first = (i == 0) if kd == "c" else jnp.logical_and(i == 0, j == 0)

                @pl.when(first)
                def _(r=r):
                    r[...] = jnp.zeros_like(r)

                r[...] += g.astype(F32)

        in_specs = ([spec(kd, w, rows) for kd, w in zip(tile_kinds, t_w)] + [spec(kd, w, rows) for kd, w in zip(param_kinds, p_w)]
                    + [spec(kd, o.shape[1], rows) for kd, o in zip(out_kinds, o_s)])
        out_specs = [spec(kd, w, rows) for kd, w in zip(tile_kinds, t_w)] + [spec(kd, w, rows) for kd, w in zip(param_kinds, p_w)]
        out_shape = [jax.ShapeDtypeStruct(a.shape, a.dtype) for a in tiles] + [jax.ShapeDtypeStruct(a.shape, F32) for a in params]
        return pl.pallas_call(
            body, name=name + "_bwd", grid=(nc, s // rows), in_specs=in_specs, out_specs=out_specs, out_shape=out_shape,
            compiler_params=_params(("arbitrary", "arbitrary")))(*args, *douts)

    @jax.custom_vjp
    def op(*args):
        return tuple(fwd_call(*args))

    def op_fwd(*args):
        return op(*args), args

    def op_bwd(args, douts):
        return tuple(bwd_call(args, douts))

    op.defvjp(op_fwd, op_bwd)
    return op


def _sigmoid(x):
    return 1.0 / (1.0 + jnp.exp(-x))


def _silu(x):
    return x * _sigmoid(x)


def _softplus(x):
    return jnp.maximum(x, 0.0) + jnp.log(1.0 + jnp.exp(-jnp.abs(x)))


def _layer_norm(h, g, b):
    mu = jnp.mean(h, axis=-1, keepdims=True)
    var = jnp.mean(jnp.square(h - mu), axis=-1, keepdims=True)
    return (h - mu) * lax.rsqrt(var + LN_EPS) * g + b


def _modulate_fn(x, sc, sh):
    return ((x * (1.0 + sc) + sh).astype(BF16),)


def _gates_fn(x, a_vec, b_vec):
    lane = lax.broadcasted_iota(jnp.int32, x.shape, 1)
    beta = _sigmoid(x)
    g = -jnp.exp(a_vec) * _softplus(x + b_vec)
    return (jnp.where(lane < DN_HEADS, beta, jnp.where(lane < 2 * DN_HEADS, g, 0.0)),)


def _gdn_post_fn(o, z, nw):
    o = o * lax.rsqrt(jnp.mean(jnp.square(o), axis=-1, keepdims=True) + RMS_EPS) * nw
    return ((o * _silu(z)).astype(BF16),)


def _mix_fn(ga, gb, ya, yb):
    return ((_sigmoid(ga) * ya + _sigmoid(gb) * yb).astype(BF16),)


def _ln1_fn(x, mixed, gt, g, b):
    return (_layer_norm(ALPHA * x + (1.0 + gt) * mixed, g, b),)


def _ln2_fn(x, f, gt, bf, g, b):
    return (_layer_norm(ALPHA * x + (1.0 + gt) * (f + bf), g, b),)


def _relu2_fn(h, b):
    return (jnp.square(jnp.maximum(h + b, 0.0)).astype(BF16),)


def _swa_block(q4, kp, kc, vp, vc, sink, first):
    rows = ATT_GROUP * WINDOW
    q = q4.reshape(rows, ATT_HD)
    k = jnp.concatenate([kp, kc], axis=0)
    v = jnp.concatenate([vp, vc], axis=0)
    s = bdot_nt(q, k) * (ATT_HD ** -0.5)
    qi = lax.broadcasted_iota(jnp.int32, (rows, 2 * WINDOW), 0) & (WINDOW - 1)
    si = lax.broadcasted_iota(jnp.int32, (rows, 2 * WINDOW), 1)
    diff = qi + WINDOW - si
    valid = (diff >= 0) & (diff < WINDOW) & (si >= jnp.where(first, WINDOW, 0))
    s = jnp.where(valid, s, -jnp.inf)
    m = lax.stop_gradient(jnp.maximum(jnp.max(s, axis=-1, keepdims=True), sink))
    p = jnp.exp(s - m)
    denom = jnp.sum(p, axis=-1, keepdims=True) + jnp.exp(sink - m)
    o = bdot_nn(p / denom, v)
    return o.reshape(ATT_GROUP, WINDOW, ATT_HD).astype(BF16)


def make_swa(name):
    def specs():
        q_spec = pl.BlockSpec((ATT_GROUP, WINDOW, ATT_HD), lambda h, n: (h, n, 0))
        cur = pl.BlockSpec((None, WINDOW, ATT_HD), lambda h, n: (h, n, 0))
        prev = pl.BlockSpec((None, WINDOW, ATT_HD), lambda h, n: (h, jnp.maximum(n - 1, 0), 0))
        sink = pl.BlockSpec((None, ATT_GROUP * WINDOW, 1), lambda h, n: (h, 0, 0))
        return q_spec, cur, prev, sink

    def fwd_call(q, k, v, sink):
        s = q.shape[1]
        q_spec, cur, prev, sink_spec = specs()

        def body(q_ref, kp_ref, kc_ref, vp_ref, vc_ref, s_ref, o_ref):
            first = pl.program_id(1) == 0
            o_ref[...] = _swa_block(q_ref[...], kp_ref[...], kc_ref[...], vp_ref[...], vc_ref[...], s_ref[...], first)

        return pl.pallas_call(
            body, name=name + "_fwd", grid=(ATT_KV, s // WINDOW), in_specs=[q_spec, prev, cur, prev, cur, sink_spec],
            out_specs=q_spec, out_shape=jax.ShapeDtypeStruct(q.shape, BF16),
            compiler_params=_params(("parallel", "parallel")))(q, k, k, v, v, sink)

    def bwd_call(q, k, v, sink, do):
        s = q.shape[1]
        q_spec, cur, prev, sink_spec = specs()

        def body(q_ref, kp_ref, kc_ref, vp_ref, vc_ref, s_ref, do_ref, dq_ref, dkp_ref, dkc_ref, dvp_ref, dvc_ref, ds_ref):
            n = pl.program_id(1)
            first = n == 0
            _, vjp = jax.vjp(lambda *a: _swa_block(*a, first), q_ref[...], kp_ref[...], kc_ref[...], vp_ref[...], vc_ref[...], s_ref[...])
            dq, dkp, dkc, dvp, dvc, ds = vjp(do_ref[...])
            dq_ref[...] = dq
            dkp_ref[...] = dkp
            dkc_ref[...] = dkc
            dvp_ref[...] = dvp
            dvc_ref[...] = dvc

            @pl.when(first)
            def _():
                ds_ref[...] = jnp.zeros_like(ds_ref)

            ds_ref[...] += ds

        kv = jax.ShapeDtypeStruct(k.shape, F32)
        return pl.pallas_call(
            body, name=name + "_bwd", grid=(ATT_KV, s // WINDOW), in_specs=[q_spec, prev, cur, prev, cur, sink_spec, q_spec],
            out_specs=[q_spec, cur, cur, cur, cur, sink_spec],
            out_shape=[jax.ShapeDtypeStruct(q.shape, F32), kv, kv, kv, kv, jax.ShapeDtypeStruct(sink.shape, F32)],
            compiler_params=_params(("parallel", "arbitrary")))(q, k, k, v, v, sink, do)

    @jax.custom_vjp
    def swa(q, k, v, sink):
        return fwd_call(q, k, v, sink)

    def fwd(q, k, v, sink):
        return swa(q, k, v, sink), (q, k, v, sink)

    def bwd(res, do):
        q, k, v, sink = res
        dq, dkp, dkc, dvp, dvc, ds = bwd_call(q, k, v, sink, do)

        def fold(cur, prev):
            return cur + jnp.concatenate([prev[:, WINDOW:], jnp.zeros_like(prev[:, :WINDOW])], axis=1)

        return dq, fold(dkc, dkp), fold(dvc, dvp), ds

    swa.defvjp(fwd, bwd)
    return swa


def _shift_down(x, d, row):
    return x if d == 0 else jnp.where(row >= d, pltpu.roll(x, d, 0), 0.0)


def _shift_up(x, d, row):
    n = x.shape[0]
    return x if d == 0 else jnp.where(row < n - d, pltpu.roll(x, n - d, 0), 0.0)


def _conv_pre(x, w, row):
    return sum(w[j:j + 1, :] * _shift_down(x, CONV_K - 1 - j, row) for j in range(CONV_K))


def _prep_post(pre, kind):
    act = _silu(pre)
    if kind == "v":
        return act
    r = lax.rsqrt(jnp.sum(jnp.square(act), axis=-1, keepdims=True) + RMS_EPS)
    return act * r * (DN_HD ** -0.5 if kind == "q" else 1.0)


def make_gdn_prep(name, kind):
    def fwd_call(x, w):
        s = x.shape[0]

        def body(x_ref, w_ref, o_ref):
            row = lax.broadcasted_iota(jnp.int32, (s, DN_HD), 0)
            o_ref[...] = _prep_post(_conv_pre(x_ref[...], w_ref[...], row), kind)

        return pl.pallas_call(
            body, name=name + "_fwd", grid=(DN_HEADS,),
            in_specs=[pl.BlockSpec((s, DN_HD), lambda j: (0, j)), pl.BlockSpec((CONV_K, DN_HD), lambda j: (0, j))],
            out_specs=pl.BlockSpec((None, s, DN_HD), lambda j: (j, 0, 0)),
            out_shape=jax.ShapeDtypeStruct((DN_HEADS, s, DN_HD), F32), compiler_params=_params(("parallel",)))(x, w)

    def bwd_call(x, w, dy):
        s = x.shape[0]

        def body(x_ref, w_ref, dy_ref, dx_ref, dw_ref):
            row = lax.broadcasted_iota(jnp.int32, (s, DN_HD), 0)
            xv, wv = x_ref[...], w_ref[...]
            _, vjp = jax.vjp(lambda p: _prep_post(p, kind), _conv_pre(xv, wv, row))
            (dpre,) = vjp(dy_ref[...])
            dx_ref[...] = sum(wv[j:j + 1, :] * _shift_up(dpre, CONV_K - 1 - j, row) for j in range(CONV_K))
            for j in range(CONV_K):
                dw_ref[j:j + 1, :] = jnp.sum(dpre * _shift_down(xv, CONV_K - 1 - j, row), axis=0, keepdims=True)

        x_spec = pl.BlockSpec((s, DN_HD), lambda j: (0, j))
        w_spec = pl.BlockSpec((CONV_K, DN_HD), lambda j: (0, j))
        return pl.pallas_call(
            body, name=name + "_bwd", grid=(DN_HEADS,),
            in_specs=[x_spec, w_spec, pl.BlockSpec((None, s, DN_HD), lambda j: (j, 0, 0))], out_specs=[x_spec, w_spec],
            out_shape=[jax.ShapeDtypeStruct(x.shape, F32), jax.ShapeDtypeStruct(w.shape, F32)],
            compiler_params=_params(("parallel",)))(x, w, dy)

    @jax.custom_vjp
    def prep(x, w):
        return fwd_call(x, w)

    def fwd(x, w):
        return prep(x, w), (x, w)

    def bwd(res, dy):
        return tuple(bwd_call(*res, dy))

    prep.defvjp(fwd, bwd)
    return prep


@jax.custom_vjp
def _saved_inverse(m, t):
    return t


def _saved_inverse_fwd(m, t):
    return t, t


def _saved_inverse_bwd(t, dt):
    return _hdg(_hdg(t, dt, 0, 0), t, 1, 1), jnp.zeros_like(t)


_saved_inverse.defvjp(_saved_inverse_fwd, _saved_inverse_bwd)


def _each(f, *lists):
    return [f(*xs) for xs in zip(*lists)]


def _chunk_fn(q, k, v, beta, g, state, t_saved=None):
    c = CHUNK
    r = lax.broadcasted_iota(jnp.int32, (c, c), 0)
    cc = lax.broadcasted_iota(jnp.int32, (c, c), 1)
    eye = (r == cc).astype(F32)
    causal, strict = r >= cc, r > cc
    g_row = _each(lambda g: jnp.sum(g * eye, axis=0, keepdims=True), g)
    gc = _each(lambda g_row: jnp.sum(jnp.where(causal, g_row, 0.0), axis=1, keepdims=True), g_row)
    gc_row = _each(lambda gc: jnp.sum(gc * eye, axis=0, keepdims=True), gc)
    decay = _each(lambda gc, gc_row: jnp.exp(jnp.where(causal, gc - gc_row, -jnp.inf)), gc, gc_row)
    kb = _each(jnp.multiply, k, beta)
    vb = _each(jnp.multiply, v, beta)
    kk = _each(bdot_nt, kb, k)
    qk = _each(bdot_nt, q, k)
    m = _each(lambda kk, decay: -jnp.where(strict, kk * decay, 0.0), kk, decay)
    if t_saved is None:
        t, p = _each(lambda m: eye + m, m), m
        for _ in range(5):
            p = _each(hdot, p, p)
            t = _each(lambda t, p: t + hdot(t, p), t, p)
    else:
        t = _each(_saved_inverse, m, t_saved)
    eg = _each(jnp.exp, gc)
    u = _each(hdot, t, vb)
    w = _each(lambda t, kb, eg: hdot(t, kb * eg), t, kb, eg)
    ws = _each(bdot_nn, w, state)
    qs = _each(lambda q, eg, state: bdot_nn(q * eg, state), q, eg, state)
    v_new = _each(jnp.subtract, u, ws)
    o = _each(lambda qs, qk, decay, v_new: qs + bdot_nn(qk * decay, v_new), qs, qk, decay, v_new)
    g_last = _each(lambda g: jnp.sum(g, axis=0, keepdims=True), g)
    kv = _each(lambda k, g_last, gc, v_new: bdot_tn(k * jnp.exp(g_last - gc), v_new), k, g_last, gc, v_new)
    new_state = _each(lambda state, g_last, kv: state * jnp.exp(g_last) + kv, state, g_last, kv)
    return o, new_state, t


def make_delta(name, heads_per_step):
    hps = heads_per_step

    def specs(n_chunks, reverse):
        def at(n):
            return n_chunks - 1 - n if reverse else n

        x_spec = pl.BlockSpec((hps, CHUNK, DN_HD), lambda h, n: (h, at(n), 0))
        s_spec = pl.BlockSpec((hps, CHUNK, 1), lambda h, n: (h, at(n), 0))
        st_spec = pl.BlockSpec((hps, None, DN_HD, DN_HD), lambda h, n: (h, at(n), 0, 0))
        t_spec = pl.BlockSpec((hps, None, CHUNK, CHUNK), lambda h, n: (h, at(n), 0, 0))
        return x_spec, s_spec, st_spec, t_spec

    def fwd_call(qh, kh, vh, beta, g):
        n_chunks = qh.shape[1] // CHUNK
        x_spec, s_spec, st_spec, t_spec = specs(n_chunks, False)

        def body(q_ref, k_ref, v_ref, b_ref, g_ref, o_ref, st_ref, t_ref, state):
            @pl.when(pl.program_id(1) == 0)
            def _():
                state[...] = jnp.zeros_like(state)

            s_in = [state[i] for i in range(hps)]
            o, s_new, t = _chunk_fn(*[[r[i] for i in range(hps)] for r in (q_ref, k_ref, v_ref, b_ref, g_ref)], s_in)
            for i in range(hps):
                st_ref[i], o_ref[i], state[i], t_ref[i] = s_in[i], o[i], s_new[i], t[i]

        return pl.pallas_call(
            body, name=name + "_fwd", grid=(DN_HEADS // hps, n_chunks), in_specs=[x_spec, x_spec, x_spec, s_spec, s_spec],
            out_specs=[x_spec, st_spec, t_spec],
            out_shape=[jax.ShapeDtypeStruct(qh.shape, F32), jax.ShapeDtypeStruct((DN_HEADS, n_chunks, DN_HD, DN_HD), F32),
                       jax.ShapeDtypeStruct((DN_HEADS, n_chunks, CHUNK, CHUNK), F32)],
            scratch_shapes=[pltpu.VMEM((hps, DN_HD, DN_HD), F32)], compiler_params=_params(("parallel", "arbitrary")))(qh, kh, vh, beta, g)

    def bwd_call(qh, kh, vh, beta, g, states, inverses, do):
        n_chunks = qh.shape[1] // CHUNK
        x_spec, s_spec, st_spec, t_spec = specs(n_chunks, True)

        def body(q_ref, k_ref, v_ref, b_ref, g_ref, st_ref, t_ref, do_ref, dq_ref, dk_ref, dv_ref, db_ref, dg_ref, dstate):
            @pl.when(pl.program_id(1) == 0)
            def _():
                dstate[...] = jnp.zeros_like(dstate)

            t_saved = [t_ref[i] for i in range(hps)]
            _, vjp = jax.vjp(lambda *a: _chunk_fn(*a, t_saved=t_saved)[:2],
                             *[[r[i] for i in range(hps)] for r in (q_ref, k_ref, v_ref, b_ref, g_ref, st_ref)])
            grads = vjp(([do_ref[i] for i in range(hps)], [dstate[i] for i in range(hps)]))
            for ref, per_head in zip((dq_ref, dk_ref, dv_ref, db_ref, dg_ref, dstate), grads):
                for i in range(hps):
                    ref[i] = per_head[i]

        big = jax.ShapeDtypeStruct(qh.shape, F32)
        small = jax.ShapeDtypeStruct(beta.shape, F32)
        return pl.pallas_call(
            body, name=name + "_bwd", grid=(DN_HEADS // hps, n_chunks),
            in_specs=[x_spec, x_spec, x_spec, s_spec, s_spec, st_spec, t_spec, x_spec], out_specs=[x_spec, x_spec, x_spec, s_spec, s_spec],
            out_shape=[big, big, big, small, small], scratch_shapes=[pltpu.VMEM((hps, DN_HD, DN_HD), F32)],
            compiler_params=_params(("parallel", "arbitrary")))(qh, kh, vh, beta, g, states, inverses, do)

    @jax.custom_vjp
    def delta(qh, kh, vh, beta, g):
        return fwd_call(qh, kh, vh, beta, g)[0]

    def fwd(qh, kh, vh, beta, g):
        o, states, inverses = fwd_call(qh, kh, vh, beta, g)
        return o, (qh, kh, vh, beta, g, states, inverses)

    def bwd(res, do):
        return tuple(bwd_call(*res, do))

    delta.defvjp(fwd, bwd)
    return delta


def loss_head(y, target):
    s, d = y.shape
    tm = min(256, s)

    def body(y_ref, t_ref, l_ref, dy_ref):
        err = y_ref[...] - t_ref[...]
        dy_ref[...] = err * (1.0 / d)

        @pl.when(pl.program_id(0) == 0)
        def _():
            l_ref[...] = jnp.zeros_like(l_ref)

        l_ref[...] += 0.5 * jnp.sum(jnp.mean(jnp.square(err), axis=-1, keepdims=True), axis=0, keepdims=True)

    spec = pl.BlockSpec((tm, d), lambda i: (i, 0))
    return pl.pallas_call(
        body, name="loss_head", grid=(s // tm,), in_specs=[spec, spec], out_specs=[pl.BlockSpec((8, LANES), lambda i: (0, 0)), spec],
        out_shape=[jax.ShapeDtypeStruct((8, LANES), F32), jax.ShapeDtypeStruct(y.shape, F32)], compiler_params=_params(("arbitrary",)))(y, target)


def ada_project(c_all, w_ada, b_cols):
    tn = 256
    cols = w_ada.shape[2]

    def body(c_ref, w_ref, b_ref, ca_ref, o_ref):
        c_act = _silu(c_ref[...])
        ca_ref[...] = c_act
        o_ref[...] = _dg(c_act, w_ref[...], 1, 0) + b_ref[...]

    return pl.pallas_call(
        body, name="ada_project", grid=(DEPTH, cols // tn),
        in_specs=[pl.BlockSpec((N_DEV, D), lambda l, j: (0, 0)), pl.BlockSpec((None, D, tn), lambda l, j: (l, 0, j)),
                  pl.BlockSpec((None, 1, tn), lambda l, j: (l, 0, j))],
        out_specs=[pl.BlockSpec((N_DEV, D), lambda l, j: (0, 0)), pl.BlockSpec((None, N_DEV, tn), lambda l, j: (l, 0, j))],
        out_shape=[jax.ShapeDtypeStruct((N_DEV, D), F32), jax.ShapeDtypeStruct((DEPTH, N_DEV, cols), F32)],
        compiler_params=_params(("arbitrary", "arbitrary")))(c_all, w_ada, b_cols.reshape(DEPTH, 1, cols))


def sum_partials(parts, name):
    n_parts, r, c = parts.shape
    tr = _tile(r, 256, 16 if parts.dtype == BF16 else 8)

    def body(p_ref, o_ref):
        total = p_ref[0].astype(F32)
        for part in range(1, n_parts):
            total = total + p_ref[part].astype(F32)
        o_ref[...] = total

    return pl.pallas_call(
        body, name=name, grid=(r // tr,), in_specs=[pl.BlockSpec((n_parts, tr, c), lambda i: (0, i, 0))],
        out_specs=pl.BlockSpec((tr, c), lambda i: (i, 0)), out_shape=jax.ShapeDtypeStruct((r, c), F32),
        compiler_params=_params(("parallel",)))(parts)


def _adamw(w, g, m, v):
    m = ADAM_B1 * m + (1.0 - ADAM_B1) * g
    v = ADAM_B2 * v + (1.0 - ADAM_B2) * jnp.square(g)
    m_hat = m / (1.0 - ADAM_B1 ** ADAM_STEP)
    v_hat = v / (1.0 - ADAM_B2 ** ADAM_STEP)
    return -ADAM_LR * (m_hat / (jnp.sqrt(v_hat) + ADAM_EPS) + ADAM_WD * w), m, v


def adamw(w, g, m, v, name):
    l, r, c = w.shape
    tr = _tile(r, 256, 8)

    def body(w_ref, g_ref, m_ref, v_ref, d_ref, nm_ref, nv_ref):
        d_ref[...], nm_ref[...], nv_ref[...] = _adamw(w_ref[...], g_ref[...], m_ref[...], v_ref[...])

    spec = pl.BlockSpec((None, tr, c), lambda a, i: (a, i, 0))
    shape = jax.ShapeDtypeStruct(w.shape, F32)
    return pl.pallas_call(body, name=name, grid=(l, r // tr), in_specs=[spec] * 4, out_specs=[spec] * 3, out_shape=[shape] * 3,
                          compiler_params=_params(("parallel", "parallel")))(w, g, m, v)


def adamw_ada(c_act_t, dmod, w, m, v):
    l, r, c = w.shape
    tr = 256

    def body(c_ref, d_ref, w_ref, m_ref, v_ref, g_ref, dl_ref, nm_ref, nv_ref):
        g = hdot(c_ref[...], d_ref[...])
        g_ref[...] = g
        dl_ref[...], nm_ref[...], nv_ref[...] = _adamw(w_ref[...], g, m_ref[...], v_ref[...])

    spec = pl.BlockSpec((None, tr, c), lambda a, i: (a, i, 0))
    shape = jax.ShapeDtypeStruct(w.shape, F32)
    return pl.pallas_call(
        body, name="adamw_ada", grid=(l, r // tr),
        in_specs=[pl.BlockSpec((tr, LANES), lambda a, i: (i, 0)), pl.BlockSpec((None, LANES, c), lambda a, i: (a, 0, 0)), spec, spec, spec],
        out_specs=[spec] * 4, out_shape=[shape] * 4, compiler_params=_params(("parallel", "parallel")))(c_act_t, dmod, w, m, v)


def _place():
    x, y, c = lax.axis_index("x"), lax.axis_index("y"), lax.axis_index("c")
    return x, y, c


def _comm_call(body, name, ins, out_shapes, space, n_sems):
    n = len(ins)
    return pl.pallas_call(
        body, name=name, out_shape=out_shapes, in_specs=[pl.BlockSpec(memory_space=space)] * n,
        out_specs=[pl.BlockSpec(memory_space=space)] * n,
        scratch_shapes=[pltpu.SemaphoreType.DMA((n, n_sems)), pltpu.SemaphoreType.DMA((n, n_sems)), pltpu.SemaphoreType.DMA((n,))],
        compiler_params=pltpu.CompilerParams(vmem_limit_bytes=VMEM_LIMIT_BYTES))(*ins)


def all_gather(shards, name, in_vmem):
    n = len(shards)

    def body(*refs):
        x_refs, out_refs, (send_sems, recv_sems, local_sems) = refs[:n], refs[n:2 * n], refs[2 * n:]
        x, y, c = _place()
        me, sibling = (x, y, c), (x, y, 1 - c)
        chips = [(1 - x, y), (x, 1 - y), (1 - x, 1 - y)]

        def rows(a, px, py, pc):
            return out_refs[a].at[4 * px + 2 * py + pc]

        def copy(a, k, block, to, from_shard=False):
            return pltpu.make_async_remote_copy(
                src_ref=x_refs[a] if from_shard else rows(a, *block), dst_ref=rows(a, *block), send_sem=send_sems.at[a, k],
                recv_sem=recv_sems.at[a, k], device_id=to, device_id_type=MESH_IDS)

        arrays = range(n)
        mine = [pltpu.make_async_copy(x_refs[a], rows(a, *me), local_sems.at[a]) for a in arrays]
        first = [copy(a, 1 + j, me, (*chip, c), True) for j, chip in enumerate(chips) for a in arrays]
        first += [copy(a, 0, me, sibling, True) for a in arrays]
        for cp in mine + first:
            cp.start()
        passed = []
        for j, chip in enumerate(chips):
            for a in arrays:
                copy(a, 1 + j, (*chip, c), me).wait_recv()
                passed.append(copy(a, 4 + j, (*chip, c), sibling))
                passed[-1].start()
        for a in arrays:
            copy(a, 0, sibling, me).wait_recv()
            for j, chip in enumerate(chips):
                copy(a, 4 + j, (*chip, 1 - c), me).wait_recv()
        for cp in first + passed:
            cp.wait_send()
        for cp in mine:
            cp.wait()

    out_shapes = [jax.ShapeDtypeStruct((N_DEV,) + s.shape, s.dtype) for s in shards]
    return _comm_call(body, name, shards, out_shapes, pltpu.VMEM if in_vmem else pl.ANY, 7)


def scatter_to_sibling(slabs, name):
    n = len(slabs)

    def body(*refs):
        in_refs, out_refs, (send_sems, recv_sems, _) = refs[:n], refs[n:2 * n], refs[2 * n:]
        x, y, c = _place()
        copies = [pltpu.make_async_remote_copy(
            src_ref=in_refs[a].at[pl.ds(4 * (1 - c), 4)], dst_ref=out_refs[a], send_sem=send_sems.at[a, 0], recv_sem=recv_sems.at[a, 0],
            device_id=(x, y, 1 - c), device_id_type=MESH_IDS) for a in range(n)]
        for cp in copies:
            cp.start()
        for cp in copies:
            cp.wait_recv()
        for cp in copies:
            cp.wait_send()

    out_shapes = [jax.ShapeDtypeStruct((4,) + s.shape[1:], s.dtype) for s in slabs]
    return _comm_call(body, name, slabs, out_shapes, pl.ANY, 1)


def scatter_to_chips(slabs, name):
    n = len(slabs)

    def body(*refs):
        in_refs, out_refs, (send_sems, recv_sems, local_sems) = refs[:n], refs[n:2 * n], refs[2 * n:]
        x, y, c = _place()
        my_chip = 2 * x + y
        mine = [pltpu.make_async_copy(in_refs[a].at[my_chip], out_refs[a].at[my_chip], local_sems.at[a]) for a in range(n)]
        copies = [pltpu.make_async_remote_copy(
            src_ref=in_refs[a].at[2 * px + py], dst_ref=out_refs[a].at[my_chip], send_sem=send_sems.at[a, j], recv_sem=recv_sems.at[a, j],
            device_id=(px, py, c), device_id_type=MESH_IDS)
            for j, (px, py) in enumerate([(1 - x, y), (x, 1 - y), (1 - x, 1 - y)]) for a in range(n)]
        for cp in mine + copies:
            cp.start()
        for cp in copies:
            cp.wait_recv()
        for cp in copies:
            cp.wait_send()
        for cp in mine:
            cp.wait()

    return _comm_call(body, name, slabs, [jax.ShapeDtypeStruct(s.shape, s.dtype) for s in slabs], pl.ANY, 3)


def pair_sum(slabs, got, name):
    _, r, c = slabs.shape
    tr = _tile(r, 256, 16)

    def body(a_ref, b_ref, o_ref):
        o_ref[...] = (a_ref[...].astype(F32) + b_ref[...].astype(F32)).astype(BF16)

    return pl.pallas_call(
        body, name=name, grid=(4, r // tr),
        in_specs=[pl.BlockSpec((None, tr, c), lambda s, i: (4 * lax.axis_index("c") + s, i, 0)), pl.BlockSpec((None, tr, c), lambda s, i: (s, i, 0))],
        out_specs=pl.BlockSpec((None, tr, c), lambda s, i: (s, i, 0)), out_shape=jax.ShapeDtypeStruct((4, r, c), BF16),
        compiler_params=_params(("parallel", "parallel")))(slabs, got)


BIG = ("w_in", "w_oa", "w_ob", "w_out", "w_ff1", "w_ff2")
SMALL = ("a_log", "dt_bias", "sinks", "dn_norm_w", "ln1_g", "ln1_b", "b_ff1", "b_ff2", "ln2_g", "ln2_b")


def _pack_rows(arrs, width, unit):
    flat = jnp.concatenate([a.reshape(-1) for a in arrs])
    rows = -(-flat.shape[0] // (width * unit)) * unit
    return jnp.pad(flat, (0, rows * width - flat.shape[0])).reshape(rows, width)


def _split_flat(flat, like):
    out, off = [], 0
    for a in like:
        n = 1
        for dim in a.shape:
            n *= dim
        out.append(flat[off:off + n].reshape(a.shape))
        off += n
    return out


ROW_SHARDED = ("w_oa", "w_ob", "w_out", "w_ff2")


def _pack_shards(given):
    rows = jnp.concatenate([given[n].reshape(-1, D) for n in ROW_SHARDED]).astype(BF16)
    return [given["w_in"].astype(BF16).reshape(DEPTH * D, -1), rows, given["w_ff1"].astype(BF16).reshape(DEPTH * D, -1)]


def _unpack_weights(g_in, g_rows, g_ff1):
    w_in = g_in.reshape(N_DEV, DEPTH, D, -1)
    w_ff1 = g_ff1.reshape(N_DEV, DEPTH, D, -1)
    layers = []
    for l in range(DEPTH):
        full_in = w_in[:, l].transpose(1, 0, 2).reshape(D, -1)
        groups, col = [], 0
        for name, width in IN_GROUPS:
            w = full_in[:, col:col + width]
            if width < LANES:
                w = jnp.pad(w, ((0, 0), (0, LANES - width)))
            groups.append(w)
            col += width
        lay, off = dict(w_in=tuple(groups), w_ff1=w_ff1[:, l].transpose(1, 0, 2).reshape(D, D_FF)), 0
        for n in ROW_SHARDED:
            per = (D_FF if n == "w_ff2" else D) // N_DEV
            lay[n] = g_rows[:, off + l * per:off + (l + 1) * per].reshape(per * N_DEV, D)
            off += DEPTH * per
        layers.append(lay)
    return layers


def _by_core(a):
    return a.reshape((2, 2, 2) + a.shape[1:]).transpose((2, 0, 1) + tuple(range(3, a.ndim + 2))).reshape(a.shape)


def _pack_grads(grads):
    w_in = jnp.stack([jnp.concatenate([g[:, :width] for (name, width), g in zip(IN_GROUPS, lay["w_in"])], axis=1) for lay in grads])
    s_in = w_in.reshape(DEPTH, D, N_DEV, -1).transpose(2, 0, 1, 3).reshape(N_DEV, DEPTH * D, -1)
    rows = []
    for n in ROW_SHARDED:
        w = jnp.stack([lay[n] for lay in grads])
        rows.append(w.reshape(DEPTH, N_DEV, -1, D).transpose(1, 0, 2, 3).reshape(N_DEV, -1, D))
    w = jnp.stack([lay["w_ff1"] for lay in grads])
    s_ff1 = w.reshape(DEPTH, D, N_DEV, -1).transpose(2, 0, 1, 3).reshape(N_DEV, DEPTH * D, -1)
    return [_by_core(s_in), _by_core(jnp.concatenate(rows, axis=1)), _by_core(s_ff1)]


def _layer(ops, x, mod, sm, conv_w, w):
    s = x.shape[0]
    sh1, sc1, gt1, sh2, sc2, gt2 = (mod[:, i * D:(i + 1) * D] for i in range(6))
    (u,) = ops["modulate1"](x, sc1, sh1)
    q, k, v, dq, dk, dv, z, small, ga, gb = ops["in_proj"](u, w["w_in"])
    qh = q.reshape(s, ATT_HEADS, ATT_HD).transpose(1, 0, 2)
    kh = k.reshape(s, ATT_KV, ATT_HD).transpose(1, 0, 2)
    vh = v.reshape(s, ATT_KV, ATT_HD).transpose(1, 0, 2)
    sink = jnp.broadcast_to(sm["sinks"].reshape(ATT_KV, ATT_GROUP, 1, 1), (ATT_KV, ATT_GROUP, WINDOW, 1)).reshape(ATT_KV, ATT_GROUP * WINDOW, 1)
    attn = ops["swa"](qh, kh, vh, sink).transpose(1, 0, 2).reshape(s, ATT_HEADS * ATT_HD)
    y_a = ops["mm_oa"](attn, w["w_oa"])
    qn = ops["prep_q"](dq, conv_w[:, :D])
    kn = ops["prep_k"](dk, conv_w[:, D:2 * D])
    vn = ops["prep_v"](dv, conv_w[:, 2 * D:])
    a_vec = jnp.pad(sm["a_log"], ((0, 0), (DN_HEADS, LANES - 2 * DN_HEADS)))
    b_vec = jnp.pad(sm["dt_bias"], ((0, 0), (DN_HEADS, LANES - 2 * DN_HEADS)))
    (gates,) = ops["gates"](small, a_vec, b_vec)
    beta = gates[:, :DN_HEADS].T[:, :, None]
    g = gates[:, DN_HEADS:2 * DN_HEADS].T[:, :, None]
    o = ops["delta"](qn, kn, vn, beta, g)
    (og,) = ops["gdn_post"](o, z, sm["dn_norm_w"])
    y_b = ops["mm_ob"](og, w["w_ob"])
    (mix,) = ops["mix"](ga, gb, y_a, y_b)
    mixed = ops["mm_out"](mix, w["w_out"])
    (x1,) = ops["ln1"](x, mixed, gt1, sm["ln1_g"], sm["ln1_b"])
    (u2,) = ops["modulate2"](x1, sc2, sh2)
    (h,) = ops["relu2"](ops["mm_ff1"](u2, w["w_ff1"]), sm["b_ff1"])
    f = ops["mm_ff2"](h, w["w_ff2"])
    (x2,) = ops["ln2"](x1, f, gt2, sm["b_ff2"], sm["ln2_g"], sm["ln2_b"])
    return x2


def _make_ops(l):
    t = f"l{l}_"
    return dict(
        modulate1=make_rowwise(_modulate_fn, t + "modulate1", "t", "cc", "t"),
        modulate2=make_rowwise(_modulate_fn, t + "modulate2", "t", "cc", "t"),
        in_proj=make_in_proj(t + "in_proj"), swa=make_swa(t + "swa"),
        mm_oa=make_mm(t + "mm_oa"), mm_ob=make_mm(t + "mm_ob"), mm_out=make_mm(t + "mm_out"),
        mm_ff1=make_mm(t + "mm_ff1"), mm_ff2=make_mm(t + "mm_ff2"),
        prep_q=make_gdn_prep(t + "prep_q", "q"), prep_k=make_gdn_prep(t + "prep_k", "k"), prep_v=make_gdn_prep(t + "prep_v", "v"),
        gates=make_rowwise(_gates_fn, t + "gates", "t", "cc", "t"), delta=make_delta(t + "delta", 8),
        gdn_post=make_rowwise(_gdn_post_fn, t + "gdn_post", "ht", "s", "t", nc=DN_HEADS),
        mix=make_rowwise(_mix_fn, t + "mix", "tttt", "", "t"),
        ln1=make_rowwise(_ln1_fn, t + "ln1", "tt", "ccc", "t"), ln2=make_rowwise(_ln2_fn, t + "ln2", "tt", "cccc", "t"),
        relu2=make_rowwise(_relu2_fn, t + "relu2", "t", "c", "t", nc=4))


def kernel(x, c, w_ada, b_ada, w_in, conv_w, a_log, dt_bias, sinks, dn_norm_w, w_oa, w_ob, w_out, ln1_g, ln1_b, w_ff1, b_ff1, w_ff2, b_ff2, ln2_g, ln2_b, loss_target, m_w_ada, m_b_ada, m_w_in, m_conv_w, m_a_log, m_dt_bias, m_sinks, m_dn_norm_w, m_w_oa, m_w_ob, m_w_out, m_ln1_g, m_ln1_b, m_w_ff1, m_b_ff1, m_w_ff2, m_b_ff2, m_ln2_g, m_ln2_b, v_w_ada, v_b_ada, v_w_in, v_conv_w, v_a_log, v_dt_bias, v_sinks, v_dn_norm_w, v_w_oa, v_w_ob, v_w_out, v_ln1_g, v_ln1_b, v_w_ff1, v_b_ff1, v_w_ff2, v_b_ff2, v_ln2_g, v_ln2_b):
    given = dict(locals())
    me = 4 * lax.axis_index("x") + 2 * lax.axis_index("y") + lax.axis_index("c")
    conv_cols = conv_w.shape[2]

    gathered = all_gather([_pack_rows([c, conv_w], LANES, 8)], "gather_c_conv", True)[0].reshape(N_DEV, -1)
    c_all = gathered[:, :D]
    conv_full = gathered[:, D:D + DEPTH * CONV_K * conv_cols].reshape(N_DEV, DEPTH, CONV_K, conv_cols).transpose(1, 2, 0, 3).reshape(DEPTH, CONV_K, -1)

    b_cols = lax.dynamic_slice_in_dim(b_ada, me * ADA_COLS, ADA_COLS, axis=1)
    c_act_all, mod_cols = ada_project(c_all, w_ada, b_cols)
    mod_all = all_gather([mod_cols.reshape(-1, LANES)], "gather_mod", True)[0].reshape(N_DEV, DEPTH, N_DEV, ADA_COLS)
    mods = lax.dynamic_index_in_dim(mod_all, me, axis=2, keepdims=False).transpose(1, 0, 2).reshape(DEPTH, 6 * D)

    weights = _unpack_weights(*all_gather(_pack_shards(given), "gather_weights", False))

    small = {n: given[n] for n in SMALL}
    ops = [_make_ops(l) for l in range(DEPTH)]

    def forward(x0, mods, small, conv_full, weights):
        h = x0
        for l in range(DEPTH):
            h = _layer(ops[l], h, mods[l:l + 1], {n: a[l:l + 1] for n, a in small.items()}, conv_full[l], weights[l])
        return h

    y, vjp = jax.vjp(forward, x[0], mods, small, conv_full, weights)
    loss_tile, dy = loss_head(y, loss_target[0])
    dx, d_mods, d_small, d_conv, d_weights = vjp(dy)
    loss = lax.psum(loss_tile[0, 0], AXES)

    slabs = _pack_grads(d_weights)
    from_sibling = scatter_to_sibling(slabs, "scatter_sibling")
    per_chip = [pair_sum(a, b, f"pair_sum_{i}") for i, (a, b) in enumerate(zip(slabs, from_sibling))]
    g_in, g_rows, g_ff1 = [sum_partials(p, f"sum_big_{i}") for i, p in enumerate(scatter_to_chips(per_chip, "scatter_chips"))]
    grad, off = dict(w_in=g_in.reshape(w_in.shape), w_ff1=g_ff1.reshape(w_ff1.shape)), 0
    for n in ROW_SHARDED:
        rows = given[n].shape[0] * given[n].shape[1]
        grad[n] = g_rows[off:off + rows].reshape(given[n].shape)
        off += rows
    delta, new_m, new_v = {}, {}, {}
    for n in BIG:
        delta[n], new_m[n], new_v[n] = adamw(given[n], grad[n], given["m_" + n], given["v_" + n], "adamw_" + n)

    partial = [d_small[n] for n in SMALL] + [d_mods, d_conv]
    (parts,) = all_gather([_pack_rows(partial, LANES, 8)], "gather_small_grads", True)
    mods_at = sum(d_small[n].size for n in SMALL)
    d_mods_all = parts.reshape(N_DEV, -1)[:, mods_at:mods_at + DEPTH * 6 * D].reshape(N_DEV, DEPTH, 6 * D)
    total = _split_flat(sum_partials(parts, "sum_small").reshape(-1), partial)
    for n, g in zip(SMALL, total):
        grad[n] = g
    grad["b_ada"] = total[len(SMALL)]
    grad["conv_w"] = lax.dynamic_slice_in_dim(total[len(SMALL) + 1], me * conv_cols, conv_cols, axis=2)
    names = SMALL + ("b_ada", "conv_w")
    packed = [_pack_rows([src[p + n] for n in names], LANES, 8)[None] for src, p in ((given, ""), (grad, ""), (given, "m_"), (given, "v_"))]
    outs = adamw(*packed, "adamw_small")
    for res, o in zip((delta, new_m, new_v), outs):
        for n, a in zip(names, _split_flat(o.reshape(-1), [given[n] for n in names])):
            res[n] = a

    dmod_mine = lax.dynamic_slice_in_dim(d_mods_all, me * ADA_COLS, ADA_COLS, axis=2).transpose(1, 0, 2)
    pad = LANES - N_DEV
    grad["w_ada"], delta["w_ada"], new_m["w_ada"], new_v["w_ada"] = adamw_ada(
        jnp.pad(c_act_all.T, ((0, 0), (0, pad))), jnp.pad(dmod_mine, ((0, 0), (0, pad), (0, 0))), w_ada, m_w_ada, v_w_ada)

    order = ("w_ada", "b_ada", "w_in", "conv_w", "a_log", "dt_bias", "sinks", "dn_norm_w", "w_oa", "w_ob", "w_out", "ln1_g", "ln1_b",
             "w_ff1", "b_ff1", "w_ff2", "b_ff2", "ln2_g", "ln2_b")
    return (loss, dx[None], *[grad[n] for n in order], *[delta[n] for n in order], *[new_m[n] for n in order], *[new_v[n] for n in order])
```

```python
import functools

import jax
import jax.numpy as jnp
from jax import lax
from jax.experimental import pallas as pl
from jax.experimental.pallas import tpu as pltpu

F32 = jnp.float32
BF16 = jnp.bfloat16

D = 1024
DEPTH = 4
N_DEV = 8
ATT_HEADS, ATT_KV, ATT_GROUP, ATT_HD, WINDOW = 16, 4, 4, 64, 128
DN_HEADS, DN_HD, CONV_K, CHUNK = 8, 128, 4, 64
D_FF = 4096
ADA_COLS = 6 * D // N_DEV
ALPHA = (2 * DEPTH) ** 0.25
LN_EPS = 1e-5
RMS_EPS = 1e-6
ADAM_LR, ADAM_B1, ADAM_B2, ADAM_EPS, ADAM_WD, ADAM_STEP = 0.001, 0.9, 0.999, 1e-08, 0.01, 10
AXES = ("x", "y", "c")
MESH_IDS = pl.DeviceIdType.MESH
VMEM_LIMIT_BYTES = 48 * 1024 * 1024
LANES = 128
HIGHEST = lax.Precision.HIGHEST

IN_SPLITS = (("q", 1024), ("k", 256), ("v", 256), ("dq", 1024), ("dk", 1024), ("dv", 1024), ("z", 1024),
             ("small", 16), ("ga", 1024), ("gb", 1024))
IN_AT = {name: sum(w for _, w in IN_SPLITS[:i]) for i, (name, _) in enumerate(IN_SPLITS)}
IN_GROUPS = ("qkvs", "dq", "dk", "dv", "z", "ga", "gb")
QKV_W = 1536
QKVS_W = QKV_W + LANES


def _params(sem=None):
    return pltpu.CompilerParams(dimension_semantics=sem, vmem_limit_bytes=VMEM_LIMIT_BYTES)


def _tile(n, pref, unit):
    if n <= pref:
        return n
    t = (pref // unit) * unit
    while t > unit and n % t:
        t -= unit
    assert n % t == 0, (n, pref, unit)
    return t


def _dg(a, b, ca, cb):
    return lax.dot_general(a.astype(BF16), b.astype(BF16), (((ca,), (cb,)), ((), ())), preferred_element_type=F32)


@jax.custom_vjp
def bdot_nn(a, b):
    return _dg(a, b, 1, 0)


def _bdot_nn_fwd(a, b):
    return _dg(a, b, 1, 0), (a, b)


def _bdot_nn_bwd(res, g):
    a, b = res
    return _dg(g, b, 1, 1).astype(a.dtype), _dg(a, g, 0, 0).astype(b.dtype)


bdot_nn.defvjp(_bdot_nn_fwd, _bdot_nn_bwd)


@jax.custom_vjp
def bdot_nt(a, b):
    return _dg(a, b, 1, 1)


def _bdot_nt_fwd(a, b):
    return _dg(a, b, 1, 1), (a, b)


def _bdot_nt_bwd(res, g):
    a, b = res
    return _dg(g, b, 1, 0).astype(a.dtype), _dg(g, a, 0, 0).astype(b.dtype)


bdot_nt.defvjp(_bdot_nt_fwd, _bdot_nt_bwd)


@jax.custom_vjp
def bdot_tn(a, b):
    return _dg(a, b, 0, 0)


def _bdot_tn_fwd(a, b):
    return _dg(a, b, 0, 0), (a, b)


def _bdot_tn_bwd(res, g):
    a, b = res
    return _dg(b, g, 1, 1).astype(a.dtype), _dg(a, g, 1, 0).astype(b.dtype)


bdot_tn.defvjp(_bdot_tn_fwd, _bdot_tn_bwd)


def _hdg(a, b, ca, cb):
    a_hi, b_hi = a.astype(BF16), b.astype(BF16)
    a_lo, b_lo = (a - a_hi.astype(F32)).astype(BF16), (b - b_hi.astype(F32)).astype(BF16)

    def dot(x, y):
        return lax.dot_general(x, y, (((ca,), (cb,)), ((), ())), preferred_element_type=F32)

    return dot(a_hi, b_hi) + (dot(a_hi, b_lo) + dot(a_lo, b_hi))


@jax.custom_vjp
def hdot(a, b):
    return _hdg(a, b, 1, 0)


def _hdot_fwd(a, b):
    return _hdg(a, b, 1, 0), (a, b)


def _hdot_bwd(res, g):
    a, b = res
    return _hdg(g, b, 1, 1), _hdg(a, g, 0, 0)


hdot.defvjp(_hdot_fwd, _hdot_bwd)


def matmul(a, b, mode, out_dtype, name, acc=None):
    if mode == "nn":
        (m, k), (k2, n) = a.shape, b.shape
    elif mode == "nt":
        (m, k), (n, k2) = a.shape, b.shape
    else:
        (k, m), (k2, n) = a.shape, b.shape
    assert k == k2, (a.shape, b.shape, mode)
    def pick(n_, pref):
        t = _tile(n_, pref, LANES)
        return n_ if t < 2 * LANES and n_ <= 2048 else t

    tm, tn, tk = pick(m, 1024), pick(n, 512), pick(k, 1024)
    nk = k // tk
    a_spec = pl.BlockSpec((tk, tm), lambda i, j, kk: (kk, i)) if mode == "tn" else pl.BlockSpec((tm, tk), lambda i, j, kk: (i, kk))
    b_spec = pl.BlockSpec((tn, tk), lambda i, j, kk: (j, kk)) if mode == "nt" else pl.BlockSpec((tk, tn), lambda i, j, kk: (kk, j))
    o_spec = pl.BlockSpec((tm, tn), lambda i, j, kk: (i, j))
    ca, cb = {"nn": (1, 0), "nt": (1, 1), "tn": (0, 0)}[mode]

    def body(*refs):
        if acc is None:
            a_ref, b_ref, o_ref, acc_ref = refs
        else:
            a_ref, b_ref, c_ref, o_ref, acc_ref = refs
        kk = pl.program_id(2)

        @pl.when(kk == 0)
        def _():
            acc_ref[...] = jnp.zeros_like(acc_ref) if acc is None else c_ref[...].astype(F32)

        acc_ref[...] += _dg(a_ref[...], b_ref[...], ca, cb)

        @pl.when(kk == nk - 1)
        def _():
            o_ref[...] = acc_ref[...].astype(out_dtype)

    ins, in_specs = [a, b], [a_spec, b_spec]
    if acc is not None:
        ins.append(acc)
        in_specs.append(o_spec)
    return pl.pallas_call(
        body, name=name, grid=(m // tm, n // tn, nk), in_specs=in_specs, out_specs=o_spec,
        out_shape=jax.ShapeDtypeStruct((m, n), out_dtype), scratch_shapes=[pltpu.VMEM((tm, tn), F32)],
        compiler_params=_params(("parallel", "parallel", "arbitrary")))(*ins)


def make_mm(name):
    @jax.custom_vjp
    def mm(a, w):
        return matmul(a, w, "nn", F32, name + "_fwd")

    def fwd(a, w):
        return mm(a, w), (a, w)

    def bwd(res, g):
        a, w = res
        return matmul(g, w, "nt", a.dtype, name + "_da"), matmul(a, g, "tn", w.dtype, name + "_dw")

    mm.defvjp(fwd, bwd)
    return mm


def make_in_proj(name):
    @jax.custom_vjp
    def in_proj(u, ws):
        return tuple(matmul(u, w, "nn", F32, f"{name}_fwd_{g}") for g, w in zip(IN_GROUPS, ws))

    def fwd(u, ws):
        return in_proj(u, ws), (u, ws)

    def bwd(res, gs):
        u, ws = res
        du = None
        for idx, (g, w, dy) in enumerate(zip(IN_GROUPS, ws, gs)):
            last = idx == len(ws) - 1
            du = matmul(dy, w, "nt", u.dtype if last else F32, f"{name}_du_{g}", acc=du)
        dws = tuple(matmul(u, dy, "tn", w.dtype, f"{name}_dw_{g}") for g, w, dy in zip(IN_GROUPS, ws, gs))
        return du, dws

    in_proj.defvjp(fwd, bwd)
    return in_proj


def make_rowwise(fn, name, tile_kinds, param_kinds, out_kinds, nc=1, tm=256):
    n_t, n_p, n_o = len(tile_kinds), len(param_kinds), len(out_kinds)

    def width(a, kind):
        if kind == "h":
            return a.shape[2]
        return a.shape[1] if kind == "s" else a.shape[1] // nc

    def spec(kind, w, rows):
        if kind == "t":
            return pl.BlockSpec((rows, w), lambda j, i: (i, j))
        if kind == "h":
            return pl.BlockSpec((None, rows, w), lambda j, i: (j, i, 0))
        if kind == "c":
            return pl.BlockSpec((1, w), lambda j, i: (0, j))
        return pl.BlockSpec((1, w), lambda j, i: (0, 0))

    def full_shape(kind, w, s):
        return (s, w * nc) if kind == "t" else (nc, s, w)

    def plan(tiles, params):
        s = tiles[0].shape[0] if tile_kinds[0] == "t" else tiles[0].shape[1]
        rows = min(tm, s)
        t_w = [width(a, kd) for a, kd in zip(tiles, tile_kinds)]
        p_w = [width(a, kd) for a, kd in zip(params, param_kinds)]
        t_s = [jax.ShapeDtypeStruct((rows, w), a.dtype) for a, w in zip(tiles, t_w)]
        p_s = [jax.ShapeDtypeStruct((1, w), a.dtype) for a, w in zip(params, p_w)]
        o_s = jax.eval_shape(fn, *t_s, *p_s)
        return s, rows, t_w, p_w, o_s

    def fwd_call(*args):
        tiles, params = args[:n_t], args[n_t:]
        s, rows, t_w, p_w, o_s = plan(tiles, params)

        def body(*refs):
            ins, outs = refs[:n_t + n_p], refs[n_t + n_p:]
            res = fn(*[r[...] for r in ins])
            for o_ref, val in zip(outs, res):
                o_ref[...] = val

        in_specs = [spec(kd, w, rows) for kd, w in zip(tile_kinds, t_w)] + [spec(kd, w, rows) for kd, w in zip(param_kinds, p_w)]
        return pl.pallas_call(
            body, name=name + "_fwd", grid=(nc, s // rows), in_specs=in_specs,
            out_specs=[spec(kd, o.shape[1], rows) for kd, o in zip(out_kinds, o_s)],
            out_shape=[jax.ShapeDtypeStruct(full_shape(kd, o.shape[1], s), o.dtype) for kd, o in zip(out_kinds, o_s)],
            compiler_params=_params(("parallel", "parallel")))(*args)

    def bwd_call(args, douts):
        tiles, params = args[:n_t], args[n_t:]
        s, rows, t_w, p_w, o_s = plan(tiles, params)

        def body(*refs):
            ins = refs[:n_t + n_p]
            dos = refs[n_t + n_p:n_t + n_p + n_o]
            dts = refs[n_t + n_p + n_o:n_t + n_p + n_o + n_t]
            dps = refs[n_t + n_p + n_o + n_t:]
            j, i = pl.program_id(0), pl.program_id(1)
            _, vjp = jax.vjp(lambda *a: tuple(fn(*a)), *[r[...] for r in ins])
            grads = vjp(tuple(r[...] for r in dos))
            for r, g in zip(dts, grads[:n_t]):
                r[...] = g.astype(r.dtype)
            for r, g, kd in zip(dps, grads[n_t:], param_kinds):
                first = (i == 0) if kd == "c" else jnp.logical_and(i == 0, j == 0)

                @pl.when(first)
                def _(r=r):
                    r[...] = jnp.zeros_like(r)

                r[...] += g.astype(F32)

        in_specs = ([spec(kd, w, rows) for kd, w in zip(tile_kinds, t_w)] + [spec(kd, w, rows) for kd, w in zip(param_kinds, p_w)]
                    + [spec(kd, o.shape[1], rows) for kd, o in zip(out_kinds, o_s)])
        out_specs = [spec(kd, w, rows) for kd, w in zip(tile_kinds, t_w)] + [spec(kd, w, rows) for kd, w in zip(param_kinds, p_w)]
        out_shape = [jax.ShapeDtypeStruct(a.shape, a.dtype) for a in tiles] + [jax.ShapeDtypeStruct(a.shape, F32) for a in params]
        return pl.pallas_call(
            body, name=name + "_bwd", grid=(nc, s // rows), in_specs=in_specs, out_specs=out_specs, out_shape=out_shape,
            compiler_params=_params(("arbitrary", "arbitrary")))(*args, *douts)

    @jax.custom_vjp
    def op(*args):
        return tuple(fwd_call(*args))

    def op_fwd(*args):
        return op(*args), args

    def op_bwd(args, douts):
        return tuple(bwd_call(args, douts))

    op.defvjp(op_fwd, op_bwd)
    return op


def _sigmoid(x):
    return 1.0 / (1.0 + jnp.exp(-x))


def _silu(x):
    return x * _sigmoid(x)


def _softplus(x):
    return jnp.maximum(x, 0.0) + jnp.log(1.0 + jnp.exp(-jnp.abs(x)))


def _layer_norm(h, g, b):
    mu = jnp.mean(h, axis=-1, keepdims=True)
    var = jnp.mean(jnp.square(h - mu), axis=-1, keepdims=True)
    return (h - mu) * lax.rsqrt(var + LN_EPS) * g + b


def _modulate_fn(x, sc, sh):
    return ((x * (1.0 + sc) + sh).astype(BF16),)


def _gates_fn(x, a_vec, b_vec):
    lane = lax.broadcasted_iota(jnp.int32, x.shape, 1)
    beta = _sigmoid(x)
    g = -jnp.exp(a_vec) * _softplus(x + b_vec)
    return (jnp.where(lane < DN_HEADS, beta, jnp.where(lane < 2 * DN_HEADS, g, 0.0)),)


def _gdn_post_fn(o, z, nw):
    o = o * lax.rsqrt(jnp.mean(jnp.square(o), axis=-1, keepdims=True) + RMS_EPS) * nw
    return ((o * _silu(z)).astype(BF16),)


def _mix_fn(ga, gb, ya, yb):
    return ((_sigmoid(ga) * ya + _sigmoid(gb) * yb).astype(BF16),)


def _ln1_fn(x, mixed, gt, g, b, sc, sh):
    y = _layer_norm(ALPHA * x + (1.0 + gt) * mixed, g, b)
    return y, _modulate_fn(y, sc, sh)[0]


def _ln2_last_fn(x, f, gt, bf, g, b):
    return (_layer_norm(ALPHA * x + (1.0 + gt) * (f + bf), g, b),)


def _ln2_fn(x, f, gt, bf, g, b, sc, sh):
    (y,) = _ln2_last_fn(x, f, gt, bf, g, b)
    return y, _modulate_fn(y, sc, sh)[0]


def _relu2_fn(h, b):
    return (jnp.square(jnp.maximum(h + b, 0.0)).astype(BF16),)


def _each(f, *lists):
    return [f(*xs) for xs in zip(*lists)]


def _swa_blocks(q4, kp, kc, vp, vc, sink, first):
    rows = ATT_GROUP * WINDOW
    qi = lax.broadcasted_iota(jnp.int32, (rows, 2 * WINDOW), 0) & (WINDOW - 1)
    si = lax.broadcasted_iota(jnp.int32, (rows, 2 * WINDOW), 1)
    diff = qi + WINDOW - si
    valid = (diff >= 0) & (diff < WINDOW) & (si >= jnp.where(first, WINDOW, 0))
    q = _each(lambda a: a.reshape(rows, ATT_HD), q4)
    k = _each(lambda a, b: jnp.concatenate([a, b], axis=0), kp, kc)
    v = _each(lambda a, b: jnp.concatenate([a, b], axis=0), vp, vc)
    s = _each(lambda q, k: jnp.where(valid, bdot_nt(q, k) * (ATT_HD ** -0.5), -jnp.inf), q, k)
    m = _each(lambda s, sink: lax.stop_gradient(jnp.maximum(jnp.max(s, axis=-1, keepdims=True), sink)), s, sink)
    p = _each(lambda s, m: jnp.exp(s - m), s, m)
    pn = _each(lambda p, sink, m: p / (jnp.sum(p, axis=-1, keepdims=True) + jnp.exp(sink - m)), p, sink, m)
    return _each(lambda pn, v: bdot_nn(pn, v).reshape(ATT_GROUP, WINDOW, ATT_HD).astype(BF16), pn, v)


def make_swa(name):
    heads = range(ATT_KV)

    def specs():
        q_spec = pl.BlockSpec((ATT_HEADS, WINDOW, ATT_HD), lambda n: (0, n, 0))
        cur = pl.BlockSpec((ATT_KV, WINDOW, ATT_HD), lambda n: (0, n, 0))
        prev = pl.BlockSpec((ATT_KV, WINDOW, ATT_HD), lambda n: (0, jnp.maximum(n - 1, 0), 0))
        sink = pl.BlockSpec((ATT_KV, ATT_GROUP * WINDOW, 1), lambda n: (0, 0, 0))
        return q_spec, cur, prev, sink

    def group(ref, h):
        return ref.at[pl.ds(h * ATT_GROUP, ATT_GROUP)]

    def load(q_ref, kp_ref, kc_ref, vp_ref, vc_ref, s_ref):
        return [[group(q_ref, h)[...] for h in heads]] + [[r[h] for h in heads] for r in (kp_ref, kc_ref, vp_ref, vc_ref, s_ref)]

    def fwd_call(q, k, v, sink):
        s = q.shape[1]
        q_spec, cur, prev, sink_spec = specs()

        def body(q_ref, kp_ref, kc_ref, vp_ref, vc_ref, s_ref, o_ref):
            o = _swa_blocks(*load(q_ref, kp_ref, kc_ref, vp_ref, vc_ref, s_ref), pl.program_id(0) == 0)
            for h in heads:
                group(o_ref, h)[...] = o[h]

        return pl.pallas_call(
            body, name=name + "_fwd", grid=(s // WINDOW,), in_specs=[q_spec, prev, cur, prev, cur, sink_spec],
            out_specs=q_spec, out_shape=jax.ShapeDtypeStruct(q.shape, BF16),
            compiler_params=_params(("parallel",)))(q, k, k, v, v, sink)

    def bwd_call(q, k, v, sink, do):
        s = q.shape[1]
        q_spec, cur, prev, sink_spec = specs()

        def body(q_ref, kp_ref, kc_ref, vp_ref, vc_ref, s_ref, do_ref, dq_ref, dkp_ref, dkc_ref, dvp_ref, dvc_ref, ds_ref):
            first = pl.program_id(0) == 0
            _, vjp = jax.vjp(lambda *a: _swa_blocks(*a, first), *load(q_ref, kp_ref, kc_ref, vp_ref, vc_ref, s_ref))
            dq, dkp, dkc, dvp, dvc, ds = vjp([group(do_ref, h)[...] for h in heads])

            @pl.when(first)
            def _():
                ds_ref[...] = jnp.zeros_like(ds_ref)

            for h in heads:
                group(dq_ref, h)[...] = dq[h]
                dkp_ref[h], dkc_ref[h], dvp_ref[h], dvc_ref[h] = dkp[h], dkc[h], dvp[h], dvc[h]
                ds_ref[h] += ds[h]

        kv = jax.ShapeDtypeStruct(k.shape, F32)
        return pl.pallas_call(
            body, name=name + "_bwd", grid=(s // WINDOW,), in_specs=[q_spec, prev, cur, prev, cur, sink_spec, q_spec],
            out_specs=[q_spec, cur, cur, cur, cur, sink_spec],
            out_shape=[jax.ShapeDtypeStruct(q.shape, F32), kv, kv, kv, kv, jax.ShapeDtypeStruct(sink.shape, F32)],
            compiler_params=_params(("arbitrary",)))(q, k, k, v, v, sink, do)

    @jax.custom_vjp
    def swa(q, k, v, sink):
        return fwd_call(q, k, v, sink)

    def fwd(q, k, v, sink):
        return swa(q, k, v, sink), (q, k, v, sink)

    def bwd(res, do):
        q, k, v, sink = res
        dq, dkp, dkc, dvp, dvc, ds = bwd_call(q, k, v, sink, do)

        def fold(cur, prev):
            return cur + jnp.concatenate([prev[:, WINDOW:], jnp.zeros_like(prev[:, :WINDOW])], axis=1)

        return dq, fold(dkc, dkp), fold(dvc, dvp), ds

    swa.defvjp(fwd, bwd)
    return swa


def _shift_down(x, d, row):
    return x if d == 0 else jnp.where(row >= d, pltpu.roll(x, d, 0), 0.0)


def _shift_up(x, d, row):
    n = x.shape[0]
    return x if d == 0 else jnp.where(row < n - d, pltpu.roll(x, n - d, 0), 0.0)


def _conv_pre(x, w, row):
    return sum(w[j:j + 1, :] * _shift_down(x, CONV_K - 1 - j, row) for j in range(CONV_K))


def _prep_post(pre, kind):
    act = _silu(pre)
    if kind == "v":
        return act
    r = lax.rsqrt(jnp.sum(jnp.square(act), axis=-1, keepdims=True) + RMS_EPS)
    return act * r * (DN_HD ** -0.5 if kind == "q" else 1.0)


def make_gdn_prep(name, kind):
    def fwd_call(x, w):
        s = x.shape[0]

        def body(x_ref, w_ref, o_ref):
            row = lax.broadcasted_iota(jnp.int32, (s, DN_HD), 0)
            o_ref[...] = _prep_post(_conv_pre(x_ref[...], w_ref[...], row), kind)

        return pl.pallas_call(
            body, name=name + "_fwd", grid=(DN_HEADS,),
            in_specs=[pl.BlockSpec((s, DN_HD), lambda j: (0, j)), pl.BlockSpec((CONV_K, DN_HD), lambda j: (0, j))],
            out_specs=pl.BlockSpec((None, s, DN_HD), lambda j: (j, 0, 0)),
            out_shape=jax.ShapeDtypeStruct((DN_HEADS, s, DN_HD), F32), compiler_params=_params(("parallel",)))(x, w)

    def bwd_call(x, w, dy):
        s = x.shape[0]

        def body(x_ref, w_ref, dy_ref, dx_ref, dw_ref):
            row = lax.broadcasted_iota(jnp.int32, (s, DN_HD), 0)
            xv, wv = x_ref[...], w_ref[...]
            _, vjp = jax.vjp(lambda p: _prep_post(p, kind), _conv_pre(xv, wv, row))
            (dpre,) = vjp(dy_ref[...])
            dx_ref[...] = sum(wv[j:j + 1, :] * _shift_up(dpre, CONV_K - 1 - j, row) for j in range(CONV_K))
            for j in range(CONV_K):
                dw_ref[j:j + 1, :] = jnp.sum(dpre * _shift_down(xv, CONV_K - 1 - j, row), axis=0, keepdims=True)

        x_spec = pl.BlockSpec((s, DN_HD), lambda j: (0, j))
        w_spec = pl.BlockSpec((CONV_K, DN_HD), lambda j: (0, j))
        return pl.pallas_call(
            body, name=name + "_bwd", grid=(DN_HEADS,),
            in_specs=[x_spec, w_spec, pl.BlockSpec((None, s, DN_HD), lambda j: (j, 0, 0))], out_specs=[x_spec, w_spec],
            out_shape=[jax.ShapeDtypeStruct(x.shape, F32), jax.ShapeDtypeStruct(w.shape, F32)],
            compiler_params=_params(("parallel",)))(x, w, dy)

    @jax.custom_vjp
    def prep(x, w):
        return fwd_call(x, w)

    def fwd(x, w):
        return prep(x, w), (x, w)

    def bwd(res, dy):
        return tuple(bwd_call(*res, dy))

    prep.defvjp(fwd, bwd)
    return prep


@jax.custom_vjp
def _saved_inverse(m, t):
    return t


def _saved_inverse_fwd(m, t):
    return t, t


def _saved_inverse_bwd(t, dt):
    return _hdg(_hdg(t, dt, 0, 0), t, 1, 1), jnp.zeros_like(t)


_saved_inverse.defvjp(_saved_inverse_fwd, _saved_inverse_bwd)


def _chunk_fn(q, k, v, beta, g, state, t_saved=None):
    c = CHUNK
    r = lax.broadcasted_iota(jnp.int32, (c, c), 0)
    cc = lax.broadcasted_iota(jnp.int32, (c, c), 1)
    eye = (r == cc).astype(F32)
    causal, strict = r >= cc, r > cc
    g_row = _each(lambda g: jnp.sum(g * eye, axis=0, keepdims=True), g)
    gc = _each(lambda g_row: jnp.sum(jnp.where(causal, g_row, 0.0), axis=1, keepdims=True), g_row)
    gc_row = _each(lambda gc: jnp.sum(gc * eye, axis=0, keepdims=True), gc)
    decay = _each(lambda gc, gc_row: jnp.exp(jnp.where(causal, gc - gc_row, -jnp.inf)), gc, gc_row)
    kb = _each(jnp.multiply, k, beta)
    vb = _each(jnp.multiply, v, beta)
    kk = _each(bdot_nt, kb, k)
    qk = _each(bdot_nt, q, k)
    m = _each(lambda kk, decay: -jnp.where(strict, kk * decay, 0.0), kk, decay)
    if t_saved is None:
        t, p = _each(lambda m: eye + m, m), m
        for _ in range(5):
            p = _each(hdot, p, p)
            t = _each(lambda t, p: t + hdot(t, p), t, p)
    else:
        t = _each(_saved_inverse, m, t_saved)
    eg = _each(jnp.exp, gc)
    u = _each(hdot, t, vb)
    w = _each(lambda t, kb, eg: hdot(t, kb * eg), t, kb, eg)
    ws = _each(bdot_nn, w, state)
    qs = _each(lambda q, eg, state: bdot_nn(q * eg, state), q, eg, state)
    v_new = _each(jnp.subtract, u, ws)
    o = _each(lambda qs, qk, decay, v_new: qs + bdot_nn(qk * decay, v_new), qs, qk, decay, v_new)
    g_last = _each(lambda g: jnp.sum(g, axis=0, keepdims=True), g)
    kv = _each(lambda k, g_last, gc, v_new: bdot_tn(k * jnp.exp(g_last - gc), v_new), k, g_last, gc, v_new)
    new_state = _each(lambda state, g_last, kv: state * jnp.exp(g_last) + kv, state, g_last, kv)
    return o, new_state, t


def make_delta(name, heads_per_step):
    hps = heads_per_step

    def specs(n_chunks, reverse):
        def at(n):
            return n_chunks - 1 - n if reverse else n

        x_spec = pl.BlockSpec((hps, CHUNK, DN_HD), lambda h, n: (h, at(n), 0))
        s_spec = pl.BlockSpec((hps, CHUNK, 1), lambda h, n: (h, at(n), 0))
        st_spec = pl.BlockSpec((hps, None, DN_HD, DN_HD), lambda h, n: (h, at(n), 0, 0))
        t_spec = pl.BlockSpec((hps, None, CHUNK, CHUNK), lambda h, n: (h, at(n), 0, 0))
        return x_spec, s_spec, st_spec, t_spec

    def fwd_call(qh, kh, vh, beta, g):
        n_chunks = qh.shape[1] // CHUNK
        x_spec, s_spec, st_spec, t_spec = specs(n_chunks, False)

        def body(q_ref, k_ref, v_ref, b_ref, g_ref, o_ref, st_ref, t_ref, state):
            @pl.when(pl.program_id(1) == 0)
            def _():
                state[...] = jnp.zeros_like(state)

            s_in = [state[i] for i in range(hps)]
            o, s_new, t = _chunk_fn(*[[r[i] for i in range(hps)] for r in (q_ref, k_ref, v_ref, b_ref, g_ref)], s_in)
            for i in range(hps):
                st_ref[i], o_ref[i], state[i], t_ref[i] = s_in[i], o[i], s_new[i], t[i]

        return pl.pallas_call(
            body, name=name + "_fwd", grid=(DN_HEADS // hps, n_chunks), in_specs=[x_spec, x_spec, x_spec, s_spec, s_spec],
            out_specs=[x_spec, st_spec, t_spec],
            out_shape=[jax.ShapeDtypeStruct(qh.shape, F32), jax.ShapeDtypeStruct((DN_HEADS, n_chunks, DN_HD, DN_HD), F32),
                       jax.ShapeDtypeStruct((DN_HEADS, n_chunks, CHUNK, CHUNK), F32)],
            scratch_shapes=[pltpu.VMEM((hps, DN_HD, DN_HD), F32)], compiler_params=_params(("parallel", "arbitrary")))(qh, kh, vh, beta, g)

    def bwd_call(qh, kh, vh, beta, g, states, inverses, do):
        n_chunks = qh.shape[1] // CHUNK
        x_spec, s_spec, st_spec, t_spec = specs(n_chunks, True)

        def body(q_ref, k_ref, v_ref, b_ref, g_ref, st_ref, t_ref, do_ref, dq_ref, dk_ref, dv_ref, db_ref, dg_ref, dstate):
            @pl.when(pl.program_id(1) == 0)
            def _():
                dstate[...] = jnp.zeros_like(dstate)

            t_saved = [t_ref[i] for i in range(hps)]
            _, vjp = jax.vjp(lambda *a: _chunk_fn(*a, t_saved=t_saved)[:2],
                             *[[r[i] for i in range(hps)] for r in (q_ref, k_ref, v_ref, b_ref, g_ref, st_ref)])
            grads = vjp(([do_ref[i] for i in range(hps)], [dstate[i] for i in range(hps)]))
            for ref, per_head in zip((dq_ref, dk_ref, dv_ref, db_ref, dg_ref, dstate), grads):
                for i in range(hps):
                    ref[i] = per_head[i]

        big = jax.ShapeDtypeStruct(qh.shape, F32)
        small = jax.ShapeDtypeStruct(beta.shape, F32)
        return pl.pallas_call(
            body, name=name + "_bwd", grid=(DN_HEADS // hps, n_chunks),
            in_specs=[x_spec, x_spec, x_spec, s_spec, s_spec, st_spec, t_spec, x_spec], out_specs=[x_spec, x_spec, x_spec, s_spec, s_spec],
            out_shape=[big, big, big, small, small], scratch_shapes=[pltpu.VMEM((hps, DN_HD, DN_HD), F32)],
            compiler_params=_params(("parallel", "arbitrary")))(qh, kh, vh, beta, g, states, inverses, do)

    @jax.custom_vjp
    def delta(qh, kh, vh, beta, g):
        return fwd_call(qh, kh, vh, beta, g)[0]

    def fwd(qh, kh, vh, beta, g):
        o, states, inverses = fwd_call(qh, kh, vh, beta, g)
        return o, (qh, kh, vh, beta, g, states, inverses)

    def bwd(res, do):
        return tuple(bwd_call(*res, do))

    delta.defvjp(fwd, bwd)
    return delta


def loss_head(y, target):
    s, d = y.shape
    tm = min(256, s)

    def body(y_ref, t_ref, l_ref, dy_ref):
        err = y_ref[...] - t_ref[...]
        dy_ref[...] = err * (1.0 / d)

        @pl.when(pl.program_id(0) == 0)
        def _():
            l_ref[...] = jnp.zeros_like(l_ref)

        l_ref[...] += 0.5 * jnp.sum(jnp.mean(jnp.square(err), axis=-1, keepdims=True), axis=0, keepdims=True)

    spec = pl.BlockSpec((tm, d), lambda i: (i, 0))
    return pl.pallas_call(
        body, name="loss_head", grid=(s // tm,), in_specs=[spec, spec], out_specs=[pl.BlockSpec((8, LANES), lambda i: (0, 0)), spec],
        out_shape=[jax.ShapeDtypeStruct((8, LANES), F32), jax.ShapeDtypeStruct(y.shape, F32)], compiler_params=_params(("arbitrary",)))(y, target)


def ada_project(c_all, w_ada, b_cols):
    tn = 256
    cols = w_ada.shape[2]

    def body(c_ref, w_ref, b_ref, ca_ref, o_ref):
        c_act = _silu(c_ref[...])
        ca_ref[...] = c_act
        o_ref[...] = _dg(c_act, w_ref[...], 1, 0) + b_ref[...]

    return pl.pallas_call(
        body, name="ada_project", grid=(DEPTH, cols // tn),
        in_specs=[pl.BlockSpec((N_DEV, D), lambda l, j: (0, 0)), pl.BlockSpec((None, D, tn), lambda l, j: (l, 0, j)),
                  pl.BlockSpec((None, 1, tn), lambda l, j: (l, 0, j))],
        out_specs=[pl.BlockSpec((N_DEV, D), lambda l, j: (0, 0)), pl.BlockSpec((None, N_DEV, tn), lambda l, j: (l, 0, j))],
        out_shape=[jax.ShapeDtypeStruct((N_DEV, D), F32), jax.ShapeDtypeStruct((DEPTH, N_DEV, cols), F32)],
        compiler_params=_params(("arbitrary", "arbitrary")))(c_all, w_ada, b_cols.reshape(DEPTH, 1, cols))


def sum_partials(parts, name):
    n_parts, r, c = parts.shape
    tr = _tile(r, 256, 16 if parts.dtype == BF16 else 8)

    def body(p_ref, o_ref):
        total = p_ref[0].astype(F32)
        for part in range(1, n_parts):
            total = total + p_ref[part].astype(F32)
        o_ref[...] = total

    return pl.pallas_call(
        body, name=name, grid=(r // tr,), in_specs=[pl.BlockSpec((n_parts, tr, c), lambda i: (0, i, 0))],
        out_specs=pl.BlockSpec((tr, c), lambda i: (i, 0)), out_shape=jax.ShapeDtypeStruct((r, c), F32),
        compiler_params=_params(("parallel",)))(parts)


def _adamw(w, g, m, v):
    m = ADAM_B1 * m + (1.0 - ADAM_B1) * g
    v = ADAM_B2 * v + (1.0 - ADAM_B2) * jnp.square(g)
    m_hat = m / (1.0 - ADAM_B1 ** ADAM_STEP)
    v_hat = v / (1.0 - ADAM_B2 ** ADAM_STEP)
    return -ADAM_LR * (m_hat / (jnp.sqrt(v_hat) + ADAM_EPS) + ADAM_WD * w), m, v


def adamw(w, g, m, v, name):
    l, r, c = w.shape
    tr = _tile(r, 256, 8)

    def body(w_ref, g_ref, m_ref, v_ref, d_ref, nm_ref, nv_ref):
        d_ref[...], nm_ref[...], nv_ref[...] = _adamw(w_ref[...], g_ref[...], m_ref[...], v_ref[...])

    spec = pl.BlockSpec((None, tr, c), lambda a, i: (a, i, 0))
    shape = jax.ShapeDtypeStruct(w.shape, F32)
    return pl.pallas_call(body, name=name, grid=(l, r // tr), in_specs=[spec] * 4, out_specs=[spec] * 3, out_shape=[shape] * 3,
                          compiler_params=_params(("parallel", "parallel")))(w, g, m, v)


def adamw_ada(c_act_t, dmod, w, m, v):
    l, r, c = w.shape
    tr = 256

    def body(c_ref, d_ref, w_ref, m_ref, v_ref, g_ref, dl_ref, nm_ref, nv_ref):
        g = hdot(c_ref[...], d_ref[...])
        g_ref[...] = g
        dl_ref[...], nm_ref[...], nv_ref[...] = _adamw(w_ref[...], g, m_ref[...], v_ref[...])

    spec = pl.BlockSpec((None, tr, c), lambda a, i: (a, i, 0))
    shape = jax.ShapeDtypeStruct(w.shape, F32)
    return pl.pallas_call(
        body, name="adamw_ada", grid=(l, r // tr),
        in_specs=[pl.BlockSpec((tr, LANES), lambda a, i: (i, 0)), pl.BlockSpec((None, LANES, c), lambda a, i: (a, 0, 0)), spec, spec, spec],
        out_specs=[spec] * 4, out_shape=[shape] * 4, compiler_params=_params(("parallel", "parallel")))(c_act_t, dmod, w, m, v)


def _place():
    x, y, c = lax.axis_index("x"), lax.axis_index("y"), lax.axis_index("c")
    return x, y, c


def _comm_call(body, name, ins, out_shapes, space, n_sems):
    n = len(ins)
    return pl.pallas_call(
        body, name=name, out_shape=out_shapes, in_specs=[pl.BlockSpec(memory_space=space)] * n,
        out_specs=[pl.BlockSpec(memory_space=space)] * n,
        scratch_shapes=[pltpu.SemaphoreType.DMA((n, n_sems)), pltpu.SemaphoreType.DMA((n, n_sems)), pltpu.SemaphoreType.DMA((n,))],
        compiler_params=pltpu.CompilerParams(vmem_limit_bytes=VMEM_LIMIT_BYTES))(*ins)


def all_gather(shards, name, in_vmem):
    n = len(shards)

    def body(*refs):
        x_refs, out_refs, (send_sems, recv_sems, local_sems) = refs[:n], refs[n:2 * n], refs[2 * n:]
        x, y, c = _place()
        me, sibling = (x, y, c), (x, y, 1 - c)
        chips = [(1 - x, y), (x, 1 - y), (1 - x, 1 - y)]

        def rows(a, px, py, pc):
            return out_refs[a].at[4 * px + 2 * py + pc]

        def copy(a, k, block, to, from_shard=False):
            return pltpu.make_async_remote_copy(
                src_ref=x_refs[a] if from_shard else rows(a, *block), dst_ref=rows(a, *block), send_sem=send_sems.at[a, k],
                recv_sem=recv_sems.at[a, k], device_id=to, device_id_type=MESH_IDS)

        arrays = range(n)
        mine = [pltpu.make_async_copy(x_refs[a], rows(a, *me), local_sems.at[a]) for a in arrays]
        first = [copy(a, 1 + j, me, (*chip, c), True) for j, chip in enumerate(chips) for a in arrays]
        first += [copy(a, 0, me, sibling, True) for a in arrays]
        for cp in mine + first:
            cp.start()
        passed = []
        for j, chip in enumerate(chips):
            for a in arrays:
                copy(a, 1 + j, (*chip, c), me).wait_recv()
                passed.append(copy(a, 4 + j, (*chip, c), sibling))
                passed[-1].start()
        for a in arrays:
            copy(a, 0, sibling, me).wait_recv()
            for j, chip in enumerate(chips):
                copy(a, 4 + j, (*chip, 1 - c), me).wait_recv()
        for cp in first + passed:
            cp.wait_send()
        for cp in mine:
            cp.wait()

    out_shapes = [jax.ShapeDtypeStruct((N_DEV,) + s.shape, s.dtype) for s in shards]
    return _comm_call(body, name, shards, out_shapes, pltpu.VMEM if in_vmem else pl.ANY, 7)


def scatter_to_sibling(slabs, name):
    n = len(slabs)

    def body(*refs):
        in_refs, out_refs, (send_sems, recv_sems, _) = refs[:n], refs[n:2 * n], refs[2 * n:]
        x, y, c = _place()
        copies = [pltpu.make_async_remote_copy(
            src_ref=in_refs[a].at[pl.ds(4 * (1 - c), 4)], dst_ref=out_refs[a], send_sem=send_sems.at[a, 0], recv_sem=recv_sems.at[a, 0],
            device_id=(x, y, 1 - c), device_id_type=MESH_IDS) for a in range(n)]
        for cp in copies:
            cp.start()
        for cp in copies:
            cp.wait_recv()
        for cp in copies:
            cp.wait_send()

    out_shapes = [jax.ShapeDtypeStruct((4,) + s.shape[1:], s.dtype) for s in slabs]
    return _comm_call(body, name, slabs, out_shapes, pl.ANY, 1)


def scatter_to_chips(slabs, name):
    n = len(slabs)

    def body(*refs):
        in_refs, out_refs, (send_sems, recv_sems, local_sems) = refs[:n], refs[n:2 * n], refs[2 * n:]
        x, y, c = _place()
        my_chip = 2 * x + y
        mine = [pltpu.make_async_copy(in_refs[a].at[my_chip], out_refs[a].at[my_chip], local_sems.at[a]) for a in range(n)]
        copies = [pltpu.make_async_remote_copy(
            src_ref=in_refs[a].at[2 * px + py], dst_ref=out_refs[a].at[my_chip], send_sem=send_sems.at[a, j], recv_sem=recv_sems.at[a, j],
            device_id=(px, py, c), device_id_type=MESH_IDS)
            for j, (px, py) in enumerate([(1 - x, y), (x, 1 - y), (1 - x, 1 - y)]) for a in range(n)]
        for cp in mine + copies:
            cp.start()
        for cp in copies:
            cp.wait_recv()
        for cp in copies:
            cp.wait_send()
        for cp in mine:
            cp.wait()

    return _comm_call(body, name, slabs, [jax.ShapeDtypeStruct(s.shape, s.dtype) for s in slabs], pl.ANY, 3)


def pair_sum(slabs, got, name):
    _, r, c = slabs.shape
    tr = _tile(r, 256, 16)

    def body(a_ref, b_ref, o_ref):
        o_ref[...] = (a_ref[...].astype(F32) + b_ref[...].astype(F32)).astype(BF16)

    return pl.pallas_call(
        body, name=name, grid=(4, r // tr),
        in_specs=[pl.BlockSpec((None, tr, c), lambda s, i: (4 * lax.axis_index("c") + s, i, 0)), pl.BlockSpec((None, tr, c), lambda s, i: (s, i, 0))],
        out_specs=pl.BlockSpec((None, tr, c), lambda s, i: (s, i, 0)), out_shape=jax.ShapeDtypeStruct((4, r, c), BF16),
        compiler_params=_params(("parallel", "parallel")))(slabs, got)


BIG = ("w_in", "w_oa", "w_ob", "w_out", "w_ff1", "w_ff2")
SMALL = ("a_log", "dt_bias", "sinks", "dn_norm_w", "ln1_g", "ln1_b", "b_ff1", "b_ff2", "ln2_g", "ln2_b")


def _pack_rows(arrs, width, unit):
    flat = jnp.concatenate([a.reshape(-1) for a in arrs])
    rows = -(-flat.shape[0] // (width * unit)) * unit
    return jnp.pad(flat, (0, rows * width - flat.shape[0])).reshape(rows, width)


def _split_flat(flat, like):
    out, off = [], 0
    for a in like:
        n = 1
        for dim in a.shape:
            n *= dim
        out.append(flat[off:off + n].reshape(a.shape))
        off += n
    return out


ROW_SHARDED = ("w_oa", "w_ob", "w_out", "w_ff2")


def _pack_shards(given):
    rows = jnp.concatenate([given[n].reshape(-1, D) for n in ROW_SHARDED]).astype(BF16)
    return [given["w_in"].astype(BF16).reshape(DEPTH * D, -1), rows, given["w_ff1"].astype(BF16).reshape(DEPTH * D, -1)]


def _unpack_weights(g_in, g_rows, g_ff1):
    w_in = g_in.reshape(N_DEV, DEPTH, D, -1)
    w_ff1 = g_ff1.reshape(N_DEV, DEPTH, D, -1)
    layers = []
    for l in range(DEPTH):
        full_in = w_in[:, l].transpose(1, 0, 2).reshape(D, -1)
        small = jnp.pad(full_in[:, IN_AT["small"]:IN_AT["ga"]], ((0, 0), (0, LANES + IN_AT["small"] - IN_AT["ga"])))
        groups = [jnp.concatenate([full_in[:, :QKV_W], small], axis=1)] + [full_in[:, IN_AT[g]:IN_AT[g] + D] for g in IN_GROUPS[1:]]
        lay, off = dict(w_in=tuple(groups), w_ff1=w_ff1[:, l].transpose(1, 0, 2).reshape(D, D_FF)), 0
        for n in ROW_SHARDED:
            per = (D_FF if n == "w_ff2" else D) // N_DEV
            lay[n] = g_rows[:, off + l * per:off + (l + 1) * per].reshape(per * N_DEV, D)
            off += DEPTH * per
        layers.append(lay)
    return layers


def _by_core(a):
    return a.reshape((2, 2, 2) + a.shape[1:]).transpose((2, 0, 1) + tuple(range(3, a.ndim + 2))).reshape(a.shape)


def _pack_grads(grads):
    def in_order(qkvs, dq, dk, dv, z, ga, gb):
        return jnp.concatenate([qkvs[:, :QKV_W], dq, dk, dv, z, qkvs[:, QKV_W:QKV_W + IN_AT["ga"] - IN_AT["small"]], ga, gb], axis=1)

    w_in = jnp.stack([in_order(*lay["w_in"]) for lay in grads])
    s_in = w_in.reshape(DEPTH, D, N_DEV, -1).transpose(2, 0, 1, 3).reshape(N_DEV, DEPTH * D, -1)
    rows = []
    for n in ROW_SHARDED:
        w = jnp.stack([lay[n] for lay in grads])
        rows.append(w.reshape(DEPTH, N_DEV, -1, D).transpose(1, 0, 2, 3).reshape(N_DEV, -1, D))
    w = jnp.stack([lay["w_ff1"] for lay in grads])
    s_ff1 = w.reshape(DEPTH, D, N_DEV, -1).transpose(2, 0, 1, 3).reshape(N_DEV, DEPTH * D, -1)
    return [_by_core(s_in), _by_core(jnp.concatenate(rows, axis=1)), _by_core(s_ff1)]


def _layer(ops, x, u, mod, next_mod, sm, conv_w, w):
    s = x.shape[0]
    sh1, sc1, gt1, sh2, sc2, gt2 = (mod[:, i * D:(i + 1) * D] for i in range(6))
    qkvs, dq, dk, dv, z, ga, gb = ops["in_proj"](u, w["w_in"])
    q, k, v, small = qkvs[:, :D], qkvs[:, D:D + 256], qkvs[:, D + 256:QKV_W], qkvs[:, QKV_W:]
    qh = q.reshape(s, ATT_HEADS, ATT_HD).transpose(1, 0, 2)
    kh = k.reshape(s, ATT_KV, ATT_HD).transpose(1, 0, 2)
    vh = v.reshape(s, ATT_KV, ATT_HD).transpose(1, 0, 2)
    sink = jnp.broadcast_to(sm["sinks"].reshape(ATT_KV, ATT_GROUP, 1, 1), (ATT_KV, ATT_GROUP, WINDOW, 1)).reshape(ATT_KV, ATT_GROUP * WINDOW, 1)
    attn = ops["swa"](qh, kh, vh, sink).transpose(1, 0, 2).reshape(s, ATT_HEADS * ATT_HD)
    y_a = ops["mm_oa"](attn, w["w_oa"])
    qn = ops["prep_q"](dq, conv_w[:, :D])
    kn = ops["prep_k"](dk, conv_w[:, D:2 * D])
    vn = ops["prep_v"](dv, conv_w[:, 2 * D:])
    a_vec = jnp.pad(sm["a_log"], ((0, 0), (DN_HEADS, LANES - 2 * DN_HEADS)))
    b_vec = jnp.pad(sm["dt_bias"], ((0, 0), (DN_HEADS, LANES - 2 * DN_HEADS)))
    (gates,) = ops["gates"](small, a_vec, b_vec)
    beta = gates[:, :DN_HEADS].T[:, :, None]
    g = gates[:, DN_HEADS:2 * DN_HEADS].T[:, :, None]
    o = ops["delta"](qn, kn, vn, beta, g)
    (og,) = ops["gdn_post"](o, z, sm["dn_norm_w"])
    y_b = ops["mm_ob"](og, w["w_ob"])
    (mix,) = ops["mix"](ga, gb, y_a, y_b)
    mixed = ops["mm_out"](mix, w["w_out"])
    x1, u2 = ops["ln1"](x, mixed, gt1, sm["ln1_g"], sm["ln1_b"], sc2, sh2)
    (h,) = ops["relu2"](ops["mm_ff1"](u2, w["w_ff1"]), sm["b_ff1"])
    f = ops["mm_ff2"](h, w["w_ff2"])
    if next_mod is None:
        return ops["ln2"](x1, f, gt2, sm["b_ff2"], sm["ln2_g"], sm["ln2_b"])[0], None
    return ops["ln2"](x1, f, gt2, sm["b_ff2"], sm["ln2_g"], sm["ln2_b"], next_mod[:, D:2 * D], next_mod[:, :D])


def _make_ops(l):
    t = f"l{l}_"
    last = l == DEPTH - 1
    return dict(
        in_proj=make_in_proj(t + "in_proj"), swa=make_swa(t + "swa"),
        mm_oa=make_mm(t + "mm_oa"), mm_ob=make_mm(t + "mm_ob"), mm_out=make_mm(t + "mm_out"),
        mm_ff1=make_mm(t + "mm_ff1"), mm_ff2=make_mm(t + "mm_ff2"),
        prep_q=make_gdn_prep(t + "prep_q", "q"), prep_k=make_gdn_prep(t + "prep_k", "k"), prep_v=make_gdn_prep(t + "prep_v", "v"),
        gates=make_rowwise(_gates_fn, t + "gates", "t", "cc", "t"), delta=make_delta(t + "delta", 8),
        gdn_post=make_rowwise(_gdn_post_fn, t + "gdn_post", "ht", "s", "t", nc=DN_HEADS, tm=1024),
        mix=make_rowwise(_mix_fn, t + "mix", "tttt", "", "t"),
        ln1=make_rowwise(_ln1_fn, t + "ln1", "tt", "ccccc", "tt"),
        ln2=make_rowwise(_ln2_last_fn, t + "ln2", "tt", "cccc", "t") if last else make_rowwise(_ln2_fn, t + "ln2", "tt", "cccccc", "tt"),
        relu2=make_rowwise(_relu2_fn, t + "relu2", "t", "c", "t", nc=4, tm=512))


def kernel(x, c, w_ada, b_ada, w_in, conv_w, a_log, dt_bias, sinks, dn_norm_w, w_oa, w_ob, w_out, ln1_g, ln1_b, w_ff1, b_ff1, w_ff2, b_ff2, ln2_g, ln2_b, loss_target, m_w_ada, m_b_ada, m_w_in, m_conv_w, m_a_log, m_dt_bias, m_sinks, m_dn_norm_w, m_w_oa, m_w_ob, m_w_out, m_ln1_g, m_ln1_b, m_w_ff1, m_b_ff1, m_w_ff2, m_b_ff2, m_ln2_g, m_ln2_b, v_w_ada, v_b_ada, v_w_in, v_conv_w, v_a_log, v_dt_bias, v_sinks, v_dn_norm_w, v_w_oa, v_w_ob, v_w_out, v_ln1_g, v_ln1_b, v_w_ff1, v_b_ff1, v_w_ff2, v_b_ff2, v_ln2_g, v_ln2_b):
    given = dict(locals())
    me = 4 * lax.axis_index("x") + 2 * lax.axis_index("y") + lax.axis_index("c")
    conv_cols = conv_w.shape[2]

    gathered = all_gather([_pack_rows([c, conv_w], LANES, 8)], "gather_c_conv", True)[0].reshape(N_DEV, -1)
    c_all = gathered[:, :D]
    conv_full = gathered[:, D:D + DEPTH * CONV_K * conv_cols].reshape(N_DEV, DEPTH, CONV_K, conv_cols).transpose(1, 2, 0, 3).reshape(DEPTH, CONV_K, -1)

    b_cols = lax.dynamic_slice_in_dim(b_ada, me * ADA_COLS, ADA_COLS, axis=1)
    c_act_all, mod_cols = ada_project(c_all, w_ada, b_cols)
    mod_all = all_gather([mod_cols.reshape(-1, LANES)], "gather_mod", True)[0].reshape(N_DEV, DEPTH, N_DEV, ADA_COLS)
    mods = lax.dynamic_index_in_dim(mod_all, me, axis=2, keepdims=False).transpose(1, 0, 2).reshape(DEPTH, 6 * D)

    weights = _unpack_weights(*all_gather(_pack_shards(given), "gather_weights", False))

    small = {n: given[n] for n in SMALL}
    ops = [_make_ops(l) for l in range(DEPTH)]

    modulate0 = make_rowwise(_modulate_fn, "modulate0", "t", "cc", "t")

    def forward(x0, mods, small, conv_full, weights):
        (u,) = modulate0(x0, mods[:1, D:2 * D], mods[:1, :D])
        h = x0
        for l in range(DEPTH):
            next_mod = mods[l + 1:l + 2] if l + 1 < DEPTH else None
            h, u = _layer(ops[l], h, u, mods[l:l + 1], next_mod, {n: a[l:l + 1] for n, a in small.items()}, conv_full[l], weights[l])
        return h

    y, vjp = jax.vjp(forward, x[0], mods, small, conv_full, weights)
    loss_tile, dy = loss_head(y, loss_target[0])
    dx, d_mods, d_small, d_conv, d_weights = vjp(dy)
    loss = lax.psum(loss_tile[0, 0], AXES)

    slabs = _pack_grads(d_weights)
    from_sibling = scatter_to_sibling(slabs, "scatter_sibling")
    per_chip = [pair_sum(a, b, f"pair_sum_{i}") for i, (a, b) in enumerate(zip(slabs, from_sibling))]
    g_in, g_rows, g_ff1 = [sum_partials(p, f"sum_big_{i}") for i, p in enumerate(scatter_to_chips(per_chip, "scatter_chips"))]
    grad, off = dict(w_in=g_in.reshape(w_in.shape), w_ff1=g_ff1.reshape(w_ff1.shape)), 0
    for n in ROW_SHARDED:
        rows = given[n].shape[0] * given[n].shape[1]
        grad[n] = g_rows[off:off + rows].reshape(given[n].shape)
        off += rows
    delta, new_m, new_v = {}, {}, {}
    for n in BIG:
        delta[n], new_m[n], new_v[n] = adamw(given[n], grad[n], given["m_" + n], given["v_" + n], "adamw_" + n)

    partial = [d_small[n] for n in SMALL] + [d_mods, d_conv]
    (parts,) = all_gather([_pack_rows(partial, LANES, 8)], "gather_small_grads", True)
    mods_at = sum(d_small[n].size for n in SMALL)
    d_mods_all = parts.reshape(N_DEV, -1)[:, mods_at:mods_at + DEPTH * 6 * D].reshape(N_DEV, DEPTH, 6 * D)
    total = _split_flat(sum_partials(parts, "sum_small").reshape(-1), partial)
    for n, g in zip(SMALL, total):
        grad[n] = g
    grad["b_ada"] = total[len(SMALL)]
    grad["conv_w"] = lax.dynamic_slice_in_dim(total[len(SMALL) + 1], me * conv_cols, conv_cols, axis=2)
    names = SMALL + ("b_ada", "conv_w")
    packed = [_pack_rows([src[p + n] for n in names], LANES, 8)[None] for src, p in ((given, ""), (grad, ""), (given, "m_"), (given, "v_"))]
    outs = adamw(*packed, "adamw_small")
    for res, o in zip((delta, new_m, new_v), outs):
        for n, a in zip(names, _split_flat(o.reshape(-1), [given[n] for n in names])):
            res[n] = a

    dmod_mine = lax.dynamic_slice_in_dim(d_mods_all, me * ADA_COLS, ADA_COLS, axis=2).transpose(1, 0, 2)
    pad = LANES - N_DEV
    grad["w_ada"], delta["w_ada"], new_m["w_ada"], new_v["w_ada"] = adamw_ada(
        jnp.pad(c_act_all.T, ((0, 0), (0, pad))), jnp.pad(dmod_mine, ((0, 0), (0, pad), (0, 0))), w_ada, m_w_ada, v_w_ada)

    order = ("w_ada", "b_ada", "w_in", "conv_w", "a_log", "dt_bias", "sinks", "dn_norm_w", "w_oa", "w_ob", "w_out", "ln1_g", "ln1_b",
             "w_ff1", "b_ff1", "w_ff2", "b_ff2", "ln2_g", "ln2_b")
    return (loss, dx[None], *[grad[n] for n in order], *[delta[n] for n in order], *[new_m[n] for n in order], *[new_v[n] for n in order])
```

```python
import functools

import jax
import jax.numpy as jnp
from jax import lax
from jax.experimental import pallas as pl
from jax.experimental.pallas import tpu as pltpu

F32 = jnp.float32
BF16 = jnp.bfloat16

D = 1024
DEPTH = 4
N_DEV = 8
ATT_HEADS, ATT_KV, ATT_GROUP, ATT_HD, WINDOW = 16, 4, 4, 64, 128
DN_HEADS, DN_HD, CONV_K, CHUNK = 8, 128, 4, 64
D_FF = 4096
ADA_COLS = 6 * D // N_DEV
ALPHA = (2 * DEPTH) ** 0.25
LN_EPS = 1e-5
RMS_EPS = 1e-6
ADAM_LR, ADAM_B1, ADAM_B2, ADAM_EPS, ADAM_WD, ADAM_STEP = 0.001, 0.9, 0.999, 1e-08, 0.01, 10
AXES = ("x", "y", "c")
MESH_IDS = pl.DeviceIdType.MESH
VMEM_LIMIT_BYTES = 48 * 1024 * 1024
LANES = 128
HIGHEST = lax.Precision.HIGHEST

IN_SPLITS = (("q", 1024), ("k", 256), ("v", 256), ("dq", 1024), ("dk", 1024), ("dv", 1024), ("z", 1024),
             ("small", 16), ("ga", 1024), ("gb", 1024))
IN_AT = {name: sum(w for _, w in IN_SPLITS[:i]) for i, (name, _) in enumerate(IN_SPLITS)}
IN_GROUPS = ("qkvs", "dq", "dk", "dv", "z", "ga", "gb")
QKV_W = 1536
QKVS_W = QKV_W + LANES


def _params(sem=None):
    return pltpu.CompilerParams(dimension_semantics=sem, vmem_limit_bytes=VMEM_LIMIT_BYTES)


def _tile(n, pref, unit):
    if n <= pref:
        return n
    t = (pref // unit) * unit
    while t > unit and n % t:
        t -= unit
    assert n % t == 0, (n, pref, unit)
    return t


def _dg(a, b, ca, cb):
    return lax.dot_general(a.astype(BF16), b.astype(BF16), (((ca,), (cb,)), ((), ())), preferred_element_type=F32)


@jax.custom_vjp
def bdot_nn(a, b):
    return _dg(a, b, 1, 0)


def _bdot_nn_fwd(a, b):
    return _dg(a, b, 1, 0), (a, b)


def _bdot_nn_bwd(res, g):
    a, b = res
    return _dg(g, b, 1, 1).astype(a.dtype), _dg(a, g, 0, 0).astype(b.dtype)


bdot_nn.defvjp(_bdot_nn_fwd, _bdot_nn_bwd)


@jax.custom_vjp
def bdot_nt(a, b):
    return _dg(a, b, 1, 1)


def _bdot_nt_fwd(a, b):
    return _dg(a, b, 1, 1), (a, b)


def _bdot_nt_bwd(res, g):
    a, b = res
    return _dg(g, b, 1, 0).astype(a.dtype), _dg(g, a, 0, 0).astype(b.dtype)


bdot_nt.defvjp(_bdot_nt_fwd, _bdot_nt_bwd)


@jax.custom_vjp
def bdot_tn(a, b):
    return _dg(a, b, 0, 0)


def _bdot_tn_fwd(a, b):
    return _dg(a, b, 0, 0), (a, b)


def _bdot_tn_bwd(res, g):
    a, b = res
    return _dg(b, g, 1, 1).astype(a.dtype), _dg(a, g, 1, 0).astype(b.dtype)


bdot_tn.defvjp(_bdot_tn_fwd, _bdot_tn_bwd)


def _hdg(a, b, ca, cb):
    a_hi, b_hi = a.astype(BF16), b.astype(BF16)
    a_lo, b_lo = (a - a_hi.astype(F32)).astype(BF16), (b - b_hi.astype(F32)).astype(BF16)

    def dot(x, y):
        return lax.dot_general(x, y, (((ca,), (cb,)), ((), ())), preferred_element_type=F32)

    return dot(a_hi, b_hi) + (dot(a_hi, b_lo) + dot(a_lo, b_hi))


@jax.custom_vjp
def hdot(a, b):
    return _hdg(a, b, 1, 0)


def _hdot_fwd(a, b):
    return _hdg(a, b, 1, 0), (a, b)


def _hdot_bwd(res, g):
    a, b = res
    return _hdg(g, b, 1, 1), _hdg(a, g, 0, 0)


hdot.defvjp(_hdot_fwd, _hdot_bwd)


def matmul(a, b, mode, out_dtype, name, acc=None):
    if mode == "nn":
        (m, k), (k2, n) = a.shape, b.shape
    elif mode == "nt":
        (m, k), (n, k2) = a.shape, b.shape
    else:
        (k, m), (k2, n) = a.shape, b.shape
    assert k == k2, (a.shape, b.shape, mode)
    def pick(n_, pref):
        t = _tile(n_, pref, LANES)
        return n_ if t < 2 * LANES and n_ <= 2048 else t

    tm, tn, tk = pick(m, 1024), pick(n, 512), pick(k, 1024)
    nk = k // tk
    a_spec = pl.BlockSpec((tk, tm), lambda i, j, kk: (kk, i)) if mode == "tn" else pl.BlockSpec((tm, tk), lambda i, j, kk: (i, kk))
    b_spec = pl.BlockSpec((tn, tk), lambda i, j, kk: (j, kk)) if mode == "nt" else pl.BlockSpec((tk, tn), lambda i, j, kk: (kk, j))
    o_spec = pl.BlockSpec((tm, tn), lambda i, j, kk: (i, j))
    ca, cb = {"nn": (1, 0), "nt": (1, 1), "tn": (0, 0)}[mode]

    def body(*refs):
        a_ref, b_ref = refs[:2]
        c_ref = None if acc is None else refs[2]
        o_ref = refs[2 if acc is None else 3]
        part = _dg(a_ref[...], b_ref[...], ca, cb)
        if nk == 1:
            o_ref[...] = (part if c_ref is None else part + c_ref[...].astype(F32)).astype(out_dtype)
            return
        acc_ref = refs[-1]
        kk = pl.program_id(2)

        @pl.when(kk == 0)
        def _():
            acc_ref[...] = part if c_ref is None else part + c_ref[...].astype(F32)

        @pl.when(jnp.logical_and(kk > 0, kk < nk - 1))
        def _():
            acc_ref[...] += part

        @pl.when(kk == nk - 1)
        def _():
            o_ref[...] = (acc_ref[...] + part).astype(out_dtype)

    ins, in_specs = [a, b], [a_spec, b_spec]
    if acc is not None:
        ins.append(acc)
        in_specs.append(o_spec)
    return pl.pallas_call(
        body, name=name, grid=(m // tm, n // tn, nk), in_specs=in_specs, out_specs=o_spec,
        out_shape=jax.ShapeDtypeStruct((m, n), out_dtype), scratch_shapes=[pltpu.VMEM((tm, tn), F32)] if nk > 1 else [],
        compiler_params=_params(("parallel", "parallel", "arbitrary")))(*ins)


def make_mm(name):
    @jax.custom_vjp
    def mm(a, w):
        return matmul(a, w, "nn", F32, name + "_fwd")

    def fwd(a, w):
        return mm(a, w), (a, w)

    def bwd(res, g):
        a, w = res
        return matmul(g, w, "nt", a.dtype, name + "_da"), matmul(a, g, "tn", w.dtype, name + "_dw")

    mm.defvjp(fwd, bwd)
    return mm


def make_in_proj(name):
    @jax.custom_vjp
    def in_proj(u, ws):
        return tuple(matmul(u, w, "nn", F32, f"{name}_fwd_{g}") for g, w in zip(IN_GROUPS, ws))

    def fwd(u, ws):
        return in_proj(u, ws), (u, ws)

    def bwd(res, gs):
        u, ws = res
        du = None
        for idx, (g, w, dy) in enumerate(zip(IN_GROUPS, ws, gs)):
            last = idx == len(ws) - 1
            du = matmul(dy, w, "nt", u.dtype if last else F32, f"{name}_du_{g}", acc=du)
        dws = tuple(matmul(u, dy, "tn", w.dtype, f"{name}_dw_{g}") for g, w, dy in zip(IN_GROUPS, ws, gs))
        return du, dws

    in_proj.defvjp(fwd, bwd)
    return in_proj


def make_rowwise(fn, name, tile_kinds, param_kinds, out_kinds, nc=1, tm=256):
    n_t, n_p, n_o = len(tile_kinds), len(param_kinds), len(out_kinds)

    def width(a, kind):
        if kind == "h":
            return a.shape[2]
        return a.shape[1] if kind == "s" else a.shape[1] // nc

    def spec(kind, w, rows):
        if kind == "t":
            return pl.BlockSpec((rows, w), lambda j, i: (i, j))
        if kind == "h":
            return pl.BlockSpec((None, rows, w), lambda j, i: (j, i, 0))
        if kind == "c":
            return pl.BlockSpec((1, w), lambda j, i: (0, j))
        return pl.BlockSpec((1, w), lambda j, i: (0, 0))

    def full_shape(kind, w, s):
        return (s, w * nc) if kind == "t" else (nc, s, w)

    def plan(tiles, params):
        s = tiles[0].shape[0] if tile_kinds[0] == "t" else tiles[0].shape[1]
        rows = min(tm, s)
        t_w = [width(a, kd) for a, kd in zip(tiles, tile_kinds)]
        p_w = [width(a, kd) for a, kd in zip(params, param_kinds)]
        t_s = [jax.ShapeDtypeStruct((rows, w), a.dtype) for a, w in zip(tiles, t_w)]
        p_s = [jax.ShapeDtypeStruct((1, w), a.dtype) for a, w in zip(params, p_w)]
        o_s = jax.eval_shape(fn, *t_s, *p_s)
        return s, rows, t_w, p_w, o_s

    def fwd_call(*args):
        tiles, params = args[:n_t], args[n_t:]
        s, rows, t_w, p_w, o_s = plan(tiles, params)

        def body(*refs):
            ins, outs = refs[:n_t + n_p], refs[n_t + n_p:]
            res = fn(*[r[...] for r in ins])
            for o_ref, val in zip(outs, res):
                o_ref[...] = val

        in_specs = [spec(kd, w, rows) for kd, w in zip(tile_kinds, t_w)] + [spec(kd, w, rows) for kd, w in zip(param_kinds, p_w)]
        return pl.pallas_call(
            body, name=name + "_fwd", grid=(nc, s // rows), in_specs=in_specs,
            out_specs=[spec(kd, o.shape[1], rows) for kd, o in zip(out_kinds, o_s)],
            out_shape=[jax.ShapeDtypeStruct(full_shape(kd, o.shape[1], s), o.dtype) for kd, o in zip(out_kinds, o_s)],
            compiler_params=_params(("parallel", "parallel")))(*args)

    def bwd_call(args, douts):
        tiles, params = args[:n_t], args[n_t:]
        s, rows, t_w, p_w, o_s = plan(tiles, params)

        def body(*refs):
            ins = refs[:n_t + n_p]
            dos = refs[n_t + n_p:n_t + n_p + n_o]
            dts = refs[n_t + n_p + n_o:n_t + n_p + n_o + n_t]
            dps = refs[n_t + n_p + n_o + n_t:]
            j, i = pl.program_id(0), pl.program_id(1)
            _, vjp = jax.vjp(lambda *a: tuple(fn(*a)), *[r[...] for r in ins])
            grads = vjp(tuple(r[...] for r in dos))
            for r, g in zip(dts, grads[:n_t]):
                r[...] = g.astype(r.dtype)
            for r, g, kd in zip(dps, grads[n_t:], param_kinds):
                first = (i == 0) if kd == "c" else jnp.logical_and(i == 0, j == 0)

                @pl.when(first)
                def _(r=r):
                    r[...] = jnp.zeros_like(r)

                r[...] += g.astype(F32)

        in_specs = ([spec(kd, w, rows) for kd, w in zip(tile_kinds, t_w)] + [spec(kd, w, rows) for kd, w in zip(param_kinds, p_w)]
                    + [spec(kd, o.shape[1], rows) for kd, o in zip(out_kinds, o_s)])
        out_specs = [spec(kd, w, rows) for kd, w in zip(tile_kinds, t_w)] + [spec(kd, w, rows) for kd, w in zip(param_kinds, p_w)]
        out_shape = [jax.ShapeDtypeStruct(a.shape, a.dtype) for a in tiles] + [jax.ShapeDtypeStruct(a.shape, F32) for a in params]
        return pl.pallas_call(
            body, name=name + "_bwd", grid=(nc, s // rows), in_specs=in_specs, out_specs=out_specs, out_shape=out_shape,
            compiler_params=_params(("arbitrary", "arbitrary")))(*args, *douts)

    @jax.custom_vjp
    def op(*args):
        return tuple(fwd_call(*args))

    def op_fwd(*args):
        return op(*args), args

    def op_bwd(args, douts):
        return tuple(bwd_call(args, douts))

    op.defvjp(op_fwd, op_bwd)
    return op


def _sigmoid(x):
    return 1.0 / (1.0 + jnp.exp(-x))


def _silu(x):
    return x * _sigmoid(x)


def _softplus(x):
    return jnp.maximum(x, 0.0) + jnp.log(1.0 + jnp.exp(-jnp.abs(x)))


def _layer_norm(h, g, b):
    mu = jnp.mean(h, axis=-1, keepdims=True)
    var = jnp.mean(jnp.square(h - mu), axis=-1, keepdims=True)
    return (h - mu) * lax.rsqrt(var + LN_EPS) * g + b


def _modulate_fn(x, sc, sh):
    return ((x * (1.0 + sc) + sh).astype(BF16),)


def _gates_fn(x, a_vec, b_vec):
    lane = lax.broadcasted_iota(jnp.int32, x.shape, 1)
    beta = _sigmoid(x)
    g = -jnp.exp(a_vec) * _softplus(x + b_vec)
    return (jnp.where(lane < DN_HEADS, beta, jnp.where(lane < 2 * DN_HEADS, g, 0.0)),)


def _gdn_post_fn(o, z, nw):
    o = o * lax.rsqrt(jnp.mean(jnp.square(o), axis=-1, keepdims=True) + RMS_EPS) * nw
    return ((o * _silu(z)).astype(BF16),)


def _mix_fn(ga, gb, ya, yb):
    return ((_sigmoid(ga) * ya + _sigmoid(gb) * yb).astype(BF16),)


def _ln1_fn(x, mixed, gt, g, b, sc, sh):
    y = _layer_norm(ALPHA * x + (1.0 + gt) * mixed, g, b)
    return y, _modulate_fn(y, sc, sh)[0]


def _ln2_last_fn(x, f, gt, bf, g, b):
    return (_layer_norm(ALPHA * x + (1.0 + gt) * (f + bf), g, b),)


def _ln2_fn(x, f, gt, bf, g, b, sc, sh):
    (y,) = _ln2_last_fn(x, f, gt, bf, g, b)
    return y, _modulate_fn(y, sc, sh)[0]


def _relu2_fn(h, b):
    return (jnp.square(jnp.maximum(h + b, 0.0)).astype(BF16),)


def _each(f, *lists):
    return [f(*xs) for xs in zip(*lists)]


def _swa_blocks(q4, kp, kc, vp, vc, sink, first):
    rows = ATT_GROUP * WINDOW
    qi = lax.broadcasted_iota(jnp.int32, (rows, 2 * WINDOW), 0) & (WINDOW - 1)
    si = lax.broadcasted_iota(jnp.int32, (rows, 2 * WINDOW), 1)
    diff = qi + WINDOW - si
    valid = (diff >= 0) & (diff < WINDOW) & (si >= jnp.where(first, WINDOW, 0))
    q = _each(lambda a: a.reshape(rows, ATT_HD), q4)
    k = _each(lambda a, b: jnp.concatenate([a, b], axis=0), kp, kc)
    v = _each(lambda a, b: jnp.concatenate([a, b], axis=0), vp, vc)
    s = _each(lambda q, k: jnp.where(valid, bdot_nt(q, k) * (ATT_HD ** -0.5), -jnp.inf), q, k)
    m = _each(lambda s, sink: lax.stop_gradient(jnp.maximum(jnp.max(s, axis=-1, keepdims=True), sink)), s, sink)
    p = _each(lambda s, m: jnp.exp(s - m), s, m)
    pn = _each(lambda p, sink, m: p / (jnp.sum(p, axis=-1, keepdims=True) + jnp.exp(sink - m)), p, sink, m)
    return _each(lambda pn, v: bdot_nn(pn, v).reshape(ATT_GROUP, WINDOW, ATT_HD).astype(BF16), pn, v)


def make_swa(name):
    heads = range(ATT_KV)

    def specs():
        q_spec = pl.BlockSpec((ATT_HEADS, WINDOW, ATT_HD), lambda n: (0, n, 0))
        cur = pl.BlockSpec((ATT_KV, WINDOW, ATT_HD), lambda n: (0, n, 0))
        prev = pl.BlockSpec((ATT_KV, WINDOW, ATT_HD), lambda n: (0, jnp.maximum(n - 1, 0), 0))
        sink = pl.BlockSpec((ATT_KV, ATT_GROUP * WINDOW, 1), lambda n: (0, 0, 0))
        return q_spec, cur, prev, sink

    def group(ref, h):
        return ref.at[pl.ds(h * ATT_GROUP, ATT_GROUP)]

    def load(q_ref, kp_ref, kc_ref, vp_ref, vc_ref, s_ref):
        return [[group(q_ref, h)[...] for h in heads]] + [[r[h] for h in heads] for r in (kp_ref, kc_ref, vp_ref, vc_ref, s_ref)]

    def fwd_call(q, k, v, sink):
        s = q.shape[1]
        q_spec, cur, prev, sink_spec = specs()

        def body(q_ref, kp_ref, kc_ref, vp_ref, vc_ref, s_ref, o_ref):
            o = _swa_blocks(*load(q_ref, kp_ref, kc_ref, vp_ref, vc_ref, s_ref), pl.program_id(0) == 0)
            for h in heads:
                group(o_ref, h)[...] = o[h]

        return pl.pallas_call(
            body, name=name + "_fwd", grid=(s // WINDOW,), in_specs=[q_spec, prev, cur, prev, cur, sink_spec],
            out_specs=q_spec, out_shape=jax.ShapeDtypeStruct(q.shape, BF16),
            compiler_params=_params(("parallel",)))(q, k, k, v, v, sink)

    def bwd_call(q, k, v, sink, do):
        s = q.shape[1]
        q_spec, cur, prev, sink_spec = specs()

        def body(q_ref, kp_ref, kc_ref, vp_ref, vc_ref, s_ref, do_ref, dq_ref, dkp_ref, dkc_ref, dvp_ref, dvc_ref, ds_ref):
            first = pl.program_id(0) == 0
            _, vjp = jax.vjp(lambda *a: _swa_blocks(*a, first), *load(q_ref, kp_ref, kc_ref, vp_ref, vc_ref, s_ref))
            dq, dkp, dkc, dvp, dvc, ds = vjp([group(do_ref, h)[...] for h in heads])

            @pl.when(first)
            def _():
                ds_ref[...] = jnp.zeros_like(ds_ref)

            for h in heads:
                group(dq_ref, h)[...] = dq[h]
                dkp_ref[h], dkc_ref[h], dvp_ref[h], dvc_ref[h] = dkp[h], dkc[h], dvp[h], dvc[h]
                ds_ref[h] += ds[h]

        kv = jax.ShapeDtypeStruct(k.shape, F32)
        return pl.pallas_call(
            body, name=name + "_bwd", grid=(s // WINDOW,), in_specs=[q_spec, prev, cur, prev, cur, sink_spec, q_spec],
            out_specs=[q_spec, cur, cur, cur, cur, sink_spec],
            out_shape=[jax.ShapeDtypeStruct(q.shape, F32), kv, kv, kv, kv, jax.ShapeDtypeStruct(sink.shape, F32)],
            compiler_params=_params(("arbitrary",)))(q, k, k, v, v, sink, do)

    @jax.custom_vjp
    def swa(q, k, v, sink):
        return fwd_call(q, k, v, sink)

    def fwd(q, k, v, sink):
        return swa(q, k, v, sink), (q, k, v, sink)

    def bwd(res, do):
        q, k, v, sink = res
        dq, dkp, dkc, dvp, dvc, ds = bwd_call(q, k, v, sink, do)

        def fold(cur, prev):
            return cur + jnp.concatenate([prev[:, WINDOW:], jnp.zeros_like(prev[:, :WINDOW])], axis=1)

        return dq, fold(dkc, dkp), fold(dvc, dvp), ds

    swa.defvjp(fwd, bwd)
    return swa


def _shift_down(x, d, row):
    return x if d == 0 else jnp.where(row >= d, pltpu.roll(x, d, 0), 0.0)


def _shift_up(x, d, row):
    n = x.shape[0]
    return x if d == 0 else jnp.where(row < n - d, pltpu.roll(x, n - d, 0), 0.0)


def _conv_pre(x, w, row):
    return sum(w[j:j + 1, :] * _shift_down(x, CONV_K - 1 - j, row) for j in range(CONV_K))


def _prep_post(pre, kind):
    act = _silu(pre)
    if kind == "v":
        return act
    r = lax.rsqrt(jnp.sum(jnp.square(act), axis=-1, keepdims=True) + RMS_EPS)
    return act * r * (DN_HD ** -0.5 if kind == "q" else 1.0)


def make_gdn_prep(name, kind):
    def fwd_call(x, w):
        s = x.shape[0]

        def body(x_ref, w_ref, o_ref):
            row = lax.broadcasted_iota(jnp.int32, (s, DN_HD), 0)
            o_ref[...] = _prep_post(_conv_pre(x_ref[...], w_ref[...], row), kind)

        return pl.pallas_call(
            body, name=name + "_fwd", grid=(DN_HEADS,),
            in_specs=[pl.BlockSpec((s, DN_HD), lambda j: (0, j)), pl.BlockSpec((CONV_K, DN_HD), lambda j: (0, j))],
            out_specs=pl.BlockSpec((None, s, DN_HD), lambda j: (j, 0, 0)),
            out_shape=jax.ShapeDtypeStruct((DN_HEADS, s, DN_HD), F32), compiler_params=_params(("parallel",)))(x, w)

    def bwd_call(x, w, dy):
        s = x.shape[0]

        def body(x_ref, w_ref, dy_ref, dx_ref, dw_ref):
            row = lax.broadcasted_iota(jnp.int32, (s, DN_HD), 0)
            xv, wv = x_ref[...], w_ref[...]
            _, vjp = jax.vjp(lambda p: _prep_post(p, kind), _conv_pre(xv, wv, row))
            (dpre,) = vjp(dy_ref[...])
            dx_ref[...] = sum(wv[j:j + 1, :] * _shift_up(dpre, CONV_K - 1 - j, row) for j in range(CONV_K))
            for j in range(CONV_K):
                dw_ref[j:j + 1, :] = jnp.sum(dpre * _shift_down(xv, CONV_K - 1 - j, row), axis=0, keepdims=True)

        x_spec = pl.BlockSpec((s, DN_HD), lambda j: (0, j))
        w_spec = pl.BlockSpec((CONV_K, DN_HD), lambda j: (0, j))
        return pl.pallas_call(
            body, name=name + "_bwd", grid=(DN_HEADS,),
            in_specs=[x_spec, w_spec, pl.BlockSpec((None, s, DN_HD), lambda j: (j, 0, 0))], out_specs=[x_spec, w_spec],
            out_shape=[jax.ShapeDtypeStruct(x.shape, F32), jax.ShapeDtypeStruct(w.shape, F32)],
            compiler_params=_params(("parallel",)))(x, w, dy)

    @jax.custom_vjp
    def prep(x, w):
        return fwd_call(x, w)

    def fwd(x, w):
        return prep(x, w), (x, w)

    def bwd(res, dy):
        return tuple(bwd_call(*res, dy))

    prep.defvjp(fwd, bwd)
    return prep


@jax.custom_vjp
def _saved_inverse(m, t):
    return t


def _saved_inverse_fwd(m, t):
    return t, t


def _saved_inverse_bwd(t, dt):
    return _hdg(_hdg(t, dt, 0, 0), t, 1, 1), jnp.zeros_like(t)


_saved_inverse.defvjp(_saved_inverse_fwd, _saved_inverse_bwd)


def _chunk_fn(q, k, v, beta, g, state, t_saved=None):
    c = CHUNK
    r = lax.broadcasted_iota(jnp.int32, (c, c), 0)
    cc = lax.broadcasted_iota(jnp.int32, (c, c), 1)
    eye = (r == cc).astype(F32)
    causal, strict = r >= cc, r > cc
    g_row = _each(lambda g: jnp.sum(g * eye, axis=0, keepdims=True), g)
    gc = _each(lambda g_row: jnp.sum(jnp.where(causal, g_row, 0.0), axis=1, keepdims=True), g_row)
    gc_row = _each(lambda gc: jnp.sum(gc * eye, axis=0, keepdims=True), gc)
    decay = _each(lambda gc, gc_row: jnp.exp(jnp.where(causal, gc - gc_row, -jnp.inf)), gc, gc_row)
    kb = _each(jnp.multiply, k, beta)
    vb = _each(jnp.multiply, v, beta)
    kk = _each(bdot_nt, kb, k)
    qk = _each(bdot_nt, q, k)
    m = _each(lambda kk, decay: -jnp.where(strict, kk * decay, 0.0), kk, decay)
    if t_saved is None:
        t, p = _each(lambda m: eye + m, m), m
        for _ in range(5):
            p = _each(hdot, p, p)
            t = _each(lambda t, p: t + hdot(t, p), t, p)
    else:
        t = _each(_saved_inverse, m, t_saved)
    eg = _each(jnp.exp, gc)
    u = _each(hdot, t, vb)
    w = _each(lambda t, kb, eg: hdot(t, kb * eg), t, kb, eg)
    ws = _each(bdot_nn, w, state)
    qs = _each(lambda q, eg, state: bdot_nn(q * eg, state), q, eg, state)
    v_new = _each(jnp.subtract, u, ws)
    o = _each(lambda qs, qk, decay, v_new: qs + bdot_nn(qk * decay, v_new), qs, qk, decay, v_new)
    g_last = _each(lambda g: jnp.sum(g, axis=0, keepdims=True), g)
    kv = _each(lambda k, g_last, gc, v_new: bdot_tn(k * jnp.exp(g_last - gc), v_new), k, g_last, gc, v_new)
    new_state = _each(lambda state, g_last, kv: state * jnp.exp(g_last) + kv, state, g_last, kv)
    return o, new_state, t


def _gate_columns(gates):
    lane = lax.broadcasted_iota(jnp.int32, gates.shape, 1)

    def column(at):
        return jnp.sum(jnp.where(lane == at, gates, 0.0), axis=1, keepdims=True)

    return [column(h) for h in range(DN_HEADS)], [column(DN_HEADS + h) for h in range(DN_HEADS)]


def make_delta(name):
    heads = range(DN_HEADS)

    def specs(n_chunks, reverse):
        def at(n):
            return n_chunks - 1 - n if reverse else n

        x_spec = pl.BlockSpec((DN_HEADS, CHUNK, DN_HD), lambda n: (0, at(n), 0))
        g_spec = pl.BlockSpec((CHUNK, LANES), lambda n: (at(n), 0))
        st_spec = pl.BlockSpec((DN_HEADS, None, DN_HD, DN_HD), lambda n: (0, at(n), 0, 0))
        t_spec = pl.BlockSpec((DN_HEADS, None, CHUNK, CHUNK), lambda n: (0, at(n), 0, 0))
        return x_spec, g_spec, st_spec, t_spec

    def per_head(ref):
        return [ref[i] for i in heads]

    def fwd_call(qh, kh, vh, gates):
        n_chunks = qh.shape[1] // CHUNK
        x_spec, g_spec, st_spec, t_spec = specs(n_chunks, False)

        def body(q_ref, k_ref, v_ref, g_ref, o_ref, st_ref, t_ref, state):
            @pl.when(pl.program_id(0) == 0)
            def _():
                state[...] = jnp.zeros_like(state)

            s_in = per_head(state)
            o, s_new, t = _chunk_fn(per_head(q_ref), per_head(k_ref), per_head(v_ref), *_gate_columns(g_ref[...]), s_in)
            for i in heads:
                st_ref[i], o_ref[i], state[i], t_ref[i] = s_in[i], o[i], s_new[i], t[i]

        return pl.pallas_call(
            body, name=name + "_fwd", grid=(n_chunks,), in_specs=[x_spec, x_spec, x_spec, g_spec], out_specs=[x_spec, st_spec, t_spec],
            out_shape=[jax.ShapeDtypeStruct(qh.shape, F32), jax.ShapeDtypeStruct((DN_HEADS, n_chunks, DN_HD, DN_HD), F32),
                       jax.ShapeDtypeStruct((DN_HEADS, n_chunks, CHUNK, CHUNK), F32)],
            scratch_shapes=[pltpu.VMEM((DN_HEADS, DN_HD, DN_HD), F32)], compiler_params=_params(("arbitrary",)))(qh, kh, vh, gates)

    def bwd_call(qh, kh, vh, gates, states, inverses, do):
        n_chunks = qh.shape[1] // CHUNK
        x_spec, g_spec, st_spec, t_spec = specs(n_chunks, True)

        def body(q_ref, k_ref, v_ref, g_ref, st_ref, t_ref, do_ref, dq_ref, dk_ref, dv_ref, dg_ref, dstate):
            @pl.when(pl.program_id(0) == 0)
            def _():
                dstate[...] = jnp.zeros_like(dstate)

            t_saved = per_head(t_ref)
            _, vjp = jax.vjp(lambda q, k, v, gates, state: _chunk_fn(q, k, v, *_gate_columns(gates), state, t_saved=t_saved)[:2],
                             per_head(q_ref), per_head(k_ref), per_head(v_ref), g_ref[...], per_head(st_ref))
            dq, dk, dv, dgates, ds = vjp((per_head(do_ref), per_head(dstate)))
            dg_ref[...] = dgates
            for i in heads:
                dq_ref[i], dk_ref[i], dv_ref[i], dstate[i] = dq[i], dk[i], dv[i], ds[i]

        big = jax.ShapeDtypeStruct(qh.shape, F32)
        return pl.pallas_call(
            body, name=name + "_bwd", grid=(n_chunks,), in_specs=[x_spec, x_spec, x_spec, g_spec, st_spec, t_spec, x_spec],
            out_specs=[x_spec, x_spec, x_spec, g_spec], out_shape=[big, big, big, jax.ShapeDtypeStruct(gates.shape, F32)],
            scratch_shapes=[pltpu.VMEM((DN_HEADS, DN_HD, DN_HD), F32)],
            compiler_params=_params(("arbitrary",)))(qh, kh, vh, gates, states, inverses, do)

    @jax.custom_vjp
    def delta(qh, kh, vh, gates):
        return fwd_call(qh, kh, vh, gates)[0]

    def fwd(qh, kh, vh, gates):
        o, states, inverses = fwd_call(qh, kh, vh, gates)
        return o, (qh, kh, vh, gates, states, inverses)

    def bwd(res, do):
        return tuple(bwd_call(*res, do))

    delta.defvjp(fwd, bwd)
    return delta


def loss_head(y, target):
    s, d = y.shape
    tm = min(256, s)

    def body(y_ref, t_ref, l_ref, dy_ref):
        err = y_ref[...] - t_ref[...]
        dy_ref[...] = err * (1.0 / d)

        @pl.when(pl.program_id(0) == 0)
        def _():
            l_ref[...] = jnp.zeros_like(l_ref)

        l_ref[...] += 0.5 * jnp.sum(jnp.mean(jnp.square(err), axis=-1, keepdims=True), axis=0, keepdims=True)

    spec = pl.BlockSpec((tm, d), lambda i: (i, 0))
    return pl.pallas_call(
        body, name="loss_head", grid=(s // tm,), in_specs=[spec, spec], out_specs=[pl.BlockSpec((8, LANES), lambda i: (0, 0)), spec],
        out_shape=[jax.ShapeDtypeStruct((8, LANES), F32), jax.ShapeDtypeStruct(y.shape, F32)], compiler_params=_params(("arbitrary",)))(y, target)


def ada_project(c_all, w_ada, b_cols):
    tn = 256
    cols = w_ada.shape[2]

    def body(c_ref, w_ref, b_ref, ca_ref, o_ref):
        c_act = _silu(c_ref[...])
        ca_ref[...] = c_act
        o_ref[...] = _dg(c_act, w_ref[...], 1, 0) + b_ref[...]

    return pl.pallas_call(
        body, name="ada_project", grid=(DEPTH, cols // tn),
        in_specs=[pl.BlockSpec((N_DEV, D), lambda l, j: (0, 0)), pl.BlockSpec((None, D, tn), lambda l, j: (l, 0, j)),
                  pl.BlockSpec((None, 1, tn), lambda l, j: (l, 0, j))],
        out_specs=[pl.BlockSpec((N_DEV, D), lambda l, j: (0, 0)), pl.BlockSpec((None, N_DEV, tn), lambda l, j: (l, 0, j))],
        out_shape=[jax.ShapeDtypeStruct((N_DEV, D), F32), jax.ShapeDtypeStruct((DEPTH, N_DEV, cols), F32)],
        compiler_params=_params(("arbitrary", "arbitrary")))(c_all, w_ada, b_cols.reshape(DEPTH, 1, cols))


def sum_partials(parts, name):
    n_parts, r, c = parts.shape
    tr = _tile(r, 256, 16 if parts.dtype == BF16 else 8)

    def body(p_ref, o_ref):
        total = p_ref[0].astype(F32)
        for part in range(1, n_parts):
            total = total + p_ref[part].astype(F32)
        o_ref[...] = total

    return pl.pallas_call(
        body, name=name, grid=(r // tr,), in_specs=[pl.BlockSpec((n_parts, tr, c), lambda i: (0, i, 0))],
        out_specs=pl.BlockSpec((tr, c), lambda i: (i, 0)), out_shape=jax.ShapeDtypeStruct((r, c), F32),
        compiler_params=_params(("parallel",)))(parts)


def _adamw(w, g, m, v):
    m = ADAM_B1 * m + (1.0 - ADAM_B1) * g
    v = ADAM_B2 * v + (1.0 - ADAM_B2) * jnp.square(g)
    m_hat = m / (1.0 - ADAM_B1 ** ADAM_STEP)
    v_hat = v / (1.0 - ADAM_B2 ** ADAM_STEP)
    return -ADAM_LR * (m_hat / (jnp.sqrt(v_hat) + ADAM_EPS) + ADAM_WD * w), m, v


def adamw(w, g, m, v, name):
    l, r, c = w.shape
    tr = _tile(r, 256, 8)

    def body(w_ref, g_ref, m_ref, v_ref, d_ref, nm_ref, nv_ref):
        d_ref[...], nm_ref[...], nv_ref[...] = _adamw(w_ref[...], g_ref[...], m_ref[...], v_ref[...])

    spec = pl.BlockSpec((None, tr, c), lambda a, i: (a, i, 0))
    shape = jax.ShapeDtypeStruct(w.shape, F32)
    return pl.pallas_call(body, name=name, grid=(l, r // tr), in_specs=[spec] * 4, out_specs=[spec] * 3, out_shape=[shape] * 3,
                          compiler_params=_params(("parallel", "parallel")))(w, g, m, v)


def adamw_ada(c_act_t, dmod, w, m, v):
    l, r, c = w.shape
    tr = 256

    def body(c_ref, d_ref, w_ref, m_ref, v_ref, g_ref, dl_ref, nm_ref, nv_ref):
        g = hdot(c_ref[...], d_ref[...])
        g_ref[...] = g
        dl_ref[...], nm_ref[...], nv_ref[...] = _adamw(w_ref[...], g, m_ref[...], v_ref[...])

    spec = pl.BlockSpec((None, tr, c), lambda a, i: (a, i, 0))
    shape = jax.ShapeDtypeStruct(w.shape, F32)
    return pl.pallas_call(
        body, name="adamw_ada", grid=(l, r // tr),
        in_specs=[pl.BlockSpec((tr, LANES), lambda a, i: (i, 0)), pl.BlockSpec((None, LANES, c), lambda a, i: (a, 0, 0)), spec, spec, spec],
        out_specs=[spec] * 4, out_shape=[shape] * 4, compiler_params=_params(("parallel", "parallel")))(c_act_t, dmod, w, m, v)


def _place():
    x, y, c = lax.axis_index("x"), lax.axis_index("y"), lax.axis_index("c")
    return x, y, c


def _comm_call(body, name, ins, out_shapes, space, n_sems):
    n = len(ins)
    return pl.pallas_call(
        body, name=name, out_shape=out_shapes, in_specs=[pl.BlockSpec(memory_space=space)] * n,
        out_specs=[pl.BlockSpec(memory_space=space)] * n,
        scratch_shapes=[pltpu.SemaphoreType.DMA((n, n_sems)), pltpu.SemaphoreType.DMA((n, n_sems)), pltpu.SemaphoreType.DMA((n,))],
        compiler_params=pltpu.CompilerParams(vmem_limit_bytes=VMEM_LIMIT_BYTES))(*ins)


def all_gather(shards, name, in_vmem):
    n = len(shards)

    def body(*refs):
        x_refs, out_refs, (send_sems, recv_sems, local_sems) = refs[:n], refs[n:2 * n], refs[2 * n:]
        x, y, c = _place()
        me, sibling = (x, y, c), (x, y, 1 - c)
        chips = [(1 - x, y), (x, 1 - y), (1 - x, 1 - y)]

        def rows(a, px, py, pc):
            return out_refs[a].at[4 * px + 2 * py + pc]

        def copy(a, k, block, to, from_shard=False):
            return pltpu.make_async_remote_copy(
                src_ref=x_refs[a] if from_shard else rows(a, *block), dst_ref=rows(a, *block), send_sem=send_sems.at[a, k],
                recv_sem=recv_sems.at[a, k], device_id=to, device_id_type=MESH_IDS)

        arrays = range(n)
        mine = [pltpu.make_async_copy(x_refs[a], rows(a, *me), local_sems.at[a]) for a in arrays]
        first = [copy(a, 1 + j, me, (*chip, c), True) for j, chip in enumerate(chips) for a in arrays]
        first += [copy(a, 0, me, sibling, True) for a in arrays]
        for cp in mine + first:
            cp.start()
        passed = []
        for j, chip in enumerate(chips):
            for a in arrays:
                copy(a, 1 + j, (*chip, c), me).wait_recv()
                passed.append(copy(a, 4 + j, (*chip, c), sibling))
                passed[-1].start()
        for a in arrays:
            copy(a, 0, sibling, me).wait_recv()
            for j, chip in enumerate(chips):
                copy(a, 4 + j, (*chip, 1 - c), me).wait_recv()
        for cp in first + passed:
            cp.wait_send()
        for cp in mine:
            cp.wait()

    out_shapes = [jax.ShapeDtypeStruct((N_DEV,) + s.shape, s.dtype) for s in shards]
    return _comm_call(body, name, shards, out_shapes, pltpu.VMEM if in_vmem else pl.ANY, 7)


def all_gather_forwarding(shards, name):
    n = len(shards)

    def body(*refs):
        x_refs, out_refs, (send_sems, recv_sems, local_sems) = refs[:n], refs[n:2 * n], refs[2 * n:]
        x, y, c = _place()
        me = (x, y, c)

        def rows(a, dev, half):
            block = out_refs[a].at[4 * dev[0] + 2 * dev[1] + dev[2]]
            half_rows = shards[a].shape[0] // 2
            return block if half is None else block.at[pl.ds(half * half_rows, half_rows)]

        def copy(a, k, block, to, half=None, from_shard=False):
            return pltpu.make_async_remote_copy(
                src_ref=x_refs[a] if from_shard else rows(a, block, half), dst_ref=rows(a, block, half), send_sem=send_sems.at[a, k],
                recv_sem=recv_sems.at[a, k], device_id=to, device_id_type=MESH_IDS)

        def other_core(dev):
            return (dev[0], dev[1], 1 - c)

        sibling, x_nbr, y_nbr, diag = other_core(me), (1 - x, y, c), (x, 1 - y, c), (1 - x, 1 - y, c)
        arrays = range(n)
        mine = [pltpu.make_async_copy(x_refs[a], rows(a, me, None), local_sems.at[a]) for a in arrays]
        sent = [copy(a, k, me, to, from_shard=True) for k, to in ((1, x_nbr), (2, y_nbr), (0, sibling)) for a in arrays]
        for cp in mine + sent:
            cp.start()

        def on_arrival(k, block, half, passes):
            for a in arrays:
                copy(a, k, block, me, half).wait_recv()
                for k_out, to, half_out in passes:
                    sent.append(copy(a, k_out, block, to, half_out))
                    sent[-1].start()

        on_arrival(1, x_nbr, None, [(3, y_nbr, 0), (5, sibling, None)])
        on_arrival(2, y_nbr, None, [(4, x_nbr, 1), (6, sibling, None)])
        on_arrival(3, diag, 0, [(7, sibling, 0)])
        on_arrival(4, diag, 1, [(8, sibling, 1)])
        on_arrival(0, sibling, None, [])
        on_arrival(5, other_core(x_nbr), None, [])
        on_arrival(6, other_core(y_nbr), None, [])
        on_arrival(7, other_core(diag), 0, [])
        on_arrival(8, other_core(diag), 1, [])
        for cp in sent:
            cp.wait_send()
        for cp in mine:
            cp.wait()

    out_shapes = [jax.ShapeDtypeStruct((N_DEV,) + s.shape, s.dtype) for s in shards]
    return _comm_call(body, name, shards, out_shapes, pl.ANY, 9)


def scatter_to_sibling(slabs, name):
    n = len(slabs)

    def body(*refs):
        in_refs, out_refs, (send_sems, recv_sems, _) = refs[:n], refs[n:2 * n], refs[2 * n:]
        x, y, c = _place()
        copies = [pltpu.make_async_remote_copy(
            src_ref=in_refs[a].at[2 * chip + 1 - c], dst_ref=out_refs[a].at[chip], send_sem=send_sems.at[a, chip],
            recv_sem=recv_sems.at[a, chip], device_id=(x, y, 1 - c), device_id_type=MESH_IDS) for chip in range(4) for a in range(n)]
        for cp in copies:
            cp.start()
        for cp in copies:
            cp.wait_recv()
        for cp in copies:
            cp.wait_send()

    out_shapes = [jax.ShapeDtypeStruct((4,) + s.shape[1:], s.dtype) for s in slabs]
    return _comm_call(body, name, slabs, out_shapes, pl.ANY, 4)


def scatter_to_chips(slabs, name):
    n = len(slabs)

    def body(*refs):
        in_refs, out_refs, (send_sems, recv_sems, local_sems) = refs[:n], refs[n:2 * n], refs[2 * n:]
        x, y, c = _place()
        my_chip = 2 * x + y
        mine = [pltpu.make_async_copy(in_refs[a].at[my_chip], out_refs[a].at[my_chip], local_sems.at[a]) for a in range(n)]
        copies = [pltpu.make_async_remote_copy(
            src_ref=in_refs[a].at[2 * px + py], dst_ref=out_refs[a].at[my_chip], send_sem=send_sems.at[a, j], recv_sem=recv_sems.at[a, j],
            device_id=(px, py, c), device_id_type=MESH_IDS)
            for j, (px, py) in enumerate([(1 - x, y), (x, 1 - y), (1 - x, 1 - y)]) for a in range(n)]
        for cp in mine + copies:
            cp.start()
        for cp in copies:
            cp.wait_recv()
        for cp in copies:
            cp.wait_send()
        for cp in mine:
            cp.wait()

    return _comm_call(body, name, slabs, [jax.ShapeDtypeStruct(s.shape, s.dtype) for s in slabs], pl.ANY, 3)


def pair_sum(slabs, got, name):
    _, r, c = slabs.shape
    tr = _tile(r, 256, 16)

    def body(a_ref, b_ref, o_ref):
        o_ref[...] = (a_ref[...].astype(F32) + b_ref[...].astype(F32)).astype(BF16)

    return pl.pallas_call(
        body, name=name, grid=(4, r // tr),
        in_specs=[pl.BlockSpec((None, tr, c), lambda s, i: (2 * s + lax.axis_index("c"), i, 0)), pl.BlockSpec((None, tr, c), lambda s, i: (s, i, 0))],
        out_specs=pl.BlockSpec((None, tr, c), lambda s, i: (s, i, 0)), out_shape=jax.ShapeDtypeStruct((4, r, c), BF16),
        compiler_params=_params(("parallel", "parallel")))(slabs, got)


BIG = ("w_in", "w_oa", "w_ob", "w_out", "w_ff1", "w_ff2")
SMALL = ("a_log", "dt_bias", "sinks", "dn_norm_w", "ln1_g", "ln1_b", "b_ff1", "b_ff2", "ln2_g", "ln2_b")


def _pack_rows(arrs, width, unit):
    flat = jnp.concatenate([a.reshape(-1) for a in arrs])
    rows = -(-flat.shape[0] // (width * unit)) * unit
    return jnp.pad(flat, (0, rows * width - flat.shape[0])).reshape(rows, width)


def _split_flat(flat, like):
    out, off = [], 0
    for a in like:
        n = 1
        for dim in a.shape:
            n *= dim
        out.append(flat[off:off + n].reshape(a.shape))
        off += n
    return out


ROW_SHARDED = ("w_oa", "w_ob", "w_out", "w_ff2")


def _pack_shards(given):
    rows = jnp.concatenate([given[n].reshape(-1, D) for n in ROW_SHARDED]).astype(BF16)
    return [given["w_in"].astype(BF16).reshape(DEPTH * D, -1), rows, given["w_ff1"].astype(BF16).reshape(DEPTH * D, -1)]


def _unpack_weights(g_in, g_rows, g_ff1):
    w_in = g_in.reshape(N_DEV, DEPTH, D, -1)
    w_ff1 = g_ff1.reshape(N_DEV, DEPTH, D, -1)
    layers = []
    for l in range(DEPTH):
        full_in = w_in[:, l].transpose(1, 0, 2).reshape(D, -1)
        small = jnp.pad(full_in[:, IN_AT["small"]:IN_AT["ga"]], ((0, 0), (0, LANES + IN_AT["small"] - IN_AT["ga"])))
        groups = [jnp.concatenate([full_in[:, :QKV_W], small], axis=1)] + [full_in[:, IN_AT[g]:IN_AT[g] + D] for g in IN_GROUPS[1:]]
        lay, off = dict(w_in=tuple(groups), w_ff1=w_ff1[:, l].transpose(1, 0, 2).reshape(D, D_FF)), 0
        for n in ROW_SHARDED:
            per = (D_FF if n == "w_ff2" else D) // N_DEV
            lay[n] = g_rows[:, off + l * per:off + (l + 1) * per].reshape(per * N_DEV, D)
            off += DEPTH * per
        layers.append(lay)
    return layers


def _pack_grads(grads):
    def in_order(qkvs, dq, dk, dv, z, ga, gb):
        return jnp.concatenate([qkvs[:, :QKV_W], dq, dk, dv, z, qkvs[:, QKV_W:QKV_W + IN_AT["ga"] - IN_AT["small"]], ga, gb], axis=1)

    w_in = jnp.stack([in_order(*lay["w_in"]) for lay in grads])
    s_in = w_in.reshape(DEPTH, D, N_DEV, -1).transpose(2, 0, 1, 3).reshape(N_DEV, DEPTH * D, -1)
    rows = []
    for n in ROW_SHARDED:
        w = jnp.stack([lay[n] for lay in grads])
        rows.append(w.reshape(DEPTH, N_DEV, -1, D).transpose(1, 0, 2, 3).reshape(N_DEV, -1, D))
    w = jnp.stack([lay["w_ff1"] for lay in grads])
    s_ff1 = w.reshape(DEPTH, D, N_DEV, -1).transpose(2, 0, 1, 3).reshape(N_DEV, DEPTH * D, -1)
    return [s_in, jnp.concatenate(rows, axis=1), s_ff1]


def _layer(ops, x, u, mod, next_mod, sm, conv_w, w):
    s = x.shape[0]
    sh1, sc1, gt1, sh2, sc2, gt2 = (mod[:, i * D:(i + 1) * D] for i in range(6))
    qkvs, dq, dk, dv, z, ga, gb = ops["in_proj"](u, w["w_in"])
    q, k, v, small = qkvs[:, :D], qkvs[:, D:D + 256], qkvs[:, D + 256:QKV_W], qkvs[:, QKV_W:]
    qh = q.reshape(s, ATT_HEADS, ATT_HD).transpose(1, 0, 2)
    kh = k.reshape(s, ATT_KV, ATT_HD).transpose(1, 0, 2)
    vh = v.reshape(s, ATT_KV, ATT_HD).transpose(1, 0, 2)
    sink = jnp.broadcast_to(sm["sinks"].reshape(ATT_KV, ATT_GROUP, 1, 1), (ATT_KV, ATT_GROUP, WINDOW, 1)).reshape(ATT_KV, ATT_GROUP * WINDOW, 1)
    attn = ops["swa"](qh, kh, vh, sink).transpose(1, 0, 2).reshape(s, ATT_HEADS * ATT_HD)
    y_a = ops["mm_oa"](attn, w["w_oa"])
    qn = ops["prep_q"](dq, conv_w[:, :D])
    kn = ops["prep_k"](dk, conv_w[:, D:2 * D])
    vn = ops["prep_v"](dv, conv_w[:, 2 * D:])
    a_vec = jnp.pad(sm["a_log"], ((0, 0), (DN_HEADS, LANES - 2 * DN_HEADS)))
    b_vec = jnp.pad(sm["dt_bias"], ((0, 0), (DN_HEADS, LANES - 2 * DN_HEADS)))
    (gates,) = ops["gates"](small, a_vec, b_vec)
    o = ops["delta"](qn, kn, vn, gates)
    (og,) = ops["gdn_post"](o, z, sm["dn_norm_w"])
    y_b = ops["mm_ob"](og, w["w_ob"])
    (mix,) = ops["mix"](ga, gb, y_a, y_b)
    mixed = ops["mm_out"](mix, w["w_out"])
    x1, u2 = ops["ln1"](x, mixed, gt1, sm["ln1_g"], sm["ln1_b"], sc2, sh2)
    (h,) = ops["relu2"](ops["mm_ff1"](u2, w["w_ff1"]), sm["b_ff1"])
    f = ops["mm_ff2"](h, w["w_ff2"])
    if next_mod is None:
        return ops["ln2"](x1, f, gt2, sm["b_ff2"], sm["ln2_g"], sm["ln2_b"])[0], None
    return ops["ln2"](x1, f, gt2, sm["b_ff2"], sm["ln2_g"], sm["ln2_b"], next_mod[:, D:2 * D], next_mod[:, :D])


def _make_ops(l):
    t = f"l{l}_"
    last = l == DEPTH - 1
    return dict(
        in_proj=make_in_proj(t + "in_proj"), swa=make_swa(t + "swa"),
        mm_oa=make_mm(t + "mm_oa"), mm_ob=make_mm(t + "mm_ob"), mm_out=make_mm(t + "mm_out"),
        mm_ff1=make_mm(t + "mm_ff1"), mm_ff2=make_mm(t + "mm_ff2"),
        prep_q=make_gdn_prep(t + "prep_q", "q"), prep_k=make_gdn_prep(t + "prep_k", "k"), prep_v=make_gdn_prep(t + "prep_v", "v"),
        gates=make_rowwise(_gates_fn, t + "gates", "t", "cc", "t"), delta=make_delta(t + "delta"),
        gdn_post=make_rowwise(_gdn_post_fn, t + "gdn_post", "ht", "s", "t", nc=DN_HEADS, tm=1024),
        mix=make_rowwise(_mix_fn, t + "mix", "tttt", "", "t"),
        ln1=make_rowwise(_ln1_fn, t + "ln1", "tt", "ccccc", "tt"),
        ln2=make_rowwise(_ln2_last_fn, t + "ln2", "tt", "cccc", "t") if last else make_rowwise(_ln2_fn, t + "ln2", "tt", "cccccc", "tt"),
        relu2=make_rowwise(_relu2_fn, t + "relu2", "t", "c", "t", nc=4, tm=512))


def kernel(x, c, w_ada, b_ada, w_in, conv_w, a_log, dt_bias, sinks, dn_norm_w, w_oa, w_ob, w_out, ln1_g, ln1_b, w_ff1, b_ff1, w_ff2, b_ff2, ln2_g, ln2_b, loss_target, m_w_ada, m_b_ada, m_w_in, m_conv_w, m_a_log, m_dt_bias, m_sinks, m_dn_norm_w, m_w_oa, m_w_ob, m_w_out, m_ln1_g, m_ln1_b, m_w_ff1, m_b_ff1, m_w_ff2, m_b_ff2, m_ln2_g, m_ln2_b, v_w_ada, v_b_ada, v_w_in, v_conv_w, v_a_log, v_dt_bias, v_sinks, v_dn_norm_w, v_w_oa, v_w_ob, v_w_out, v_ln1_g, v_ln1_b, v_w_ff1, v_b_ff1, v_w_ff2, v_b_ff2, v_ln2_g, v_ln2_b):
    given = dict(locals())
    me = 4 * lax.axis_index("x") + 2 * lax.axis_index("y") + lax.axis_index("c")
    conv_cols = conv_w.shape[2]

    gathered = all_gather([_pack_rows([c, conv_w], LANES, 8)], "gather_c_conv", True)[0].reshape(N_DEV, -1)
    c_all = gathered[:, :D]
    conv_full = gathered[:, D:D + DEPTH * CONV_K * conv_cols].reshape(N_DEV, DEPTH, CONV_K, conv_cols).transpose(1, 2, 0, 3).reshape(DEPTH, CONV_K, -1)

    b_cols = lax.dynamic_slice_in_dim(b_ada, me * ADA_COLS, ADA_COLS, axis=1)
    c_act_all, mod_cols = ada_project(c_all, w_ada, b_cols)
    mod_all = all_gather([mod_cols.reshape(-1, LANES)], "gather_mod", True)[0].reshape(N_DEV, DEPTH, N_DEV, ADA_COLS)
    mods = lax.dynamic_index_in_dim(mod_all, me, axis=2, keepdims=False).transpose(1, 0, 2).reshape(DEPTH, 6 * D)

    weights = _unpack_weights(*all_gather_forwarding(_pack_shards(given), "gather_weights"))

    small = {n: given[n] for n in SMALL}
    ops = [_make_ops(l) for l in range(DEPTH)]

    modulate0 = make_rowwise(_modulate_fn, "modulate0", "t", "cc", "t")

    def forward(x0, mods, small, conv_full, weights):
        (u,) = modulate0(x0, mods[:1, D:2 * D], mods[:1, :D])
        h = x0
        for l in range(DEPTH):
            next_mod = mods[l + 1:l + 2] if l + 1 < DEPTH else None
            h, u = _layer(ops[l], h, u, mods[l:l + 1], next_mod, {n: a[l:l + 1] for n, a in small.items()}, conv_full[l], weights[l])
        return h

    y, vjp = jax.vjp(forward, x[0], mods, small, conv_full, weights)
    loss_tile, dy = loss_head(y, loss_target[0])
    dx, d_mods, d_small, d_conv, d_weights = vjp(dy)
    loss = lax.psum(loss_tile[0, 0], AXES)

    slabs = _pack_grads(d_weights)
    from_sibling = scatter_to_sibling(slabs, "scatter_sibling")
    per_chip = [pair_sum(a, b, f"pair_sum_{i}") for i, (a, b) in enumerate(zip(slabs, from_sibling))]
    g_in, g_rows, g_ff1 = [sum_partials(p, f"sum_big_{i}") for i, p in enumerate(scatter_to_chips(per_chip, "scatter_chips"))]
    grad, off = dict(w_in=g_in.reshape(w_in.shape), w_ff1=g_ff1.reshape(w_ff1.shape)), 0
    for n in ROW_SHARDED:
        rows = given[n].shape[0] * given[n].shape[1]
        grad[n] = g_rows[off:off + rows].reshape(given[n].shape)
        off += rows
    delta, new_m, new_v = {}, {}, {}
    for n in BIG:
        delta[n], new_m[n], new_v[n] = adamw(given[n], grad[n], given["m_" + n], given["v_" + n], "adamw_" + n)

    partial = [d_small[n] for n in SMALL] + [d_mods, d_conv]
    (parts,) = all_gather([_pack_rows(partial, LANES, 8)], "gather_small_grads", True)
    mods_at = sum(d_small[n].size for n in SMALL)
    d_mods_all = parts.reshape(N_DEV, -1)[:, mods_at:mods_at + DEPTH * 6 * D].reshape(N_DEV, DEPTH, 6 * D)
    total = _split_flat(sum_partials(parts, "sum_small").reshape(-1), partial)
    for n, g in zip(SMALL, total):
        grad[n] = g
    grad["b_ada"] = total[len(SMALL)]
    grad["conv_w"] = lax.dynamic_slice_in_dim(total[len(SMALL) + 1], me * conv_cols, conv_cols, axis=2)
    names = SMALL + ("b_ada", "conv_w")
    packed = [_pack_rows([src[p + n] for n in names], LANES, 8)[None] for src, p in ((given, ""), (grad, ""), (given, "m_"), (given, "v_"))]
    outs = adamw(*packed, "adamw_small")
    for res, o in zip((delta, new_m, new_v), outs):
        for n, a in zip(names, _split_flat(o.reshape(-1), [given[n] for n in names])):
            res[n] = a

    dmod_mine = lax.dynamic_slice_in_dim(d_mods_all, me * ADA_COLS, ADA_COLS, axis=2).transpose(1, 0, 2)
    pad = LANES - N_DEV
    grad["w_ada"], delta["w_ada"], new_m["w_ada"], new_v["w_ada"] = adamw_ada(
        jnp.pad(c_act_all.T, ((0, 0), (0, pad))), jnp.pad(dmod_mine, ((0, 0), (0, pad), (0, 0))), w_ada, m_w_ada, v_w_ada)

    order = ("w_ada", "b_ada", "w_in", "conv_w", "a_log", "dt_bias", "sinks", "dn_norm_w", "w_oa", "w_ob", "w_out", "ln1_g", "ln1_b",
             "w_ff1", "b_ff1", "w_ff2", "b_ff2", "ln2_g", "ln2_b")
    return (loss, dx[None], *[grad[n] for n in order], *[delta[n] for n in order], *[new_m[n] for n in order], *[new_v[n] for n in order])
```

```python
import functools

import jax
import jax.numpy as jnp
from jax import lax
from jax.experimental import pallas as pl
from jax.experimental.pallas import tpu as pltpu

F32 = jnp.float32
BF16 = jnp.bfloat16

D = 1024
DEPTH = 4
N_DEV = 8
ATT_HEADS, ATT_KV, ATT_GROUP, ATT_HD, WINDOW = 16, 4, 4, 64, 128
DN_HEADS, DN_HD, CONV_K, CHUNK = 8, 128, 4, 64
D_FF = 4096
ADA_COLS = 6 * D // N_DEV
ALPHA = (2 * DEPTH) ** 0.25
LN_EPS = 1e-5
RMS_EPS = 1e-6
ADAM_LR, ADAM_B1, ADAM_B2, ADAM_EPS, ADAM_WD, ADAM_STEP = 0.001, 0.9, 0.999, 1e-08, 0.01, 10
AXES = ("x", "y", "c")
MESH_IDS = pl.DeviceIdType.MESH
VMEM_LIMIT_BYTES = 48 * 1024 * 1024
LANES = 128
HIGHEST = lax.Precision.HIGHEST

IN_SPLITS = (("q", 1024), ("k", 256), ("v", 256), ("dq", 1024), ("dk", 1024), ("dv", 1024), ("z", 1024),
             ("small", 16), ("ga", 1024), ("gb", 1024))
IN_AT = {name: sum(w for _, w in IN_SPLITS[:i]) for i, (name, _) in enumerate(IN_SPLITS)}
IN_GROUPS = ("qkvs", "dq", "dk", "dv", "z", "ga", "gb")
QKV_W = 1536
QKVS_W = QKV_W + LANES


def _params(sem=None):
    return pltpu.CompilerParams(dimension_semantics=sem, vmem_limit_bytes=VMEM_LIMIT_BYTES)


def _tile(n, pref, unit):
    if n <= pref:
        return n
    t = (pref // unit) * unit
    while t > unit and n % t:
        t -= unit
    assert n % t == 0, (n, pref, unit)
    return t


def _dg(a, b, ca, cb):
    return lax.dot_general(a.astype(BF16), b.astype(BF16), (((ca,), (cb,)), ((), ())), preferred_element_type=F32)


@jax.custom_vjp
def bdot_nn(a, b):
    return _dg(a, b, 1, 0)


def _bdot_nn_fwd(a, b):
    return _dg(a, b, 1, 0), (a, b)


def _bdot_nn_bwd(res, g):
    a, b = res
    return _dg(g, b, 1, 1).astype(a.dtype), _dg(a, g, 0, 0).astype(b.dtype)


bdot_nn.defvjp(_bdot_nn_fwd, _bdot_nn_bwd)


@jax.custom_vjp
def bdot_nt(a, b):
    return _dg(a, b, 1, 1)


def _bdot_nt_fwd(a, b):
    return _dg(a, b, 1, 1), (a, b)


def _bdot_nt_bwd(res, g):
    a, b = res
    return _dg(g, b, 1, 0).astype(a.dtype), _dg(g, a, 0, 0).astype(b.dtype)


bdot_nt.defvjp(_bdot_nt_fwd, _bdot_nt_bwd)


@jax.custom_vjp
def bdot_tn(a, b):
    return _dg(a, b, 0, 0)


def _bdot_tn_fwd(a, b):
    return _dg(a, b, 0, 0), (a, b)


def _bdot_tn_bwd(res, g):
    a, b = res
    return _dg(b, g, 1, 1).astype(a.dtype), _dg(a, g, 1, 0).astype(b.dtype)


bdot_tn.defvjp(_bdot_tn_fwd, _bdot_tn_bwd)


def _hdg(a, b, ca, cb):
    a_hi, b_hi = a.astype(BF16), b.astype(BF16)
    a_lo, b_lo = (a - a_hi.astype(F32)).astype(BF16), (b - b_hi.astype(F32)).astype(BF16)

    def dot(x, y):
        return lax.dot_general(x, y, (((ca,), (cb,)), ((), ())), preferred_element_type=F32)

    return dot(a_hi, b_hi) + (dot(a_hi, b_lo) + dot(a_lo, b_hi))


@jax.custom_vjp
def hdot(a, b):
    return _hdg(a, b, 1, 0)


def _hdot_fwd(a, b):
    return _hdg(a, b, 1, 0), (a, b)


def _hdot_bwd(res, g):
    a, b = res
    return _hdg(g, b, 1, 1), _hdg(a, g, 0, 0)


hdot.defvjp(_hdot_fwd, _hdot_bwd)


def matmul(a, b, mode, out_dtype, name, acc=None):
    if mode == "nn":
        (m, k), (k2, n) = a.shape, b.shape
    elif mode == "nt":
        (m, k), (n, k2) = a.shape, b.shape
    else:
        (k, m), (k2, n) = a.shape, b.shape
    assert k == k2, (a.shape, b.shape, mode)
    def pick(n_, pref):
        t = _tile(n_, pref, LANES)
        return n_ if t < 2 * LANES and n_ <= 2048 else t

    tm, tn, tk = pick(m, 1024), pick(n, 512), pick(k, 1024)
    nk = k // tk
    a_spec = pl.BlockSpec((tk, tm), lambda i, j, kk: (kk, i)) if mode == "tn" else pl.BlockSpec((tm, tk), lambda i, j, kk: (i, kk))
    b_spec = pl.BlockSpec((tn, tk), lambda i, j, kk: (j, kk)) if mode == "nt" else pl.BlockSpec((tk, tn), lambda i, j, kk: (kk, j))
    o_spec = pl.BlockSpec((tm, tn), lambda i, j, kk: (i, j))
    ca, cb = {"nn": (1, 0), "nt": (1, 1), "tn": (0, 0)}[mode]

    def body(*refs):
        a_ref, b_ref = refs[:2]
        c_ref = None if acc is None else refs[2]
        o_ref = refs[2 if acc is None else 3]
        part = _dg(a_ref[...], b_ref[...], ca, cb)
        if nk == 1:
            o_ref[...] = (part if c_ref is None else part + c_ref[...].astype(F32)).astype(out_dtype)
            return
        acc_ref = refs[-1]
        kk = pl.program_id(2)

        @pl.when(kk == 0)
        def _():
            acc_ref[...] = part if c_ref is None else part + c_ref[...].astype(F32)

        @pl.when(jnp.logical_and(kk > 0, kk < nk - 1))
        def _():
            acc_ref[...] += part

        @pl.when(kk == nk - 1)
        def _():
            o_ref[...] = (acc_ref[...] + part).astype(out_dtype)

    ins, in_specs = [a, b], [a_spec, b_spec]
    if acc is not None:
        ins.append(acc)
        in_specs.append(o_spec)
    return pl.pallas_call(
        body, name=name, grid=(m // tm, n // tn, nk), in_specs=in_specs, out_specs=o_spec,
        out_shape=jax.ShapeDtypeStruct((m, n), out_dtype), scratch_shapes=[pltpu.VMEM((tm, tn), F32)] if nk > 1 else [],
        compiler_params=_params(("parallel", "parallel", "arbitrary")))(*ins)


def make_mm(name, out_dtype=F32):
    @jax.custom_vjp
    def mm(a, w):
        return matmul(a, w, "nn", out_dtype, name + "_fwd")

    def fwd(a, w):
        return mm(a, w), (a, w)

    def bwd(res, g):
        a, w = res
        return matmul(g, w, "nt", a.dtype, name + "_da"), matmul(a, g, "tn", w.dtype, name + "_dw")

    mm.defvjp(fwd, bwd)
    return mm


def make_in_proj(name):
    @jax.custom_vjp
    def in_proj(u, ws):
        return tuple(matmul(u, w, "nn", BF16, f"{name}_fwd_{g}") for g, w in zip(IN_GROUPS, ws))

    def fwd(u, ws):
        return in_proj(u, ws), (u, ws)

    def bwd(res, gs):
        u, ws = res
        du = None
        for idx, (g, w, dy) in enumerate(zip(IN_GROUPS, ws, gs)):
            last = idx == len(ws) - 1
            du = matmul(dy, w, "nt", u.dtype if last else F32, f"{name}_du_{g}", acc=du)
        dws = tuple(matmul(u, dy, "tn", w.dtype, f"{name}_dw_{g}") for g, w, dy in zip(IN_GROUPS, ws, gs))
        return du, dws

    in_proj.defvjp(fwd, bwd)
    return in_proj


def make_rowwise(fn, name, tile_kinds, param_kinds, out_kinds, nc=1, tm=256):
    n_t, n_p, n_o = len(tile_kinds), len(param_kinds), len(out_kinds)

    def width(a, kind):
        if kind == "h":
            return a.shape[2]
        return a.shape[1] if kind == "s" else a.shape[1] // nc

    def spec(kind, w, rows):
        if kind == "t":
            return pl.BlockSpec((rows, w), lambda j, i: (i, j))
        if kind == "h":
            return pl.BlockSpec((None, rows, w), lambda j, i: (j, i, 0))
        if kind == "c":
            return pl.BlockSpec((1, w), lambda j, i: (0, j))
        return pl.BlockSpec((1, w), lambda j, i: (0, 0))

    def full_shape(kind, w, s):
        return (s, w * nc) if kind == "t" else (nc, s, w)

    def plan(tiles, params):
        s = tiles[0].shape[0] if tile_kinds[0] == "t" else tiles[0].shape[1]
        rows = min(tm, s)
        t_w = [width(a, kd) for a, kd in zip(tiles, tile_kinds)]
        p_w = [width(a, kd) for a, kd in zip(params, param_kinds)]
        t_s = [jax.ShapeDtypeStruct((rows, w), a.dtype) for a, w in zip(tiles, t_w)]
        p_s = [jax.ShapeDtypeStruct((1, w), a.dtype) for a, w in zip(params, p_w)]
        o_s = jax.eval_shape(fn, *t_s, *p_s)
        return s, rows, t_w, p_w, o_s

    def fwd_call(*args):
        tiles, params = args[:n_t], args[n_t:]
        s, rows, t_w, p_w, o_s = plan(tiles, params)

        def body(*refs):
            ins, outs = refs[:n_t + n_p], refs[n_t + n_p:]
            res = fn(*[r[...] for r in ins])
            for o_ref, val in zip(outs, res):
                o_ref[...] = val

        in_specs = [spec(kd, w, rows) for kd, w in zip(tile_kinds, t_w)] + [spec(kd, w, rows) for kd, w in zip(param_kinds, p_w)]
        return pl.pallas_call(
            body, name=name + "_fwd", grid=(nc, s // rows), in_specs=in_specs,
            out_specs=[spec(kd, o.shape[1], rows) for kd, o in zip(out_kinds, o_s)],
            out_shape=[jax.ShapeDtypeStruct(full_shape(kd, o.shape[1], s), o.dtype) for kd, o in zip(out_kinds, o_s)],
            compiler_params=_params(("parallel", "parallel")))(*args)

    def bwd_call(args, douts):
        tiles, params = args[:n_t], args[n_t:]
        s, rows, t_w, p_w, o_s = plan(tiles, params)

        def body(*refs):
            ins = refs[:n_t + n_p]
            dos = refs[n_t + n_p:n_t + n_p + n_o]
            dts = refs[n_t + n_p + n_o:n_t + n_p + n_o + n_t]
            dps = refs[n_t + n_p + n_o + n_t:]
            j, i = pl.program_id(0), pl.program_id(1)
            _, vjp = jax.vjp(lambda *a: tuple(fn(*a)), *[r[...] for r in ins])
            grads = vjp(tuple(r[...] for r in dos))
            for r, g in zip(dts, grads[:n_t]):
                r[...] = g.astype(r.dtype)
            for r, g, kd in zip(dps, grads[n_t:], param_kinds):
                first = (i == 0) if kd == "c" else jnp.logical_and(i == 0, j == 0)

                @pl.when(first)
                def _(r=r):
                    r[...] = jnp.zeros_like(r)

                r[...] += g.astype(F32)

        in_specs = ([spec(kd, w, rows) for kd, w in zip(tile_kinds, t_w)] + [spec(kd, w, rows) for kd, w in zip(param_kinds, p_w)]
                    + [spec(kd, o.shape[1], rows) for kd, o in zip(out_kinds, o_s)])
        out_specs = [spec(kd, w, rows) for kd, w in zip(tile_kinds, t_w)] + [spec(kd, w, rows) for kd, w in zip(param_kinds, p_w)]
        out_shape = [jax.ShapeDtypeStruct(a.shape, a.dtype) for a in tiles] + [jax.ShapeDtypeStruct(a.shape, F32) for a in params]
        return pl.pallas_call(
            body, name=name + "_bwd", grid=(nc, s // rows), in_specs=in_specs, out_specs=out_specs, out_shape=out_shape,
            compiler_params=_params(("arbitrary", "arbitrary")))(*args, *douts)

    @jax.custom_vjp
    def op(*args):
        return tuple(fwd_call(*args))

    def op_fwd(*args):
        return op(*args), args

    def op_bwd(args, douts):
        return tuple(bwd_call(args, douts))

    op.defvjp(op_fwd, op_bwd)
    return op


def _sigmoid(x):
    return 1.0 / (1.0 + jnp.exp(-x))


def _silu(x):
    return x * _sigmoid(x)


def _softplus(x):
    return jnp.maximum(x, 0.0) + jnp.log(1.0 + jnp.exp(-jnp.abs(x)))


def _layer_norm(h, g, b):
    mu = jnp.mean(h, axis=-1, keepdims=True)
    var = jnp.mean(jnp.square(h - mu), axis=-1, keepdims=True)
    return (h - mu) * lax.rsqrt(var + LN_EPS) * g + b


def _modulate_fn(x, sc, sh):
    return ((x * (1.0 + sc) + sh).astype(BF16),)


def _gates_fn(x, a_vec, b_vec):
    x = x.astype(F32)
    lane = lax.broadcasted_iota(jnp.int32, x.shape, 1)
    beta = _sigmoid(x)
    g = -jnp.exp(a_vec) * _softplus(x + b_vec)
    return (jnp.where(lane < DN_HEADS, beta, jnp.where(lane < 2 * DN_HEADS, g, 0.0)),)


def _gdn_post_fn(o, z, nw):
    o = o * lax.rsqrt(jnp.mean(jnp.square(o), axis=-1, keepdims=True) + RMS_EPS) * nw
    return ((o * _silu(z.astype(F32))).astype(BF16),)


def _mix_fn(ga, gb, ya, yb):
    ga, gb, ya, yb = (t.astype(F32) for t in (ga, gb, ya, yb))
    return ((_sigmoid(ga) * ya + _sigmoid(gb) * yb).astype(BF16),)


def _ln1_fn(x, mixed, gt, g, b, sc, sh):
    y = _layer_norm(ALPHA * x + (1.0 + gt) * mixed, g, b)
    return y, _modulate_fn(y, sc, sh)[0]


def _ln2_last_fn(x, f, gt, bf, g, b):
    return (_layer_norm(ALPHA * x + (1.0 + gt) * (f + bf), g, b),)


def _ln2_fn(x, f, gt, bf, g, b, sc, sh):
    (y,) = _ln2_last_fn(x, f, gt, bf, g, b)
    return y, _modulate_fn(y, sc, sh)[0]


def _relu2_fn(h, b):
    return (jnp.square(jnp.maximum(h.astype(F32) + b, 0.0)).astype(BF16),)


def _each(f, *lists):
    return [f(*xs) for xs in zip(*lists)]


def _swa_blocks(q4, kp, kc, vp, vc, sink, first):
    rows = ATT_GROUP * WINDOW
    qi = lax.broadcasted_iota(jnp.int32, (rows, 2 * WINDOW), 0) & (WINDOW - 1)
    si = lax.broadcasted_iota(jnp.int32, (rows, 2 * WINDOW), 1)
    diff = qi + WINDOW - si
    valid = (diff >= 0) & (diff < WINDOW) & (si >= jnp.where(first, WINDOW, 0))
    q = _each(lambda a: a.reshape(rows, ATT_HD), q4)
    k = _each(lambda a, b: jnp.concatenate([a, b], axis=0), kp, kc)
    v = _each(lambda a, b: jnp.concatenate([a, b], axis=0), vp, vc)
    s = _each(lambda q, k: jnp.where(valid, bdot_nt(q, k) * (ATT_HD ** -0.5), -jnp.inf), q, k)
    m = _each(lambda s, sink: lax.stop_gradient(jnp.maximum(jnp.max(s, axis=-1, keepdims=True), sink)), s, sink)
    p = _each(lambda s, m: jnp.exp(s - m), s, m)
    pn = _each(lambda p, sink, m: p / (jnp.sum(p, axis=-1, keepdims=True) + jnp.exp(sink - m)), p, sink, m)
    return _each(lambda pn, v: bdot_nn(pn, v).reshape(ATT_GROUP, WINDOW, ATT_HD).astype(BF16), pn, v)


def make_swa(name):
    heads = range(ATT_KV)

    def specs():
        q_spec = pl.BlockSpec((ATT_HEADS, WINDOW, ATT_HD), lambda n: (0, n, 0))
        cur = pl.BlockSpec((ATT_KV, WINDOW, ATT_HD), lambda n: (0, n, 0))
        prev = pl.BlockSpec((ATT_KV, WINDOW, ATT_HD), lambda n: (0, jnp.maximum(n - 1, 0), 0))
        sink = pl.BlockSpec((ATT_KV, ATT_GROUP * WINDOW, 1), lambda n: (0, 0, 0))
        return q_spec, cur, prev, sink

    def group(ref, h):
        return ref.at[pl.ds(h * ATT_GROUP, ATT_GROUP)]

    def load(q_ref, kp_ref, kc_ref, vp_ref, vc_ref, s_ref):
        return [[group(q_ref, h)[...] for h in heads]] + [[r[h] for h in heads] for r in (kp_ref, kc_ref, vp_ref, vc_ref, s_ref)]

    def fwd_call(q, k, v, sink):
        s = q.shape[1]
        q_spec, cur, prev, sink_spec = specs()

        def body(q_ref, kp_ref, kc_ref, vp_ref, vc_ref, s_ref, o_ref):
            o = _swa_blocks(*load(q_ref, kp_ref, kc_ref, vp_ref, vc_ref, s_ref), pl.program_id(0) == 0)
            for h in heads:
                group(o_ref, h)[...] = o[h]

        return pl.pallas_call(
            body, name=name + "_fwd", grid=(s // WINDOW,), in_specs=[q_spec, prev, cur, prev, cur, sink_spec],
            out_specs=q_spec, out_shape=jax.ShapeDtypeStruct(q.shape, BF16),
            compiler_params=_params(("parallel",)))(q, k, k, v, v, sink)

    def bwd_call(q, k, v, sink, do):
        s = q.shape[1]
        q_spec, cur, prev, sink_spec = specs()

        def body(q_ref, kp_ref, kc_ref, vp_ref, vc_ref, s_ref, do_ref, dq_ref, dkp_ref, dkc_ref, dvp_ref, dvc_ref, ds_ref):
            first = pl.program_id(0) == 0
            _, vjp = jax.vjp(lambda *a: _swa_blocks(*a, first), *load(q_ref, kp_ref, kc_ref, vp_ref, vc_ref, s_ref))
            dq, dkp, dkc, dvp, dvc, ds = vjp([group(do_ref, h)[...] for h in heads])

            @pl.when(first)
            def _():
                ds_ref[...] = jnp.zeros_like(ds_ref)

            for h in heads:
                group(dq_ref, h)[...] = dq[h]
                dkp_ref[h], dkc_ref[h], dvp_ref[h], dvc_ref[h] = dkp[h], dkc[h], dvp[h], dvc[h]
                ds_ref[h] += ds[h]

        kv = jax.ShapeDtypeStruct(k.shape, k.dtype)
        return pl.pallas_call(
            body, name=name + "_bwd", grid=(s // WINDOW,), in_specs=[q_spec, prev, cur, prev, cur, sink_spec, q_spec],
            out_specs=[q_spec, cur, cur, cur, cur, sink_spec],
            out_shape=[jax.ShapeDtypeStruct(q.shape, q.dtype), kv, kv, kv, kv, jax.ShapeDtypeStruct(sink.shape, F32)],
            compiler_params=_params(("arbitrary",)))(q, k, k, v, v, sink, do)

    @jax.custom_vjp
    def swa(q, k, v, sink):
        return fwd_call(q, k, v, sink)

    def fwd(q, k, v, sink):
        return swa(q, k, v, sink), (q, k, v, sink)

    def bwd(res, do):
        q, k, v, sink = res
        dq, dkp, dkc, dvp, dvc, ds = bwd_call(q, k, v, sink, do)

        def fold(cur, prev):
            shifted = jnp.concatenate([prev[:, WINDOW:], jnp.zeros_like(prev[:, :WINDOW])], axis=1)
            return (cur.astype(F32) + shifted.astype(F32)).astype(cur.dtype)

        return dq, fold(dkc, dkp), fold(dvc, dvp), ds

    swa.defvjp(fwd, bwd)
    return swa


def _shift_down(x, d, row):
    return x if d == 0 else jnp.where(row >= d, pltpu.roll(x, d, 0), 0.0)


def _shift_up(x, d, row):
    n = x.shape[0]
    return x if d == 0 else jnp.where(row < n - d, pltpu.roll(x, n - d, 0), 0.0)


def _conv_pre(x, w, row):
    return sum(w[j:j + 1, :] * _shift_down(x, CONV_K - 1 - j, row) for j in range(CONV_K))


def _prep_post(pre, kind):
    act = _silu(pre)
    if kind == "v":
        return act
    r = lax.rsqrt(jnp.sum(jnp.square(act), axis=-1, keepdims=True) + RMS_EPS)
    return act * r * (DN_HD ** -0.5 if kind == "q" else 1.0)


def make_gdn_prep(name, kind):
    def fwd_call(x, w):
        s = x.shape[0]

        def body(x_ref, w_ref, o_ref):
            row = lax.broadcasted_iota(jnp.int32, (s, DN_HD), 0)
            o_ref[...] = _prep_post(_conv_pre(x_ref[...].astype(F32), w_ref[...], row), kind)

        return pl.pallas_call(
            body, name=name + "_fwd", grid=(DN_HEADS,),
            in_specs=[pl.BlockSpec((s, DN_HD), lambda j: (0, j)), pl.BlockSpec((CONV_K, DN_HD), lambda j: (0, j))],
            out_specs=pl.BlockSpec((None, s, DN_HD), lambda j: (j, 0, 0)),
            out_shape=jax.ShapeDtypeStruct((DN_HEADS, s, DN_HD), F32), compiler_params=_params(("parallel",)))(x, w)

    def bwd_call(x, w, dy):
        s = x.shape[0]

        def body(x_ref, w_ref, dy_ref, dx_ref, dw_ref):
            row = lax.broadcasted_iota(jnp.int32, (s, DN_HD), 0)
            xv, wv = x_ref[...].astype(F32), w_ref[...]
            _, vjp = jax.vjp(lambda p: _prep_post(p, kind), _conv_pre(xv, wv, row))
            (dpre,) = vjp(dy_ref[...])
            dx_ref[...] = sum(wv[j:j + 1, :] * _shift_up(dpre, CONV_K - 1 - j, row) for j in range(CONV_K)).astype(dx_ref.dtype)
            for j in range(CONV_K):
                dw_ref[j:j + 1, :] = jnp.sum(dpre * _shift_down(xv, CONV_K - 1 - j, row), axis=0, keepdims=True)

        x_spec = pl.BlockSpec((s, DN_HD), lambda j: (0, j))
        w_spec = pl.BlockSpec((CONV_K, DN_HD), lambda j: (0, j))
        return pl.pallas_call(
            body, name=name + "_bwd", grid=(DN_HEADS,),
            in_specs=[x_spec, w_spec, pl.BlockSpec((None, s, DN_HD), lambda j: (j, 0, 0))], out_specs=[x_spec, w_spec],
            out_shape=[jax.ShapeDtypeStruct(x.shape, x.dtype), jax.ShapeDtypeStruct(w.shape, F32)],
            compiler_params=_params(("parallel",)))(x, w, dy)

    @jax.custom_vjp
    def prep(x, w):
        return fwd_call(x, w)

    def fwd(x, w):
        return prep(x, w), (x, w)

    def bwd(res, dy):
        return tuple(bwd_call(*res, dy))

    prep.defvjp(fwd, bwd)
    return prep


@jax.custom_vjp
def _saved_inverse(m, t):
    return t


def _saved_inverse_fwd(m, t):
    return t, t


def _saved_inverse_bwd(t, dt):
    return _hdg(_hdg(t, dt, 0, 0), t, 1, 1), jnp.zeros_like(t)


_saved_inverse.defvjp(_saved_inverse_fwd, _saved_inverse_bwd)


def _chunk_fn(q, k, v, beta, g, state, t_saved=None):
    c = CHUNK
    r = lax.broadcasted_iota(jnp.int32, (c, c), 0)
    cc = lax.broadcasted_iota(jnp.int32, (c, c), 1)
    eye = (r == cc).astype(F32)
    causal, strict = r >= cc, r > cc
    g_row = _each(lambda g: jnp.sum(g * eye, axis=0, keepdims=True), g)
    gc = _each(lambda g_row: jnp.sum(jnp.where(causal, g_row, 0.0), axis=1, keepdims=True), g_row)
    gc_row = _each(lambda gc: jnp.sum(gc * eye, axis=0, keepdims=True), gc)
    decay = _each(lambda gc, gc_row: jnp.exp(jnp.where(causal, gc - gc_row, -jnp.inf)), gc, gc_row)
    kb = _each(jnp.multiply, k, beta)
    vb = _each(jnp.multiply, v, beta)
    kk = _each(bdot_nt, kb, k)
    qk = _each(bdot_nt, q, k)
    m = _each(lambda kk, decay: -jnp.where(strict, kk * decay, 0.0), kk, decay)
    if t_saved is None:
        t, p = _each(lambda m: eye + m, m), m
        for _ in range(5):
            p = _each(hdot, p, p)
            t = _each(lambda t, p: t + hdot(t, p), t, p)
    else:
        t = _each(_saved_inverse, m, t_saved)
    eg = _each(jnp.exp, gc)
    u = _each(hdot, t, vb)
    w = _each(lambda t, kb, eg: hdot(t, kb * eg), t, kb, eg)
    ws = _each(bdot_nn, w, state)
    qs = _each(lambda q, eg, state: bdot_nn(q * eg, state), q, eg, state)
    v_new = _each(jnp.subtract, u, ws)
    o = _each(lambda qs, qk, decay, v_new: qs + bdot_nn(qk * decay, v_new), qs, qk, decay, v_new)
    g_last = _each(lambda g: jnp.sum(g, axis=0, keepdims=True), g)
    kv = _each(lambda k, g_last, gc, v_new: bdot_tn(k * jnp.exp(g_last - gc), v_new), k, g_last, gc, v_new)
    new_state = _each(lambda state, g_last, kv: state * jnp.exp(g_last) + kv, state, g_last, kv)
    return o, new_state, t


def _gate_columns(gates):
    lane = lax.broadcasted_iota(jnp.int32, gates.shape, 1)

    def column(at):
        return jnp.sum(jnp.where(lane == at, gates, 0.0), axis=1, keepdims=True)

    return [column(h) for h in range(DN_HEADS)], [column(DN_HEADS + h) for h in range(DN_HEADS)]


def make_delta(name):
    heads = range(DN_HEADS)

    def specs(n_chunks, reverse):
        def at(n):
            return n_chunks - 1 - n if reverse else n

        x_spec = pl.BlockSpec((DN_HEADS, CHUNK, DN_HD), lambda n: (0, at(n), 0))
        g_spec = pl.BlockSpec((CHUNK, LANES), lambda n: (at(n), 0))
        st_spec = pl.BlockSpec((DN_HEADS, None, DN_HD, DN_HD), lambda n: (0, at(n), 0, 0))
        t_spec = pl.BlockSpec((DN_HEADS, None, CHUNK, CHUNK), lambda n: (0, at(n), 0, 0))
        return x_spec, g_spec, st_spec, t_spec

    def per_head(ref):
        return [ref[i] for i in heads]

    def fwd_call(qh, kh, vh, gates):
        n_chunks = qh.shape[1] // CHUNK
        x_spec, g_spec, st_spec, t_spec = specs(n_chunks, False)

        def body(q_ref, k_ref, v_ref, g_ref, o_ref, st_ref, t_ref, state):
            @pl.when(pl.program_id(0) == 0)
            def _():
                state[...] = jnp.zeros_like(state)

            s_in = per_head(state)
            o, s_new, t = _chunk_fn(per_head(q_ref), per_head(k_ref), per_head(v_ref), *_gate_columns(g_ref[...]), s_in)
            for i in heads:
                st_ref[i], o_ref[i], state[i], t_ref[i] = s_in[i], o[i], s_new[i], t[i]

        return pl.pallas_call(
            body, name=name + "_fwd", grid=(n_chunks,), in_specs=[x_spec, x_spec, x_spec, g_spec], out_specs=[x_spec, st_spec, t_spec],
            out_shape=[jax.ShapeDtypeStruct(qh.shape, F32), jax.ShapeDtypeStruct((DN_HEADS, n_chunks, DN_HD, DN_HD), F32),
                       jax.ShapeDtypeStruct((DN_HEADS, n_chunks, CHUNK, CHUNK), F32)],
            scratch_shapes=[pltpu.VMEM((DN_HEADS, DN_HD, DN_HD), F32)], compiler_params=_params(("arbitrary",)))(qh, kh, vh, gates)

    def bwd_call(qh, kh, vh, gates, states, inverses, do):
        n_chunks = qh.shape[1] // CHUNK
        x_spec, g_spec, st_spec, t_spec = specs(n_chunks, True)

        def body(q_ref, k_ref, v_ref, g_ref, st_ref, t_ref, do_ref, dq_ref, dk_ref, dv_ref, dg_ref, dstate):
            @pl.when(pl.program_id(0) == 0)
            def _():
                dstate[...] = jnp.zeros_like(dstate)

            t_saved = per_head(t_ref)
            _, vjp = jax.vjp(lambda q, k, v, gates, state: _chunk_fn(q, k, v, *_gate_columns(gates), state, t_saved=t_saved)[:2],
                             per_head(q_ref), per_head(k_ref), per_head(v_ref), g_ref[...], per_head(st_ref))
            dq, dk, dv, dgates, ds = vjp((per_head(do_ref), per_head(dstate)))
            dg_ref[...] = dgates
            for i in heads:
                dq_ref[i], dk_ref[i], dv_ref[i], dstate[i] = dq[i], dk[i], dv[i], ds[i]

        big = jax.ShapeDtypeStruct(qh.shape, F32)
        return pl.pallas_call(
            body, name=name + "_bwd", grid=(n_chunks,), in_specs=[x_spec, x_spec, x_spec, g_spec, st_spec, t_spec, x_spec],
            out_specs=[x_spec, x_spec, x_spec, g_spec], out_shape=[big, big, big, jax.ShapeDtypeStruct(gates.shape, F32)],
            scratch_shapes=[pltpu.VMEM((DN_HEADS, DN_HD, DN_HD), F32)],
            compiler_params=_params(("arbitrary",)))(qh, kh, vh, gates, states, inverses, do)

    @jax.custom_vjp
    def delta(qh, kh, vh, gates):
        return fwd_call(qh, kh, vh, gates)[0]

    def fwd(qh, kh, vh, gates):
        o, states, inverses = fwd_call(qh, kh, vh, gates)
        return o, (qh, kh, vh, gates, states, inverses)

    def bwd(res, do):
        return tuple(bwd_call(*res, do))

    delta.defvjp(fwd, bwd)
    return delta


def loss_head(y, target):
    s, d = y.shape
    tm = min(256, s)

    def body(y_ref, t_ref, l_ref, dy_ref):
        err = y_ref[...] - t_ref[...]
        dy_ref[...] = err * (1.0 / d)

        @pl.when(pl.program_id(0) == 0)
        def _():
            l_ref[...] = jnp.zeros_like(l_ref)

        l_ref[...] += 0.5 * jnp.sum(jnp.mean(jnp.square(err), axis=-1, keepdims=True), axis=0, keepdims=True)

    spec = pl.BlockSpec((tm, d), lambda i: (i, 0))
    return pl.pallas_call(
        body, name="loss_head", grid=(s // tm,), in_specs=[spec, spec], out_specs=[pl.BlockSpec((8, LANES), lambda i: (0, 0)), spec],
        out_shape=[jax.ShapeDtypeStruct((8, LANES), F32), jax.ShapeDtypeStruct(y.shape, F32)], compiler_params=_params(("arbitrary",)))(y, target)


def ada_project(c_all, w_ada, b_cols):
    tn = 256
    cols = w_ada.shape[2]

    def body(c_ref, w_ref, b_ref, ca_ref, o_ref):
        c_act = _silu(c_ref[...])
        ca_ref[...] = c_act
        o_ref[...] = _dg(c_act, w_ref[...], 1, 0) + b_ref[...]

    return pl.pallas_call(
        body, name="ada_project", grid=(DEPTH, cols // tn),
        in_specs=[pl.BlockSpec((N_DEV, D), lambda l, j: (0, 0)), pl.BlockSpec((None, D, tn), lambda l, j: (l, 0, j)),
                  pl.BlockSpec((None, 1, tn), lambda l, j: (l, 0, j))],
        out_specs=[pl.BlockSpec((N_DEV, D), lambda l, j: (0, 0)), pl.BlockSpec((None, N_DEV, tn), lambda l, j: (l, 0, j))],
        out_shape=[jax.ShapeDtypeStruct((N_DEV, D), F32), jax.ShapeDtypeStruct((DEPTH, N_DEV, cols), F32)],
        compiler_params=_params(("arbitrary", "arbitrary")))(c_all, w_ada, b_cols.reshape(DEPTH, 1, cols))


def sum_partials(parts, name):
    n_parts, r, c = parts.shape
    tr = _tile(r, 256, 16 if parts.dtype == BF16 else 8)

    def body(p_ref, o_ref):
        total = p_ref[0].astype(F32)
        for part in range(1, n_parts):
            total = total + p_ref[part].astype(F32)
        o_ref[...] = total

    return pl.pallas_call(
        body, name=name, grid=(r // tr,), in_specs=[pl.BlockSpec((n_parts, tr, c), lambda i: (0, i, 0))],
        out_specs=pl.BlockSpec((tr, c), lambda i: (i, 0)), out_shape=jax.ShapeDtypeStruct((r, c), F32),
        compiler_params=_params(("parallel",)))(parts)


def _adamw(w, g, m, v):
    m = ADAM_B1 * m + (1.0 - ADAM_B1) * g
    v = ADAM_B2 * v + (1.0 - ADAM_B2) * jnp.square(g)
    m_hat = m / (1.0 - ADAM_B1 ** ADAM_STEP)
    v_hat = v / (1.0 - ADAM_B2 ** ADAM_STEP)
    return -ADAM_LR * (m_hat / (jnp.sqrt(v_hat) + ADAM_EPS) + ADAM_WD * w), m, v


def adamw(w, g, m, v, name):
    l, r, c = w.shape
    tr = _tile(r, 256, 8)

    def body(w_ref, g_ref, m_ref, v_ref, d_ref, nm_ref, nv_ref):
        d_ref[...], nm_ref[...], nv_ref[...] = _adamw(w_ref[...], g_ref[...], m_ref[...], v_ref[...])

    spec = pl.BlockSpec((None, tr, c), lambda a, i: (a, i, 0))
    shape = jax.ShapeDtypeStruct(w.shape, F32)
    return pl.pallas_call(body, name=name, grid=(l, r // tr), in_specs=[spec] * 4, out_specs=[spec] * 3, out_shape=[shape] * 3,
                          compiler_params=_params(("parallel", "parallel")))(w, g, m, v)


def adamw_ada(c_act_t, dmod, w, m, v):
    l, r, c = w.shape
    tr = 256

    def body(c_ref, d_ref, w_ref, m_ref, v_ref, g_ref, dl_ref, nm_ref, nv_ref):
        g = hdot(c_ref[...], d_ref[...])
        g_ref[...] = g
        dl_ref[...], nm_ref[...], nv_ref[...] = _adamw(w_ref[...], g, m_ref[...], v_ref[...])

    spec = pl.BlockSpec((None, tr, c), lambda a, i: (a, i, 0))
    shape = jax.ShapeDtypeStruct(w.shape, F32)
    return pl.pallas_call(
        body, name="adamw_ada", grid=(l, r // tr),
        in_specs=[pl.BlockSpec((tr, LANES), lambda a, i: (i, 0)), pl.BlockSpec((None, LANES, c), lambda a, i: (a, 0, 0)), spec, spec, spec],
        out_specs=[spec] * 4, out_shape=[shape] * 4, compiler_params=_params(("parallel", "parallel")))(c_act_t, dmod, w, m, v)


def _place():
    x, y, c = lax.axis_index("x"), lax.axis_index("y"), lax.axis_index("c")
    return x, y, c


def _comm_call(body, name, ins, out_shapes, space, n_sems):
    n = len(ins)
    return pl.pallas_call(
        body, name=name, out_shape=out_shapes, in_specs=[pl.BlockSpec(memory_space=space)] * n,
        out_specs=[pl.BlockSpec(memory_space=space)] * n,
        scratch_shapes=[pltpu.SemaphoreType.DMA((n, n_sems)), pltpu.SemaphoreType.DMA((n, n_sems)), pltpu.SemaphoreType.DMA((n,))],
        compiler_params=pltpu.CompilerParams(vmem_limit_bytes=VMEM_LIMIT_BYTES))(*ins)


def all_gather(shards, name, in_vmem):
    n = len(shards)

    def body(*refs):
        x_refs, out_refs, (send_sems, recv_sems, local_sems) = refs[:n], refs[n:2 * n], refs[2 * n:]
        x, y, c = _place()
        me, sibling = (x, y, c), (x, y, 1 - c)
        chips = [(1 - x, y), (x, 1 - y), (1 - x, 1 - y)]

        def rows(a, px, py, pc):
            return out_refs[a].at[4 * px + 2 * py + pc]

        def copy(a, k, block, to, from_shard=False):
            return pltpu.make_async_remote_copy(
                src_ref=x_refs[a] if from_shard else rows(a, *block), dst_ref=rows(a, *block), send_sem=send_sems.at[a, k],
                recv_sem=recv_sems.at[a, k], device_id=to, device_id_type=MESH_IDS)

        arrays = range(n)
        mine = [pltpu.make_async_copy(x_refs[a], rows(a, *me), local_sems.at[a]) for a in arrays]
        first = [copy(a, 1 + j, me, (*chip, c), True) for j, chip in enumerate(chips) for a in arrays]
        first += [copy(a, 0, me, sibling, True) for a in arrays]
        for cp in mine + first:
            cp.start()
        passed = []
        for j, chip in enumerate(chips):
            for a in arrays:
                copy(a, 1 + j, (*chip, c), me).wait_recv()
                passed.append(copy(a, 4 + j, (*chip, c), sibling))
                passed[-1].start()
        for a in arrays:
            copy(a, 0, sibling, me).wait_recv()
            for j, chip in enumerate(chips):
                copy(a, 4 + j, (*chip, 1 - c), me).wait_recv()
        for cp in first + passed:
            cp.wait_send()
        for cp in mine:
            cp.wait()

    out_shapes = [jax.ShapeDtypeStruct((N_DEV,) + s.shape, s.dtype) for s in shards]
    return _comm_call(body, name, shards, out_shapes, pltpu.VMEM if in_vmem else pl.ANY, 7)


def all_gather_forwarding(shards, name):
    n = len(shards)

    def body(*refs):
        x_refs, out_refs, (send_sems, recv_sems, local_sems) = refs[:n], refs[n:2 * n], refs[2 * n:]
        x, y, c = _place()
        me = (x, y, c)

        def rows(a, dev, half):
            block = out_refs[a].at[4 * dev[0] + 2 * dev[1] + dev[2]]
            half_rows = shards[a].shape[0] // 2
            return block if half is None else block.at[pl.ds(half * half_rows, half_rows)]

        def copy(a, k, block, to, half=None, from_shard=False):
            return pltpu.make_async_remote_copy(
                src_ref=x_refs[a] if from_shard else rows(a, block, half), dst_ref=rows(a, block, half), send_sem=send_sems.at[a, k],
                recv_sem=recv_sems.at[a, k], device_id=to, device_id_type=MESH_IDS)

        def other_core(dev):
            return (dev[0], dev[1], 1 - c)

        sibling, x_nbr, y_nbr, diag = other_core(me), (1 - x, y, c), (x, 1 - y, c), (1 - x, 1 - y, c)
        arrays = range(n)
        mine = [pltpu.make_async_copy(x_refs[a], rows(a, me, None), local_sems.at[a]) for a in arrays]
        sent = [copy(a, k, me, to, from_shard=True) for k, to in ((1, x_nbr), (2, y_nbr), (0, sibling)) for a in arrays]
        for cp in mine + sent:
            cp.start()

        def on_arrival(k, block, half, passes):
            for a in arrays:
                copy(a, k, block, me, half).wait_recv()
                for k_out, to, half_out in passes:
                    sent.append(copy(a, k_out, block, to, half_out))
                    sent[-1].start()

        on_arrival(1, x_nbr, None, [(3, y_nbr, 0), (5, sibling, None)])
        on_arrival(2, y_nbr, None, [(4, x_nbr, 1), (6, sibling, None)])
        on_arrival(3, diag, 0, [(7, sibling, 0)])
        on_arrival(4, diag, 1, [(8, sibling, 1)])
        on_arrival(0, sibling, None, [])
        on_arrival(5, other_core(x_nbr), None, [])
        on_arrival(6, other_core(y_nbr), None, [])
        on_arrival(7, other_core(diag), 0, [])
        on_arrival(8, other_core(diag), 1, [])
        for cp in sent:
            cp.wait_send()
        for cp in mine:
            cp.wait()

    out_shapes = [jax.ShapeDtypeStruct((N_DEV,) + s.shape, s.dtype) for s in shards]
    return _comm_call(body, name, shards, out_shapes, pl.ANY, 9)


def scatter_to_sibling(slabs, name):
    n = len(slabs)

    def body(*refs):
        in_refs, out_refs, (send_sems, recv_sems, _) = refs[:n], refs[n:2 * n], refs[2 * n:]
        x, y, c = _place()
        copies = [pltpu.make_async_remote_copy(
            src_ref=in_refs[a].at[2 * chip + 1 - c], dst_ref=out_refs[a].at[chip], send_sem=send_sems.at[a, chip],
            recv_sem=recv_sems.at[a, chip], device_id=(x, y, 1 - c), device_id_type=MESH_IDS) for chip in range(4) for a in range(n)]
        for cp in copies:
            cp.start()
        for cp in copies:
            cp.wait_recv()
        for cp in copies:
            cp.wait_send()

    out_shapes = [jax.ShapeDtypeStruct((4,) + s.shape[1:], s.dtype) for s in slabs]
    return _comm_call(body, name, slabs, out_shapes, pl.ANY, 4)


def scatter_to_chips(slabs, name):
    n = len(slabs)

    def body(*refs):
        in_refs, out_refs, (send_sems, recv_sems, local_sems) = refs[:n], refs[n:2 * n], refs[2 * n:]
        x, y, c = _place()
        my_chip = 2 * x + y
        mine = [pltpu.make_async_copy(in_refs[a].at[my_chip], out_refs[a].at[my_chip], local_sems.at[a]) for a in range(n)]
        copies = [pltpu.make_async_remote_copy(
            src_ref=in_refs[a].at[2 * px + py], dst_ref=out_refs[a].at[my_chip], send_sem=send_sems.at[a, j], recv_sem=recv_sems.at[a, j],
            device_id=(px, py, c), device_id_type=MESH_IDS)
            for j, (px, py) in enumerate([(1 - x, y), (x, 1 - y), (1 - x, 1 - y)]) for a in range(n)]
        for cp in mine + copies:
            cp.start()
        for cp in copies:
            cp.wait_recv()
        for cp in copies:
            cp.wait_send()
        for cp in mine:
            cp.wait()

    return _comm_call(body, name, slabs, [jax.ShapeDtypeStruct(s.shape, s.dtype) for s in slabs], pl.ANY, 3)


def pair_sum(slabs, got, name):
    _, r, c = slabs.shape
    tr = _tile(r, 256, 16)

    def body(a_ref, b_ref, o_ref):
        o_ref[...] = (a_ref[...].astype(F32) + b_ref[...].astype(F32)).astype(BF16)

    return pl.pallas_call(
        body, name=name, grid=(4, r // tr),
        in_specs=[pl.BlockSpec((None, tr, c), lambda s, i: (2 * s + lax.axis_index("c"), i, 0)), pl.BlockSpec((None, tr, c), lambda s, i: (s, i, 0))],
        out_specs=pl.BlockSpec((None, tr, c), lambda s, i: (s, i, 0)), out_shape=jax.ShapeDtypeStruct((4, r, c), BF16),
        compiler_params=_params(("parallel", "parallel")))(slabs, got)


BIG = ("w_in", "w_oa", "w_ob", "w_out", "w_ff1", "w_ff2")
SMALL = ("a_log", "dt_bias", "sinks", "dn_norm_w", "ln1_g", "ln1_b", "b_ff1", "b_ff2", "ln2_g", "ln2_b")


def _pack_rows(arrs, width, unit):
    flat = jnp.concatenate([a.reshape(-1) for a in arrs])
    rows = -(-flat.shape[0] // (width * unit)) * unit
    return jnp.pad(flat, (0, rows * width - flat.shape[0])).reshape(rows, width)


def _split_flat(flat, like):
    out, off = [], 0
    for a in like:
        n = 1
        for dim in a.shape:
            n *= dim
        out.append(flat[off:off + n].reshape(a.shape))
        off += n
    return out


ROW_SHARDED = ("w_oa", "w_ob", "w_out", "w_ff2")


def _pack_shards(given):
    rows = jnp.concatenate([given[n].reshape(-1, D) for n in ROW_SHARDED]).astype(BF16)
    return [given["w_in"].astype(BF16).reshape(DEPTH * D, -1), rows, given["w_ff1"].astype(BF16).reshape(DEPTH * D, -1)]


def _unpack_weights(g_in, g_rows, g_ff1):
    w_in = g_in.reshape(N_DEV, DEPTH, D, -1)
    w_ff1 = g_ff1.reshape(N_DEV, DEPTH, D, -1)
    layers = []
    for l in range(DEPTH):
        full_in = w_in[:, l].transpose(1, 0, 2).reshape(D, -1)
        small = jnp.pad(full_in[:, IN_AT["small"]:IN_AT["ga"]], ((0, 0), (0, LANES + IN_AT["small"] - IN_AT["ga"])))
        groups = [jnp.concatenate([full_in[:, :QKV_W], small], axis=1)] + [full_in[:, IN_AT[g]:IN_AT[g] + D] for g in IN_GROUPS[1:]]
        lay, off = dict(w_in=tuple(groups), w_ff1=w_ff1[:, l].transpose(1, 0, 2).reshape(D, D_FF)), 0
        for n in ROW_SHARDED:
            per = (D_FF if n == "w_ff2" else D) // N_DEV
            lay[n] = g_rows[:, off + l * per:off + (l + 1) * per].reshape(per * N_DEV, D)
            off += DEPTH * per
        layers.append(lay)
    return layers


def _pack_grads(grads):
    def in_order(qkvs, dq, dk, dv, z, ga, gb):
        return jnp.concatenate([qkvs[:, :QKV_W], dq, dk, dv, z, qkvs[:, QKV_W:QKV_W + IN_AT["ga"] - IN_AT["small"]], ga, gb], axis=1)

    w_in = jnp.stack([in_order(*lay["w_in"]) for lay in grads])
    s_in = w_in.reshape(DEPTH, D, N_DEV, -1).transpose(2, 0, 1, 3).reshape(N_DEV, DEPTH * D, -1)
    rows = []
    for n in ROW_SHARDED:
        w = jnp.stack([lay[n] for lay in grads])
        rows.append(w.reshape(DEPTH, N_DEV, -1, D).transpose(1, 0, 2, 3).reshape(N_DEV, -1, D))
    w = jnp.stack([lay["w_ff1"] for lay in grads])
    s_ff1 = w.reshape(DEPTH, D, N_DEV, -1).transpose(2, 0, 1, 3).reshape(N_DEV, DEPTH * D, -1)
    return [s_in, jnp.concatenate(rows, axis=1), s_ff1]


def _layer(ops, x, u, mod, next_mod, sm, conv_w, w):
    s = x.shape[0]
    sh1, sc1, gt1, sh2, sc2, gt2 = (mod[:, i * D:(i + 1) * D] for i in range(6))
    qkvs, dq, dk, dv, z, ga, gb = ops["in_proj"](u, w["w_in"])
    q, k, v, small = qkvs[:, :D], qkvs[:, D:D + 256], qkvs[:, D + 256:QKV_W], qkvs[:, QKV_W:]
    qh = q.reshape(s, ATT_HEADS, ATT_HD).transpose(1, 0, 2)
    kh = k.reshape(s, ATT_KV, ATT_HD).transpose(1, 0, 2)
    vh = v.reshape(s, ATT_KV, ATT_HD).transpose(1, 0, 2)
    sink = jnp.broadcast_to(sm["sinks"].reshape(ATT_KV, ATT_GROUP, 1, 1), (ATT_KV, ATT_GROUP, WINDOW, 1)).reshape(ATT_KV, ATT_GROUP * WINDOW, 1)
    attn = ops["swa"](qh, kh, vh, sink).transpose(1, 0, 2).reshape(s, ATT_HEADS * ATT_HD)
    y_a = ops["mm_oa"](attn, w["w_oa"])
    qn = ops["prep_q"](dq, conv_w[:, :D])
    kn = ops["prep_k"](dk, conv_w[:, D:2 * D])
    vn = ops["prep_v"](dv, conv_w[:, 2 * D:])
    a_vec = jnp.pad(sm["a_log"], ((0, 0), (DN_HEADS, LANES - 2 * DN_HEADS)))
    b_vec = jnp.pad(sm["dt_bias"], ((0, 0), (DN_HEADS, LANES - 2 * DN_HEADS)))
    (gates,) = ops["gates"](small, a_vec, b_vec)
    o = ops["delta"](qn, kn, vn, gates)
    (og,) = ops["gdn_post"](o, z, sm["dn_norm_w"])
    y_b = ops["mm_ob"](og, w["w_ob"])
    (mix,) = ops["mix"](ga, gb, y_a, y_b)
    mixed = ops["mm_out"](mix, w["w_out"])
    x1, u2 = ops["ln1"](x, mixed, gt1, sm["ln1_g"], sm["ln1_b"], sc2, sh2)
    (h,) = ops["relu2"](ops["mm_ff1"](u2, w["w_ff1"]), sm["b_ff1"])
    f = ops["mm_ff2"](h, w["w_ff2"])
    if next_mod is None:
        return ops["ln2"](x1, f, gt2, sm["b_ff2"], sm["ln2_g"], sm["ln2_b"])[0], None
    return ops["ln2"](x1, f, gt2, sm["b_ff2"], sm["ln2_g"], sm["ln2_b"], next_mod[:, D:2 * D], next_mod[:, :D])


def _make_ops(l):
    t = f"l{l}_"
    last = l == DEPTH - 1
    return dict(
        in_proj=make_in_proj(t + "in_proj"), swa=make_swa(t + "swa"),
        mm_oa=make_mm(t + "mm_oa", BF16), mm_ob=make_mm(t + "mm_ob", BF16), mm_out=make_mm(t + "mm_out"),
        mm_ff1=make_mm(t + "mm_ff1", BF16), mm_ff2=make_mm(t + "mm_ff2"),
        prep_q=make_gdn_prep(t + "prep_q", "q"), prep_k=make_gdn_prep(t + "prep_k", "k"), prep_v=make_gdn_prep(t + "prep_v", "v"),
        gates=make_rowwise(_gates_fn, t + "gates", "t", "cc", "t"), delta=make_delta(t + "delta"),
        gdn_post=make_rowwise(_gdn_post_fn, t + "gdn_post", "ht", "s", "t", nc=DN_HEADS, tm=1024),
        mix=make_rowwise(_mix_fn, t + "mix", "tttt", "", "t"),
        ln1=make_rowwise(_ln1_fn, t + "ln1", "tt", "ccccc", "tt"),
        ln2=make_rowwise(_ln2_last_fn, t + "ln2", "tt", "cccc", "t") if last else make_rowwise(_ln2_fn, t + "ln2", "tt", "cccccc", "tt"),
        relu2=make_rowwise(_relu2_fn, t + "relu2", "t", "c", "t", nc=4, tm=512))


def kernel(x, c, w_ada, b_ada, w_in, conv_w, a_log, dt_bias, sinks, dn_norm_w, w_oa, w_ob, w_out, ln1_g, ln1_b, w_ff1, b_ff1, w_ff2, b_ff2, ln2_g, ln2_b, loss_target, m_w_ada, m_b_ada, m_w_in, m_conv_w, m_a_log, m_dt_bias, m_sinks, m_dn_norm_w, m_w_oa, m_w_ob, m_w_out, m_ln1_g, m_ln1_b, m_w_ff1, m_b_ff1, m_w_ff2, m_b_ff2, m_ln2_g, m_ln2_b, v_w_ada, v_b_ada, v_w_in, v_conv_w, v_a_log, v_dt_bias, v_sinks, v_dn_norm_w, v_w_oa, v_w_ob, v_w_out, v_ln1_g, v_ln1_b, v_w_ff1, v_b_ff1, v_w_ff2, v_b_ff2, v_ln2_g, v_ln2_b):
    given = dict(locals())
    me = 4 * lax.axis_index("x") + 2 * lax.axis_index("y") + lax.axis_index("c")
    conv_cols = conv_w.shape[2]

    gathered = all_gather([_pack_rows([c, conv_w], LANES, 8)], "gather_c_conv", True)[0].reshape(N_DEV, -1)
    c_all = gathered[:, :D]
    conv_full = gathered[:, D:D + DEPTH * CONV_K * conv_cols].reshape(N_DEV, DEPTH, CONV_K, conv_cols).transpose(1, 2, 0, 3).reshape(DEPTH, CONV_K, -1)

    b_cols = lax.dynamic_slice_in_dim(b_ada, me * ADA_COLS, ADA_COLS, axis=1)
    c_act_all, mod_cols = ada_project(c_all, w_ada, b_cols)
    mod_all = all_gather([mod_cols.reshape(-1, LANES)], "gather_mod", True)[0].reshape(N_DEV, DEPTH, N_DEV, ADA_COLS)
    mods = lax.dynamic_index_in_dim(mod_all, me, axis=2, keepdims=False).transpose(1, 0, 2).reshape(DEPTH, 6 * D)

    weights = _unpack_weights(*all_gather_forwarding(_pack_shards(given), "gather_weights"))

    small = {n: given[n] for n in SMALL}
    ops = [_make_ops(l) for l in range(DEPTH)]

    modulate0 = make_rowwise(_modulate_fn, "modulate0", "t", "cc", "t")

    def forward(x0, mods, small, conv_full, weights):
        (u,) = modulate0(x0, mods[:1, D:2 * D], mods[:1, :D])
        h = x0
        for l in range(DEPTH):
            next_mod = mods[l + 1:l + 2] if l + 1 < DEPTH else None
            h, u = _layer(ops[l], h, u, mods[l:l + 1], next_mod, {n: a[l:l + 1] for n, a in small.items()}, conv_full[l], weights[l])
        return h

    y, vjp = jax.vjp(forward, x[0], mods, small, conv_full, weights)
    loss_tile, dy = loss_head(y, loss_target[0])
    dx, d_mods, d_small, d_conv, d_weights = vjp(dy)
    loss = lax.psum(loss_tile[0, 0], AXES)

    slabs = _pack_grads(d_weights)
    from_sibling = scatter_to_sibling(slabs, "scatter_sibling")
    per_chip = [pair_sum(a, b, f"pair_sum_{i}") for i, (a, b) in enumerate(zip(slabs, from_sibling))]
    g_in, g_rows, g_ff1 = [sum_partials(p, f"sum_big_{i}") for i, p in enumerate(scatter_to_chips(per_chip, "scatter_chips"))]
    grad, off = dict(w_in=g_in.reshape(w_in.shape), w_ff1=g_ff1.reshape(w_ff1.shape)), 0
    for n in ROW_SHARDED:
        rows = given[n].shape[0] * given[n].shape[1]
        grad[n] = g_rows[off:off + rows].reshape(given[n].shape)
        off += rows
    delta, new_m, new_v = {}, {}, {}
    for n in BIG:
        delta[n], new_m[n], new_v[n] = adamw(given[n], grad[n], given["m_" + n], given["v_" + n], "adamw_" + n)

    partial = [d_small[n] for n in SMALL] + [d_mods, d_conv]
    (parts,) = all_gather([_pack_rows(partial, LANES, 8)], "gather_small_grads", True)
    mods_at = sum(d_small[n].size for n in SMALL)
    d_mods_all = parts.reshape(N_DEV, -1)[:, mods_at:mods_at + DEPTH * 6 * D].reshape(N_DEV, DEPTH, 6 * D)
    total = _split_flat(sum_partials(parts, "sum_small").reshape(-1), partial)
    for n, g in zip(SMALL, total):
        grad[n] = g
    grad["b_ada"] = total[len(SMALL)]
    grad["conv_w"] = lax.dynamic_slice_in_dim(total[len(SMALL) + 1], me * conv_cols, conv_cols, axis=2)
    names = SMALL + ("b_ada", "conv_w")
    packed = [_pack_rows([src[p + n] for n in names], LANES, 8)[None] for src, p in ((given, ""), (grad, ""), (given, "m_"), (given, "v_"))]
    outs = adamw(*packed, "adamw_small")
    for res, o in zip((delta, new_m, new_v), outs):
        for n, a in zip(names, _split_flat(o.reshape(-1), [given[n] for n in names])):
            res[n] = a

    dmod_mine = lax.dynamic_slice_in_dim(d_mods_all, me * ADA_COLS, ADA_COLS, axis=2).transpose(1, 0, 2)
    pad = LANES - N_DEV
    grad["w_ada"], delta["w_ada"], new_m["w_ada"], new_v["w_ada"] = adamw_ada(
        jnp.pad(c_act_all.T, ((0, 0), (0, pad))), jnp.pad(dmod_mine, ((0, 0), (0, pad), (0, 0))), w_ada, m_w_ada, v_w_ada)

    order = ("w_ada", "b_ada", "w_in", "conv_w", "a_log", "dt_bias", "sinks", "dn_norm_w", "w_oa", "w_ob", "w_out", "ln1_g", "ln1_b",
             "w_ff1", "b_ff1", "w_ff2", "b_ff2", "ln2_g", "ln2_b")
    return (loss, dx[None], *[grad[n] for n in order], *[delta[n] for n in order], *[new_m[n] for n in order], *[new_v[n] for n in order])
```

```python
import functools

import jax
import jax.numpy as jnp
from jax import lax
from jax.experimental import pallas as pl
from jax.experimental.pallas import tpu as pltpu

F32 = jnp.float32
BF16 = jnp.bfloat16

D = 1024
DEPTH = 4
N_DEV = 8
ATT_HEADS, ATT_KV, ATT_GROUP, ATT_HD, WINDOW = 16, 4, 4, 64, 128
DN_HEADS, DN_HD, CONV_K, CHUNK = 8, 128, 4, 64
D_FF = 4096
ADA_COLS = 6 * D // N_DEV
ALPHA = (2 * DEPTH) ** 0.25
LN_EPS = 1e-5
RMS_EPS = 1e-6
ADAM_LR, ADAM_B1, ADAM_B2, ADAM_EPS, ADAM_WD, ADAM_STEP = 0.001, 0.9, 0.999, 1e-08, 0.01, 10
AXES = ("x", "y", "c")
MESH_IDS = pl.DeviceIdType.MESH
VMEM_LIMIT_BYTES = 48 * 1024 * 1024
LANES = 128
HIGHEST = lax.Precision.HIGHEST

IN_SPLITS = (("q", 1024), ("k", 256), ("v", 256), ("dq", 1024), ("dk", 1024), ("dv", 1024), ("z", 1024),
             ("small", 16), ("ga", 1024), ("gb", 1024))
IN_AT = {name: sum(w for _, w in IN_SPLITS[:i]) for i, (name, _) in enumerate(IN_SPLITS)}
IN_GROUPS = ("qkvs", "dq", "dk", "dv", "z", "ga", "gb")
QKV_W = 1536
QKVS_W = QKV_W + LANES


def _params(sem=None):
    return pltpu.CompilerParams(dimension_semantics=sem, vmem_limit_bytes=VMEM_LIMIT_BYTES)


def _tile(n, pref, unit):
    if n <= pref:
        return n
    t = (pref // unit) * unit
    while t > unit and n % t:
        t -= unit
    assert n % t == 0, (n, pref, unit)
    return t


def _dg(a, b, ca, cb):
    return lax.dot_general(a.astype(BF16), b.astype(BF16), (((ca,), (cb,)), ((), ())), preferred_element_type=F32)


@jax.custom_vjp
def bdot_nn(a, b):
    return _dg(a, b, 1, 0)


def _bdot_nn_fwd(a, b):
    return _dg(a, b, 1, 0), (a, b)


def _bdot_nn_bwd(res, g):
    a, b = res
    return _dg(g, b, 1, 1).astype(a.dtype), _dg(a, g, 0, 0).astype(b.dtype)


bdot_nn.defvjp(_bdot_nn_fwd, _bdot_nn_bwd)


@jax.custom_vjp
def bdot_nt(a, b):
    return _dg(a, b, 1, 1)


def _bdot_nt_fwd(a, b):
    return _dg(a, b, 1, 1), (a, b)


def _bdot_nt_bwd(res, g):
    a, b = res
    return _dg(g, b, 1, 0).astype(a.dtype), _dg(g, a, 0, 0).astype(b.dtype)


bdot_nt.defvjp(_bdot_nt_fwd, _bdot_nt_bwd)


@jax.custom_vjp
def bdot_tn(a, b):
    return _dg(a, b, 0, 0)


def _bdot_tn_fwd(a, b):
    return _dg(a, b, 0, 0), (a, b)


def _bdot_tn_bwd(res, g):
    a, b = res
    return _dg(b, g, 1, 1).astype(a.dtype), _dg(a, g, 1, 0).astype(b.dtype)


bdot_tn.defvjp(_bdot_tn_fwd, _bdot_tn_bwd)


def _hdg(a, b, ca, cb):
    a_hi, b_hi = a.astype(BF16), b.astype(BF16)
    a_lo, b_lo = (a - a_hi.astype(F32)).astype(BF16), (b - b_hi.astype(F32)).astype(BF16)

    def dot(x, y):
        return lax.dot_general(x, y, (((ca,), (cb,)), ((), ())), preferred_element_type=F32)

    return dot(a_hi, b_hi) + (dot(a_hi, b_lo) + dot(a_lo, b_hi))


@jax.custom_vjp
def hdot(a, b):
    return _hdg(a, b, 1, 0)


def _hdot_fwd(a, b):
    return _hdg(a, b, 1, 0), (a, b)


def _hdot_bwd(res, g):
    a, b = res
    return _hdg(g, b, 1, 1), _hdg(a, g, 0, 0)


hdot.defvjp(_hdot_fwd, _hdot_bwd)


def matmul(a, b, mode, out_dtype, name, acc=None):
    if mode == "nn":
        (m, k), (k2, n) = a.shape, b.shape
    elif mode == "nt":
        (m, k), (n, k2) = a.shape, b.shape
    else:
        (k, m), (k2, n) = a.shape, b.shape
    assert k == k2, (a.shape, b.shape, mode)
    def pick(n_, pref):
        t = _tile(n_, pref, LANES)
        return n_ if t < 2 * LANES and n_ <= 2048 else t

    tm, tn, tk = pick(m, 1024), pick(n, 512), pick(k, 1024)
    nk = k // tk
    a_spec = pl.BlockSpec((tk, tm), lambda i, j, kk: (kk, i)) if mode == "tn" else pl.BlockSpec((tm, tk), lambda i, j, kk: (i, kk))
    b_spec = pl.BlockSpec((tn, tk), lambda i, j, kk: (j, kk)) if mode == "nt" else pl.BlockSpec((tk, tn), lambda i, j, kk: (kk, j))
    o_spec = pl.BlockSpec((tm, tn), lambda i, j, kk: (i, j))
    ca, cb = {"nn": (1, 0), "nt": (1, 1), "tn": (0, 0)}[mode]

    def body(*refs):
        a_ref, b_ref = refs[:2]
        c_ref = None if acc is None else refs[2]
        o_ref = refs[2 if acc is None else 3]
        part = _dg(a_ref[...], b_ref[...], ca, cb)
        if nk == 1:
            o_ref[...] = (part if c_ref is None else part + c_ref[...].astype(F32)).astype(out_dtype)
            return
        acc_ref = refs[-1]
        kk = pl.program_id(2)

        @pl.when(kk == 0)
        def _():
            acc_ref[...] = part if c_ref is None else part + c_ref[...].astype(F32)

        @pl.when(jnp.logical_and(kk > 0, kk < nk - 1))
        def _():
            acc_ref[...] += part

        @pl.when(kk == nk - 1)
        def _():
            o_ref[...] = (acc_ref[...] + part).astype(out_dtype)

    ins, in_specs = [a, b], [a_spec, b_spec]
    if acc is not None:
        ins.append(acc)
        in_specs.append(o_spec)
    return pl.pallas_call(
        body, name=name, grid=(m // tm, n // tn, nk), in_specs=in_specs, out_specs=o_spec,
        out_shape=jax.ShapeDtypeStruct((m, n), out_dtype), scratch_shapes=[pltpu.VMEM((tm, tn), F32)] if nk > 1 else [],
        compiler_params=_params(("parallel", "parallel", "arbitrary")))(*ins)


def make_mm(name, out_dtype=F32):
    @jax.custom_vjp
    def mm(a, w):
        return matmul(a, w, "nn", out_dtype, name + "_fwd")

    def fwd(a, w):
        return mm(a, w), (a, w)

    def bwd(res, g):
        a, w = res
        return matmul(g, w, "nt", a.dtype, name + "_da"), matmul(a, g, "tn", w.dtype, name + "_dw")

    mm.defvjp(fwd, bwd)
    return mm


def make_in_proj(name):
    @jax.custom_vjp
    def in_proj(u, ws):
        return tuple(matmul(u, w, "nn", BF16, f"{name}_fwd_{g}") for g, w in zip(IN_GROUPS, ws))

    def fwd(u, ws):
        return in_proj(u, ws), (u, ws)

    def bwd(res, gs):
        u, ws = res
        du = None
        for idx, (g, w, dy) in enumerate(zip(IN_GROUPS, ws, gs)):
            last = idx == len(ws) - 1
            du = matmul(dy, w, "nt", u.dtype if last else F32, f"{name}_du_{g}", acc=du)
        dws = tuple(matmul(u, dy, "tn", w.dtype, f"{name}_dw_{g}") for g, w, dy in zip(IN_GROUPS, ws, gs))
        return du, dws

    in_proj.defvjp(fwd, bwd)
    return in_proj


def make_rowwise(fn, name, tile_kinds, param_kinds, out_kinds, nc=1, tm=256):
    n_t, n_p, n_o = len(tile_kinds), len(param_kinds), len(out_kinds)

    def width(a, kind):
        if kind == "h":
            return a.shape[2]
        return a.shape[1] if kind == "s" else a.shape[1] // nc

    def spec(kind, w, rows):
        if kind == "t":
            return pl.BlockSpec((rows, w), lambda j, i: (i, j))
        if kind == "h":
            return pl.BlockSpec((None, rows, w), lambda j, i: (j, i, 0))
        if kind == "c":
            return pl.BlockSpec((1, w), lambda j, i: (0, j))
        return pl.BlockSpec((1, w), lambda j, i: (0, 0))

    def full_shape(kind, w, s):
        return (s, w * nc) if kind == "t" else (nc, s, w)

    def plan(tiles, params):
        s = tiles[0].shape[0] if tile_kinds[0] == "t" else tiles[0].shape[1]
        rows = min(tm, s)
        t_w = [width(a, kd) for a, kd in zip(tiles, tile_kinds)]
        p_w = [width(a, kd) for a, kd in zip(params, param_kinds)]
        t_s = [jax.ShapeDtypeStruct((rows, w), a.dtype) for a, w in zip(tiles, t_w)]
        p_s = [jax.ShapeDtypeStruct((1, w), a.dtype) for a, w in zip(params, p_w)]
        o_s = jax.eval_shape(fn, *t_s, *p_s)
        return s, rows, t_w, p_w, o_s

    def fwd_call(*args):
        tiles, params = args[:n_t], args[n_t:]
        s, rows, t_w, p_w, o_s = plan(tiles, params)

        def body(*refs):
            ins, outs = refs[:n_t + n_p], refs[n_t + n_p:]
            res = fn(*[r[...] for r in ins])
            for o_ref, val in zip(outs, res):
                o_ref[...] = val

        in_specs = [spec(kd, w, rows) for kd, w in zip(tile_kinds, t_w)] + [spec(kd, w, rows) for kd, w in zip(param_kinds, p_w)]
        return pl.pallas_call(
            body, name=name + "_fwd", grid=(nc, s // rows), in_specs=in_specs,
            out_specs=[spec(kd, o.shape[1], rows) for kd, o in zip(out_kinds, o_s)],
            out_shape=[jax.ShapeDtypeStruct(full_shape(kd, o.shape[1], s), o.dtype) for kd, o in zip(out_kinds, o_s)],
            compiler_params=_params(("parallel", "parallel")))(*args)

    def bwd_call(args, douts):
        tiles, params = args[:n_t], args[n_t:]
        s, rows, t_w, p_w, o_s = plan(tiles, params)

        def body(*refs):
            ins = refs[:n_t + n_p]
            dos = refs[n_t + n_p:n_t + n_p + n_o]
            dts = refs[n_t + n_p + n_o:n_t + n_p + n_o + n_t]
            dps = refs[n_t + n_p + n_o + n_t:]
            j, i = pl.program_id(0), pl.program_id(1)
            _, vjp = jax.vjp(lambda *a: tuple(fn(*a)), *[r[...] for r in ins])
            grads = vjp(tuple(r[...] for r in dos))
            for r, g in zip(dts, grads[:n_t]):
                r[...] = g.astype(r.dtype)
            for r, g, kd in zip(dps, grads[n_t:], param_kinds):
                first = (i == 0) if kd == "c" else jnp.logical_and(i == 0, j == 0)

                @pl.when(first)
                def _(r=r):
                    r[...] = jnp.zeros_like(r)

                r[...] += g.astype(F32)

        in_specs = ([spec(kd, w, rows) for kd, w in zip(tile_kinds, t_w)] + [spec(kd, w, rows) for kd, w in zip(param_kinds, p_w)]
                    + [spec(kd, o.shape[1], rows) for kd, o in zip(out_kinds, o_s)])
        out_specs = [spec(kd, w, rows) for kd, w in zip(tile_kinds, t_w)] + [spec(kd, w, rows) for kd, w in zip(param_kinds, p_w)]
        out_shape = [jax.ShapeDtypeStruct(a.shape, a.dtype) for a in tiles] + [jax.ShapeDtypeStruct(a.shape, F32) for a in params]
        return pl.pallas_call(
            body, name=name + "_bwd", grid=(nc, s // rows), in_specs=in_specs, out_specs=out_specs, out_shape=out_shape,
            compiler_params=_params(("arbitrary", "arbitrary")))(*args, *douts)

    @jax.custom_vjp
    def op(*args):
        return tuple(fwd_call(*args))

    def op_fwd(*args):
        return op(*args), args

    def op_bwd(args, douts):
        return tuple(bwd_call(args, douts))

    op.defvjp(op_fwd, op_bwd)
    return op


def _sigmoid(x):
    return 1.0 / (1.0 + jnp.exp(-x))


def _silu(x):
    return x * _sigmoid(x)


def _softplus(x):
    return jnp.maximum(x, 0.0) + jnp.log(1.0 + jnp.exp(-jnp.abs(x)))


def _layer_norm(h, g, b):
    mu = jnp.mean(h, axis=-1, keepdims=True)
    var = jnp.mean(jnp.square(h - mu), axis=-1, keepdims=True)
    return (h - mu) * lax.rsqrt(var + LN_EPS) * g + b


def _modulate_fn(x, sc, sh):
    return ((x * (1.0 + sc) + sh).astype(BF16),)


def _gates_fn(x, a_vec, b_vec):
    x = x.astype(F32)
    lane = lax.broadcasted_iota(jnp.int32, x.shape, 1)
    beta = _sigmoid(x)
    g = -jnp.exp(a_vec) * _softplus(x + b_vec)
    return (jnp.where(lane < DN_HEADS, beta, jnp.where(lane < 2 * DN_HEADS, g, 0.0)),)


def _gdn_post_fn(o, z, nw):
    o = o * lax.rsqrt(jnp.mean(jnp.square(o), axis=-1, keepdims=True) + RMS_EPS) * nw
    return ((o * _silu(z.astype(F32))).astype(BF16),)


def _mix_fn(ga, gb, ya, yb):
    ga, gb, ya, yb = (t.astype(F32) for t in (ga, gb, ya, yb))
    return ((_sigmoid(ga) * ya + _sigmoid(gb) * yb).astype(BF16),)


def _ln1_fn(x, mixed, gt, g, b, sc, sh):
    y = _layer_norm(ALPHA * x + (1.0 + gt) * mixed, g, b)
    return y, _modulate_fn(y, sc, sh)[0]


def _ln2_last_fn(x, f, gt, bf, g, b):
    return (_layer_norm(ALPHA * x + (1.0 + gt) * (f + bf), g, b),)


def _ln2_fn(x, f, gt, bf, g, b, sc, sh):
    (y,) = _ln2_last_fn(x, f, gt, bf, g, b)
    return y, _modulate_fn(y, sc, sh)[0]


def _relu2_fn(h, b):
    return (jnp.square(jnp.maximum(h.astype(F32) + b, 0.0)).astype(BF16),)


def _each(f, *lists):
    return [f(*xs) for xs in zip(*lists)]


def _swa_blocks(q4, kp, kc, vp, vc, sink, first):
    rows = ATT_GROUP * WINDOW
    qi = lax.broadcasted_iota(jnp.int32, (rows, 2 * WINDOW), 0) & (WINDOW - 1)
    si = lax.broadcasted_iota(jnp.int32, (rows, 2 * WINDOW), 1)
    diff = qi + WINDOW - si
    valid = (diff >= 0) & (diff < WINDOW) & (si >= jnp.where(first, WINDOW, 0))
    q = _each(lambda a: a.reshape(rows, ATT_HD), q4)
    k = _each(lambda a, b: jnp.concatenate([a, b], axis=0), kp, kc)
    v = _each(lambda a, b: jnp.concatenate([a, b], axis=0), vp, vc)
    s = _each(lambda q, k: jnp.where(valid, bdot_nt(q, k) * (ATT_HD ** -0.5), -jnp.inf), q, k)
    m = _each(lambda s, sink: lax.stop_gradient(jnp.maximum(jnp.max(s, axis=-1, keepdims=True), sink)), s, sink)
    p = _each(lambda s, m: jnp.exp(s - m), s, m)
    pn = _each(lambda p, sink, m: p / (jnp.sum(p, axis=-1, keepdims=True) + jnp.exp(sink - m)), p, sink, m)
    return _each(lambda pn, v: bdot_nn(pn, v).reshape(ATT_GROUP, WINDOW, ATT_HD).astype(BF16), pn, v)


def make_swa(name):
    heads = range(ATT_KV)

    def specs():
        q_spec = pl.BlockSpec((ATT_HEADS, WINDOW, ATT_HD), lambda n: (0, n, 0))
        cur = pl.BlockSpec((ATT_KV, WINDOW, ATT_HD), lambda n: (0, n, 0))
        prev = pl.BlockSpec((ATT_KV, WINDOW, ATT_HD), lambda n: (0, jnp.maximum(n - 1, 0), 0))
        sink = pl.BlockSpec((ATT_KV, ATT_GROUP * WINDOW, 1), lambda n: (0, 0, 0))
        return q_spec, cur, prev, sink

    def group(ref, h):
        return ref.at[pl.ds(h * ATT_GROUP, ATT_GROUP)]

    def load(q_ref, kp_ref, kc_ref, vp_ref, vc_ref, s_ref):
        return [[group(q_ref, h)[...] for h in heads]] + [[r[h] for h in heads] for r in (kp_ref, kc_ref, vp_ref, vc_ref, s_ref)]

    def fwd_call(q, k, v, sink):
        s = q.shape[1]
        q_spec, cur, prev, sink_spec = specs()

        def body(q_ref, kp_ref, kc_ref, vp_ref, vc_ref, s_ref, o_ref):
            o = _swa_blocks(*load(q_ref, kp_ref, kc_ref, vp_ref, vc_ref, s_ref), pl.program_id(0) == 0)
            for h in heads:
                group(o_ref, h)[...] = o[h]

        return pl.pallas_call(
            body, name=name + "_fwd", grid=(s // WINDOW,), in_specs=[q_spec, prev, cur, prev, cur, sink_spec],
            out_specs=q_spec, out_shape=jax.ShapeDtypeStruct(q.shape, BF16),
            compiler_params=_params(("parallel",)))(q, k, k, v, v, sink)

    def bwd_call(q, k, v, sink, do):
        s = q.shape[1]
        q_spec, cur, prev, sink_spec = specs()

        def body(q_ref, kp_ref, kc_ref, vp_ref, vc_ref, s_ref, do_ref, dq_ref, dkp_ref, dkc_ref, dvp_ref, dvc_ref, ds_ref):
            first = pl.program_id(0) == 0
            _, vjp = jax.vjp(lambda *a: _swa_blocks(*a, first), *load(q_ref, kp_ref, kc_ref, vp_ref, vc_ref, s_ref))
            dq, dkp, dkc, dvp, dvc, ds = vjp([group(do_ref, h)[...] for h in heads])

            @pl.when(first)
            def _():
                ds_ref[...] = jnp.zeros_like(ds_ref)

            for h in heads:
                group(dq_ref, h)[...] = dq[h]
                dkp_ref[h], dkc_ref[h], dvp_ref[h], dvc_ref[h] = dkp[h], dkc[h], dvp[h], dvc[h]
                ds_ref[h] += ds[h]

        kv = jax.ShapeDtypeStruct(k.shape, k.dtype)
        return pl.pallas_call(
            body, name=name + "_bwd", grid=(s // WINDOW,), in_specs=[q_spec, prev, cur, prev, cur, sink_spec, q_spec],
            out_specs=[q_spec, cur, cur, cur, cur, sink_spec],
            out_shape=[jax.ShapeDtypeStruct(q.shape, q.dtype), kv, kv, kv, kv, jax.ShapeDtypeStruct(sink.shape, F32)],
            compiler_params=_params(("arbitrary",)))(q, k, k, v, v, sink, do)

    @jax.custom_vjp
    def swa(q, k, v, sink):
        return fwd_call(q, k, v, sink)

    def fwd(q, k, v, sink):
        return swa(q, k, v, sink), (q, k, v, sink)

    def bwd(res, do):
        q, k, v, sink = res
        dq, dkp, dkc, dvp, dvc, ds = bwd_call(q, k, v, sink, do)

        def fold(cur, prev):
            shifted = jnp.concatenate([prev[:, WINDOW:], jnp.zeros_like(prev[:, :WINDOW])], axis=1)
            return (cur.astype(F32) + shifted.astype(F32)).astype(cur.dtype)

        return dq, fold(dkc, dkp), fold(dvc, dvp), ds

    swa.defvjp(fwd, bwd)
    return swa


def _shift_down(x, d, row):
    return x if d == 0 else jnp.where(row >= d, pltpu.roll(x, d, 0), 0.0)


def _shift_up(x, d, row):
    n = x.shape[0]
    return x if d == 0 else jnp.where(row < n - d, pltpu.roll(x, n - d, 0), 0.0)


def _conv_pre(x, w, row):
    return sum(w[j:j + 1, :] * _shift_down(x, CONV_K - 1 - j, row) for j in range(CONV_K))


def _prep_post(pre, kind):
    act = _silu(pre)
    if kind == "v":
        return act
    r = lax.rsqrt(jnp.sum(jnp.square(act), axis=-1, keepdims=True) + RMS_EPS)
    return act * r * (DN_HD ** -0.5 if kind == "q" else 1.0)


def make_gdn_prep(name, kind):
    def fwd_call(x, w):
        s = x.shape[0]

        def body(x_ref, w_ref, o_ref):
            row = lax.broadcasted_iota(jnp.int32, (s, DN_HD), 0)
            o_ref[...] = _prep_post(_conv_pre(x_ref[...].astype(F32), w_ref[...], row), kind)

        return pl.pallas_call(
            body, name=name + "_fwd", grid=(DN_HEADS,),
            in_specs=[pl.BlockSpec((s, DN_HD), lambda j: (0, j)), pl.BlockSpec((CONV_K, DN_HD), lambda j: (0, j))],
            out_specs=pl.BlockSpec((None, s, DN_HD), lambda j: (j, 0, 0)),
            out_shape=jax.ShapeDtypeStruct((DN_HEADS, s, DN_HD), F32), compiler_params=_params(("parallel",)))(x, w)

    def bwd_call(x, w, dy):
        s = x.shape[0]

        def body(x_ref, w_ref, dy_ref, dx_ref, dw_ref):
            row = lax.broadcasted_iota(jnp.int32, (s, DN_HD), 0)
            xv, wv = x_ref[...].astype(F32), w_ref[...]
            _, vjp = jax.vjp(lambda p: _prep_post(p, kind), _conv_pre(xv, wv, row))
            (dpre,) = vjp(dy_ref[...])
            dx_ref[...] = sum(wv[j:j + 1, :] * _shift_up(dpre, CONV_K - 1 - j, row) for j in range(CONV_K)).astype(dx_ref.dtype)
            for j in range(CONV_K):
                dw_ref[j:j + 1, :] = jnp.sum(dpre * _shift_down(xv, CONV_K - 1 - j, row), axis=0, keepdims=True)

        x_spec = pl.BlockSpec((s, DN_HD), lambda j: (0, j))
        w_spec = pl.BlockSpec((CONV_K, DN_HD), lambda j: (0, j))
        return pl.pallas_call(
            body, name=name + "_bwd", grid=(DN_HEADS,),
            in_specs=[x_spec, w_spec, pl.BlockSpec((None, s, DN_HD), lambda j: (j, 0, 0))], out_specs=[x_spec, w_spec],
            out_shape=[jax.ShapeDtypeStruct(x.shape, x.dtype), jax.ShapeDtypeStruct(w.shape, F32)],
            compiler_params=_params(("parallel",)))(x, w, dy)

    @jax.custom_vjp
    def prep(x, w):
        return fwd_call(x, w)

    def fwd(x, w):
        return prep(x, w), (x, w)

    def bwd(res, dy):
        return tuple(bwd_call(*res, dy))

    prep.defvjp(fwd, bwd)
    return prep


@jax.custom_vjp
def _saved_inverse(m, t):
    return t


def _saved_inverse_fwd(m, t):
    return t, t


def _saved_inverse_bwd(t, dt):
    return _hdg(_hdg(t, dt, 0, 0), t, 1, 1), jnp.zeros_like(t)


_saved_inverse.defvjp(_saved_inverse_fwd, _saved_inverse_bwd)


def _chunk_fn(q, k, v, beta, g, state, t_saved=None):
    c = CHUNK
    r = lax.broadcasted_iota(jnp.int32, (c, c), 0)
    cc = lax.broadcasted_iota(jnp.int32, (c, c), 1)
    eye = (r == cc).astype(F32)
    causal, strict = r >= cc, r > cc
    g_row = _each(lambda g: jnp.sum(g * eye, axis=0, keepdims=True), g)
    gc = _each(lambda g_row: jnp.sum(jnp.where(causal, g_row, 0.0), axis=1, keepdims=True), g_row)
    gc_row = _each(lambda gc: jnp.sum(gc * eye, axis=0, keepdims=True), gc)
    decay = _each(lambda gc, gc_row: jnp.exp(jnp.where(causal, gc - gc_row, -jnp.inf)), gc, gc_row)
    kb = _each(jnp.multiply, k, beta)
    vb = _each(jnp.multiply, v, beta)
    kk = _each(bdot_nt, kb, k)
    qk = _each(bdot_nt, q, k)
    m = _each(lambda kk, decay: -jnp.where(strict, kk * decay, 0.0), kk, decay)
    if t_saved is None:
        t, p = _each(lambda m: eye + m, m), m
        for _ in range(5):
            p = _each(hdot, p, p)
            t = _each(lambda t, p: t + hdot(t, p), t, p)
    else:
        t = _each(_saved_inverse, m, t_saved)
    eg = _each(jnp.exp, gc)
    u = _each(hdot, t, vb)
    w = _each(lambda t, kb, eg: hdot(t, kb * eg), t, kb, eg)
    ws = _each(bdot_nn, w, state)
    qs = _each(lambda q, eg, state: bdot_nn(q * eg, state), q, eg, state)
    v_new = _each(jnp.subtract, u, ws)
    o = _each(lambda qs, qk, decay, v_new: qs + bdot_nn(qk * decay, v_new), qs, qk, decay, v_new)
    g_last = _each(lambda g: jnp.sum(g, axis=0, keepdims=True), g)
    kv = _each(lambda k, g_last, gc, v_new: bdot_tn(k * jnp.exp(g_last - gc), v_new), k, g_last, gc, v_new)
    new_state = _each(lambda state, g_last, kv: state * jnp.exp(g_last) + kv, state, g_last, kv)
    return o, new_state, t


def _gate_columns(gates):
    lane = lax.broadcasted_iota(jnp.int32, gates.shape, 1)

    def column(at):
        return jnp.sum(jnp.where(lane == at, gates, 0.0), axis=1, keepdims=True)

    return [column(h) for h in range(DN_HEADS)], [column(DN_HEADS + h) for h in range(DN_HEADS)]


def make_delta(name):
    heads = range(DN_HEADS)

    def specs(n_chunks, reverse):
        def at(n):
            return n_chunks - 1 - n if reverse else n

        x_spec = pl.BlockSpec((DN_HEADS, CHUNK, DN_HD), lambda n: (0, at(n), 0))
        g_spec = pl.BlockSpec((CHUNK, LANES), lambda n: (at(n), 0))
        st_spec = pl.BlockSpec((DN_HEADS, None, DN_HD, DN_HD), lambda n: (0, at(n), 0, 0))
        t_spec = pl.BlockSpec((DN_HEADS, None, CHUNK, CHUNK), lambda n: (0, at(n), 0, 0))
        return x_spec, g_spec, st_spec, t_spec

    def per_head(ref):
        return [ref[i] for i in heads]

    def fwd_call(qh, kh, vh, gates):
        n_chunks = qh.shape[1] // CHUNK
        x_spec, g_spec, st_spec, t_spec = specs(n_chunks, False)

        def body(q_ref, k_ref, v_ref, g_ref, o_ref, st_ref, t_ref, state):
            @pl.when(pl.program_id(0) == 0)
            def _():
                state[...] = jnp.zeros_like(state)

            s_in = per_head(state)
            o, s_new, t = _chunk_fn(per_head(q_ref), per_head(k_ref), per_head(v_ref), *_gate_columns(g_ref[...]), s_in)
            for i in heads:
                st_ref[i], o_ref[i], state[i], t_ref[i] = s_in[i], o[i], s_new[i], t[i]

        return pl.pallas_call(
            body, name=name + "_fwd", grid=(n_chunks,), in_specs=[x_spec, x_spec, x_spec, g_spec], out_specs=[x_spec, st_spec, t_spec],
            out_shape=[jax.ShapeDtypeStruct(qh.shape, F32), jax.ShapeDtypeStruct((DN_HEADS, n_chunks, DN_HD, DN_HD), F32),
                       jax.ShapeDtypeStruct((DN_HEADS, n_chunks, CHUNK, CHUNK), F32)],
            scratch_shapes=[pltpu.VMEM((DN_HEADS, DN_HD, DN_HD), F32)], compiler_params=_params(("arbitrary",)))(qh, kh, vh, gates)

    def bwd_call(qh, kh, vh, gates, states, inverses, do):
        n_chunks = qh.shape[1] // CHUNK
        x_spec, g_spec, st_spec, t_spec = specs(n_chunks, True)

        def body(q_ref, k_ref, v_ref, g_ref, st_ref, t_ref, do_ref, dq_ref, dk_ref, dv_ref, dg_ref, dstate):
            @pl.when(pl.program_id(0) == 0)
            def _():
                dstate[...] = jnp.zeros_like(dstate)

            t_saved = per_head(t_ref)
            _, vjp = jax.vjp(lambda q, k, v, gates, state: _chunk_fn(q, k, v, *_gate_columns(gates), state, t_saved=t_saved)[:2],
                             per_head(q_ref), per_head(k_ref), per_head(v_ref), g_ref[...], per_head(st_ref))
            dq, dk, dv, dgates, ds = vjp((per_head(do_ref), per_head(dstate)))
            dg_ref[...] = dgates
            for i in heads:
                dq_ref[i], dk_ref[i], dv_ref[i], dstate[i] = dq[i], dk[i], dv[i], ds[i]

        big = jax.ShapeDtypeStruct(qh.shape, F32)
        return pl.pallas_call(
            body, name=name + "_bwd", grid=(n_chunks,), in_specs=[x_spec, x_spec, x_spec, g_spec, st_spec, t_spec, x_spec],
            out_specs=[x_spec, x_spec, x_spec, g_spec], out_shape=[big, big, big, jax.ShapeDtypeStruct(gates.shape, F32)],
            scratch_shapes=[pltpu.VMEM((DN_HEADS, DN_HD, DN_HD), F32)],
            compiler_params=_params(("arbitrary",)))(qh, kh, vh, gates, states, inverses, do)

    @jax.custom_vjp
    def delta(qh, kh, vh, gates):
        return fwd_call(qh, kh, vh, gates)[0]

    def fwd(qh, kh, vh, gates):
        o, states, inverses = fwd_call(qh, kh, vh, gates)
        return o, (qh, kh, vh, gates, states, inverses)

    def bwd(res, do):
        return tuple(bwd_call(*res, do))

    delta.defvjp(fwd, bwd)
    return delta


def loss_head(y, target):
    s, d = y.shape
    tm = min(256, s)

    def body(y_ref, t_ref, l_ref, dy_ref):
        err = y_ref[...] - t_ref[...]
        dy_ref[...] = err * (1.0 / d)

        @pl.when(pl.program_id(0) == 0)
        def _():
            l_ref[...] = jnp.zeros_like(l_ref)

        l_ref[...] += 0.5 * jnp.sum(jnp.mean(jnp.square(err), axis=-1, keepdims=True), axis=0, keepdims=True)

    spec = pl.BlockSpec((tm, d), lambda i: (i, 0))
    return pl.pallas_call(
        body, name="loss_head", grid=(s // tm,), in_specs=[spec, spec], out_specs=[pl.BlockSpec((8, LANES), lambda i: (0, 0)), spec],
        out_shape=[jax.ShapeDtypeStruct((8, LANES), F32), jax.ShapeDtypeStruct(y.shape, F32)], compiler_params=_params(("arbitrary",)))(y, target)


def ada_project(c_all, w_ada, b_cols):
    tn = 256
    cols = w_ada.shape[2]

    def body(c_ref, w_ref, b_ref, ca_ref, o_ref):
        c_act = _silu(c_ref[...])
        ca_ref[...] = c_act
        o_ref[...] = _dg(c_act, w_ref[...], 1, 0) + b_ref[...]

    return pl.pallas_call(
        body, name="ada_project", grid=(DEPTH, cols // tn),
        in_specs=[pl.BlockSpec((N_DEV, D), lambda l, j: (0, 0)), pl.BlockSpec((None, D, tn), lambda l, j: (l, 0, j)),
                  pl.BlockSpec((None, 1, tn), lambda l, j: (l, 0, j))],
        out_specs=[pl.BlockSpec((N_DEV, D), lambda l, j: (0, 0)), pl.BlockSpec((None, N_DEV, tn), lambda l, j: (l, 0, j))],
        out_shape=[jax.ShapeDtypeStruct((N_DEV, D), F32), jax.ShapeDtypeStruct((DEPTH, N_DEV, cols), F32)],
        compiler_params=_params(("arbitrary", "arbitrary")))(c_all, w_ada, b_cols.reshape(DEPTH, 1, cols))


def sum_partials(parts, name):
    n_parts, r, c = parts.shape
    tr = _tile(r, 512, 16 if parts.dtype == BF16 else 8)

    def body(p_ref, o_ref):
        total = p_ref[0].astype(F32)
        for part in range(1, n_parts):
            total = total + p_ref[part].astype(F32)
        o_ref[...] = total

    return pl.pallas_call(
        body, name=name, grid=(r // tr,), in_specs=[pl.BlockSpec((n_parts, tr, c), lambda i: (0, i, 0))],
        out_specs=pl.BlockSpec((tr, c), lambda i: (i, 0)), out_shape=jax.ShapeDtypeStruct((r, c), F32),
        compiler_params=_params(("parallel",)))(parts)


def _adamw(w, g, m, v):
    m = ADAM_B1 * m + (1.0 - ADAM_B1) * g
    v = ADAM_B2 * v + (1.0 - ADAM_B2) * jnp.square(g)
    m_hat = m / (1.0 - ADAM_B1 ** ADAM_STEP)
    v_hat = v / (1.0 - ADAM_B2 ** ADAM_STEP)
    return -ADAM_LR * (m_hat / (jnp.sqrt(v_hat) + ADAM_EPS) + ADAM_WD * w), m, v


def adamw(w, g, m, v, name, lead_tile=None):
    l, r, c = w.shape

    def body(w_ref, g_ref, m_ref, v_ref, d_ref, nm_ref, nv_ref):
        d_ref[...], nm_ref[...], nv_ref[...] = _adamw(w_ref[...], g_ref[...], m_ref[...], v_ref[...])

    if lead_tile is None:
        tr = _tile(r, 512, 8)
        spec, grid = pl.BlockSpec((None, tr, c), lambda a, i: (a, i, 0)), (l, r // tr)
    else:
        spec, grid = pl.BlockSpec((lead_tile, r, c), lambda a, i: (a, 0, 0)), (l // lead_tile, 1)
    shape = jax.ShapeDtypeStruct(w.shape, F32)
    return pl.pallas_call(body, name=name, grid=grid, in_specs=[spec] * 4, out_specs=[spec] * 3, out_shape=[shape] * 3,
                          compiler_params=_params(("parallel", "parallel")))(w, g, m, v)


def adamw_ada(c_act_t, dmod, w, m, v):
    l, r, c = w.shape
    tr = 256

    def body(c_ref, d_ref, w_ref, m_ref, v_ref, g_ref, dl_ref, nm_ref, nv_ref):
        g = hdot(c_ref[...], d_ref[...])
        g_ref[...] = g
        dl_ref[...], nm_ref[...], nv_ref[...] = _adamw(w_ref[...], g, m_ref[...], v_ref[...])

    spec = pl.BlockSpec((None, tr, c), lambda a, i: (a, i, 0))
    shape = jax.ShapeDtypeStruct(w.shape, F32)
    return pl.pallas_call(
        body, name="adamw_ada", grid=(l, r // tr),
        in_specs=[pl.BlockSpec((tr, LANES), lambda a, i: (i, 0)), pl.BlockSpec((None, LANES, c), lambda a, i: (a, 0, 0)), spec, spec, spec],
        out_specs=[spec] * 4, out_shape=[shape] * 4, compiler_params=_params(("parallel", "parallel")))(c_act_t, dmod, w, m, v)


def _place():
    x, y, c = lax.axis_index("x"), lax.axis_index("y"), lax.axis_index("c")
    return x, y, c


def _comm_call(body, name, ins, out_shapes, space, n_sems):
    n = len(ins)
    return pl.pallas_call(
        body, name=name, out_shape=out_shapes, in_specs=[pl.BlockSpec(memory_space=space)] * n,
        out_specs=[pl.BlockSpec(memory_space=space)] * n,
        scratch_shapes=[pltpu.SemaphoreType.DMA((n, n_sems)), pltpu.SemaphoreType.DMA((n, n_sems)), pltpu.SemaphoreType.DMA((n,))],
        compiler_params=pltpu.CompilerParams(vmem_limit_bytes=VMEM_LIMIT_BYTES))(*ins)


def all_gather(shards, name, in_vmem):
    n = len(shards)

    def body(*refs):
        x_refs, out_refs, (send_sems, recv_sems, local_sems) = refs[:n], refs[n:2 * n], refs[2 * n:]
        x, y, c = _place()
        me, sibling = (x, y, c), (x, y, 1 - c)
        chips = [(1 - x, y), (x, 1 - y), (1 - x, 1 - y)]

        def rows(a, px, py, pc):
            return out_refs[a].at[4 * px + 2 * py + pc]

        def copy(a, k, block, to, from_shard=False):
            return pltpu.make_async_remote_copy(
                src_ref=x_refs[a] if from_shard else rows(a, *block), dst_ref=rows(a, *block), send_sem=send_sems.at[a, k],
                recv_sem=recv_sems.at[a, k], device_id=to, device_id_type=MESH_IDS)

        arrays = range(n)
        mine = [pltpu.make_async_copy(x_refs[a], rows(a, *me), local_sems.at[a]) for a in arrays]
        first = [copy(a, 1 + j, me, (*chip, c), True) for j, chip in enumerate(chips) for a in arrays]
        first += [copy(a, 0, me, sibling, True) for a in arrays]
        for cp in mine + first:
            cp.start()
        passed = []
        for j, chip in enumerate(chips):
            for a in arrays:
                copy(a, 1 + j, (*chip, c), me).wait_recv()
                passed.append(copy(a, 4 + j, (*chip, c), sibling))
                passed[-1].start()
        for a in arrays:
            copy(a, 0, sibling, me).wait_recv()
            for j, chip in enumerate(chips):
                copy(a, 4 + j, (*chip, 1 - c), me).wait_recv()
        for cp in first + passed:
            cp.wait_send()
        for cp in mine:
            cp.wait()

    out_shapes = [jax.ShapeDtypeStruct((N_DEV,) + s.shape, s.dtype) for s in shards]
    return _comm_call(body, name, shards, out_shapes, pltpu.VMEM if in_vmem else pl.ANY, 7)


def all_gather_forwarding(shards, name):
    n = len(shards)

    def body(*refs):
        x_refs, out_refs, (send_sems, recv_sems, local_sems) = refs[:n], refs[n:2 * n], refs[2 * n:]
        x, y, c = _place()
        me = (x, y, c)

        def rows(a, dev, half):
            block = out_refs[a].at[4 * dev[0] + 2 * dev[1] + dev[2]]
            half_rows = shards[a].shape[0] // 2
            return block if half is None else block.at[pl.ds(half * half_rows, half_rows)]

        def copy(a, k, block, to, half=None, from_shard=False):
            return pltpu.make_async_remote_copy(
                src_ref=x_refs[a] if from_shard else rows(a, block, half), dst_ref=rows(a, block, half), send_sem=send_sems.at[a, k],
                recv_sem=recv_sems.at[a, k], device_id=to, device_id_type=MESH_IDS)

        def other_core(dev):
            return (dev[0], dev[1], 1 - c)

        sibling, x_nbr, y_nbr, diag = other_core(me), (1 - x, y, c), (x, 1 - y, c), (1 - x, 1 - y, c)
        arrays = range(n)
        mine = [pltpu.make_async_copy(x_refs[a], rows(a, me, None), local_sems.at[a]) for a in arrays]
        sent = [copy(a, k, me, to, from_shard=True) for k, to in ((1, x_nbr), (2, y_nbr), (0, sibling)) for a in arrays]
        for cp in mine + sent:
            cp.start()

        def on_arrival(k, block, half, passes):
            for a in arrays:
                copy(a, k, block, me, half).wait_recv()
                for k_out, to, half_out in passes:
                    sent.append(copy(a, k_out, block, to, half_out))
                    sent[-1].start()

        on_arrival(1, x_nbr, None, [(3, y_nbr, 0), (5, sibling, None)])
        on_arrival(2, y_nbr, None, [(4, x_nbr, 1), (6, sibling, None)])
        on_arrival(3, diag, 0, [(7, sibling, 0)])
        on_arrival(4, diag, 1, [(8, sibling, 1)])
        on_arrival(0, sibling, None, [])
        on_arrival(5, other_core(x_nbr), None, [])
        on_arrival(6, other_core(y_nbr), None, [])
        on_arrival(7, other_core(diag), 0, [])
        on_arrival(8, other_core(diag), 1, [])
        for cp in sent:
            cp.wait_send()
        for cp in mine:
            cp.wait()

    out_shapes = [jax.ShapeDtypeStruct((N_DEV,) + s.shape, s.dtype) for s in shards]
    return _comm_call(body, name, shards, out_shapes, pl.ANY, 9)


def scatter_to_sibling(slabs, name):
    n = len(slabs)

    def body(*refs):
        in_refs, out_refs, (send_sems, recv_sems, _) = refs[:n], refs[n:2 * n], refs[2 * n:]
        x, y, c = _place()
        copies = [pltpu.make_async_remote_copy(
            src_ref=in_refs[a].at[2 * chip + 1 - c], dst_ref=out_refs[a].at[chip], send_sem=send_sems.at[a, chip],
            recv_sem=recv_sems.at[a, chip], device_id=(x, y, 1 - c), device_id_type=MESH_IDS) for chip in range(4) for a in range(n)]
        for cp in copies:
            cp.start()
        for cp in copies:
            cp.wait_recv()
        for cp in copies:
            cp.wait_send()

    out_shapes = [jax.ShapeDtypeStruct((4,) + s.shape[1:], s.dtype) for s in slabs]
    return _comm_call(body, name, slabs, out_shapes, pl.ANY, 4)


def scatter_to_chips(slabs, name):
    n = len(slabs)

    def body(*refs):
        in_refs, out_refs, (send_sems, recv_sems, local_sems) = refs[:n], refs[n:2 * n], refs[2 * n:]
        x, y, c = _place()
        my_chip = 2 * x + y
        mine = [pltpu.make_async_copy(in_refs[a].at[my_chip], out_refs[a].at[my_chip], local_sems.at[a]) for a in range(n)]
        copies = [pltpu.make_async_remote_copy(
            src_ref=in_refs[a].at[2 * px + py], dst_ref=out_refs[a].at[my_chip], send_sem=send_sems.at[a, j], recv_sem=recv_sems.at[a, j],
            device_id=(px, py, c), device_id_type=MESH_IDS)
            for j, (px, py) in enumerate([(1 - x, y), (x, 1 - y), (1 - x, 1 - y)]) for a in range(n)]
        for cp in mine + copies:
            cp.start()
        for cp in copies:
            cp.wait_recv()
        for cp in copies:
            cp.wait_send()
        for cp in mine:
            cp.wait()

    return _comm_call(body, name, slabs, [jax.ShapeDtypeStruct(s.shape, s.dtype) for s in slabs], pl.ANY, 3)


def pair_sum(slabs, got, name):
    _, r, c = slabs.shape
    tr = _tile(r, 1024, 16)

    def body(a_ref, b_ref, o_ref):
        o_ref[...] = (a_ref[...].astype(F32) + b_ref[...].astype(F32)).astype(BF16)

    return pl.pallas_call(
        body, name=name, grid=(4, r // tr),
        in_specs=[pl.BlockSpec((None, tr, c), lambda s, i: (2 * s + lax.axis_index("c"), i, 0)), pl.BlockSpec((None, tr, c), lambda s, i: (s, i, 0))],
        out_specs=pl.BlockSpec((None, tr, c), lambda s, i: (s, i, 0)), out_shape=jax.ShapeDtypeStruct((4, r, c), BF16),
        compiler_params=_params(("parallel", "parallel")))(slabs, got)


BIG = ("w_in", "w_oa", "w_ob", "w_out", "w_ff1", "w_ff2")
SMALL = ("a_log", "dt_bias", "sinks", "dn_norm_w", "ln1_g", "ln1_b", "b_ff1", "b_ff2", "ln2_g", "ln2_b")


def _pack_rows(arrs, width, unit):
    flat = jnp.concatenate([a.reshape(-1) for a in arrs])
    rows = -(-flat.shape[0] // (width * unit)) * unit
    return jnp.pad(flat, (0, rows * width - flat.shape[0])).reshape(rows, width)


def _split_flat(flat, like):
    out, off = [], 0
    for a in like:
        n = 1
        for dim in a.shape:
            n *= dim
        out.append(flat[off:off + n].reshape(a.shape))
        off += n
    return out


ROW_SHARDED = ("w_oa", "w_ob", "w_out", "w_ff2")
W_IN_COL_TILE = 74


def _pack_shards(given):
    rows = jnp.concatenate([given[n].reshape(-1, D) for n in ROW_SHARDED]).astype(BF16)
    return [given["w_in"].astype(BF16).reshape(DEPTH * D, -1), rows, given["w_ff1"].astype(BF16).reshape(DEPTH * D, -1)]


def _unpack_weights(g_in, g_rows, g_ff1):
    w_in = g_in.reshape(N_DEV, DEPTH, D, -1)
    w_ff1 = g_ff1.reshape(N_DEV, DEPTH, D, -1)
    layers = []
    for l in range(DEPTH):
        full_in = w_in[:, l].transpose(1, 0, 2).reshape(D, -1)
        small = jnp.pad(full_in[:, IN_AT["small"]:IN_AT["ga"]], ((0, 0), (0, LANES + IN_AT["small"] - IN_AT["ga"])))
        groups = [jnp.concatenate([full_in[:, :QKV_W], small], axis=1)] + [full_in[:, IN_AT[g]:IN_AT[g] + D] for g in IN_GROUPS[1:]]
        lay, off = dict(w_in=tuple(groups), w_ff1=w_ff1[:, l].transpose(1, 0, 2).reshape(D, D_FF)), 0
        for n in ROW_SHARDED:
            per = (D_FF if n == "w_ff2" else D) // N_DEV
            lay[n] = g_rows[:, off + l * per:off + (l + 1) * per].reshape(per * N_DEV, D)
            off += DEPTH * per
        layers.append(lay)
    return layers


def _pack_grads(grads):
    def in_order(qkvs, dq, dk, dv, z, ga, gb):
        return jnp.concatenate([qkvs[:, :QKV_W], dq, dk, dv, z, qkvs[:, QKV_W:QKV_W + IN_AT["ga"] - IN_AT["small"]], ga, gb], axis=1)

    w_in = jnp.stack([in_order(*lay["w_in"]) for lay in grads])
    s_in = w_in.reshape(DEPTH, D, N_DEV, -1).transpose(2, 0, 1, 3).reshape(N_DEV, DEPTH * D, -1)
    rows = []
    for n in ROW_SHARDED:
        w = jnp.stack([lay[n] for lay in grads])
        rows.append(w.reshape(DEPTH, N_DEV, -1, D).transpose(1, 0, 2, 3).reshape(N_DEV, -1, D))
    w = jnp.stack([lay["w_ff1"] for lay in grads])
    s_ff1 = w.reshape(DEPTH, D, N_DEV, -1).transpose(2, 0, 1, 3).reshape(N_DEV, DEPTH * D, -1)
    return [s_in, jnp.concatenate(rows, axis=1), s_ff1]


def _layer(ops, x, u, mod, next_mod, sm, conv_w, w):
    s = x.shape[0]
    sh1, sc1, gt1, sh2, sc2, gt2 = (mod[:, i * D:(i + 1) * D] for i in range(6))
    qkvs, dq, dk, dv, z, ga, gb = ops["in_proj"](u, w["w_in"])
    q, k, v, small = qkvs[:, :D], qkvs[:, D:D + 256], qkvs[:, D + 256:QKV_W], qkvs[:, QKV_W:]
    qh = q.reshape(s, ATT_HEADS, ATT_HD).transpose(1, 0, 2)
    kh = k.reshape(s, ATT_KV, ATT_HD).transpose(1, 0, 2)
    vh = v.reshape(s, ATT_KV, ATT_HD).transpose(1, 0, 2)
    sink = jnp.broadcast_to(sm["sinks"].reshape(ATT_KV, ATT_GROUP, 1, 1), (ATT_KV, ATT_GROUP, WINDOW, 1)).reshape(ATT_KV, ATT_GROUP * WINDOW, 1)
    attn = ops["swa"](qh, kh, vh, sink).transpose(1, 0, 2).reshape(s, ATT_HEADS * ATT_HD)
    y_a = ops["mm_oa"](attn, w["w_oa"])
    qn = ops["prep_q"](dq, conv_w[:, :D])
    kn = ops["prep_k"](dk, conv_w[:, D:2 * D])
    vn = ops["prep_v"](dv, conv_w[:, 2 * D:])
    a_vec = jnp.pad(sm["a_log"], ((0, 0), (DN_HEADS, LANES - 2 * DN_HEADS)))
    b_vec = jnp.pad(sm["dt_bias"], ((0, 0), (DN_HEADS, LANES - 2 * DN_HEADS)))
    (gates,) = ops["gates"](small, a_vec, b_vec)
    o = ops["delta"](qn, kn, vn, gates)
    (og,) = ops["gdn_post"](o, z, sm["dn_norm_w"])
    y_b = ops["mm_ob"](og, w["w_ob"])
    (mix,) = ops["mix"](ga, gb, y_a, y_b)
    mixed = ops["mm_out"](mix, w["w_out"])
    x1, u2 = ops["ln1"](x, mixed, gt1, sm["ln1_g"], sm["ln1_b"], sc2, sh2)
    (h,) = ops["relu2"](ops["mm_ff1"](u2, w["w_ff1"]), sm["b_ff1"])
    f = ops["mm_ff2"](h, w["w_ff2"])
    if next_mod is None:
        return ops["ln2"](x1, f, gt2, sm["b_ff2"], sm["ln2_g"], sm["ln2_b"])[0], None
    return ops["ln2"](x1, f, gt2, sm["b_ff2"], sm["ln2_g"], sm["ln2_b"], next_mod[:, D:2 * D], next_mod[:, :D])


def _make_ops(l):
    t = f"l{l}_"
    last = l == DEPTH - 1
    return dict(
        in_proj=make_in_proj(t + "in_proj"), swa=make_swa(t + "swa"),
        mm_oa=make_mm(t + "mm_oa", BF16), mm_ob=make_mm(t + "mm_ob", BF16), mm_out=make_mm(t + "mm_out"),
        mm_ff1=make_mm(t + "mm_ff1", BF16), mm_ff2=make_mm(t + "mm_ff2"),
        prep_q=make_gdn_prep(t + "prep_q", "q"), prep_k=make_gdn_prep(t + "prep_k", "k"), prep_v=make_gdn_prep(t + "prep_v", "v"),
        gates=make_rowwise(_gates_fn, t + "gates", "t", "cc", "t"), delta=make_delta(t + "delta"),
        gdn_post=make_rowwise(_gdn_post_fn, t + "gdn_post", "ht", "s", "t", nc=DN_HEADS, tm=1024),
        mix=make_rowwise(_mix_fn, t + "mix", "tttt", "", "t"),
        ln1=make_rowwise(_ln1_fn, t + "ln1", "tt", "ccccc", "tt"),
        ln2=make_rowwise(_ln2_last_fn, t + "ln2", "tt", "cccc", "t") if last else make_rowwise(_ln2_fn, t + "ln2", "tt", "cccccc", "tt"),
        relu2=make_rowwise(_relu2_fn, t + "relu2", "t", "c", "t", nc=4, tm=512))


def kernel(x, c, w_ada, b_ada, w_in, conv_w, a_log, dt_bias, sinks, dn_norm_w, w_oa, w_ob, w_out, ln1_g, ln1_b, w_ff1, b_ff1, w_ff2, b_ff2, ln2_g, ln2_b, loss_target, m_w_ada, m_b_ada, m_w_in, m_conv_w, m_a_log, m_dt_bias, m_sinks, m_dn_norm_w, m_w_oa, m_w_ob, m_w_out, m_ln1_g, m_ln1_b, m_w_ff1, m_b_ff1, m_w_ff2, m_b_ff2, m_ln2_g, m_ln2_b, v_w_ada, v_b_ada, v_w_in, v_conv_w, v_a_log, v_dt_bias, v_sinks, v_dn_norm_w, v_w_oa, v_w_ob, v_w_out, v_ln1_g, v_ln1_b, v_w_ff1, v_b_ff1, v_w_ff2, v_b_ff2, v_ln2_g, v_ln2_b):
    given = dict(locals())
    me = 4 * lax.axis_index("x") + 2 * lax.axis_index("y") + lax.axis_index("c")
    conv_cols = conv_w.shape[2]

    gathered = all_gather([_pack_rows([c, conv_w], LANES, 8)], "gather_c_conv", True)[0].reshape(N_DEV, -1)
    c_all = gathered[:, :D]
    conv_full = gathered[:, D:D + DEPTH * CONV_K * conv_cols].reshape(N_DEV, DEPTH, CONV_K, conv_cols).transpose(1, 2, 0, 3).reshape(DEPTH, CONV_K, -1)

    b_cols = lax.dynamic_slice_in_dim(b_ada, me * ADA_COLS, ADA_COLS, axis=1)
    c_act_all, mod_cols = ada_project(c_all, w_ada, b_cols)
    mod_all = all_gather([mod_cols.reshape(-1, LANES)], "gather_mod", True)[0].reshape(N_DEV, DEPTH, N_DEV, ADA_COLS)
    mods = lax.dynamic_index_in_dim(mod_all, me, axis=2, keepdims=False).transpose(1, 0, 2).reshape(DEPTH, 6 * D)

    weights = _unpack_weights(*all_gather_forwarding(_pack_shards(given), "gather_weights"))

    small = {n: given[n] for n in SMALL}
    ops = [_make_ops(l) for l in range(DEPTH)]

    modulate0 = make_rowwise(_modulate_fn, "modulate0", "t", "cc", "t")

    def forward(x0, mods, small, conv_full, weights):
        (u,) = modulate0(x0, mods[:1, D:2 * D], mods[:1, :D])
        h = x0
        for l in range(DEPTH):
            next_mod = mods[l + 1:l + 2] if l + 1 < DEPTH else None
            h, u = _layer(ops[l], h, u, mods[l:l + 1], next_mod, {n: a[l:l + 1] for n, a in small.items()}, conv_full[l], weights[l])
        return h

    y, vjp = jax.vjp(forward, x[0], mods, small, conv_full, weights)
    loss_tile, dy = loss_head(y, loss_target[0])
    dx, d_mods, d_small, d_conv, d_weights = vjp(dy)
    loss = lax.psum(loss_tile[0, 0], AXES)

    slabs = _pack_grads(d_weights)
    from_sibling = scatter_to_sibling(slabs, "scatter_sibling")
    per_chip = [pair_sum(a, b, f"pair_sum_{i}") for i, (a, b) in enumerate(zip(slabs, from_sibling))]
    g_in, g_rows, g_ff1 = [sum_partials(p, f"sum_big_{i}") for i, p in enumerate(scatter_to_chips(per_chip, "scatter_chips"))]
    grad, off = dict(w_in=g_in.reshape(w_in.shape), w_ff1=g_ff1.reshape(w_ff1.shape)), 0
    for n in ROW_SHARDED:
        rows = given[n].shape[0] * given[n].shape[1]
        grad[n] = g_rows[off:off + rows].reshape(given[n].shape)
        off += rows
    delta, new_m, new_v = {}, {}, {}
    for n in BIG[1:]:
        delta[n], new_m[n], new_v[n] = adamw(given[n], grad[n], given["m_" + n], given["v_" + n], "adamw_" + n)
    cols_first = [a.transpose(2, 0, 1) for a in (w_in, grad["w_in"], m_w_in, v_w_in)]
    delta["w_in"], new_m["w_in"], new_v["w_in"] = (
        a.transpose(1, 2, 0) for a in adamw(*cols_first, "adamw_w_in", lead_tile=W_IN_COL_TILE))

    partial = [d_small[n] for n in SMALL] + [d_mods, d_conv]
    (parts,) = all_gather([_pack_rows(partial, LANES, 8)], "gather_small_grads", True)
    mods_at = sum(d_small[n].size for n in SMALL)
    d_mods_all = parts.reshape(N_DEV, -1)[:, mods_at:mods_at + DEPTH * 6 * D].reshape(N_DEV, DEPTH, 6 * D)
    total = _split_flat(sum_partials(parts, "sum_small").reshape(-1), partial)
    for n, g in zip(SMALL, total):
        grad[n] = g
    grad["b_ada"] = total[len(SMALL)]
    grad["conv_w"] = lax.dynamic_slice_in_dim(total[len(SMALL) + 1], me * conv_cols, conv_cols, axis=2)
    names = SMALL + ("b_ada", "conv_w")
    packed = [_pack_rows([src[p + n] for n in names], LANES, 8)[None] for src, p in ((given, ""), (grad, ""), (given, "m_"), (given, "v_"))]
    outs = adamw(*packed, "adamw_small")
    for res, o in zip((delta, new_m, new_v), outs):
        for n, a in zip(names, _split_flat(o.reshape(-1), [given[n] for n in names])):
            res[n] = a

    dmod_mine = lax.dynamic_slice_in_dim(d_mods_all, me * ADA_COLS, ADA_COLS, axis=2).transpose(1, 0, 2)
    pad = LANES - N_DEV
    grad["w_ada"], delta["w_ada"], new_m["w_ada"], new_v["w_ada"] = adamw_ada(
        jnp.pad(c_act_all.T, ((0, 0), (0, pad))), jnp.pad(dmod_mine, ((0, 0), (0, pad), (0, 0))), w_ada, m_w_ada, v_w_ada)

    order = ("w_ada", "b_ada", "w_in", "conv_w", "a_log", "dt_bias", "sinks", "dn_norm_w", "w_oa", "w_ob", "w_out", "ln1_g", "ln1_b",
             "w_ff1", "b_ff1", "w_ff2", "b_ff2", "ln2_g", "ln2_b")
    return (loss, dx[None], *[grad[n] for n in order], *[delta[n] for n in order], *[new_m[n] for n in order], *[new_v[n] for n in order])
```

```python
import functools

import jax
import jax.numpy as jnp
from jax import lax
from jax.experimental import pallas as pl
from jax.experimental.pallas import tpu as pltpu

F32 = jnp.float32
BF16 = jnp.bfloat16

D = 1024
DEPTH = 4
N_DEV = 8
ATT_HEADS, ATT_KV, ATT_GROUP, ATT_HD, WINDOW = 16, 4, 4, 64, 128
DN_HEADS, DN_HD, CONV_K, CHUNK = 8, 128, 4, 64
D_FF = 4096
ADA_COLS = 6 * D // N_DEV
ALPHA = (2 * DEPTH) ** 0.25
LN_EPS = 1e-5
RMS_EPS = 1e-6
ADAM_LR, ADAM_B1, ADAM_B2, ADAM_EPS, ADAM_WD, ADAM_STEP = 0.001, 0.9, 0.999, 1e-08, 0.01, 10
AXES = ("x", "y", "c")
MESH_IDS = pl.DeviceIdType.MESH
VMEM_LIMIT_BYTES = 48 * 1024 * 1024
LANES = 128
HIGHEST = lax.Precision.HIGHEST

IN_SPLITS = (("q", 1024), ("k", 256), ("v", 256), ("dq", 1024), ("dk", 1024), ("dv", 1024), ("z", 1024),
             ("small", 16), ("ga", 1024), ("gb", 1024))
IN_AT = {name: sum(w for _, w in IN_SPLITS[:i]) for i, (name, _) in enumerate(IN_SPLITS)}
IN_GROUPS = ("qkvs", "dq", "dk", "dv", "z", "ga", "gb")
QKV_W = 1536
QKVS_W = QKV_W + LANES


def _params(sem=None):
    return pltpu.CompilerParams(dimension_semantics=sem, vmem_limit_bytes=VMEM_LIMIT_BYTES)


def _tile(n, pref, unit):
    if n <= pref:
        return n
    t = (pref // unit) * unit
    while t > unit and n % t:
        t -= unit
    assert n % t == 0, (n, pref, unit)
    return t


def _dg(a, b, ca, cb):
    return lax.dot_general(a.astype(BF16), b.astype(BF16), (((ca,), (cb,)), ((), ())), preferred_element_type=F32)


@jax.custom_vjp
def bdot_nn(a, b):
    return _dg(a, b, 1, 0)


def _bdot_nn_fwd(a, b):
    return _dg(a, b, 1, 0), (a, b)


def _bdot_nn_bwd(res, g):
    a, b = res
    return _dg(g, b, 1, 1).astype(a.dtype), _dg(a, g, 0, 0).astype(b.dtype)


bdot_nn.defvjp(_bdot_nn_fwd, _bdot_nn_bwd)


@jax.custom_vjp
def bdot_nt(a, b):
    return _dg(a, b, 1, 1)


def _bdot_nt_fwd(a, b):
    return _dg(a, b, 1, 1), (a, b)


def _bdot_nt_bwd(res, g):
    a, b = res
    return _dg(g, b, 1, 0).astype(a.dtype), _dg(g, a, 0, 0).astype(b.dtype)


bdot_nt.defvjp(_bdot_nt_fwd, _bdot_nt_bwd)


@jax.custom_vjp
def bdot_tn(a, b):
    return _dg(a, b, 0, 0)


def _bdot_tn_fwd(a, b):
    return _dg(a, b, 0, 0), (a, b)


def _bdot_tn_bwd(res, g):
    a, b = res
    return _dg(b, g, 1, 1).astype(a.dtype), _dg(a, g, 1, 0).astype(b.dtype)


bdot_tn.defvjp(_bdot_tn_fwd, _bdot_tn_bwd)


def _hdg(a, b, ca, cb):
    a_hi, b_hi = a.astype(BF16), b.astype(BF16)
    a_lo, b_lo = (a - a_hi.astype(F32)).astype(BF16), (b - b_hi.astype(F32)).astype(BF16)

    def dot(x, y):
        return lax.dot_general(x, y, (((ca,), (cb,)), ((), ())), preferred_element_type=F32)

    return dot(a_hi, b_hi) + (dot(a_hi, b_lo) + dot(a_lo, b_hi))


@jax.custom_vjp
def hdot(a, b):
    return _hdg(a, b, 1, 0)


def _hdot_fwd(a, b):
    return _hdg(a, b, 1, 0), (a, b)


def _hdot_bwd(res, g):
    a, b = res
    return _hdg(g, b, 1, 1), _hdg(a, g, 0, 0)


hdot.defvjp(_hdot_fwd, _hdot_bwd)


def matmul(a, b, mode, out_dtype, name, acc=None):
    if mode == "nn":
        (m, k), (k2, n) = a.shape, b.shape
    elif mode == "nt":
        (m, k), (n, k2) = a.shape, b.shape
    else:
        (k, m), (k2, n) = a.shape, b.shape
    assert k == k2, (a.shape, b.shape, mode)
    def pick(n_, pref):
        t = _tile(n_, pref, LANES)
        return n_ if t < 2 * LANES and n_ <= 2048 else t

    tm, tn, tk = pick(m, 1024), pick(n, 512), pick(k, 2048)
    nk = k // tk
    a_spec = pl.BlockSpec((tk, tm), lambda i, j, kk: (kk, i)) if mode == "tn" else pl.BlockSpec((tm, tk), lambda i, j, kk: (i, kk))
    b_spec = pl.BlockSpec((tn, tk), lambda i, j, kk: (j, kk)) if mode == "nt" else pl.BlockSpec((tk, tn), lambda i, j, kk: (kk, j))
    o_spec = pl.BlockSpec((tm, tn), lambda i, j, kk: (i, j))
    ca, cb = {"nn": (1, 0), "nt": (1, 1), "tn": (0, 0)}[mode]

    def body(*refs):
        a_ref, b_ref = refs[:2]
        c_ref = None if acc is None else refs[2]
        o_ref = refs[2 if acc is None else 3]
        part = _dg(a_ref[...], b_ref[...], ca, cb)
        if nk == 1:
            o_ref[...] = (part if c_ref is None else part + c_ref[...].astype(F32)).astype(out_dtype)
            return
        acc_ref = refs[-1]
        kk = pl.program_id(2)

        @pl.when(kk == 0)
        def _():
            acc_ref[...] = part if c_ref is None else part + c_ref[...].astype(F32)

        @pl.when(jnp.logical_and(kk > 0, kk < nk - 1))
        def _():
            acc_ref[...] += part

        @pl.when(kk == nk - 1)
        def _():
            o_ref[...] = (acc_ref[...] + part).astype(out_dtype)

    ins, in_specs = [a, b], [a_spec, b_spec]
    if acc is not None:
        ins.append(acc)
        in_specs.append(o_spec)
    return pl.pallas_call(
        body, name=name, grid=(m // tm, n // tn, nk), in_specs=in_specs, out_specs=o_spec,
        out_shape=jax.ShapeDtypeStruct((m, n), out_dtype), scratch_shapes=[pltpu.VMEM((tm, tn), F32)] if nk > 1 else [],
        compiler_params=_params(("parallel", "parallel", "arbitrary")))(*ins)


def make_mm(name, out_dtype=F32):
    @jax.custom_vjp
    def mm(a, w):
        return matmul(a, w, "nn", out_dtype, name + "_fwd")

    def fwd(a, w):
        return mm(a, w), (a, w)

    def bwd(res, g):
        a, w = res
        return matmul(g, w, "nt", a.dtype, name + "_da"), matmul(a, g, "tn", w.dtype, name + "_dw")

    mm.defvjp(fwd, bwd)
    return mm


def make_in_proj(name):
    @jax.custom_vjp
    def in_proj(u, ws):
        return tuple(matmul(u, w, "nn", BF16, f"{name}_fwd_{g}") for g, w in zip(IN_GROUPS, ws))

    def fwd(u, ws):
        return in_proj(u, ws), (u, ws)

    def bwd(res, gs):
        u, ws = res
        du = None
        for idx, (g, w, dy) in enumerate(zip(IN_GROUPS, ws, gs)):
            last = idx == len(ws) - 1
            du = matmul(dy, w, "nt", u.dtype if last else F32, f"{name}_du_{g}", acc=du)
        dws = tuple(matmul(u, dy, "tn", w.dtype, f"{name}_dw_{g}") for g, w, dy in zip(IN_GROUPS, ws, gs))
        return du, dws

    in_proj.defvjp(fwd, bwd)
    return in_proj


def make_rowwise(fn, name, tile_kinds, param_kinds, out_kinds, nc=1, tm=256):
    n_t, n_p, n_o = len(tile_kinds), len(param_kinds), len(out_kinds)

    def width(a, kind):
        if kind == "h":
            return a.shape[2]
        return a.shape[1] if kind == "s" else a.shape[1] // nc

    def spec(kind, w, rows):
        if kind == "t":
            return pl.BlockSpec((rows, w), lambda j, i: (i, j))
        if kind == "h":
            return pl.BlockSpec((None, rows, w), lambda j, i: (j, i, 0))
        if kind == "c":
            return pl.BlockSpec((1, w), lambda j, i: (0, j))
        return pl.BlockSpec((1, w), lambda j, i: (0, 0))

    def full_shape(kind, w, s):
        return (s, w * nc) if kind == "t" else (nc, s, w)

    def plan(tiles, params):
        s = tiles[0].shape[0] if tile_kinds[0] == "t" else tiles[0].shape[1]
        rows = min(tm, s)
        t_w = [width(a, kd) for a, kd in zip(tiles, tile_kinds)]
        p_w = [width(a, kd) for a, kd in zip(params, param_kinds)]
        t_s = [jax.ShapeDtypeStruct((rows, w), a.dtype) for a, w in zip(tiles, t_w)]
        p_s = [jax.ShapeDtypeStruct((1, w), a.dtype) for a, w in zip(params, p_w)]
        o_s = jax.eval_shape(fn, *t_s, *p_s)
        return s, rows, t_w, p_w, o_s

    def fwd_call(*args):
        tiles, params = args[:n_t], args[n_t:]
        s, rows, t_w, p_w, o_s = plan(tiles, params)

        def body(*refs):
            ins, outs = refs[:n_t + n_p], refs[n_t + n_p:]
            res = fn(*[r[...] for r in ins])
            for o_ref, val in zip(outs, res):
                o_ref[...] = val

        in_specs = [spec(kd, w, rows) for kd, w in zip(tile_kinds, t_w)] + [spec(kd, w, rows) for kd, w in zip(param_kinds, p_w)]
        return pl.pallas_call(
            body, name=name + "_fwd", grid=(nc, s // rows), in_specs=in_specs,
            out_specs=[spec(kd, o.shape[1], rows) for kd, o in zip(out_kinds, o_s)],
            out_shape=[jax.ShapeDtypeStruct(full_shape(kd, o.shape[1], s), o.dtype) for kd, o in zip(out_kinds, o_s)],
            compiler_params=_params(("parallel", "parallel")))(*args)

    def bwd_call(args, douts):
        tiles, params = args[:n_t], args[n_t:]
        s, rows, t_w, p_w, o_s = plan(tiles, params)

        def body(*refs):
            ins = refs[:n_t + n_p]
            dos = refs[n_t + n_p:n_t + n_p + n_o]
            dts = refs[n_t + n_p + n_o:n_t + n_p + n_o + n_t]
            dps = refs[n_t + n_p + n_o + n_t:]
            j, i = pl.program_id(0), pl.program_id(1)
            _, vjp = jax.vjp(lambda *a: tuple(fn(*a)), *[r[...] for r in ins])
            grads = vjp(tuple(r[...] for r in dos))
            for r, g in zip(dts, grads[:n_t]):
                r[...] = g.astype(r.dtype)
            for r, g, kd in zip(dps, grads[n_t:], param_kinds):
                first = (i == 0) if kd == "c" else jnp.logical_and(i == 0, j == 0)

                @pl.when(first)
                def _(r=r):
                    r[...] = jnp.zeros_like(r)

                r[...] += g.astype(F32)

        in_specs = ([spec(kd, w, rows) for kd, w in zip(tile_kinds, t_w)] + [spec(kd, w, rows) for kd, w in zip(param_kinds, p_w)]
                    + [spec(kd, o.shape[1], rows) for kd, o in zip(out_kinds, o_s)])
        out_specs = [spec(kd, w, rows) for kd, w in zip(tile_kinds, t_w)] + [spec(kd, w, rows) for kd, w in zip(param_kinds, p_w)]
        out_shape = [jax.ShapeDtypeStruct(a.shape, a.dtype) for a in tiles] + [jax.ShapeDtypeStruct(a.shape, F32) for a in params]
        return pl.pallas_call(
            body, name=name + "_bwd", grid=(nc, s // rows), in_specs=in_specs, out_specs=out_specs, out_shape=out_shape,
            compiler_params=_params(("arbitrary", "arbitrary")))(*args, *douts)

    @jax.custom_vjp
    def op(*args):
        return tuple(fwd_call(*args))

    def op_fwd(*args):
        return op(*args), args

    def op_bwd(args, douts):
        return tuple(bwd_call(args, douts))

    op.defvjp(op_fwd, op_bwd)
    return op


def _sigmoid(x):
    return 1.0 / (1.0 + jnp.exp(-x))


def _silu(x):
    return x * _sigmoid(x)


def _softplus(x):
    return jnp.maximum(x, 0.0) + jnp.log(1.0 + jnp.exp(-jnp.abs(x)))


def _layer_norm(h, g, b):
    mu = jnp.mean(h, axis=-1, keepdims=True)
    var = jnp.mean(jnp.square(h - mu), axis=-1, keepdims=True)
    return (h - mu) * lax.rsqrt(var + LN_EPS) * g + b


def _modulate_fn(x, sc, sh):
    return ((x * (1.0 + sc) + sh).astype(BF16),)


def _gates_fn(x, a_vec, b_vec):
    x = x.astype(F32)
    lane = lax.broadcasted_iota(jnp.int32, x.shape, 1)
    beta = _sigmoid(x)
    g = -jnp.exp(a_vec) * _softplus(x + b_vec)
    return (jnp.where(lane < DN_HEADS, beta, jnp.where(lane < 2 * DN_HEADS, g, 0.0)),)


def _gdn_post_fn(o, z, nw):
    o = o * lax.rsqrt(jnp.mean(jnp.square(o), axis=-1, keepdims=True) + RMS_EPS) * nw
    return ((o * _silu(z.astype(F32))).astype(BF16),)


def _mix_fn(ga, gb, ya, yb):
    ga, gb, ya, yb = (t.astype(F32) for t in (ga, gb, ya, yb))
    return ((_sigmoid(ga) * ya + _sigmoid(gb) * yb).astype(BF16),)


def _ln1_fn(x, mixed, gt, g, b, sc, sh):
    y = _layer_norm(ALPHA * x + (1.0 + gt) * mixed, g, b)
    return y, _modulate_fn(y, sc, sh)[0]


def _ln2_last_fn(x, f, gt, bf, g, b):
    return (_layer_norm(ALPHA * x + (1.0 + gt) * (f + bf), g, b),)


def _ln2_fn(x, f, gt, bf, g, b, sc, sh):
    (y,) = _ln2_last_fn(x, f, gt, bf, g, b)
    return y, _modulate_fn(y, sc, sh)[0]


def _relu2_fn(h, b):
    return (jnp.square(jnp.maximum(h.astype(F32) + b, 0.0)).astype(BF16),)


def _each(f, *lists):
    return [f(*xs) for xs in zip(*lists)]


def _swa_blocks(q4, kp, kc, vp, vc, sink, first):
    rows = ATT_GROUP * WINDOW
    qi = lax.broadcasted_iota(jnp.int32, (rows, 2 * WINDOW), 0) & (WINDOW - 1)
    si = lax.broadcasted_iota(jnp.int32, (rows, 2 * WINDOW), 1)
    diff = qi + WINDOW - si
    valid = (diff >= 0) & (diff < WINDOW) & (si >= jnp.where(first, WINDOW, 0))
    q = _each(lambda a: a.reshape(rows, ATT_HD), q4)
    k = _each(lambda a, b: jnp.concatenate([a, b], axis=0), kp, kc)
    v = _each(lambda a, b: jnp.concatenate([a, b], axis=0), vp, vc)
    s = _each(lambda q, k: jnp.where(valid, bdot_nt(q, k) * (ATT_HD ** -0.5), -jnp.inf), q, k)
    m = _each(lambda s, sink: lax.stop_gradient(jnp.maximum(jnp.max(s, axis=-1, keepdims=True), sink)), s, sink)
    p = _each(lambda s, m: jnp.exp(s - m), s, m)
    pn = _each(lambda p, sink, m: p / (jnp.sum(p, axis=-1, keepdims=True) + jnp.exp(sink - m)), p, sink, m)
    return _each(lambda pn, v: bdot_nn(pn, v).reshape(ATT_GROUP, WINDOW, ATT_HD).astype(BF16), pn, v)


def make_swa(name):
    heads = range(ATT_KV)

    def specs():
        q_spec = pl.BlockSpec((ATT_HEADS, WINDOW, ATT_HD), lambda n: (0, n, 0))
        cur = pl.BlockSpec((ATT_KV, WINDOW, ATT_HD), lambda n: (0, n, 0))
        prev = pl.BlockSpec((ATT_KV, WINDOW, ATT_HD), lambda n: (0, jnp.maximum(n - 1, 0), 0))
        sink = pl.BlockSpec((ATT_KV, ATT_GROUP * WINDOW, 1), lambda n: (0, 0, 0))
        return q_spec, cur, prev, sink

    def group(ref, h):
        return ref.at[pl.ds(h * ATT_GROUP, ATT_GROUP)]

    def load(q_ref, kp_ref, kc_ref, vp_ref, vc_ref, s_ref):
        return [[group(q_ref, h)[...] for h in heads]] + [[r[h] for h in heads] for r in (kp_ref, kc_ref, vp_ref, vc_ref, s_ref)]

    def fwd_call(q, k, v, sink):
        s = q.shape[1]
        q_spec, cur, prev, sink_spec = specs()

        def body(q_ref, kp_ref, kc_ref, vp_ref, vc_ref, s_ref, o_ref):
            o = _swa_blocks(*load(q_ref, kp_ref, kc_ref, vp_ref, vc_ref, s_ref), pl.program_id(0) == 0)
            for h in heads:
                group(o_ref, h)[...] = o[h]

        return pl.pallas_call(
            body, name=name + "_fwd", grid=(s // WINDOW,), in_specs=[q_spec, prev, cur, prev, cur, sink_spec],
            out_specs=q_spec, out_shape=jax.ShapeDtypeStruct(q.shape, BF16),
            compiler_params=_params(("parallel",)))(q, k, k, v, v, sink)

    def bwd_call(q, k, v, sink, do):
        s = q.shape[1]
        q_spec, cur, prev, sink_spec = specs()

        def body(q_ref, kp_ref, kc_ref, vp_ref, vc_ref, s_ref, do_ref, dq_ref, dkp_ref, dkc_ref, dvp_ref, dvc_ref, ds_ref):
            first = pl.program_id(0) == 0
            _, vjp = jax.vjp(lambda *a: _swa_blocks(*a, first), *load(q_ref, kp_ref, kc_ref, vp_ref, vc_ref, s_ref))
            dq, dkp, dkc, dvp, dvc, ds = vjp([group(do_ref, h)[...] for h in heads])

            @pl.when(first)
            def _():
                ds_ref[...] = jnp.zeros_like(ds_ref)

            for h in heads:
                group(dq_ref, h)[...] = dq[h]
                dkp_ref[h], dkc_ref[h], dvp_ref[h], dvc_ref[h] = dkp[h], dkc[h], dvp[h], dvc[h]
                ds_ref[h] += ds[h]

        kv = jax.ShapeDtypeStruct(k.shape, k.dtype)
        return pl.pallas_call(
            body, name=name + "_bwd", grid=(s // WINDOW,), in_specs=[q_spec, prev, cur, prev, cur, sink_spec, q_spec],
            out_specs=[q_spec, cur, cur, cur, cur, sink_spec],
            out_shape=[jax.ShapeDtypeStruct(q.shape, q.dtype), kv, kv, kv, kv, jax.ShapeDtypeStruct(sink.shape, F32)],
            compiler_params=_params(("arbitrary",)))(q, k, k, v, v, sink, do)

    @jax.custom_vjp
    def swa(q, k, v, sink):
        return fwd_call(q, k, v, sink)

    def fwd(q, k, v, sink):
        return swa(q, k, v, sink), (q, k, v, sink)

    def bwd(res, do):
        q, k, v, sink = res
        dq, dkp, dkc, dvp, dvc, ds = bwd_call(q, k, v, sink, do)

        def fold(cur, prev):
            shifted = jnp.concatenate([prev[:, WINDOW:], jnp.zeros_like(prev[:, :WINDOW])], axis=1)
            return (cur.astype(F32) + shifted.astype(F32)).astype(cur.dtype)

        return dq, fold(dkc, dkp), fold(dvc, dvp), ds

    swa.defvjp(fwd, bwd)
    return swa


def _shift_down(x, d, row):
    return x if d == 0 else jnp.where(row >= d, pltpu.roll(x, d, 0), 0.0)


def _shift_up(x, d, row):
    n = x.shape[0]
    return x if d == 0 else jnp.where(row < n - d, pltpu.roll(x, n - d, 0), 0.0)


def _conv_pre(x, w, row):
    return sum(w[j:j + 1, :] * _shift_down(x, CONV_K - 1 - j, row) for j in range(CONV_K))


def _prep_post(pre, kind):
    act = _silu(pre)
    if kind == "v":
        return act
    r = lax.rsqrt(jnp.sum(jnp.square(act), axis=-1, keepdims=True) + RMS_EPS)
    return act * r * (DN_HD ** -0.5 if kind == "q" else 1.0)


def make_gdn_prep(name, kind):
    def fwd_call(x, w):
        s = x.shape[0]

        def body(x_ref, w_ref, o_ref):
            row = lax.broadcasted_iota(jnp.int32, (s, DN_HD), 0)
            o_ref[...] = _prep_post(_conv_pre(x_ref[...].astype(F32), w_ref[...], row), kind)

        return pl.pallas_call(
            body, name=name + "_fwd", grid=(DN_HEADS,),
            in_specs=[pl.BlockSpec((s, DN_HD), lambda j: (0, j)), pl.BlockSpec((CONV_K, DN_HD), lambda j: (0, j))],
            out_specs=pl.BlockSpec((None, s, DN_HD), lambda j: (j, 0, 0)),
            out_shape=jax.ShapeDtypeStruct((DN_HEADS, s, DN_HD), F32), compiler_params=_params(("parallel",)))(x, w)

    def bwd_call(x, w, dy):
        s = x.shape[0]

        def body(x_ref, w_ref, dy_ref, dx_ref, dw_ref):
            row = lax.broadcasted_iota(jnp.int32, (s, DN_HD), 0)
            xv, wv = x_ref[...].astype(F32), w_ref[...]
            _, vjp = jax.vjp(lambda p: _prep_post(p, kind), _conv_pre(xv, wv, row))
            (dpre,) = vjp(dy_ref[...])
            dx_ref[...] = sum(wv[j:j + 1, :] * _shift_up(dpre, CONV_K - 1 - j, row) for j in range(CONV_K)).astype(dx_ref.dtype)
            for j in range(CONV_K):
                dw_ref[j:j + 1, :] = jnp.sum(dpre * _shift_down(xv, CONV_K - 1 - j, row), axis=0, keepdims=True)

        x_spec = pl.BlockSpec((s, DN_HD), lambda j: (0, j))
        w_spec = pl.BlockSpec((CONV_K, DN_HD), lambda j: (0, j))
        return pl.pallas_call(
            body, name=name + "_bwd", grid=(DN_HEADS,),
            in_specs=[x_spec, w_spec, pl.BlockSpec((None, s, DN_HD), lambda j: (j, 0, 0))], out_specs=[x_spec, w_spec],
            out_shape=[jax.ShapeDtypeStruct(x.shape, x.dtype), jax.ShapeDtypeStruct(w.shape, F32)],
            compiler_params=_params(("parallel",)))(x, w, dy)

    @jax.custom_vjp
    def prep(x, w):
        return fwd_call(x, w)

    def fwd(x, w):
        return prep(x, w), (x, w)

    def bwd(res, dy):
        return tuple(bwd_call(*res, dy))

    prep.defvjp(fwd, bwd)
    return prep


@jax.custom_vjp
def _saved_inverse(m, t):
    return t


def _saved_inverse_fwd(m, t):
    return t, t


def _saved_inverse_bwd(t, dt):
    return _hdg(_hdg(t, dt, 0, 0), t, 1, 1), jnp.zeros_like(t)


_saved_inverse.defvjp(_saved_inverse_fwd, _saved_inverse_bwd)


def _chunk_fn(q, k, v, beta, g, state, t_saved=None):
    c = CHUNK
    r = lax.broadcasted_iota(jnp.int32, (c, c), 0)
    cc = lax.broadcasted_iota(jnp.int32, (c, c), 1)
    eye = (r == cc).astype(F32)
    causal, strict = r >= cc, r > cc
    g_row = _each(lambda g: jnp.sum(g * eye, axis=0, keepdims=True), g)
    gc = _each(lambda g_row: jnp.sum(jnp.where(causal, g_row, 0.0), axis=1, keepdims=True), g_row)
    gc_row = _each(lambda gc: jnp.sum(gc * eye, axis=0, keepdims=True), gc)
    decay = _each(lambda gc, gc_row: jnp.exp(jnp.where(causal, gc - gc_row, -jnp.inf)), gc, gc_row)
    kb = _each(jnp.multiply, k, beta)
    vb = _each(jnp.multiply, v, beta)
    kk = _each(bdot_nt, kb, k)
    qk = _each(bdot_nt, q, k)
    m = _each(lambda kk, decay: -jnp.where(strict, kk * decay, 0.0), kk, decay)
    if t_saved is None:
        t, p = _each(lambda m: eye + m, m), m
        for _ in range(5):
            p = _each(hdot, p, p)
            t = _each(lambda t, p: t + hdot(t, p), t, p)
    else:
        t = _each(_saved_inverse, m, t_saved)
    eg = _each(jnp.exp, gc)
    u = _each(hdot, t, vb)
    w = _each(lambda t, kb, eg: hdot(t, kb * eg), t, kb, eg)
    ws = _each(bdot_nn, w, state)
    qs = _each(lambda q, eg, state: bdot_nn(q * eg, state), q, eg, state)
    v_new = _each(jnp.subtract, u, ws)
    o = _each(lambda qs, qk, decay, v_new: qs + bdot_nn(qk * decay, v_new), qs, qk, decay, v_new)
    g_last = _each(lambda g: jnp.sum(g, axis=0, keepdims=True), g)
    kv = _each(lambda k, g_last, gc, v_new: bdot_tn(k * jnp.exp(g_last - gc), v_new), k, g_last, gc, v_new)
    new_state = _each(lambda state, g_last, kv: state * jnp.exp(g_last) + kv, state, g_last, kv)
    return o, new_state, t


def _gate_columns(gates):
    lane = lax.broadcasted_iota(jnp.int32, gates.shape, 1)

    def column(at):
        return jnp.sum(jnp.where(lane == at, gates, 0.0), axis=1, keepdims=True)

    return [column(h) for h in range(DN_HEADS)], [column(DN_HEADS + h) for h in range(DN_HEADS)]


def make_delta(name):
    heads = range(DN_HEADS)

    def specs(n_chunks, reverse):
        def at(n):
            return n_chunks - 1 - n if reverse else n

        x_spec = pl.BlockSpec((DN_HEADS, CHUNK, DN_HD), lambda n: (0, at(n), 0))
        g_spec = pl.BlockSpec((CHUNK, LANES), lambda n: (at(n), 0))
        st_spec = pl.BlockSpec((DN_HEADS, None, DN_HD, DN_HD), lambda n: (0, at(n), 0, 0))
        t_spec = pl.BlockSpec((DN_HEADS, None, CHUNK, CHUNK), lambda n: (0, at(n), 0, 0))
        return x_spec, g_spec, st_spec, t_spec

    def per_head(ref):
        return [ref[i] for i in heads]

    def fwd_call(qh, kh, vh, gates):
        n_chunks = qh.shape[1] // CHUNK
        x_spec, g_spec, st_spec, t_spec = specs(n_chunks, False)

        def body(q_ref, k_ref, v_ref, g_ref, o_ref, st_ref, t_ref, state):
            @pl.when(pl.program_id(0) == 0)
            def _():
                state[...] = jnp.zeros_like(state)

            s_in = per_head(state)
            o, s_new, t = _chunk_fn(per_head(q_ref), per_head(k_ref), per_head(v_ref), *_gate_columns(g_ref[...]), s_in)
            for i in heads:
                st_ref[i], o_ref[i], state[i], t_ref[i] = s_in[i], o[i], s_new[i], t[i]

        return pl.pallas_call(
            body, name=name + "_fwd", grid=(n_chunks,), in_specs=[x_spec, x_spec, x_spec, g_spec], out_specs=[x_spec, st_spec, t_spec],
            out_shape=[jax.ShapeDtypeStruct(qh.shape, F32), jax.ShapeDtypeStruct((DN_HEADS, n_chunks, DN_HD, DN_HD), F32),
                       jax.ShapeDtypeStruct((DN_HEADS, n_chunks, CHUNK, CHUNK), F32)],
            scratch_shapes=[pltpu.VMEM((DN_HEADS, DN_HD, DN_HD), F32)], compiler_params=_params(("arbitrary",)))(qh, kh, vh, gates)

    def bwd_call(qh, kh, vh, gates, states, inverses, do):
        n_chunks = qh.shape[1] // CHUNK
        x_spec, g_spec, st_spec, t_spec = specs(n_chunks, True)

        def body(q_ref, k_ref, v_ref, g_ref, st_ref, t_ref, do_ref, dq_ref, dk_ref, dv_ref, dg_ref, dstate):
            @pl.when(pl.program_id(0) == 0)
            def _():
                dstate[...] = jnp.zeros_like(dstate)

            t_saved = per_head(t_ref)
            _, vjp = jax.vjp(lambda q, k, v, gates, state: _chunk_fn(q, k, v, *_gate_columns(gates), state, t_saved=t_saved)[:2],
                             per_head(q_ref), per_head(k_ref), per_head(v_ref), g_ref[...], per_head(st_ref))
            dq, dk, dv, dgates, ds = vjp((per_head(do_ref), per_head(dstate)))
            dg_ref[...] = dgates
            for i in heads:
                dq_ref[i], dk_ref[i], dv_ref[i], dstate[i] = dq[i], dk[i], dv[i], ds[i]

        big = jax.ShapeDtypeStruct(qh.shape, F32)
        return pl.pallas_call(
            body, name=name + "_bwd", grid=(n_chunks,), in_specs=[x_spec, x_spec, x_spec, g_spec, st_spec, t_spec, x_spec],
            out_specs=[x_spec, x_spec, x_spec, g_spec], out_shape=[big, big, big, jax.ShapeDtypeStruct(gates.shape, F32)],
            scratch_shapes=[pltpu.VMEM((DN_HEADS, DN_HD, DN_HD), F32)],
            compiler_params=_params(("arbitrary",)))(qh, kh, vh, gates, states, inverses, do)

    @jax.custom_vjp
    def delta(qh, kh, vh, gates):
        return fwd_call(qh, kh, vh, gates)[0]

    def fwd(qh, kh, vh, gates):
        o, states, inverses = fwd_call(qh, kh, vh, gates)
        return o, (qh, kh, vh, gates, states, inverses)

    def bwd(res, do):
        return tuple(bwd_call(*res, do))

    delta.defvjp(fwd, bwd)
    return delta


def loss_head(y, target):
    s, d = y.shape
    tm = min(256, s)

    def body(y_ref, t_ref, l_ref, dy_ref):
        err = y_ref[...] - t_ref[...]
        dy_ref[...] = err * (1.0 / d)

        @pl.when(pl.program_id(0) == 0)
        def _():
            l_ref[...] = jnp.zeros_like(l_ref)

        l_ref[...] += 0.5 * jnp.sum(jnp.mean(jnp.square(err), axis=-1, keepdims=True), axis=0, keepdims=True)

    spec = pl.BlockSpec((tm, d), lambda i: (i, 0))
    return pl.pallas_call(
        body, name="loss_head", grid=(s // tm,), in_specs=[spec, spec], out_specs=[pl.BlockSpec((8, LANES), lambda i: (0, 0)), spec],
        out_shape=[jax.ShapeDtypeStruct((8, LANES), F32), jax.ShapeDtypeStruct(y.shape, F32)], compiler_params=_params(("arbitrary",)))(y, target)


def ada_project(c_all, w_ada, b_cols):
    tn = 256
    cols = w_ada.shape[2]

    def body(c_ref, w_ref, b_ref, ca_ref, o_ref):
        c_act = _silu(c_ref[...])
        ca_ref[...] = c_act
        o_ref[...] = _dg(c_act, w_ref[...], 1, 0) + b_ref[...]

    return pl.pallas_call(
        body, name="ada_project", grid=(DEPTH, cols // tn),
        in_specs=[pl.BlockSpec((N_DEV, D), lambda l, j: (0, 0)), pl.BlockSpec((None, D, tn), lambda l, j: (l, 0, j)),
                  pl.BlockSpec((None, 1, tn), lambda l, j: (l, 0, j))],
        out_specs=[pl.BlockSpec((N_DEV, D), lambda l, j: (0, 0)), pl.BlockSpec((None, N_DEV, tn), lambda l, j: (l, 0, j))],
        out_shape=[jax.ShapeDtypeStruct((N_DEV, D), F32), jax.ShapeDtypeStruct((DEPTH, N_DEV, cols), F32)],
        compiler_params=_params(("arbitrary", "arbitrary")))(c_all, w_ada, b_cols.reshape(DEPTH, 1, cols))


def sum_partials(parts, name):
    n_parts, r, c = parts.shape
    tr = _tile(r, 512, 16 if parts.dtype == BF16 else 8)

    def body(p_ref, o_ref):
        total = p_ref[0].astype(F32)
        for part in range(1, n_parts):
            total = total + p_ref[part].astype(F32)
        o_ref[...] = total

    return pl.pallas_call(
        body, name=name, grid=(r // tr,), in_specs=[pl.BlockSpec((n_parts, tr, c), lambda i: (0, i, 0))],
        out_specs=pl.BlockSpec((tr, c), lambda i: (i, 0)), out_shape=jax.ShapeDtypeStruct((r, c), F32),
        compiler_params=_params(("parallel",)))(parts)


def _adamw(w, g, m, v):
    m = ADAM_B1 * m + (1.0 - ADAM_B1) * g
    v = ADAM_B2 * v + (1.0 - ADAM_B2) * jnp.square(g)
    m_hat = m / (1.0 - ADAM_B1 ** ADAM_STEP)
    v_hat = v / (1.0 - ADAM_B2 ** ADAM_STEP)
    return -ADAM_LR * (m_hat / (jnp.sqrt(v_hat) + ADAM_EPS) + ADAM_WD * w), m, v


def adamw(w, g, m, v, name, lead_tile=None):
    l, r, c = w.shape

    def body(w_ref, g_ref, m_ref, v_ref, d_ref, nm_ref, nv_ref):
        d_ref[...], nm_ref[...], nv_ref[...] = _adamw(w_ref[...], g_ref[...], m_ref[...], v_ref[...])

    if lead_tile is None:
        tr = _tile(r, 512, 8)
        spec, grid = pl.BlockSpec((None, tr, c), lambda a, i: (a, i, 0)), (l, r // tr)
    else:
        spec, grid = pl.BlockSpec((lead_tile, r, c), lambda a, i: (a, 0, 0)), (l // lead_tile, 1)
    shape = jax.ShapeDtypeStruct(w.shape, F32)
    return pl.pallas_call(body, name=name, grid=grid, in_specs=[spec] * 4, out_specs=[spec] * 3, out_shape=[shape] * 3,
                          compiler_params=_params(("parallel", "parallel")))(w, g, m, v)


def adamw_ada(c_act_t, dmod, w, m, v):
    l, r, c = w.shape
    tr = 256

    def body(c_ref, d_ref, w_ref, m_ref, v_ref, g_ref, dl_ref, nm_ref, nv_ref):
        g = hdot(c_ref[...], d_ref[...])
        g_ref[...] = g
        dl_ref[...], nm_ref[...], nv_ref[...] = _adamw(w_ref[...], g, m_ref[...], v_ref[...])

    spec = pl.BlockSpec((None, tr, c), lambda a, i: (a, i, 0))
    shape = jax.ShapeDtypeStruct(w.shape, F32)
    return pl.pallas_call(
        body, name="adamw_ada", grid=(l, r // tr),
        in_specs=[pl.BlockSpec((tr, LANES), lambda a, i: (i, 0)), pl.BlockSpec((None, LANES, c), lambda a, i: (a, 0, 0)), spec, spec, spec],
        out_specs=[spec] * 4, out_shape=[shape] * 4, compiler_params=_params(("parallel", "parallel")))(c_act_t, dmod, w, m, v)


def _place():
    x, y, c = lax.axis_index("x"), lax.axis_index("y"), lax.axis_index("c")
    return x, y, c


def _comm_call(body, name, ins, out_shapes, space, n_sems):
    n = len(ins)
    return pl.pallas_call(
        body, name=name, out_shape=out_shapes, in_specs=[pl.BlockSpec(memory_space=space)] * n,
        out_specs=[pl.BlockSpec(memory_space=space)] * n,
        scratch_shapes=[pltpu.SemaphoreType.DMA((n, n_sems)), pltpu.SemaphoreType.DMA((n, n_sems)), pltpu.SemaphoreType.DMA((n,))],
        compiler_params=pltpu.CompilerParams(vmem_limit_bytes=VMEM_LIMIT_BYTES))(*ins)


def all_gather(shards, name, in_vmem):
    n = len(shards)

    def body(*refs):
        x_refs, out_refs, (send_sems, recv_sems, local_sems) = refs[:n], refs[n:2 * n], refs[2 * n:]
        x, y, c = _place()
        me, sibling = (x, y, c), (x, y, 1 - c)
        chips = [(1 - x, y), (x, 1 - y), (1 - x, 1 - y)]

        def rows(a, px, py, pc):
            return out_refs[a].at[4 * px + 2 * py + pc]

        def copy(a, k, block, to, from_shard=False):
            return pltpu.make_async_remote_copy(
                src_ref=x_refs[a] if from_shard else rows(a, *block), dst_ref=rows(a, *block), send_sem=send_sems.at[a, k],
                recv_sem=recv_sems.at[a, k], device_id=to, device_id_type=MESH_IDS)

        arrays = range(n)
        mine = [pltpu.make_async_copy(x_refs[a], rows(a, *me), local_sems.at[a]) for a in arrays]
        first = [copy(a, 1 + j, me, (*chip, c), True) for j, chip in enumerate(chips) for a in arrays]
        first += [copy(a, 0, me, sibling, True) for a in arrays]
        for cp in mine + first:
            cp.start()
        passed = []
        for j, chip in enumerate(chips):
            for a in arrays:
                copy(a, 1 + j, (*chip, c), me).wait_recv()
                passed.append(copy(a, 4 + j, (*chip, c), sibling))
                passed[-1].start()
        for a in arrays:
            copy(a, 0, sibling, me).wait_recv()
            for j, chip in enumerate(chips):
                copy(a, 4 + j, (*chip, 1 - c), me).wait_recv()
        for cp in first + passed:
            cp.wait_send()
        for cp in mine:
            cp.wait()

    out_shapes = [jax.ShapeDtypeStruct((N_DEV,) + s.shape, s.dtype) for s in shards]
    return _comm_call(body, name, shards, out_shapes, pltpu.VMEM if in_vmem else pl.ANY, 7)


def all_gather_forwarding(shards, name):
    n = len(shards)

    def body(*refs):
        x_refs, out_refs, (send_sems, recv_sems, local_sems) = refs[:n], refs[n:2 * n], refs[2 * n:]
        x, y, c = _place()
        me = (x, y, c)

        def rows(a, dev, half):
            block = out_refs[a].at[4 * dev[0] + 2 * dev[1] + dev[2]]
            half_rows = shards[a].shape[0] // 2
            return block if half is None else block.at[pl.ds(half * half_rows, half_rows)]

        def copy(a, k, block, to, half=None, from_shard=False):
            return pltpu.make_async_remote_copy(
                src_ref=x_refs[a] if from_shard else rows(a, block, half), dst_ref=rows(a, block, half), send_sem=send_sems.at[a, k],
                recv_sem=recv_sems.at[a, k], device_id=to, device_id_type=MESH_IDS)

        def other_core(dev):
            return (dev[0], dev[1], 1 - c)

        sibling, x_nbr, y_nbr, diag = other_core(me), (1 - x, y, c), (x, 1 - y, c), (1 - x, 1 - y, c)
        arrays = range(n)
        mine = [pltpu.make_async_copy(x_refs[a], rows(a, me, None), local_sems.at[a]) for a in arrays]
        sent = [copy(a, k, me, to, from_shard=True) for k, to in ((1, x_nbr), (2, y_nbr), (0, sibling)) for a in arrays]
        for cp in mine + sent:
            cp.start()

        def on_arrival(k, block, half, passes):
            for a in arrays:
                copy(a, k, block, me, half).wait_recv()
                for k_out, to, half_out in passes:
                    sent.append(copy(a, k_out, block, to, half_out))
                    sent[-1].start()

        on_arrival(1, x_nbr, None, [(3, y_nbr, 0), (5, sibling, None)])
        on_arrival(2, y_nbr, None, [(4, x_nbr, 1), (6, sibling, None)])
        on_arrival(3, diag, 0, [(7, sibling, 0)])
        on_arrival(4, diag, 1, [(8, sibling, 1)])
        on_arrival(0, sibling, None, [])
        on_arrival(5, other_core(x_nbr), None, [])
        on_arrival(6, other_core(y_nbr), None, [])
        on_arrival(7, other_core(diag), 0, [])
        on_arrival(8, other_core(diag), 1, [])
        for cp in sent:
            cp.wait_send()
        for cp in mine:
            cp.wait()

    out_shapes = [jax.ShapeDtypeStruct((N_DEV,) + s.shape, s.dtype) for s in shards]
    return _comm_call(body, name, shards, out_shapes, pl.ANY, 9)


def scatter_to_sibling(slabs, name):
    n = len(slabs)

    def body(*refs):
        in_refs, out_refs, (send_sems, recv_sems, _) = refs[:n], refs[n:2 * n], refs[2 * n:]
        x, y, c = _place()
        copies = [pltpu.make_async_remote_copy(
            src_ref=in_refs[a].at[2 * chip + 1 - c], dst_ref=out_refs[a].at[chip], send_sem=send_sems.at[a, chip],
            recv_sem=recv_sems.at[a, chip], device_id=(x, y, 1 - c), device_id_type=MESH_IDS) for chip in range(4) for a in range(n)]
        for cp in copies:
            cp.start()
        for cp in copies:
            cp.wait_recv()
        for cp in copies:
            cp.wait_send()

    out_shapes = [jax.ShapeDtypeStruct((4,) + s.shape[1:], s.dtype) for s in slabs]
    return _comm_call(body, name, slabs, out_shapes, pl.ANY, 4)


def scatter_to_chips(slabs, name):
    n = len(slabs)

    def body(*refs):
        in_refs, out_refs, (send_sems, recv_sems, local_sems) = refs[:n], refs[n:2 * n], refs[2 * n:]
        x, y, c = _place()
        my_chip = 2 * x + y
        mine = [pltpu.make_async_copy(in_refs[a].at[my_chip], out_refs[a].at[my_chip], local_sems.at[a]) for a in range(n)]
        copies = [pltpu.make_async_remote_copy(
            src_ref=in_refs[a].at[2 * px + py], dst_ref=out_refs[a].at[my_chip], send_sem=send_sems.at[a, j], recv_sem=recv_sems.at[a, j],
            device_id=(px, py, c), device_id_type=MESH_IDS)
            for j, (px, py) in enumerate([(1 - x, y), (x, 1 - y), (1 - x, 1 - y)]) for a in range(n)]
        for cp in mine + copies:
            cp.start()
        for cp in copies:
            cp.wait_recv()
        for cp in copies:
            cp.wait_send()
        for cp in mine:
            cp.wait()

    return _comm_call(body, name, slabs, [jax.ShapeDtypeStruct(s.shape, s.dtype) for s in slabs], pl.ANY, 3)


def pair_sum(slabs, got, name):
    _, r, c = slabs.shape
    tr = _tile(r, 1024, 16)

    def body(a_ref, b_ref, o_ref):
        o_ref[...] = (a_ref[...].astype(F32) + b_ref[...].astype(F32)).astype(BF16)

    return pl.pallas_call(
        body, name=name, grid=(4, r // tr),
        in_specs=[pl.BlockSpec((None, tr, c), lambda s, i: (2 * s + lax.axis_index("c"), i, 0)), pl.BlockSpec((None, tr, c), lambda s, i: (s, i, 0))],
        out_specs=pl.BlockSpec((None, tr, c), lambda s, i: (s, i, 0)), out_shape=jax.ShapeDtypeStruct((4, r, c), BF16),
        compiler_params=_params(("parallel", "parallel")))(slabs, got)


BIG = ("w_in", "w_oa", "w_ob", "w_out", "w_ff1", "w_ff2")
SMALL = ("a_log", "dt_bias", "sinks", "dn_norm_w", "ln1_g", "ln1_b", "b_ff1", "b_ff2", "ln2_g", "ln2_b")


def _pack_rows(arrs, width, unit):
    flat = jnp.concatenate([a.reshape(-1) for a in arrs])
    rows = -(-flat.shape[0] // (width * unit)) * unit
    return jnp.pad(flat, (0, rows * width - flat.shape[0])).reshape(rows, width)


def _split_flat(flat, like):
    out, off = [], 0
    for a in like:
        n = 1
        for dim in a.shape:
            n *= dim
        out.append(flat[off:off + n].reshape(a.shape))
        off += n
    return out


ROW_SHARDED = ("w_oa", "w_ob", "w_out", "w_ff2")
W_IN_COL_TILE = 74


def _pack_shards(given):
    rows = jnp.concatenate([given[n].reshape(-1, D) for n in ROW_SHARDED]).astype(BF16)
    return [given["w_in"].astype(BF16).reshape(DEPTH * D, -1), rows, given["w_ff1"].astype(BF16).reshape(DEPTH * D, -1)]


def _unpack_weights(g_in, g_rows, g_ff1):
    w_in = g_in.reshape(N_DEV, DEPTH, D, -1)
    w_ff1 = g_ff1.reshape(N_DEV, DEPTH, D, -1)
    layers = []
    for l in range(DEPTH):
        full_in = w_in[:, l].transpose(1, 0, 2).reshape(D, -1)
        small = jnp.pad(full_in[:, IN_AT["small"]:IN_AT["ga"]], ((0, 0), (0, LANES + IN_AT["small"] - IN_AT["ga"])))
        groups = [jnp.concatenate([full_in[:, :QKV_W], small], axis=1)] + [full_in[:, IN_AT[g]:IN_AT[g] + D] for g in IN_GROUPS[1:]]
        lay, off = dict(w_in=tuple(groups), w_ff1=w_ff1[:, l].transpose(1, 0, 2).reshape(D, D_FF)), 0
        for n in ROW_SHARDED:
            per = (D_FF if n == "w_ff2" else D) // N_DEV
            lay[n] = g_rows[:, off + l * per:off + (l + 1) * per].reshape(per * N_DEV, D)
            off += DEPTH * per
        layers.append(lay)
    return layers


def _pack_grads(grads):
    def in_order(qkvs, dq, dk, dv, z, ga, gb):
        return jnp.concatenate([qkvs[:, :QKV_W], dq, dk, dv, z, qkvs[:, QKV_W:QKV_W + IN_AT["ga"] - IN_AT["small"]], ga, gb], axis=1)

    w_in = jnp.stack([in_order(*lay["w_in"]) for lay in grads])
    s_in = w_in.reshape(DEPTH, D, N_DEV, -1).transpose(2, 0, 1, 3).reshape(N_DEV, DEPTH * D, -1)
    rows = []
    for n in ROW_SHARDED:
        w = jnp.stack([lay[n] for lay in grads])
        rows.append(w.reshape(DEPTH, N_DEV, -1, D).transpose(1, 0, 2, 3).reshape(N_DEV, -1, D))
    w = jnp.stack([lay["w_ff1"] for lay in grads])
    s_ff1 = w.reshape(DEPTH, D, N_DEV, -1).transpose(2, 0, 1, 3).reshape(N_DEV, DEPTH * D, -1)
    return [s_in, jnp.concatenate(rows, axis=1), s_ff1]


def _layer(ops, x, u, mod, next_mod, sm, conv_w, w):
    s = x.shape[0]
    sh1, sc1, gt1, sh2, sc2, gt2 = (mod[:, i * D:(i + 1) * D] for i in range(6))
    qkvs, dq, dk, dv, z, ga, gb = ops["in_proj"](u, w["w_in"])
    q, k, v, small = qkvs[:, :D], qkvs[:, D:D + 256], qkvs[:, D + 256:QKV_W], qkvs[:, QKV_W:]
    qh = q.reshape(s, ATT_HEADS, ATT_HD).transpose(1, 0, 2)
    kh = k.reshape(s, ATT_KV, ATT_HD).transpose(1, 0, 2)
    vh = v.reshape(s, ATT_KV, ATT_HD).transpose(1, 0, 2)
    sink = jnp.broadcast_to(sm["sinks"].reshape(ATT_KV, ATT_GROUP, 1, 1), (ATT_KV, ATT_GROUP, WINDOW, 1)).reshape(ATT_KV, ATT_GROUP * WINDOW, 1)
    attn = ops["swa"](qh, kh, vh, sink).transpose(1, 0, 2).reshape(s, ATT_HEADS * ATT_HD)
    y_a = ops["mm_oa"](attn, w["w_oa"])
    qn = ops["prep_q"](dq, conv_w[:, :D])
    kn = ops["prep_k"](dk, conv_w[:, D:2 * D])
    vn = ops["prep_v"](dv, conv_w[:, 2 * D:])
    a_vec = jnp.pad(sm["a_log"], ((0, 0), (DN_HEADS, LANES - 2 * DN_HEADS)))
    b_vec = jnp.pad(sm["dt_bias"], ((0, 0), (DN_HEADS, LANES - 2 * DN_HEADS)))
    (gates,) = ops["gates"](small, a_vec, b_vec)
    o = ops["delta"](qn, kn, vn, gates)
    (og,) = ops["gdn_post"](o, z, sm["dn_norm_w"])
    y_b = ops["mm_ob"](og, w["w_ob"])
    (mix,) = ops["mix"](ga, gb, y_a, y_b)
    mixed = ops["mm_out"](mix, w["w_out"])
    x1, u2 = ops["ln1"](x, mixed, gt1, sm["ln1_g"], sm["ln1_b"], sc2, sh2)
    (h,) = ops["relu2"](ops["mm_ff1"](u2, w["w_ff1"]), sm["b_ff1"])
    f = ops["mm_ff2"](h, w["w_ff2"])
    if next_mod is None:
        return ops["ln2"](x1, f, gt2, sm["b_ff2"], sm["ln2_g"], sm["ln2_b"])[0], None
    return ops["ln2"](x1, f, gt2, sm["b_ff2"], sm["ln2_g"], sm["ln2_b"], next_mod[:, D:2 * D], next_mod[:, :D])


def _make_ops(l):
    t = f"l{l}_"
    last = l == DEPTH - 1
    return dict(
        in_proj=make_in_proj(t + "in_proj"), swa=make_swa(t + "swa"),
        mm_oa=make_mm(t + "mm_oa", BF16), mm_ob=make_mm(t + "mm_ob", BF16), mm_out=make_mm(t + "mm_out"),
        mm_ff1=make_mm(t + "mm_ff1", BF16), mm_ff2=make_mm(t + "mm_ff2"),
        prep_q=make_gdn_prep(t + "prep_q", "q"), prep_k=make_gdn_prep(t + "prep_k", "k"), prep_v=make_gdn_prep(t + "prep_v", "v"),
        gates=make_rowwise(_gates_fn, t + "gates", "t", "cc", "t"), delta=make_delta(t + "delta"),
        gdn_post=make_rowwise(_gdn_post_fn, t + "gdn_post", "ht", "s", "t", nc=DN_HEADS, tm=1024),
        mix=make_rowwise(_mix_fn, t + "mix", "tttt", "", "t"),
        ln1=make_rowwise(_ln1_fn, t + "ln1", "tt", "ccccc", "tt"),
        ln2=make_rowwise(_ln2_last_fn, t + "ln2", "tt", "cccc", "t") if last else make_rowwise(_ln2_fn, t + "ln2", "tt", "cccccc", "tt"),
        relu2=make_rowwise(_relu2_fn, t + "relu2", "t", "c", "t", nc=4, tm=512))


def kernel(x, c, w_ada, b_ada, w_in, conv_w, a_log, dt_bias, sinks, dn_norm_w, w_oa, w_ob, w_out, ln1_g, ln1_b, w_ff1, b_ff1, w_ff2, b_ff2, ln2_g, ln2_b, loss_target, m_w_ada, m_b_ada, m_w_in, m_conv_w, m_a_log, m_dt_bias, m_sinks, m_dn_norm_w, m_w_oa, m_w_ob, m_w_out, m_ln1_g, m_ln1_b, m_w_ff1, m_b_ff1, m_w_ff2, m_b_ff2, m_ln2_g, m_ln2_b, v_w_ada, v_b_ada, v_w_in, v_conv_w, v_a_log, v_dt_bias, v_sinks, v_dn_norm_w, v_w_oa, v_w_ob, v_w_out, v_ln1_g, v_ln1_b, v_w_ff1, v_b_ff1, v_w_ff2, v_b_ff2, v_ln2_g, v_ln2_b):
    given = dict(locals())
    me = 4 * lax.axis_index("x") + 2 * lax.axis_index("y") + lax.axis_index("c")
    conv_cols = conv_w.shape[2]

    gathered = all_gather([_pack_rows([c, conv_w], LANES, 8)], "gather_c_conv", True)[0].reshape(N_DEV, -1)
    c_all = gathered[:, :D]
    conv_full = gathered[:, D:D + DEPTH * CONV_K * conv_cols].reshape(N_DEV, DEPTH, CONV_K, conv_cols).transpose(1, 2, 0, 3).reshape(DEPTH, CONV_K, -1)

    b_cols = lax.dynamic_slice_in_dim(b_ada, me * ADA_COLS, ADA_COLS, axis=1)
    c_act_all, mod_cols = ada_project(c_all, w_ada, b_cols)
    mod_all = all_gather([mod_cols.reshape(-1, LANES)], "gather_mod", True)[0].reshape(N_DEV, DEPTH, N_DEV, ADA_COLS)
    mods = lax.dynamic_index_in_dim(mod_all, me, axis=2, keepdims=False).transpose(1, 0, 2).reshape(DEPTH, 6 * D)

    weights = _unpack_weights(*all_gather_forwarding(_pack_shards(given), "gather_weights"))

    small = {n: given[n] for n in SMALL}
    ops = [_make_ops(l) for l in range(DEPTH)]

    modulate0 = make_rowwise(_modulate_fn, "modulate0", "t", "cc", "t")

    def forward(x0, mods, small, conv_full, weights):
        (u,) = modulate0(x0, mods[:1, D:2 * D], mods[:1, :D])
        h = x0
        for l in range(DEPTH):
            next_mod = mods[l + 1:l + 2] if l + 1 < DEPTH else None
            h, u = _layer(ops[l], h, u, mods[l:l + 1], next_mod, {n: a[l:l + 1] for n, a in small.items()}, conv_full[l], weights[l])
        return h

    y, vjp = jax.vjp(forward, x[0], mods, small, conv_full, weights)
    loss_tile, dy = loss_head(y, loss_target[0])
    dx, d_mods, d_small, d_conv, d_weights = vjp(dy)
    loss = lax.psum(loss_tile[0, 0], AXES)

    slabs = _pack_grads(d_weights)
    from_sibling = scatter_to_sibling(slabs, "scatter_sibling")
    per_chip = [pair_sum(a, b, f"pair_sum_{i}") for i, (a, b) in enumerate(zip(slabs, from_sibling))]
    g_in, g_rows, g_ff1 = [sum_partials(p, f"sum_big_{i}") for i, p in enumerate(scatter_to_chips(per_chip, "scatter_chips"))]
    grad, off = dict(w_in=g_in.reshape(w_in.shape), w_ff1=g_ff1.reshape(w_ff1.shape)), 0
    for n in ROW_SHARDED:
        rows = given[n].shape[0] * given[n].shape[1]
        grad[n] = g_rows[off:off + rows].reshape(given[n].shape)
        off += rows
    delta, new_m, new_v = {}, {}, {}
    for n in BIG[1:]:
        delta[n], new_m[n], new_v[n] = adamw(given[n], grad[n], given["m_" + n], given["v_" + n], "adamw_" + n)
    cols_first = [a.transpose(2, 0, 1) for a in (w_in, grad["w_in"], m_w_in, v_w_in)]
    delta["w_in"], new_m["w_in"], new_v["w_in"] = (
        a.transpose(1, 2, 0) for a in adamw(*cols_first, "adamw_w_in", lead_tile=W_IN_COL_TILE))

    partial = [d_small[n] for n in SMALL] + [d_mods, d_conv]
    (parts,) = all_gather([_pack_rows(partial, LANES, 8)], "gather_small_grads", True)
    mods_at = sum(d_small[n].size for n in SMALL)
    d_mods_all = parts.reshape(N_DEV, -1)[:, mods_at:mods_at + DEPTH * 6 * D].reshape(N_DEV, DEPTH, 6 * D)
    total = _split_flat(sum_partials(parts, "sum_small").reshape(-1), partial)
    for n, g in zip(SMALL, total):
        grad[n] = g
    grad["b_ada"] = total[len(SMALL)]
    grad["conv_w"] = lax.dynamic_slice_in_dim(total[len(SMALL) + 1], me * conv_cols, conv_cols, axis=2)
    names = SMALL + ("b_ada", "conv_w")
    packed = [_pack_rows([src[p + n] for n in names], LANES, 8)[None] for src, p in ((given, ""), (grad, ""), (given, "m_"), (given, "v_"))]
    outs = adamw(*packed, "adamw_small")
    for res, o in zip((delta, new_m, new_v), outs):
        for n, a in zip(names, _split_flat(o.reshape(-1), [given[n] for n in names])):
            res[n] = a

    dmod_mine = lax.dynamic_slice_in_dim(d_mods_all, me * ADA_COLS, ADA_COLS, axis=2).transpose(1, 0, 2)
    pad = LANES - N_DEV
    grad["w_ada"], delta["w_ada"], new_m["w_ada"], new_v["w_ada"] = adamw_ada(
        jnp.pad(c_act_all.T, ((0, 0), (0, pad))), jnp.pad(dmod_mine, ((0, 0), (0, pad), (0, 0))), w_ada, m_w_ada, v_w_ada)

    order = ("w_ada", "b_ada", "w_in", "conv_w", "a_log", "dt_bias", "sinks", "dn_norm_w", "w_oa", "w_ob", "w_out", "ln1_g", "ln1_b",
             "w_ff1", "b_ff1", "w_ff2", "b_ff2", "ln2_g", "ln2_b")
    return (loss, dx[None], *[grad[n] for n in order], *[delta[n] for n in order], *[new_m[n] for n in order], *[new_v[n] for n in order])
```

```python
import functools

import jax
import jax.numpy as jnp
from jax import lax
from jax.experimental import pallas as pl
from jax.experimental.pallas import tpu as pltpu

F32 = jnp.float32
BF16 = jnp.bfloat16

D = 1024
DEPTH = 4
N_DEV = 8
ATT_HEADS, ATT_KV, ATT_GROUP, ATT_HD, WINDOW = 16, 4, 4, 64, 128
DN_HEADS, DN_HD, CONV_K, CHUNK = 8, 128, 4, 64
D_FF = 4096
ADA_COLS = 6 * D // N_DEV
ALPHA = (2 * DEPTH) ** 0.25
LN_EPS = 1e-5
RMS_EPS = 1e-6
ADAM_LR, ADAM_B1, ADAM_B2, ADAM_EPS, ADAM_WD, ADAM_STEP = 0.001, 0.9, 0.999, 1e-08, 0.01, 10
AXES = ("x", "y", "c")
MESH_IDS = pl.DeviceIdType.MESH
VMEM_LIMIT_BYTES = 48 * 1024 * 1024
LANES = 128
HIGHEST = lax.Precision.HIGHEST

IN_SPLITS = (("q", 1024), ("k", 256), ("v", 256), ("dq", 1024), ("dk", 1024), ("dv", 1024), ("z", 1024),
             ("small", 16), ("ga", 1024), ("gb", 1024))
IN_AT = {name: sum(w for _, w in IN_SPLITS[:i]) for i, (name, _) in enumerate(IN_SPLITS)}
IN_GROUPS = ("qkvs", "dq", "dk", "dv", "z", "ga", "gb")
QKV_W = 1536
QKVS_W = QKV_W + LANES


def _params(sem=None):
    return pltpu.CompilerParams(dimension_semantics=sem, vmem_limit_bytes=VMEM_LIMIT_BYTES)


def _tile(n, pref, unit):
    if n <= pref:
        return n
    t = (pref // unit) * unit
    while t > unit and n % t:
        t -= unit
    assert n % t == 0, (n, pref, unit)
    return t


def _dg(a, b, ca, cb):
    return lax.dot_general(a.astype(BF16), b.astype(BF16), (((ca,), (cb,)), ((), ())), preferred_element_type=F32)


@jax.custom_vjp
def bdot_nn(a, b):
    return _dg(a, b, 1, 0)


def _bdot_nn_fwd(a, b):
    return _dg(a, b, 1, 0), (a, b)


def _bdot_nn_bwd(res, g):
    a, b = res
    return _dg(g, b, 1, 1).astype(a.dtype), _dg(a, g, 0, 0).astype(b.dtype)


bdot_nn.defvjp(_bdot_nn_fwd, _bdot_nn_bwd)


@jax.custom_vjp
def bdot_nt(a, b):
    return _dg(a, b, 1, 1)


def _bdot_nt_fwd(a, b):
    return _dg(a, b, 1, 1), (a, b)


def _bdot_nt_bwd(res, g):
    a, b = res
    return _dg(g, b, 1, 0).astype(a.dtype), _dg(g, a, 0, 0).astype(b.dtype)


bdot_nt.defvjp(_bdot_nt_fwd, _bdot_nt_bwd)


@jax.custom_vjp
def bdot_tn(a, b):
    return _dg(a, b, 0, 0)


def _bdot_tn_fwd(a, b):
    return _dg(a, b, 0, 0), (a, b)


def _bdot_tn_bwd(res, g):
    a, b = res
    return _dg(b, g, 1, 1).astype(a.dtype), _dg(a, g, 1, 0).astype(b.dtype)


bdot_tn.defvjp(_bdot_tn_fwd, _bdot_tn_bwd)


def _hdg(a, b, ca, cb):
    a_hi, b_hi = a.astype(BF16), b.astype(BF16)
    a_lo, b_lo = (a - a_hi.astype(F32)).astype(BF16), (b - b_hi.astype(F32)).astype(BF16)

    def dot(x, y):
        return lax.dot_general(x, y, (((ca,), (cb,)), ((), ())), preferred_element_type=F32)

    return dot(a_hi, b_hi) + (dot(a_hi, b_lo) + dot(a_lo, b_hi))


@jax.custom_vjp
def hdot(a, b):
    return _hdg(a, b, 1, 0)


def _hdot_fwd(a, b):
    return _hdg(a, b, 1, 0), (a, b)


def _hdot_bwd(res, g):
    a, b = res
    return _hdg(g, b, 1, 1), _hdg(a, g, 0, 0)


hdot.defvjp(_hdot_fwd, _hdot_bwd)


def matmul(a, b, mode, out_dtype, name, acc=None):
    if mode == "nn":
        (m, k), (k2, n) = a.shape, b.shape
    elif mode == "nt":
        (m, k), (n, k2) = a.shape, b.shape
    else:
        (k, m), (k2, n) = a.shape, b.shape
    assert k == k2, (a.shape, b.shape, mode)
    def pick(n_, pref):
        t = _tile(n_, pref, LANES)
        return n_ if t < 2 * LANES and n_ <= 2048 else t

    tm, tn, tk = pick(m, 1024), pick(n, 1024), pick(k, 2048)
    nk = k // tk
    a_spec = pl.BlockSpec((tk, tm), lambda i, j, kk: (kk, i)) if mode == "tn" else pl.BlockSpec((tm, tk), lambda i, j, kk: (i, kk))
    b_spec = pl.BlockSpec((tn, tk), lambda i, j, kk: (j, kk)) if mode == "nt" else pl.BlockSpec((tk, tn), lambda i, j, kk: (kk, j))
    o_spec = pl.BlockSpec((tm, tn), lambda i, j, kk: (i, j))
    ca, cb = {"nn": (1, 0), "nt": (1, 1), "tn": (0, 0)}[mode]

    def body(*refs):
        a_ref, b_ref = refs[:2]
        c_ref = None if acc is None else refs[2]
        o_ref = refs[2 if acc is None else 3]
        part = _dg(a_ref[...], b_ref[...], ca, cb)
        if nk == 1:
            o_ref[...] = (part if c_ref is None else part + c_ref[...].astype(F32)).astype(out_dtype)
            return
        acc_ref = refs[-1]
        kk = pl.program_id(2)

        @pl.when(kk == 0)
        def _():
            acc_ref[...] = part if c_ref is None else part + c_ref[...].astype(F32)

        @pl.when(jnp.logical_and(kk > 0, kk < nk - 1))
        def _():
            acc_ref[...] += part

        @pl.when(kk == nk - 1)
        def _():
            o_ref[...] = (acc_ref[...] + part).astype(out_dtype)

    ins, in_specs = [a, b], [a_spec, b_spec]
    if acc is not None:
        ins.append(acc)
        in_specs.append(o_spec)
    return pl.pallas_call(
        body, name=name, grid=(m // tm, n // tn, nk), in_specs=in_specs, out_specs=o_spec,
        out_shape=jax.ShapeDtypeStruct((m, n), out_dtype), scratch_shapes=[pltpu.VMEM((tm, tn), F32)] if nk > 1 else [],
        compiler_params=_params(("parallel", "parallel", "arbitrary")))(*ins)


def make_mm(name, out_dtype=F32):
    @jax.custom_vjp
    def mm(a, w):
        return matmul(a, w, "nn", out_dtype, name + "_fwd")

    def fwd(a, w):
        return mm(a, w), (a, w)

    def bwd(res, g):
        a, w = res
        return matmul(g, w, "nt", a.dtype, name + "_da"), matmul(a, g, "tn", w.dtype, name + "_dw")

    mm.defvjp(fwd, bwd)
    return mm


def make_in_proj(name):
    @jax.custom_vjp
    def in_proj(u, ws):
        return tuple(matmul(u, w, "nn", BF16, f"{name}_fwd_{g}") for g, w in zip(IN_GROUPS, ws))

    def fwd(u, ws):
        return in_proj(u, ws), (u, ws)

    def bwd(res, gs):
        u, ws = res
        du = None
        for idx, (g, w, dy) in enumerate(zip(IN_GROUPS, ws, gs)):
            last = idx == len(ws) - 1
            du = matmul(dy, w, "nt", u.dtype if last else F32, f"{name}_du_{g}", acc=du)
        dws = tuple(matmul(u, dy, "tn", w.dtype, f"{name}_dw_{g}") for g, w, dy in zip(IN_GROUPS, ws, gs))
        return du, dws

    in_proj.defvjp(fwd, bwd)
    return in_proj


def make_rowwise(fn, name, tile_kinds, param_kinds, out_kinds, nc=1, tm=256):
    n_t, n_p, n_o = len(tile_kinds), len(param_kinds), len(out_kinds)

    def width(a, kind):
        if kind == "h":
            return a.shape[2]
        return a.shape[1] if kind == "s" else a.shape[1] // nc

    def spec(kind, w, rows):
        if kind == "t":
            return pl.BlockSpec((rows, w), lambda j, i: (i, j))
        if kind == "h":
            return pl.BlockSpec((None, rows, w), lambda j, i: (j, i, 0))
        if kind == "c":
            return pl.BlockSpec((1, w), lambda j, i: (0, j))
        return pl.BlockSpec((1, w), lambda j, i: (0, 0))

    def full_shape(kind, w, s):
        return (s, w * nc) if kind == "t" else (nc, s, w)

    def plan(tiles, params):
        s = tiles[0].shape[0] if tile_kinds[0] == "t" else tiles[0].shape[1]
        rows = min(tm, s)
        t_w = [width(a, kd) for a, kd in zip(tiles, tile_kinds)]
        p_w = [width(a, kd) for a, kd in zip(params, param_kinds)]
        t_s = [jax.ShapeDtypeStruct((rows, w), a.dtype) for a, w in zip(tiles, t_w)]
        p_s = [jax.ShapeDtypeStruct((1, w), a.dtype) for a, w in zip(params, p_w)]
        o_s = jax.eval_shape(fn, *t_s, *p_s)
        return s, rows, t_w, p_w, o_s

    def fwd_call(*args):
        tiles, params = args[:n_t], args[n_t:]
        s, rows, t_w, p_w, o_s = plan(tiles, params)

        def body(*refs):
            ins, outs = refs[:n_t + n_p], refs[n_t + n_p:]
            res = fn(*[r[...] for r in ins])
            for o_ref, val in zip(outs, res):
                o_ref[...] = val

        in_specs = [spec(kd, w, rows) for kd, w in zip(tile_kinds, t_w)] + [spec(kd, w, rows) for kd, w in zip(param_kinds, p_w)]
        return pl.pallas_call(
            body, name=name + "_fwd", grid=(nc, s // rows), in_specs=in_specs,
            out_specs=[spec(kd, o.shape[1], rows) for kd, o in zip(out_kinds, o_s)],
            out_shape=[jax.ShapeDtypeStruct(full_shape(kd, o.shape[1], s), o.dtype) for kd, o in zip(out_kinds, o_s)],
            compiler_params=_params(("parallel", "parallel")))(*args)

    def bwd_call(args, douts):
        tiles, params = args[:n_t], args[n_t:]
        s, rows, t_w, p_w, o_s = plan(tiles, params)

        def body(*refs):
            ins = refs[:n_t + n_p]
            dos = refs[n_t + n_p:n_t + n_p + n_o]
            dts = refs[n_t + n_p + n_o:n_t + n_p + n_o + n_t]
            dps = refs[n_t + n_p + n_o + n_t:]
            j, i = pl.program_id(0), pl.program_id(1)
            _, vjp = jax.vjp(lambda *a: tuple(fn(*a)), *[r[...] for r in ins])
            grads = vjp(tuple(r[...] for r in dos))
            for r, g in zip(dts, grads[:n_t]):
                r[...] = g.astype(r.dtype)
            for r, g, kd in zip(dps, grads[n_t:], param_kinds):
                first = (i == 0) if kd == "c" else jnp.logical_and(i == 0, j == 0)

                @pl.when(first)
                def _(r=r):
                    r[...] = jnp.zeros_like(r)

                r[...] += g.astype(F32)

        in_specs = ([spec(kd, w, rows) for kd, w in zip(tile_kinds, t_w)] + [spec(kd, w, rows) for kd, w in zip(param_kinds, p_w)]
                    + [spec(kd, o.shape[1], rows) for kd, o in zip(out_kinds, o_s)])
        out_specs = [spec(kd, w, rows) for kd, w in zip(tile_kinds, t_w)] + [spec(kd, w, rows) for kd, w in zip(param_kinds, p_w)]
        out_shape = [jax.ShapeDtypeStruct(a.shape, a.dtype) for a in tiles] + [jax.ShapeDtypeStruct(a.shape, F32) for a in params]
        return pl.pallas_call(
            body, name=name + "_bwd", grid=(nc, s // rows), in_specs=in_specs, out_specs=out_specs, out_shape=out_shape,
            compiler_params=_params(("arbitrary", "arbitrary")))(*args, *douts)

    @jax.custom_vjp
    def op(*args):
        return tuple(fwd_call(*args))

    def op_fwd(*args):
        return op(*args), args

    def op_bwd(args, douts):
        return tuple(bwd_call(args, douts))

    op.defvjp(op_fwd, op_bwd)
    return op


def _sigmoid(x):
    return 1.0 / (1.0 + jnp.exp(-x))


def _silu(x):
    return x * _sigmoid(x)


def _softplus(x):
    return jnp.maximum(x, 0.0) + jnp.log(1.0 + jnp.exp(-jnp.abs(x)))


def _layer_norm(h, g, b):
    mu = jnp.mean(h, axis=-1, keepdims=True)
    var = jnp.mean(jnp.square(h - mu), axis=-1, keepdims=True)
    return (h - mu) * lax.rsqrt(var + LN_EPS) * g + b


def _modulate_fn(x, sc, sh):
    return ((x * (1.0 + sc) + sh).astype(BF16),)


def _gates_fn(x, a_vec, b_vec):
    x = x.astype(F32)
    lane = lax.broadcasted_iota(jnp.int32, x.shape, 1)
    beta = _sigmoid(x)
    g = -jnp.exp(a_vec) * _softplus(x + b_vec)
    return (jnp.where(lane < DN_HEADS, beta, jnp.where(lane < 2 * DN_HEADS, g, 0.0)),)


def _gdn_post_fn(o, z, nw):
    o = o * lax.rsqrt(jnp.mean(jnp.square(o), axis=-1, keepdims=True) + RMS_EPS) * nw
    return ((o * _silu(z.astype(F32))).astype(BF16),)


def _mix_fn(ga, gb, ya, yb):
    ga, gb, ya, yb = (t.astype(F32) for t in (ga, gb, ya, yb))
    return ((_sigmoid(ga) * ya + _sigmoid(gb) * yb).astype(BF16),)


def _ln1_fn(x, mixed, gt, g, b, sc, sh):
    y = _layer_norm(ALPHA * x + (1.0 + gt) * mixed, g, b)
    return y, _modulate_fn(y, sc, sh)[0]


def _ln2_last_fn(x, f, gt, bf, g, b):
    return (_layer_norm(ALPHA * x + (1.0 + gt) * (f + bf), g, b),)


def _ln2_fn(x, f, gt, bf, g, b, sc, sh):
    (y,) = _ln2_last_fn(x, f, gt, bf, g, b)
    return y, _modulate_fn(y, sc, sh)[0]


def _relu2_fn(h, b):
    return (jnp.square(jnp.maximum(h.astype(F32) + b, 0.0)).astype(BF16),)


def _each(f, *lists):
    return [f(*xs) for xs in zip(*lists)]


def _swa_blocks(q4, kp, kc, vp, vc, sink, first):
    rows = ATT_GROUP * WINDOW
    qi = lax.broadcasted_iota(jnp.int32, (rows, 2 * WINDOW), 0) & (WINDOW - 1)
    si = lax.broadcasted_iota(jnp.int32, (rows, 2 * WINDOW), 1)
    diff = qi + WINDOW - si
    valid = (diff >= 0) & (diff < WINDOW) & (si >= jnp.where(first, WINDOW, 0))
    q = _each(lambda a: a.reshape(rows, ATT_HD), q4)
    k = _each(lambda a, b: jnp.concatenate([a, b], axis=0), kp, kc)
    v = _each(lambda a, b: jnp.concatenate([a, b], axis=0), vp, vc)
    s = _each(lambda q, k: jnp.where(valid, bdot_nt(q, k) * (ATT_HD ** -0.5), -jnp.inf), q, k)
    m = _each(lambda s, sink: lax.stop_gradient(jnp.maximum(jnp.max(s, axis=-1, keepdims=True), sink)), s, sink)
    p = _each(lambda s, m: jnp.exp(s - m), s, m)
    pn = _each(lambda p, sink, m: p / (jnp.sum(p, axis=-1, keepdims=True) + jnp.exp(sink - m)), p, sink, m)
    return _each(lambda pn, v: bdot_nn(pn, v).reshape(ATT_GROUP, WINDOW, ATT_HD).astype(BF16), pn, v)


def make_swa(name):
    heads = range(ATT_KV)

    def specs():
        q_spec = pl.BlockSpec((ATT_HEADS, WINDOW, ATT_HD), lambda n: (0, n, 0))
        cur = pl.BlockSpec((ATT_KV, WINDOW, ATT_HD), lambda n: (0, n, 0))
        prev = pl.BlockSpec((ATT_KV, WINDOW, ATT_HD), lambda n: (0, jnp.maximum(n - 1, 0), 0))
        sink = pl.BlockSpec((ATT_KV, ATT_GROUP * WINDOW, 1), lambda n: (0, 0, 0))
        return q_spec, cur, prev, sink

    def group(ref, h):
        return ref.at[pl.ds(h * ATT_GROUP, ATT_GROUP)]

    def load(q_ref, kp_ref, kc_ref, vp_ref, vc_ref, s_ref):
        return [[group(q_ref, h)[...] for h in heads]] + [[r[h] for h in heads] for r in (kp_ref, kc_ref, vp_ref, vc_ref, s_ref)]

    def fwd_call(q, k, v, sink):
        s = q.shape[1]
        q_spec, cur, prev, sink_spec = specs()

        def body(q_ref, kp_ref, kc_ref, vp_ref, vc_ref, s_ref, o_ref):
            o = _swa_blocks(*load(q_ref, kp_ref, kc_ref, vp_ref, vc_ref, s_ref), pl.program_id(0) == 0)
            for h in heads:
                group(o_ref, h)[...] = o[h]

        return pl.pallas_call(
            body, name=name + "_fwd", grid=(s // WINDOW,), in_specs=[q_spec, prev, cur, prev, cur, sink_spec],
            out_specs=q_spec, out_shape=jax.ShapeDtypeStruct(q.shape, BF16),
            compiler_params=_params(("parallel",)))(q, k, k, v, v, sink)

    def bwd_call(q, k, v, sink, do):
        s = q.shape[1]
        q_spec, cur, prev, sink_spec = specs()

        def body(q_ref, kp_ref, kc_ref, vp_ref, vc_ref, s_ref, do_ref, dq_ref, dkp_ref, dkc_ref, dvp_ref, dvc_ref, ds_ref):
            first = pl.program_id(0) == 0
            _, vjp = jax.vjp(lambda *a: _swa_blocks(*a, first), *load(q_ref, kp_ref, kc_ref, vp_ref, vc_ref, s_ref))
            dq, dkp, dkc, dvp, dvc, ds = vjp([group(do_ref, h)[...] for h in heads])

            @pl.when(first)
            def _():
                ds_ref[...] = jnp.zeros_like(ds_ref)

            for h in heads:
                group(dq_ref, h)[...] = dq[h]
                dkp_ref[h], dkc_ref[h], dvp_ref[h], dvc_ref[h] = dkp[h], dkc[h], dvp[h], dvc[h]
                ds_ref[h] += ds[h]

        kv = jax.ShapeDtypeStruct(k.shape, k.dtype)
        return pl.pallas_call(
            body, name=name + "_bwd", grid=(s // WINDOW,), in_specs=[q_spec, prev, cur, prev, cur, sink_spec, q_spec],
            out_specs=[q_spec, cur, cur, cur, cur, sink_spec],
            out_shape=[jax.ShapeDtypeStruct(q.shape, q.dtype), kv, kv, kv, kv, jax.ShapeDtypeStruct(sink.shape, F32)],
            compiler_params=_params(("arbitrary",)))(q, k, k, v, v, sink, do)

    @jax.custom_vjp
    def swa(q, k, v, sink):
        return fwd_call(q, k, v, sink)

    def fwd(q, k, v, sink):
        return swa(q, k, v, sink), (q, k, v, sink)

    def bwd(res, do):
        q, k, v, sink = res
        dq, dkp, dkc, dvp, dvc, ds = bwd_call(q, k, v, sink, do)

        def fold(cur, prev):
            shifted = jnp.concatenate([prev[:, WINDOW:], jnp.zeros_like(prev[:, :WINDOW])], axis=1)
            return (cur.astype(F32) + shifted.astype(F32)).astype(cur.dtype)

        return dq, fold(dkc, dkp), fold(dvc, dvp), ds

    swa.defvjp(fwd, bwd)
    return swa


def _shift_down(x, d, row):
    return x if d == 0 else jnp.where(row >= d, pltpu.roll(x, d, 0), 0.0)


def _shift_up(x, d, row):
    n = x.shape[0]
    return x if d == 0 else jnp.where(row < n - d, pltpu.roll(x, n - d, 0), 0.0)


def _conv_pre(x, w, row):
    return sum(w[j:j + 1, :] * _shift_down(x, CONV_K - 1 - j, row) for j in range(CONV_K))


def _prep_post(pre, kind):
    act = _silu(pre)
    if kind == "v":
        return act
    r = lax.rsqrt(jnp.sum(jnp.square(act), axis=-1, keepdims=True) + RMS_EPS)
    return act * r * (DN_HD ** -0.5 if kind == "q" else 1.0)


def make_gdn_prep(name, kind):
    def fwd_call(x, w):
        s = x.shape[0]

        def body(x_ref, w_ref, o_ref):
            row = lax.broadcasted_iota(jnp.int32, (s, DN_HD), 0)
            o_ref[...] = _prep_post(_conv_pre(x_ref[...].astype(F32), w_ref[...], row), kind)

        return pl.pallas_call(
            body, name=name + "_fwd", grid=(DN_HEADS,),
            in_specs=[pl.BlockSpec((s, DN_HD), lambda j: (0, j)), pl.BlockSpec((CONV_K, DN_HD), lambda j: (0, j))],
            out_specs=pl.BlockSpec((None, s, DN_HD), lambda j: (j, 0, 0)),
            out_shape=jax.ShapeDtypeStruct((DN_HEADS, s, DN_HD), F32), compiler_params=_params(("parallel",)))(x, w)

    def bwd_call(x, w, dy):
        s = x.shape[0]

        def body(x_ref, w_ref, dy_ref, dx_ref, dw_ref):
            row = lax.broadcasted_iota(jnp.int32, (s, DN_HD), 0)
            xv, wv = x_ref[...].astype(F32), w_ref[...]
            _, vjp = jax.vjp(lambda p: _prep_post(p, kind), _conv_pre(xv, wv, row))
            (dpre,) = vjp(dy_ref[...])
            dx_ref[...] = sum(wv[j:j + 1, :] * _shift_up(dpre, CONV_K - 1 - j, row) for j in range(CONV_K)).astype(dx_ref.dtype)
            for j in range(CONV_K):
                dw_ref[j:j + 1, :] = jnp.sum(dpre * _shift_down(xv, CONV_K - 1 - j, row), axis=0, keepdims=True)

        x_spec = pl.BlockSpec((s, DN_HD), lambda j: (0, j))
        w_spec = pl.BlockSpec((CONV_K, DN_HD), lambda j: (0, j))
        return pl.pallas_call(
            body, name=name + "_bwd", grid=(DN_HEADS,),
            in_specs=[x_spec, w_spec, pl.BlockSpec((None, s, DN_HD), lambda j: (j, 0, 0))], out_specs=[x_spec, w_spec],
            out_shape=[jax.ShapeDtypeStruct(x.shape, x.dtype), jax.ShapeDtypeStruct(w.shape, F32)],
            compiler_params=_params(("parallel",)))(x, w, dy)

    @jax.custom_vjp
    def prep(x, w):
        return fwd_call(x, w)

    def fwd(x, w):
        return prep(x, w), (x, w)

    def bwd(res, dy):
        return tuple(bwd_call(*res, dy))

    prep.defvjp(fwd, bwd)
    return prep


@jax.custom_vjp
def _saved_inverse(m, t):
    return t


def _saved_inverse_fwd(m, t):
    return t, t


def _saved_inverse_bwd(t, dt):
    return _hdg(_hdg(t, dt, 0, 0), t, 1, 1), jnp.zeros_like(t)


_saved_inverse.defvjp(_saved_inverse_fwd, _saved_inverse_bwd)


def _chunk_fn(q, k, v, beta, g, state, t_saved=None):
    c = CHUNK
    r = lax.broadcasted_iota(jnp.int32, (c, c), 0)
    cc = lax.broadcasted_iota(jnp.int32, (c, c), 1)
    eye = (r == cc).astype(F32)
    causal, strict = r >= cc, r > cc
    g_row = _each(lambda g: jnp.sum(g * eye, axis=0, keepdims=True), g)
    gc = _each(lambda g_row: jnp.sum(jnp.where(causal, g_row, 0.0), axis=1, keepdims=True), g_row)
    gc_row = _each(lambda gc: jnp.sum(gc * eye, axis=0, keepdims=True), gc)
    decay = _each(lambda gc, gc_row: jnp.exp(jnp.where(causal, gc - gc_row, -jnp.inf)), gc, gc_row)
    kb = _each(jnp.multiply, k, beta)
    vb = _each(jnp.multiply, v, beta)
    kk = _each(bdot_nt, kb, k)
    qk = _each(bdot_nt, q, k)
    m = _each(lambda kk, decay: -jnp.where(strict, kk * decay, 0.0), kk, decay)
    if t_saved is None:
        t, p = _each(lambda m: eye + m, m), m
        for _ in range(5):
            p = _each(hdot, p, p)
            t = _each(lambda t, p: t + hdot(t, p), t, p)
    else:
        t = _each(_saved_inverse, m, t_saved)
    eg = _each(jnp.exp, gc)
    u = _each(hdot, t, vb)
    w = _each(lambda t, kb, eg: hdot(t, kb * eg), t, kb, eg)
    ws = _each(bdot_nn, w, state)
    qs = _each(lambda q, eg, state: bdot_nn(q * eg, state), q, eg, state)
    v_new = _each(jnp.subtract, u, ws)
    o = _each(lambda qs, qk, decay, v_new: qs + bdot_nn(qk * decay, v_new), qs, qk, decay, v_new)
    g_last = _each(lambda g: jnp.sum(g, axis=0, keepdims=True), g)
    kv = _each(lambda k, g_last, gc, v_new: bdot_tn(k * jnp.exp(g_last - gc), v_new), k, g_last, gc, v_new)
    new_state = _each(lambda state, g_last, kv: state * jnp.exp(g_last) + kv, state, g_last, kv)
    return o, new_state, t


def _gate_columns(gates):
    lane = lax.broadcasted_iota(jnp.int32, gates.shape, 1)

    def column(at):
        return jnp.sum(jnp.where(lane == at, gates, 0.0), axis=1, keepdims=True)

    return [column(h) for h in range(DN_HEADS)], [column(DN_HEADS + h) for h in range(DN_HEADS)]


def make_delta(name):
    heads = range(DN_HEADS)

    def specs(n_chunks, reverse):
        def at(n):
            return n_chunks - 1 - n if reverse else n

        x_spec = pl.BlockSpec((DN_HEADS, CHUNK, DN_HD), lambda n: (0, at(n), 0))
        g_spec = pl.BlockSpec((CHUNK, LANES), lambda n: (at(n), 0))
        st_spec = pl.BlockSpec((DN_HEADS, None, DN_HD, DN_HD), lambda n: (0, at(n), 0, 0))
        t_spec = pl.BlockSpec((DN_HEADS, None, CHUNK, CHUNK), lambda n: (0, at(n), 0, 0))
        return x_spec, g_spec, st_spec, t_spec

    def per_head(ref):
        return [ref[i] for i in heads]

    def fwd_call(qh, kh, vh, gates):
        n_chunks = qh.shape[1] // CHUNK
        x_spec, g_spec, st_spec, t_spec = specs(n_chunks, False)

        def body(q_ref, k_ref, v_ref, g_ref, o_ref, st_ref, t_ref, state):
            @pl.when(pl.program_id(0) == 0)
            def _():
                state[...] = jnp.zeros_like(state)

            s_in = per_head(state)
            o, s_new, t = _chunk_fn(per_head(q_ref), per_head(k_ref), per_head(v_ref), *_gate_columns(g_ref[...]), s_in)
            for i in heads:
                st_ref[i], o_ref[i], state[i], t_ref[i] = s_in[i], o[i], s_new[i], t[i]

        return pl.pallas_call(
            body, name=name + "_fwd", grid=(n_chunks,), in_specs=[x_spec, x_spec, x_spec, g_spec], out_specs=[x_spec, st_spec, t_spec],
            out_shape=[jax.ShapeDtypeStruct(qh.shape, F32), jax.ShapeDtypeStruct((DN_HEADS, n_chunks, DN_HD, DN_HD), F32),
                       jax.ShapeDtypeStruct((DN_HEADS, n_chunks, CHUNK, CHUNK), F32)],
            scratch_shapes=[pltpu.VMEM((DN_HEADS, DN_HD, DN_HD), F32)], compiler_params=_params(("arbitrary",)))(qh, kh, vh, gates)

    def bwd_call(qh, kh, vh, gates, states, inverses, do):
        n_chunks = qh.shape[1] // CHUNK
        x_spec, g_spec, st_spec, t_spec = specs(n_chunks, True)

        def body(q_ref, k_ref, v_ref, g_ref, st_ref, t_ref, do_ref, dq_ref, dk_ref, dv_ref, dg_ref, dstate):
            @pl.when(pl.program_id(0) == 0)
            def _():
                dstate[...] = jnp.zeros_like(dstate)

            t_saved = per_head(t_ref)
            _, vjp = jax.vjp(lambda q, k, v, gates, state: _chunk_fn(q, k, v, *_gate_columns(gates), state, t_saved=t_saved)[:2],
                             per_head(q_ref), per_head(k_ref), per_head(v_ref), g_ref[...], per_head(st_ref))
            dq, dk, dv, dgates, ds = vjp((per_head(do_ref), per_head(dstate)))
            dg_ref[...] = dgates
            for i in heads:
                dq_ref[i], dk_ref[i], dv_ref[i], dstate[i] = dq[i], dk[i], dv[i], ds[i]

        big = jax.ShapeDtypeStruct(qh.shape, F32)
        return pl.pallas_call(
            body, name=name + "_bwd", grid=(n_chunks,), in_specs=[x_spec, x_spec, x_spec, g_spec, st_spec, t_spec, x_spec],
            out_specs=[x_spec, x_spec, x_spec, g_spec], out_shape=[big, big, big, jax.ShapeDtypeStruct(gates.shape, F32)],
            scratch_shapes=[pltpu.VMEM((DN_HEADS, DN_HD, DN_HD), F32)],
            compiler_params=_params(("arbitrary",)))(qh, kh, vh, gates, states, inverses, do)

    @jax.custom_vjp
    def delta(qh, kh, vh, gates):
        return fwd_call(qh, kh, vh, gates)[0]

    def fwd(qh, kh, vh, gates):
        o, states, inverses = fwd_call(qh, kh, vh, gates)
        return o, (qh, kh, vh, gates, states, inverses)

    def bwd(res, do):
        return tuple(bwd_call(*res, do))

    delta.defvjp(fwd, bwd)
    return delta


def loss_head(y, target):
    s, d = y.shape
    tm = min(256, s)

    def body(y_ref, t_ref, l_ref, dy_ref):
        err = y_ref[...] - t_ref[...]
        dy_ref[...] = err * (1.0 / d)

        @pl.when(pl.program_id(0) == 0)
        def _():
            l_ref[...] = jnp.zeros_like(l_ref)

        l_ref[...] += 0.5 * jnp.sum(jnp.mean(jnp.square(err), axis=-1, keepdims=True), axis=0, keepdims=True)

    spec = pl.BlockSpec((tm, d), lambda i: (i, 0))
    return pl.pallas_call(
        body, name="loss_head", grid=(s // tm,), in_specs=[spec, spec], out_specs=[pl.BlockSpec((8, LANES), lambda i: (0, 0)), spec],
        out_shape=[jax.ShapeDtypeStruct((8, LANES), F32), jax.ShapeDtypeStruct(y.shape, F32)], compiler_params=_params(("arbitrary",)))(y, target)


def ada_project(c_all, w_ada, b_cols):
    tn = 256
    cols = w_ada.shape[2]

    def body(c_ref, w_ref, b_ref, ca_ref, o_ref):
        c_act = _silu(c_ref[...])
        ca_ref[...] = c_act
        o_ref[...] = _dg(c_act, w_ref[...], 1, 0) + b_ref[...]

    return pl.pallas_call(
        body, name="ada_project", grid=(DEPTH, cols // tn),
        in_specs=[pl.BlockSpec((N_DEV, D), lambda l, j: (0, 0)), pl.BlockSpec((None, D, tn), lambda l, j: (l, 0, j)),
                  pl.BlockSpec((None, 1, tn), lambda l, j: (l, 0, j))],
        out_specs=[pl.BlockSpec((N_DEV, D), lambda l, j: (0, 0)), pl.BlockSpec((None, N_DEV, tn), lambda l, j: (l, 0, j))],
        out_shape=[jax.ShapeDtypeStruct((N_DEV, D), F32), jax.ShapeDtypeStruct((DEPTH, N_DEV, cols), F32)],
        compiler_params=_params(("arbitrary", "arbitrary")))(c_all, w_ada, b_cols.reshape(DEPTH, 1, cols))


def sum_partials(parts, name):
    n_parts, r, c = parts.shape
    tr = _tile(r, 512, 16 if parts.dtype == BF16 else 8)

    def body(p_ref, o_ref):
        total = p_ref[0].astype(F32)
        for part in range(1, n_parts):
            total = total + p_ref[part].astype(F32)
        o_ref[...] = total

    return pl.pallas_call(
        body, name=name, grid=(r // tr,), in_specs=[pl.BlockSpec((n_parts, tr, c), lambda i: (0, i, 0))],
        out_specs=pl.BlockSpec((tr, c), lambda i: (i, 0)), out_shape=jax.ShapeDtypeStruct((r, c), F32),
        compiler_params=_params(("parallel",)))(parts)


def _adamw(w, g, m, v):
    m = ADAM_B1 * m + (1.0 - ADAM_B1) * g
    v = ADAM_B2 * v + (1.0 - ADAM_B2) * jnp.square(g)
    m_hat = m / (1.0 - ADAM_B1 ** ADAM_STEP)
    v_hat = v / (1.0 - ADAM_B2 ** ADAM_STEP)
    return -ADAM_LR * (m_hat / (jnp.sqrt(v_hat) + ADAM_EPS) + ADAM_WD * w), m, v


def adamw(w, g, m, v, name, lead_tile=None):
    l, r, c = w.shape

    def body(w_ref, g_ref, m_ref, v_ref, d_ref, nm_ref, nv_ref):
        d_ref[...], nm_ref[...], nv_ref[...] = _adamw(w_ref[...], g_ref[...], m_ref[...], v_ref[...])

    if lead_tile is None:
        tr = _tile(r, 512, 8)
        spec, grid = pl.BlockSpec((None, tr, c), lambda a, i: (a, i, 0)), (l, r // tr)
    else:
        spec, grid = pl.BlockSpec((lead_tile, r, c), lambda a, i: (a, 0, 0)), (l // lead_tile, 1)
    shape = jax.ShapeDtypeStruct(w.shape, F32)
    return pl.pallas_call(body, name=name, grid=grid, in_specs=[spec] * 4, out_specs=[spec] * 3, out_shape=[shape] * 3,
                          compiler_params=_params(("parallel", "parallel")))(w, g, m, v)


def adamw_ada(c_act_t, dmod, w, m, v):
    l, r, c = w.shape
    tr = 256

    def body(c_ref, d_ref, w_ref, m_ref, v_ref, g_ref, dl_ref, nm_ref, nv_ref):
        g = hdot(c_ref[...], d_ref[...])
        g_ref[...] = g
        dl_ref[...], nm_ref[...], nv_ref[...] = _adamw(w_ref[...], g, m_ref[...], v_ref[...])

    spec = pl.BlockSpec((None, tr, c), lambda a, i: (a, i, 0))
    shape = jax.ShapeDtypeStruct(w.shape, F32)
    return pl.pallas_call(
        body, name="adamw_ada", grid=(l, r // tr),
        in_specs=[pl.BlockSpec((tr, LANES), lambda a, i: (i, 0)), pl.BlockSpec((None, LANES, c), lambda a, i: (a, 0, 0)), spec, spec, spec],
        out_specs=[spec] * 4, out_shape=[shape] * 4, compiler_params=_params(("parallel", "parallel")))(c_act_t, dmod, w, m, v)


def _place():
    x, y, c = lax.axis_index("x"), lax.axis_index("y"), lax.axis_index("c")
    return x, y, c


def _comm_call(body, name, ins, out_shapes, space, n_sems):
    n = len(ins)
    return pl.pallas_call(
        body, name=name, out_shape=out_shapes, in_specs=[pl.BlockSpec(memory_space=space)] * n,
        out_specs=[pl.BlockSpec(memory_space=space)] * n,
        scratch_shapes=[pltpu.SemaphoreType.DMA((n, n_sems)), pltpu.SemaphoreType.DMA((n, n_sems)), pltpu.SemaphoreType.DMA((n,))],
        compiler_params=pltpu.CompilerParams(vmem_limit_bytes=VMEM_LIMIT_BYTES))(*ins)


def all_gather(shards, name, in_vmem):
    n = len(shards)

    def body(*refs):
        x_refs, out_refs, (send_sems, recv_sems, local_sems) = refs[:n], refs[n:2 * n], refs[2 * n:]
        x, y, c = _place()
        me, sibling = (x, y, c), (x, y, 1 - c)
        chips = [(1 - x, y), (x, 1 - y), (1 - x, 1 - y)]

        def rows(a, px, py, pc):
            return out_refs[a].at[4 * px + 2 * py + pc]

        def copy(a, k, block, to, from_shard=False):
            return pltpu.make_async_remote_copy(
                src_ref=x_refs[a] if from_shard else rows(a, *block), dst_ref=rows(a, *block), send_sem=send_sems.at[a, k],
                recv_sem=recv_sems.at[a, k], device_id=to, device_id_type=MESH_IDS)

        arrays = range(n)
        mine = [pltpu.make_async_copy(x_refs[a], rows(a, *me), local_sems.at[a]) for a in arrays]
        first = [copy(a, 1 + j, me, (*chip, c), True) for j, chip in enumerate(chips) for a in arrays]
        first += [copy(a, 0, me, sibling, True) for a in arrays]
        for cp in mine + first:
            cp.start()
        passed = []
        for j, chip in enumerate(chips):
            for a in arrays:
                copy(a, 1 + j, (*chip, c), me).wait_recv()
                passed.append(copy(a, 4 + j, (*chip, c), sibling))
                passed[-1].start()
        for a in arrays:
            copy(a, 0, sibling, me).wait_recv()
            for j, chip in enumerate(chips):
                copy(a, 4 + j, (*chip, 1 - c), me).wait_recv()
        for cp in first + passed:
            cp.wait_send()
        for cp in mine:
            cp.wait()

    out_shapes = [jax.ShapeDtypeStruct((N_DEV,) + s.shape, s.dtype) for s in shards]
    return _comm_call(body, name, shards, out_shapes, pltpu.VMEM if in_vmem else pl.ANY, 7)


def all_gather_forwarding(shards, name):
    n = len(shards)

    def body(*refs):
        x_refs, out_refs, (send_sems, recv_sems, local_sems) = refs[:n], refs[n:2 * n], refs[2 * n:]
        x, y, c = _place()
        me = (x, y, c)

        def rows(a, dev, half):
            block = out_refs[a].at[4 * dev[0] + 2 * dev[1] + dev[2]]
            half_rows = shards[a].shape[0] // 2
            return block if half is None else block.at[pl.ds(half * half_rows, half_rows)]

        def copy(a, k, block, to, half=None, from_shard=False):
            return pltpu.make_async_remote_copy(
                src_ref=x_refs[a] if from_shard else rows(a, block, half), dst_ref=rows(a, block, half), send_sem=send_sems.at[a, k],
                recv_sem=recv_sems.at[a, k], device_id=to, device_id_type=MESH_IDS)

        def other_core(dev):
            return (dev[0], dev[1], 1 - c)

        sibling, x_nbr, y_nbr, diag = other_core(me), (1 - x, y, c), (x, 1 - y, c), (1 - x, 1 - y, c)
        arrays = range(n)
        mine = [pltpu.make_async_copy(x_refs[a], rows(a, me, None), local_sems.at[a]) for a in arrays]
        sent = [copy(a, k, me, to, from_shard=True) for k, to in ((1, x_nbr), (2, y_nbr), (0, sibling)) for a in arrays]
        for cp in mine + sent:
            cp.start()

        def on_arrival(k, block, half, passes):
            for a in arrays:
                copy(a, k, block, me, half).wait_recv()
                for k_out, to, half_out in passes:
                    sent.append(copy(a, k_out, block, to, half_out))
                    sent[-1].start()

        on_arrival(1, x_nbr, None, [(3, y_nbr, 0), (5, sibling, None)])
        on_arrival(2, y_nbr, None, [(4, x_nbr, 1), (6, sibling, None)])
        on_arrival(3, diag, 0, [(7, sibling, 0)])
        on_arrival(4, diag, 1, [(8, sibling, 1)])
        on_arrival(0, sibling, None, [])
        on_arrival(5, other_core(x_nbr), None, [])
        on_arrival(6, other_core(y_nbr), None, [])
        on_arrival(7, other_core(diag), 0, [])
        on_arrival(8, other_core(diag), 1, [])
        for cp in sent:
            cp.wait_send()
        for cp in mine:
            cp.wait()

    out_shapes = [jax.ShapeDtypeStruct((N_DEV,) + s.shape, s.dtype) for s in shards]
    return _comm_call(body, name, shards, out_shapes, pl.ANY, 9)


def scatter_to_sibling(slabs, name):
    n = len(slabs)

    def body(*refs):
        in_refs, out_refs, (send_sems, recv_sems, _) = refs[:n], refs[n:2 * n], refs[2 * n:]
        x, y, c = _place()
        copies = [pltpu.make_async_remote_copy(
            src_ref=in_refs[a].at[2 * chip + 1 - c], dst_ref=out_refs[a].at[chip], send_sem=send_sems.at[a, chip],
            recv_sem=recv_sems.at[a, chip], device_id=(x, y, 1 - c), device_id_type=MESH_IDS) for chip in range(4) for a in range(n)]
        for cp in copies:
            cp.start()
        for cp in copies:
            cp.wait_recv()
        for cp in copies:
            cp.wait_send()

    out_shapes = [jax.ShapeDtypeStruct((4,) + s.shape[1:], s.dtype) for s in slabs]
    return _comm_call(body, name, slabs, out_shapes, pl.ANY, 4)


def scatter_to_chips(slabs, name):
    n = len(slabs)

    def body(*refs):
        in_refs, out_refs, (send_sems, recv_sems, local_sems) = refs[:n], refs[n:2 * n], refs[2 * n:]
        x, y, c = _place()
        my_chip = 2 * x + y
        mine = [pltpu.make_async_copy(in_refs[a].at[my_chip], out_refs[a].at[my_chip], local_sems.at[a]) for a in range(n)]
        copies = [pltpu.make_async_remote_copy(
            src_ref=in_refs[a].at[2 * px + py], dst_ref=out_refs[a].at[my_chip], send_sem=send_sems.at[a, j], recv_sem=recv_sems.at[a, j],
            device_id=(px, py, c), device_id_type=MESH_IDS)
            for j, (px, py) in enumerate([(1 - x, y), (x, 1 - y), (1 - x, 1 - y)]) for a in range(n)]
        for cp in mine + copies:
            cp.start()
        for cp in copies:
            cp.wait_recv()
        for cp in copies:
            cp.wait_send()
        for cp in mine:
            cp.wait()

    return _comm_call(body, name, slabs, [jax.ShapeDtypeStruct(s.shape, s.dtype) for s in slabs], pl.ANY, 3)


def pair_sum(slabs, got, name):
    _, r, c = slabs.shape
    tr = _tile(r, 1024, 16)

    def body(a_ref, b_ref, o_ref):
        o_ref[...] = (a_ref[...].astype(F32) + b_ref[...].astype(F32)).astype(BF16)

    return pl.pallas_call(
        body, name=name, grid=(4, r // tr),
        in_specs=[pl.BlockSpec((None, tr, c), lambda s, i: (2 * s + lax.axis_index("c"), i, 0)), pl.BlockSpec((None, tr, c), lambda s, i: (s, i, 0))],
        out_specs=pl.BlockSpec((None, tr, c), lambda s, i: (s, i, 0)), out_shape=jax.ShapeDtypeStruct((4, r, c), BF16),
        compiler_params=_params(("parallel", "parallel")))(slabs, got)


BIG = ("w_in", "w_oa", "w_ob", "w_out", "w_ff1", "w_ff2")
SMALL = ("a_log", "dt_bias", "sinks", "dn_norm_w", "ln1_g", "ln1_b", "b_ff1", "b_ff2", "ln2_g", "ln2_b")


def _pack_rows(arrs, width, unit):
    flat = jnp.concatenate([a.reshape(-1) for a in arrs])
    rows = -(-flat.shape[0] // (width * unit)) * unit
    return jnp.pad(flat, (0, rows * width - flat.shape[0])).reshape(rows, width)


def _split_flat(flat, like):
    out, off = [], 0
    for a in like:
        n = 1
        for dim in a.shape:
            n *= dim
        out.append(flat[off:off + n].reshape(a.shape))
        off += n
    return out


ROW_SHARDED = ("w_oa", "w_ob", "w_out", "w_ff2")
W_IN_COL_TILE = 74


def _pack_shards(given):
    rows = jnp.concatenate([given[n].reshape(-1, D) for n in ROW_SHARDED]).astype(BF16)
    return [given["w_in"].astype(BF16).reshape(DEPTH * D, -1), rows, given["w_ff1"].astype(BF16).reshape(DEPTH * D, -1)]


def _unpack_weights(g_in, g_rows, g_ff1):
    w_in = g_in.reshape(N_DEV, DEPTH, D, -1)
    w_ff1 = g_ff1.reshape(N_DEV, DEPTH, D, -1)
    layers = []
    for l in range(DEPTH):
        full_in = w_in[:, l].transpose(1, 0, 2).reshape(D, -1)
        small = jnp.pad(full_in[:, IN_AT["small"]:IN_AT["ga"]], ((0, 0), (0, LANES + IN_AT["small"] - IN_AT["ga"])))
        groups = [jnp.concatenate([full_in[:, :QKV_W], small], axis=1)] + [full_in[:, IN_AT[g]:IN_AT[g] + D] for g in IN_GROUPS[1:]]
        lay, off = dict(w_in=tuple(groups), w_ff1=w_ff1[:, l].transpose(1, 0, 2).reshape(D, D_FF)), 0
        for n in ROW_SHARDED:
            per = (D_FF if n == "w_ff2" else D) // N_DEV
            lay[n] = g_rows[:, off + l * per:off + (l + 1) * per].reshape(per * N_DEV, D)
            off += DEPTH * per
        layers.append(lay)
    return layers


def _pack_grads(grads):
    def in_order(qkvs, dq, dk, dv, z, ga, gb):
        return jnp.concatenate([qkvs[:, :QKV_W], dq, dk, dv, z, qkvs[:, QKV_W:QKV_W + IN_AT["ga"] - IN_AT["small"]], ga, gb], axis=1)

    w_in = jnp.stack([in_order(*lay["w_in"]) for lay in grads])
    s_in = w_in.reshape(DEPTH, D, N_DEV, -1).transpose(2, 0, 1, 3).reshape(N_DEV, DEPTH * D, -1)
    rows = []
    for n in ROW_SHARDED:
        w = jnp.stack([lay[n] for lay in grads])
        rows.append(w.reshape(DEPTH, N_DEV, -1, D).transpose(1, 0, 2, 3).reshape(N_DEV, -1, D))
    w = jnp.stack([lay["w_ff1"] for lay in grads])
    s_ff1 = w.reshape(DEPTH, D, N_DEV, -1).transpose(2, 0, 1, 3).reshape(N_DEV, DEPTH * D, -1)
    return [s_in, jnp.concatenate(rows, axis=1), s_ff1]


def _layer(ops, x, u, mod, next_mod, sm, conv_w, w):
    s = x.shape[0]
    sh1, sc1, gt1, sh2, sc2, gt2 = (mod[:, i * D:(i + 1) * D] for i in range(6))
    qkvs, dq, dk, dv, z, ga, gb = ops["in_proj"](u, w["w_in"])
    q, k, v, small = qkvs[:, :D], qkvs[:, D:D + 256], qkvs[:, D + 256:QKV_W], qkvs[:, QKV_W:]
    qh = q.reshape(s, ATT_HEADS, ATT_HD).transpose(1, 0, 2)
    kh = k.reshape(s, ATT_KV, ATT_HD).transpose(1, 0, 2)
    vh = v.reshape(s, ATT_KV, ATT_HD).transpose(1, 0, 2)
    sink = jnp.broadcast_to(sm["sinks"].reshape(ATT_KV, ATT_GROUP, 1, 1), (ATT_KV, ATT_GROUP, WINDOW, 1)).reshape(ATT_KV, ATT_GROUP * WINDOW, 1)
    attn = ops["swa"](qh, kh, vh, sink).transpose(1, 0, 2).reshape(s, ATT_HEADS * ATT_HD)
    y_a = ops["mm_oa"](attn, w["w_oa"])
    qn = ops["prep_q"](dq, conv_w[:, :D])
    kn = ops["prep_k"](dk, conv_w[:, D:2 * D])
    vn = ops["prep_v"](dv, conv_w[:, 2 * D:])
    a_vec = jnp.pad(sm["a_log"], ((0, 0), (DN_HEADS, LANES - 2 * DN_HEADS)))
    b_vec = jnp.pad(sm["dt_bias"], ((0, 0), (DN_HEADS, LANES - 2 * DN_HEADS)))
    (gates,) = ops["gates"](small, a_vec, b_vec)
    o = ops["delta"](qn, kn, vn, gates)
    (og,) = ops["gdn_post"](o, z, sm["dn_norm_w"])
    y_b = ops["mm_ob"](og, w["w_ob"])
    (mix,) = ops["mix"](ga, gb, y_a, y_b)
    mixed = ops["mm_out"](mix, w["w_out"])
    x1, u2 = ops["ln1"](x, mixed, gt1, sm["ln1_g"], sm["ln1_b"], sc2, sh2)
    (h,) = ops["relu2"](ops["mm_ff1"](u2, w["w_ff1"]), sm["b_ff1"])
    f = ops["mm_ff2"](h, w["w_ff2"])
    if next_mod is None:
        return ops["ln2"](x1, f, gt2, sm["b_ff2"], sm["ln2_g"], sm["ln2_b"])[0], None
    return ops["ln2"](x1, f, gt2, sm["b_ff2"], sm["ln2_g"], sm["ln2_b"], next_mod[:, D:2 * D], next_mod[:, :D])


def _make_ops(l):
    t = f"l{l}_"
    last = l == DEPTH - 1
    return dict(
        in_proj=make_in_proj(t + "in_proj"), swa=make_swa(t + "swa"),
        mm_oa=make_mm(t + "mm_oa", BF16), mm_ob=make_mm(t + "mm_ob", BF16), mm_out=make_mm(t + "mm_out"),
        mm_ff1=make_mm(t + "mm_ff1", BF16), mm_ff2=make_mm(t + "mm_ff2"),
        prep_q=make_gdn_prep(t + "prep_q", "q"), prep_k=make_gdn_prep(t + "prep_k", "k"), prep_v=make_gdn_prep(t + "prep_v", "v"),
        gates=make_rowwise(_gates_fn, t + "gates", "t", "cc", "t"), delta=make_delta(t + "delta"),
        gdn_post=make_rowwise(_gdn_post_fn, t + "gdn_post", "ht", "s", "t", nc=DN_HEADS, tm=1024),
        mix=make_rowwise(_mix_fn, t + "mix", "tttt", "", "t"),
        ln1=make_rowwise(_ln1_fn, t + "ln1", "tt", "ccccc", "tt"),
        ln2=make_rowwise(_ln2_last_fn, t + "ln2", "tt", "cccc", "t") if last else make_rowwise(_ln2_fn, t + "ln2", "tt", "cccccc", "tt"),
        relu2=make_rowwise(_relu2_fn, t + "relu2", "t", "c", "t", nc=4, tm=512))


def kernel(x, c, w_ada, b_ada, w_in, conv_w, a_log, dt_bias, sinks, dn_norm_w, w_oa, w_ob, w_out, ln1_g, ln1_b, w_ff1, b_ff1, w_ff2, b_ff2, ln2_g, ln2_b, loss_target, m_w_ada, m_b_ada, m_w_in, m_conv_w, m_a_log, m_dt_bias, m_sinks, m_dn_norm_w, m_w_oa, m_w_ob, m_w_out, m_ln1_g, m_ln1_b, m_w_ff1, m_b_ff1, m_w_ff2, m_b_ff2, m_ln2_g, m_ln2_b, v_w_ada, v_b_ada, v_w_in, v_conv_w, v_a_log, v_dt_bias, v_sinks, v_dn_norm_w, v_w_oa, v_w_ob, v_w_out, v_ln1_g, v_ln1_b, v_w_ff1, v_b_ff1, v_w_ff2, v_b_ff2, v_ln2_g, v_ln2_b):
    given = dict(locals())
    me = 4 * lax.axis_index("x") + 2 * lax.axis_index("y") + lax.axis_index("c")
    conv_cols = conv_w.shape[2]

    gathered = all_gather([_pack_rows([c, conv_w], LANES, 8)], "gather_c_conv", True)[0].reshape(N_DEV, -1)
    c_all = gathered[:, :D]
    conv_full = gathered[:, D:D + DEPTH * CONV_K * conv_cols].reshape(N_DEV, DEPTH, CONV_K, conv_cols).transpose(1, 2, 0, 3).reshape(DEPTH, CONV_K, -1)

    b_cols = lax.dynamic_slice_in_dim(b_ada, me * ADA_COLS, ADA_COLS, axis=1)
    c_act_all, mod_cols = ada_project(c_all, w_ada, b_cols)
    mod_all = all_gather([mod_cols.reshape(-1, LANES)], "gather_mod", True)[0].reshape(N_DEV, DEPTH, N_DEV, ADA_COLS)
    mods = lax.dynamic_index_in_dim(mod_all, me, axis=2, keepdims=False).transpose(1, 0, 2).reshape(DEPTH, 6 * D)

    weights = _unpack_weights(*all_gather_forwarding(_pack_shards(given), "gather_weights"))

    small = {n: given[n] for n in SMALL}
    ops = [_make_ops(l) for l in range(DEPTH)]

    modulate0 = make_rowwise(_modulate_fn, "modulate0", "t", "cc", "t")

    def forward(x0, mods, small, conv_full, weights):
        (u,) = modulate0(x0, mods[:1, D:2 * D], mods[:1, :D])
        h = x0
        for l in range(DEPTH):
            next_mod = mods[l + 1:l + 2] if l + 1 < DEPTH else None
            h, u = _layer(ops[l], h, u, mods[l:l + 1], next_mod, {n: a[l:l + 1] for n, a in small.items()}, conv_full[l], weights[l])
        return h

    y, vjp = jax.vjp(forward, x[0], mods, small, conv_full, weights)
    loss_tile, dy = loss_head(y, loss_target[0])
    dx, d_mods, d_small, d_conv, d_weights = vjp(dy)
    loss = lax.psum(loss_tile[0, 0], AXES)

    slabs = _pack_grads(d_weights)
    from_sibling = scatter_to_sibling(slabs, "scatter_sibling")
    per_chip = [pair_sum(a, b, f"pair_sum_{i}") for i, (a, b) in enumerate(zip(slabs, from_sibling))]
    g_in, g_rows, g_ff1 = [sum_partials(p, f"sum_big_{i}") for i, p in enumerate(scatter_to_chips(per_chip, "scatter_chips"))]
    grad, off = dict(w_in=g_in.reshape(w_in.shape), w_ff1=g_ff1.reshape(w_ff1.shape)), 0
    for n in ROW_SHARDED:
        rows = given[n].shape[0] * given[n].shape[1]
        grad[n] = g_rows[off:off + rows].reshape(given[n].shape)
        off += rows
    delta, new_m, new_v = {}, {}, {}
    for n in BIG[1:]:
        delta[n], new_m[n], new_v[n] = adamw(given[n], grad[n], given["m_" + n], given["v_" + n], "adamw_" + n)
    cols_first = [a.transpose(2, 0, 1) for a in (w_in, grad["w_in"], m_w_in, v_w_in)]
    delta["w_in"], new_m["w_in"], new_v["w_in"] = (
        a.transpose(1, 2, 0) for a in adamw(*cols_first, "adamw_w_in", lead_tile=W_IN_COL_TILE))

    partial = [d_small[n] for n in SMALL] + [d_mods, d_conv]
    (parts,) = all_gather([_pack_rows(partial, LANES, 8)], "gather_small_grads", True)
    mods_at = sum(d_small[n].size for n in SMALL)
    d_mods_all = parts.reshape(N_DEV, -1)[:, mods_at:mods_at + DEPTH * 6 * D].reshape(N_DEV, DEPTH, 6 * D)
    total = _split_flat(sum_partials(parts, "sum_small").reshape(-1), partial)
    for n, g in zip(SMALL, total):
        grad[n] = g
    grad["b_ada"] = total[len(SMALL)]
    grad["conv_w"] = lax.dynamic_slice_in_dim(total[len(SMALL) + 1], me * conv_cols, conv_cols, axis=2)
    names = SMALL + ("b_ada", "conv_w")
    packed = [_pack_rows([src[p + n] for n in names], LANES, 8)[None] for src, p in ((given, ""), (grad, ""), (given, "m_"), (given, "v_"))]
    outs = adamw(*packed, "adamw_small")
    for res, o in zip((delta, new_m, new_v), outs):
        for n, a in zip(names, _split_flat(o.reshape(-1), [given[n] for n in names])):
            res[n] = a

    dmod_mine = lax.dynamic_slice_in_dim(d_mods_all, me * ADA_COLS, ADA_COLS, axis=2).transpose(1, 0, 2)
    pad = LANES - N_DEV
    grad["w_ada"], delta["w_ada"], new_m["w_ada"], new_v["w_ada"] = adamw_ada(
        jnp.pad(c_act_all.T, ((0, 0), (0, pad))), jnp.pad(dmod_mine, ((0, 0), (0, pad), (0, 0))), w_ada, m_w_ada, v_w_ada)

    order = ("w_ada", "b_ada", "w_in", "conv_w", "a_log", "dt_bias", "sinks", "dn_norm_w", "w_oa", "w_ob", "w_out", "ln1_g", "ln1_b",
             "w_ff1", "b_ff1", "w_ff2", "b_ff2", "ln2_g", "ln2_b")
    return (loss, dx[None], *[grad[n] for n in order], *[delta[n] for n in order], *[new_m[n] for n in order], *[new_v[n] for n in order])
```

```python
import functools

import jax
import jax.numpy as jnp
from jax import lax
from jax.experimental import pallas as pl
from jax.experimental.pallas import tpu as pltpu

F32 = jnp.float32
BF16 = jnp.bfloat16

D = 1024
DEPTH = 4
N_DEV = 8
ATT_HEADS, ATT_KV, ATT_GROUP, ATT_HD, WINDOW = 16, 4, 4, 64, 128
DN_HEADS, DN_HD, CONV_K, CHUNK = 8, 128, 4, 64
D_FF = 4096
ADA_COLS = 6 * D // N_DEV
ALPHA = (2 * DEPTH) ** 0.25
LN_EPS = 1e-5
RMS_EPS = 1e-6
ADAM_LR, ADAM_B1, ADAM_B2, ADAM_EPS, ADAM_WD, ADAM_STEP = 0.001, 0.9, 0.999, 1e-08, 0.01, 10
AXES = ("x", "y", "c")
MESH_IDS = pl.DeviceIdType.MESH
VMEM_LIMIT_BYTES = 48 * 1024 * 1024
LANES = 128
HIGHEST = lax.Precision.HIGHEST

IN_SPLITS = (("q", 1024), ("k", 256), ("v", 256), ("dq", 1024), ("dk", 1024), ("dv", 1024), ("z", 1024),
             ("small", 16), ("ga", 1024), ("gb", 1024))
IN_AT = {name: sum(w for _, w in IN_SPLITS[:i]) for i, (name, _) in enumerate(IN_SPLITS)}
IN_GROUPS = ("qkvs", "dq", "dk", "dv", "z", "ga", "gb")
QKV_W = 1536
QKVS_W = QKV_W + LANES


def _params(sem=None):
    return pltpu.CompilerParams(dimension_semantics=sem, vmem_limit_bytes=VMEM_LIMIT_BYTES)


def _tile(n, pref, unit):
    if n <= pref:
        return n
    t = (pref // unit) * unit
    while t > unit and n % t:
        t -= unit
    assert n % t == 0, (n, pref, unit)
    return t


def _dg(a, b, ca, cb):
    return lax.dot_general(a.astype(BF16), b.astype(BF16), (((ca,), (cb,)), ((), ())), preferred_element_type=F32)


@jax.custom_vjp
def bdot_nn(a, b):
    return _dg(a, b, 1, 0)


def _bdot_nn_fwd(a, b):
    return _dg(a, b, 1, 0), (a, b)


def _bdot_nn_bwd(res, g):
    a, b = res
    return _dg(g, b, 1, 1).astype(a.dtype), _dg(a, g, 0, 0).astype(b.dtype)


bdot_nn.defvjp(_bdot_nn_fwd, _bdot_nn_bwd)


@jax.custom_vjp
def bdot_nt(a, b):
    return _dg(a, b, 1, 1)


def _bdot_nt_fwd(a, b):
    return _dg(a, b, 1, 1), (a, b)


def _bdot_nt_bwd(res, g):
    a, b = res
    return _dg(g, b, 1, 0).astype(a.dtype), _dg(g, a, 0, 0).astype(b.dtype)


bdot_nt.defvjp(_bdot_nt_fwd, _bdot_nt_bwd)


@jax.custom_vjp
def bdot_tn(a, b):
    return _dg(a, b, 0, 0)


def _bdot_tn_fwd(a, b):
    return _dg(a, b, 0, 0), (a, b)


def _bdot_tn_bwd(res, g):
    a, b = res
    return _dg(b, g, 1, 1).astype(a.dtype), _dg(a, g, 1, 0).astype(b.dtype)


bdot_tn.defvjp(_bdot_tn_fwd, _bdot_tn_bwd)


def _hdg(a, b, ca, cb):
    a_hi, b_hi = a.astype(BF16), b.astype(BF16)
    a_lo, b_lo = (a - a_hi.astype(F32)).astype(BF16), (b - b_hi.astype(F32)).astype(BF16)

    def dot(x, y):
        return lax.dot_general(x, y, (((ca,), (cb,)), ((), ())), preferred_element_type=F32)

    return dot(a_hi, b_hi) + (dot(a_hi, b_lo) + dot(a_lo, b_hi))


@jax.custom_vjp
def hdot(a, b):
    return _hdg(a, b, 1, 0)


def _hdot_fwd(a, b):
    return _hdg(a, b, 1, 0), (a, b)


def _hdot_bwd(res, g):
    a, b = res
    return _hdg(g, b, 1, 1), _hdg(a, g, 0, 0)


hdot.defvjp(_hdot_fwd, _hdot_bwd)


def matmul(a, b, mode, out_dtype, name, acc=None):
    if mode == "nn":
        (m, k), (k2, n) = a.shape, b.shape
    elif mode == "nt":
        (m, k), (n, k2) = a.shape, b.shape
    else:
        (k, m), (k2, n) = a.shape, b.shape
    assert k == k2, (a.shape, b.shape, mode)
    def pick(n_, pref):
        t = _tile(n_, pref, LANES)
        return n_ if t < 2 * LANES and n_ <= 2048 else t

    tm, tn, tk = pick(m, 1024), pick(n, 1024), pick(k, 4096)
    nk = k // tk
    a_spec = pl.BlockSpec((tk, tm), lambda i, j, kk: (kk, i)) if mode == "tn" else pl.BlockSpec((tm, tk), lambda i, j, kk: (i, kk))
    b_spec = pl.BlockSpec((tn, tk), lambda i, j, kk: (j, kk)) if mode == "nt" else pl.BlockSpec((tk, tn), lambda i, j, kk: (kk, j))
    o_spec = pl.BlockSpec((tm, tn), lambda i, j, kk: (i, j))
    ca, cb = {"nn": (1, 0), "nt": (1, 1), "tn": (0, 0)}[mode]

    def body(*refs):
        a_ref, b_ref = refs[:2]
        c_ref = None if acc is None else refs[2]
        o_ref = refs[2 if acc is None else 3]
        part = _dg(a_ref[...], b_ref[...], ca, cb)
        if nk == 1:
            o_ref[...] = (part if c_ref is None else part + c_ref[...].astype(F32)).astype(out_dtype)
            return
        acc_ref = refs[-1]
        kk = pl.program_id(2)

        @pl.when(kk == 0)
        def _():
            acc_ref[...] = part if c_ref is None else part + c_ref[...].astype(F32)

        @pl.when(jnp.logical_and(kk > 0, kk < nk - 1))
        def _():
            acc_ref[...] += part

        @pl.when(kk == nk - 1)
        def _():
            o_ref[...] = (acc_ref[...] + part).astype(out_dtype)

    ins, in_specs = [a, b], [a_spec, b_spec]
    if acc is not None:
        ins.append(acc)
        in_specs.append(o_spec)
    return pl.pallas_call(
        body, name=name, grid=(m // tm, n // tn, nk), in_specs=in_specs, out_specs=o_spec,
        out_shape=jax.ShapeDtypeStruct((m, n), out_dtype), scratch_shapes=[pltpu.VMEM((tm, tn), F32)] if nk > 1 else [],
        compiler_params=_params(("parallel", "parallel", "arbitrary")))(*ins)


def make_mm(name, out_dtype=F32):
    @jax.custom_vjp
    def mm(a, w):
        return matmul(a, w, "nn", out_dtype, name + "_fwd")

    def fwd(a, w):
        return mm(a, w), (a, w)

    def bwd(res, g):
        a, w = res
        return matmul(g, w, "nt", a.dtype, name + "_da"), matmul(a, g, "tn", w.dtype, name + "_dw")

    mm.defvjp(fwd, bwd)
    return mm


def make_in_proj(name):
    @jax.custom_vjp
    def in_proj(u, ws):
        return tuple(matmul(u, w, "nn", BF16, f"{name}_fwd_{g}") for g, w in zip(IN_GROUPS, ws))

    def fwd(u, ws):
        return in_proj(u, ws), (u, ws)

    def bwd(res, gs):
        u, ws = res
        du = None
        for idx, (g, w, dy) in enumerate(zip(IN_GROUPS, ws, gs)):
            last = idx == len(ws) - 1
            du = matmul(dy, w, "nt", u.dtype if last else F32, f"{name}_du_{g}", acc=du)
        dws = tuple(matmul(u, dy, "tn", w.dtype, f"{name}_dw_{g}") for g, w, dy in zip(IN_GROUPS, ws, gs))
        return du, dws

    in_proj.defvjp(fwd, bwd)
    return in_proj


def make_rowwise(fn, name, tile_kinds, param_kinds, out_kinds, nc=1, tm=512):
    n_t, n_p, n_o = len(tile_kinds), len(param_kinds), len(out_kinds)

    def width(a, kind):
        if kind == "h":
            return a.shape[2]
        return a.shape[1] if kind == "s" else a.shape[1] // nc

    def spec(kind, w, rows):
        if kind == "t":
            return pl.BlockSpec((rows, w), lambda j, i: (i, j))
        if kind == "h":
            return pl.BlockSpec((None, rows, w), lambda j, i: (j, i, 0))
        if kind == "c":
            return pl.BlockSpec((1, w), lambda j, i: (0, j))
        return pl.BlockSpec((1, w), lambda j, i: (0, 0))

    def full_shape(kind, w, s):
        return (s, w * nc) if kind == "t" else (nc, s, w)

    def plan(tiles, params):
        s = tiles[0].shape[0] if tile_kinds[0] == "t" else tiles[0].shape[1]
        rows = min(tm, s)
        t_w = [width(a, kd) for a, kd in zip(tiles, tile_kinds)]
        p_w = [width(a, kd) for a, kd in zip(params, param_kinds)]
        t_s = [jax.ShapeDtypeStruct((rows, w), a.dtype) for a, w in zip(tiles, t_w)]
        p_s = [jax.ShapeDtypeStruct((1, w), a.dtype) for a, w in zip(params, p_w)]
        o_s = jax.eval_shape(fn, *t_s, *p_s)
        return s, rows, t_w, p_w, o_s

    def fwd_call(*args):
        tiles, params = args[:n_t], args[n_t:]
        s, rows, t_w, p_w, o_s = plan(tiles, params)

        def body(*refs):
            ins, outs = refs[:n_t + n_p], refs[n_t + n_p:]
            res = fn(*[r[...] for r in ins])
            for o_ref, val in zip(outs, res):
                o_ref[...] = val

        in_specs = [spec(kd, w, rows) for kd, w in zip(tile_kinds, t_w)] + [spec(kd, w, rows) for kd, w in zip(param_kinds, p_w)]
        return pl.pallas_call(
            body, name=name + "_fwd", grid=(nc, s // rows), in_specs=in_specs,
            out_specs=[spec(kd, o.shape[1], rows) for kd, o in zip(out_kinds, o_s)],
            out_shape=[jax.ShapeDtypeStruct(full_shape(kd, o.shape[1], s), o.dtype) for kd, o in zip(out_kinds, o_s)],
            compiler_params=_params(("parallel", "parallel")))(*args)

    def bwd_call(args, douts):
        tiles, params = args[:n_t], args[n_t:]
        s, rows, t_w, p_w, o_s = plan(tiles, params)

        def body(*refs):
            ins = refs[:n_t + n_p]
            dos = refs[n_t + n_p:n_t + n_p + n_o]
            dts = refs[n_t + n_p + n_o:n_t + n_p + n_o + n_t]
            dps = refs[n_t + n_p + n_o + n_t:]
            j, i = pl.program_id(0), pl.program_id(1)
            _, vjp = jax.vjp(lambda *a: tuple(fn(*a)), *[r[...] for r in ins])
            grads = vjp(tuple(r[...] for r in dos))
            for r, g in zip(dts, grads[:n_t]):
                r[...] = g.astype(r.dtype)
            for r, g, kd in zip(dps, grads[n_t:], param_kinds):
                first = (i == 0) if kd == "c" else jnp.logical_and(i == 0, j == 0)

                @pl.when(first)
                def _(r=r):
                    r[...] = jnp.zeros_like(r)

                r[...] += g.astype(F32)

        in_specs = ([spec(kd, w, rows) for kd, w in zip(tile_kinds, t_w)] + [spec(kd, w, rows) for kd, w in zip(param_kinds, p_w)]
                    + [spec(kd, o.shape[1], rows) for kd, o in zip(out_kinds, o_s)])
        out_specs = [spec(kd, w, rows) for kd, w in zip(tile_kinds, t_w)] + [spec(kd, w, rows) for kd, w in zip(param_kinds, p_w)]
        out_shape = [jax.ShapeDtypeStruct(a.shape, a.dtype) for a in tiles] + [jax.ShapeDtypeStruct(a.shape, F32) for a in params]
        return pl.pallas_call(
            body, name=name + "_bwd", grid=(nc, s // rows), in_specs=in_specs, out_specs=out_specs, out_shape=out_shape,
            compiler_params=_params(("arbitrary", "arbitrary")))(*args, *douts)

    @jax.custom_vjp
    def op(*args):
        return tuple(fwd_call(*args))

    def op_fwd(*args):
        return op(*args), args

    def op_bwd(args, douts):
        return tuple(bwd_call(args, douts))

    op.defvjp(op_fwd, op_bwd)
    return op


def _sigmoid(x):
    return 1.0 / (1.0 + jnp.exp(-x))


def _silu(x):
    return x * _sigmoid(x)


def _softplus(x):
    return jnp.maximum(x, 0.0) + jnp.log(1.0 + jnp.exp(-jnp.abs(x)))


def _layer_norm(h, g, b):
    mu = jnp.mean(h, axis=-1, keepdims=True)
    var = jnp.mean(jnp.square(h - mu), axis=-1, keepdims=True)
    return (h - mu) * lax.rsqrt(var + LN_EPS) * g + b


def _modulate_fn(x, sc, sh):
    return ((x * (1.0 + sc) + sh).astype(BF16),)


def _gates_fn(x, a_vec, b_vec):
    x = x.astype(F32)
    lane = lax.broadcasted_iota(jnp.int32, x.shape, 1)
    beta = _sigmoid(x)
    g = -jnp.exp(a_vec) * _softplus(x + b_vec)
    return (jnp.where(lane < DN_HEADS, beta, jnp.where(lane < 2 * DN_HEADS, g, 0.0)),)


def _gdn_post_fn(o, z, nw):
    o = o * lax.rsqrt(jnp.mean(jnp.square(o), axis=-1, keepdims=True) + RMS_EPS) * nw
    return ((o * _silu(z.astype(F32))).astype(BF16),)


def _mix_fn(ga, gb, ya, yb):
    ga, gb, ya, yb = (t.astype(F32) for t in (ga, gb, ya, yb))
    return ((_sigmoid(ga) * ya + _sigmoid(gb) * yb).astype(BF16),)


def _ln1_fn(x, mixed, gt, g, b, sc, sh):
    y = _layer_norm(ALPHA * x + (1.0 + gt) * mixed, g, b)
    return y, _modulate_fn(y, sc, sh)[0]


def _ln2_last_fn(x, f, gt, bf, g, b):
    return (_layer_norm(ALPHA * x + (1.0 + gt) * (f + bf), g, b),)


def _ln2_fn(x, f, gt, bf, g, b, sc, sh):
    (y,) = _ln2_last_fn(x, f, gt, bf, g, b)
    return y, _modulate_fn(y, sc, sh)[0]


def _relu2_fn(h, b):
    return (jnp.square(jnp.maximum(h.astype(F32) + b, 0.0)).astype(BF16),)


def _each(f, *lists):
    return [f(*xs) for xs in zip(*lists)]


def _swa_blocks(q4, kp, kc, vp, vc, sink, first):
    rows = ATT_GROUP * WINDOW
    qi = lax.broadcasted_iota(jnp.int32, (rows, 2 * WINDOW), 0) & (WINDOW - 1)
    si = lax.broadcasted_iota(jnp.int32, (rows, 2 * WINDOW), 1)
    diff = qi + WINDOW - si
    valid = (diff >= 0) & (diff < WINDOW) & (si >= jnp.where(first, WINDOW, 0))
    q = _each(lambda a: a.reshape(rows, ATT_HD), q4)
    k = _each(lambda a, b: jnp.concatenate([a, b], axis=0), kp, kc)
    v = _each(lambda a, b: jnp.concatenate([a, b], axis=0), vp, vc)
    s = _each(lambda q, k: jnp.where(valid, bdot_nt(q, k) * (ATT_HD ** -0.5), -jnp.inf), q, k)
    m = _each(lambda s, sink: lax.stop_gradient(jnp.maximum(jnp.max(s, axis=-1, keepdims=True), sink)), s, sink)
    p = _each(lambda s, m: jnp.exp(s - m), s, m)
    pn = _each(lambda p, sink, m: p / (jnp.sum(p, axis=-1, keepdims=True) + jnp.exp(sink - m)), p, sink, m)
    return _each(lambda pn, v: bdot_nn(pn, v).reshape(ATT_GROUP, WINDOW, ATT_HD).astype(BF16), pn, v)


def make_swa(name):
    heads = range(ATT_KV)

    def specs():
        q_spec = pl.BlockSpec((ATT_HEADS, WINDOW, ATT_HD), lambda n: (0, n, 0))
        cur = pl.BlockSpec((ATT_KV, WINDOW, ATT_HD), lambda n: (0, n, 0))
        prev = pl.BlockSpec((ATT_KV, WINDOW, ATT_HD), lambda n: (0, jnp.maximum(n - 1, 0), 0))
        sink = pl.BlockSpec((ATT_KV, ATT_GROUP * WINDOW, 1), lambda n: (0, 0, 0))
        return q_spec, cur, prev, sink

    def group(ref, h):
        return ref.at[pl.ds(h * ATT_GROUP, ATT_GROUP)]

    def load(q_ref, kp_ref, kc_ref, vp_ref, vc_ref, s_ref):
        return [[group(q_ref, h)[...] for h in heads]] + [[r[h] for h in heads] for r in (kp_ref, kc_ref, vp_ref, vc_ref, s_ref)]

    def fwd_call(q, k, v, sink):
        s = q.shape[1]
        q_spec, cur, prev, sink_spec = specs()

        def body(q_ref, kp_ref, kc_ref, vp_ref, vc_ref, s_ref, o_ref):
            o = _swa_blocks(*load(q_ref, kp_ref, kc_ref, vp_ref, vc_ref, s_ref), pl.program_id(0) == 0)
            for h in heads:
                group(o_ref, h)[...] = o[h]

        return pl.pallas_call(
            body, name=name + "_fwd", grid=(s // WINDOW,), in_specs=[q_spec, prev, cur, prev, cur, sink_spec],
            out_specs=q_spec, out_shape=jax.ShapeDtypeStruct(q.shape, BF16),
            compiler_params=_params(("parallel",)))(q, k, k, v, v, sink)

    def bwd_call(q, k, v, sink, do):
        s = q.shape[1]
        q_spec, cur, prev, sink_spec = specs()

        def body(q_ref, kp_ref, kc_ref, vp_ref, vc_ref, s_ref, do_ref, dq_ref, dkp_ref, dkc_ref, dvp_ref, dvc_ref, ds_ref):
            first = pl.program_id(0) == 0
            _, vjp = jax.vjp(lambda *a: _swa_blocks(*a, first), *load(q_ref, kp_ref, kc_ref, vp_ref, vc_ref, s_ref))
            dq, dkp, dkc, dvp, dvc, ds = vjp([group(do_ref, h)[...] for h in heads])

            @pl.when(first)
            def _():
                ds_ref[...] = jnp.zeros_like(ds_ref)

            for h in heads:
                group(dq_ref, h)[...] = dq[h]
                dkp_ref[h], dkc_ref[h], dvp_ref[h], dvc_ref[h] = dkp[h], dkc[h], dvp[h], dvc[h]
                ds_ref[h] += ds[h]

        kv = jax.ShapeDtypeStruct(k.shape, k.dtype)
        return pl.pallas_call(
            body, name=name + "_bwd", grid=(s // WINDOW,), in_specs=[q_spec, prev, cur, prev, cur, sink_spec, q_spec],
            out_specs=[q_spec, cur, cur, cur, cur, sink_spec],
            out_shape=[jax.ShapeDtypeStruct(q.shape, q.dtype), kv, kv, kv, kv, jax.ShapeDtypeStruct(sink.shape, F32)],
            compiler_params=_params(("arbitrary",)))(q, k, k, v, v, sink, do)

    @jax.custom_vjp
    def swa(q, k, v, sink):
        return fwd_call(q, k, v, sink)

    def fwd(q, k, v, sink):
        return swa(q, k, v, sink), (q, k, v, sink)

    def bwd(res, do):
        q, k, v, sink = res
        dq, dkp, dkc, dvp, dvc, ds = bwd_call(q, k, v, sink, do)

        def fold(cur, prev):
            shifted = jnp.concatenate([prev[:, WINDOW:], jnp.zeros_like(prev[:, :WINDOW])], axis=1)
            return (cur.astype(F32) + shifted.astype(F32)).astype(cur.dtype)

        return dq, fold(dkc, dkp), fold(dvc, dvp), ds

    swa.defvjp(fwd, bwd)
    return swa


def _shift_down(x, d, row):
    return x if d == 0 else jnp.where(row >= d, pltpu.roll(x, d, 0), 0.0)


def _shift_up(x, d, row):
    n = x.shape[0]
    return x if d == 0 else jnp.where(row < n - d, pltpu.roll(x, n - d, 0), 0.0)


def _conv_pre(x, w, row):
    return sum(w[j:j + 1, :] * _shift_down(x, CONV_K - 1 - j, row) for j in range(CONV_K))


def _prep_post(pre, kind):
    act = _silu(pre)
    if kind == "v":
        return act
    r = lax.rsqrt(jnp.sum(jnp.square(act), axis=-1, keepdims=True) + RMS_EPS)
    return act * r * (DN_HD ** -0.5 if kind == "q" else 1.0)


def make_gdn_prep(name, kind):
    def fwd_call(x, w):
        s = x.shape[0]

        def body(x_ref, w_ref, o_ref):
            row = lax.broadcasted_iota(jnp.int32, (s, DN_HD), 0)
            o_ref[...] = _prep_post(_conv_pre(x_ref[...].astype(F32), w_ref[...], row), kind)

        return pl.pallas_call(
            body, name=name + "_fwd", grid=(DN_HEADS,),
            in_specs=[pl.BlockSpec((s, DN_HD), lambda j: (0, j)), pl.BlockSpec((CONV_K, DN_HD), lambda j: (0, j))],
            out_specs=pl.BlockSpec((None, s, DN_HD), lambda j: (j, 0, 0)),
            out_shape=jax.ShapeDtypeStruct((DN_HEADS, s, DN_HD), F32), compiler_params=_params(("parallel",)))(x, w)

    def bwd_call(x, w, dy):
        s = x.shape[0]

        def body(x_ref, w_ref, dy_ref, dx_ref, dw_ref):
            row = lax.broadcasted_iota(jnp.int32, (s, DN_HD), 0)
            xv, wv = x_ref[...].astype(F32), w_ref[...]
            _, vjp = jax.vjp(lambda p: _prep_post(p, kind), _conv_pre(xv, wv, row))
            (dpre,) = vjp(dy_ref[...])
            dx_ref[...] = sum(wv[j:j + 1, :] * _shift_up(dpre, CONV_K - 1 - j, row) for j in range(CONV_K)).astype(dx_ref.dtype)
            for j in range(CONV_K):
                dw_ref[j:j + 1, :] = jnp.sum(dpre * _shift_down(xv, CONV_K - 1 - j, row), axis=0, keepdims=True)

        x_spec = pl.BlockSpec((s, DN_HD), lambda j: (0, j))
        w_spec = pl.BlockSpec((CONV_K, DN_HD), lambda j: (0, j))
        return pl.pallas_call(
            body, name=name + "_bwd", grid=(DN_HEADS,),
            in_specs=[x_spec, w_spec, pl.BlockSpec((None, s, DN_HD), lambda j: (j, 0, 0))], out_specs=[x_spec, w_spec],
            out_shape=[jax.ShapeDtypeStruct(x.shape, x.dtype), jax.ShapeDtypeStruct(w.shape, F32)],
            compiler_params=_params(("parallel",)))(x, w, dy)

    @jax.custom_vjp
    def prep(x, w):
        return fwd_call(x, w)

    def fwd(x, w):
        return prep(x, w), (x, w)

    def bwd(res, dy):
        return tuple(bwd_call(*res, dy))

    prep.defvjp(fwd, bwd)
    return prep


@jax.custom_vjp
def _saved_inverse(m, t):
    return t


def _saved_inverse_fwd(m, t):
    return t, t


def _saved_inverse_bwd(t, dt):
    return _hdg(_hdg(t, dt, 0, 0), t, 1, 1), jnp.zeros_like(t)


_saved_inverse.defvjp(_saved_inverse_fwd, _saved_inverse_bwd)


def _chunk_fn(q, k, v, beta, g, state, t_saved=None):
    c = CHUNK
    r = lax.broadcasted_iota(jnp.int32, (c, c), 0)
    cc = lax.broadcasted_iota(jnp.int32, (c, c), 1)
    eye = (r == cc).astype(F32)
    causal, strict = r >= cc, r > cc
    g_row = _each(lambda g: jnp.sum(g * eye, axis=0, keepdims=True), g)
    gc = _each(lambda g_row: jnp.sum(jnp.where(causal, g_row, 0.0), axis=1, keepdims=True), g_row)
    gc_row = _each(lambda gc: jnp.sum(gc * eye, axis=0, keepdims=True), gc)
    decay = _each(lambda gc, gc_row: jnp.exp(jnp.where(causal, gc - gc_row, -jnp.inf)), gc, gc_row)
    kb = _each(jnp.multiply, k, beta)
    vb = _each(jnp.multiply, v, beta)
    kk = _each(bdot_nt, kb, k)
    qk = _each(bdot_nt, q, k)
    m = _each(lambda kk, decay: -jnp.where(strict, kk * decay, 0.0), kk, decay)
    if t_saved is None:
        t, p = _each(lambda m: eye + m, m), m
        for _ in range(5):
            p = _each(hdot, p, p)
            t = _each(lambda t, p: t + hdot(t, p), t, p)
    else:
        t = _each(_saved_inverse, m, t_saved)
    eg = _each(jnp.exp, gc)
    u = _each(hdot, t, vb)
    w = _each(lambda t, kb, eg: hdot(t, kb * eg), t, kb, eg)
    ws = _each(bdot_nn, w, state)
    qs = _each(lambda q, eg, state: bdot_nn(q * eg, state), q, eg, state)
    v_new = _each(jnp.subtract, u, ws)
    o = _each(lambda qs, qk, decay, v_new: qs + bdot_nn(qk * decay, v_new), qs, qk, decay, v_new)
    g_last = _each(lambda g: jnp.sum(g, axis=0, keepdims=True), g)
    kv = _each(lambda k, g_last, gc, v_new: bdot_tn(k * jnp.exp(g_last - gc), v_new), k, g_last, gc, v_new)
    new_state = _each(lambda state, g_last, kv: state * jnp.exp(g_last) + kv, state, g_last, kv)
    return o, new_state, t


def _gate_columns(gates):
    lane = lax.broadcasted_iota(jnp.int32, gates.shape, 1)

    def column(at):
        return jnp.sum(jnp.where(lane == at, gates, 0.0), axis=1, keepdims=True)

    return [column(h) for h in range(DN_HEADS)], [column(DN_HEADS + h) for h in range(DN_HEADS)]


def make_delta(name):
    heads = range(DN_HEADS)

    def specs(n_chunks, reverse):
        def at(n):
            return n_chunks - 1 - n if reverse else n

        x_spec = pl.BlockSpec((DN_HEADS, CHUNK, DN_HD), lambda n: (0, at(n), 0))
        g_spec = pl.BlockSpec((CHUNK, LANES), lambda n: (at(n), 0))
        st_spec = pl.BlockSpec((DN_HEADS, None, DN_HD, DN_HD), lambda n: (0, at(n), 0, 0))
        t_spec = pl.BlockSpec((DN_HEADS, None, CHUNK, CHUNK), lambda n: (0, at(n), 0, 0))
        return x_spec, g_spec, st_spec, t_spec

    def per_head(ref):
        return [ref[i] for i in heads]

    def fwd_call(qh, kh, vh, gates):
        n_chunks = qh.shape[1] // CHUNK
        x_spec, g_spec, st_spec, t_spec = specs(n_chunks, False)

        def body(q_ref, k_ref, v_ref, g_ref, o_ref, st_ref, t_ref, state):
            @pl.when(pl.program_id(0) == 0)
            def _():
                state[...] = jnp.zeros_like(state)

            s_in = per_head(state)
            o, s_new, t = _chunk_fn(per_head(q_ref), per_head(k_ref), per_head(v_ref), *_gate_columns(g_ref[...]), s_in)
            for i in heads:
                st_ref[i], o_ref[i], state[i], t_ref[i] = s_in[i], o[i], s_new[i], t[i]

        return pl.pallas_call(
            body, name=name + "_fwd", grid=(n_chunks,), in_specs=[x_spec, x_spec, x_spec, g_spec], out_specs=[x_spec, st_spec, t_spec],
            out_shape=[jax.ShapeDtypeStruct(qh.shape, F32), jax.ShapeDtypeStruct((DN_HEADS, n_chunks, DN_HD, DN_HD), F32),
                       jax.ShapeDtypeStruct((DN_HEADS, n_chunks, CHUNK, CHUNK), F32)],
            scratch_shapes=[pltpu.VMEM((DN_HEADS, DN_HD, DN_HD), F32)], compiler_params=_params(("arbitrary",)))(qh, kh, vh, gates)

    def bwd_call(qh, kh, vh, gates, states, inverses, do):
        n_chunks = qh.shape[1] // CHUNK
        x_spec, g_spec, st_spec, t_spec = specs(n_chunks, True)

        def body(q_ref, k_ref, v_ref, g_ref, st_ref, t_ref, do_ref, dq_ref, dk_ref, dv_ref, dg_ref, dstate):
            @pl.when(pl.program_id(0) == 0)
            def _():
                dstate[...] = jnp.zeros_like(dstate)

            t_saved = per_head(t_ref)
            _, vjp = jax.vjp(lambda q, k, v, gates, state: _chunk_fn(q, k, v, *_gate_columns(gates), state, t_saved=t_saved)[:2],
                             per_head(q_ref), per_head(k_ref), per_head(v_ref), g_ref[...], per_head(st_ref))
            dq, dk, dv, dgates, ds = vjp((per_head(do_ref), per_head(dstate)))
            dg_ref[...] = dgates
            for i in heads:
                dq_ref[i], dk_ref[i], dv_ref[i], dstate[i] = dq[i], dk[i], dv[i], ds[i]

        big = jax.ShapeDtypeStruct(qh.shape, F32)
        return pl.pallas_call(
            body, name=name + "_bwd", grid=(n_chunks,), in_specs=[x_spec, x_spec, x_spec, g_spec, st_spec, t_spec, x_spec],
            out_specs=[x_spec, x_spec, x_spec, g_spec], out_shape=[big, big, big, jax.ShapeDtypeStruct(gates.shape, F32)],
            scratch_shapes=[pltpu.VMEM((DN_HEADS, DN_HD, DN_HD), F32)],
            compiler_params=_params(("arbitrary",)))(qh, kh, vh, gates, states, inverses, do)

    @jax.custom_vjp
    def delta(qh, kh, vh, gates):
        return fwd_call(qh, kh, vh, gates)[0]

    def fwd(qh, kh, vh, gates):
        o, states, inverses = fwd_call(qh, kh, vh, gates)
        return o, (qh, kh, vh, gates, states, inverses)

    def bwd(res, do):
        return tuple(bwd_call(*res, do))

    delta.defvjp(fwd, bwd)
    return delta


def loss_head(y, target):
    s, d = y.shape
    tm = min(256, s)

    def body(y_ref, t_ref, l_ref, dy_ref):
        err = y_ref[...] - t_ref[...]
        dy_ref[...] = err * (1.0 / d)

        @pl.when(pl.program_id(0) == 0)
        def _():
            l_ref[...] = jnp.zeros_like(l_ref)

        l_ref[...] += 0.5 * jnp.sum(jnp.mean(jnp.square(err), axis=-1, keepdims=True), axis=0, keepdims=True)

    spec = pl.BlockSpec((tm, d), lambda i: (i, 0))
    return pl.pallas_call(
        body, name="loss_head", grid=(s // tm,), in_specs=[spec, spec], out_specs=[pl.BlockSpec((8, LANES), lambda i: (0, 0)), spec],
        out_shape=[jax.ShapeDtypeStruct((8, LANES), F32), jax.ShapeDtypeStruct(y.shape, F32)], compiler_params=_params(("arbitrary",)))(y, target)


def ada_project(c_all, w_ada, b_cols):
    tn = 256
    cols = w_ada.shape[2]

    def body(c_ref, w_ref, b_ref, ca_ref, o_ref):
        c_act = _silu(c_ref[...])
        ca_ref[...] = c_act
        o_ref[...] = _dg(c_act, w_ref[...], 1, 0) + b_ref[...]

    return pl.pallas_call(
        body, name="ada_project", grid=(DEPTH, cols // tn),
        in_specs=[pl.BlockSpec((N_DEV, D), lambda l, j: (0, 0)), pl.BlockSpec((None, D, tn), lambda l, j: (l, 0, j)),
                  pl.BlockSpec((None, 1, tn), lambda l, j: (l, 0, j))],
        out_specs=[pl.BlockSpec((N_DEV, D), lambda l, j: (0, 0)), pl.BlockSpec((None, N_DEV, tn), lambda l, j: (l, 0, j))],
        out_shape=[jax.ShapeDtypeStruct((N_DEV, D), F32), jax.ShapeDtypeStruct((DEPTH, N_DEV, cols), F32)],
        compiler_params=_params(("arbitrary", "arbitrary")))(c_all, w_ada, b_cols.reshape(DEPTH, 1, cols))


def sum_partials(parts, name):
    n_parts, r, c = parts.shape
    tr = _tile(r, 512, 16 if parts.dtype == BF16 else 8)

    def body(p_ref, o_ref):
        total = p_ref[0].astype(F32)
        for part in range(1, n_parts):
            total = total + p_ref[part].astype(F32)
        o_ref[...] = total

    return pl.pallas_call(
        body, name=name, grid=(r // tr,), in_specs=[pl.BlockSpec((n_parts, tr, c), lambda i: (0, i, 0))],
        out_specs=pl.BlockSpec((tr, c), lambda i: (i, 0)), out_shape=jax.ShapeDtypeStruct((r, c), F32),
        compiler_params=_params(("parallel",)))(parts)


def _adamw(w, g, m, v):
    m = ADAM_B1 * m + (1.0 - ADAM_B1) * g
    v = ADAM_B2 * v + (1.0 - ADAM_B2) * jnp.square(g)
    m_hat = m / (1.0 - ADAM_B1 ** ADAM_STEP)
    v_hat = v / (1.0 - ADAM_B2 ** ADAM_STEP)
    return -ADAM_LR * (m_hat / (jnp.sqrt(v_hat) + ADAM_EPS) + ADAM_WD * w), m, v


def adamw(w, g, m, v, name, lead_tile=None):
    l, r, c = w.shape

    def body(w_ref, g_ref, m_ref, v_ref, d_ref, nm_ref, nv_ref):
        d_ref[...], nm_ref[...], nv_ref[...] = _adamw(w_ref[...], g_ref[...], m_ref[...], v_ref[...])

    if lead_tile is None:
        tr = _tile(r, 512, 8)
        spec, grid = pl.BlockSpec((None, tr, c), lambda a, i: (a, i, 0)), (l, r // tr)
    else:
        spec, grid = pl.BlockSpec((lead_tile, r, c), lambda a, i: (a, 0, 0)), (l // lead_tile, 1)
    shape = jax.ShapeDtypeStruct(w.shape, F32)
    return pl.pallas_call(body, name=name, grid=grid, in_specs=[spec] * 4, out_specs=[spec] * 3, out_shape=[shape] * 3,
                          compiler_params=_params(("parallel", "parallel")))(w, g, m, v)


def adamw_ada(c_act_t, dmod, w, m, v):
    l, r, c = w.shape
    tr = 256

    def body(c_ref, d_ref, w_ref, m_ref, v_ref, g_ref, dl_ref, nm_ref, nv_ref):
        g = hdot(c_ref[...], d_ref[...])
        g_ref[...] = g
        dl_ref[...], nm_ref[...], nv_ref[...] = _adamw(w_ref[...], g, m_ref[...], v_ref[...])

    spec = pl.BlockSpec((None, tr, c), lambda a, i: (a, i, 0))
    shape = jax.ShapeDtypeStruct(w.shape, F32)
    return pl.pallas_call(
        body, name="adamw_ada", grid=(l, r // tr),
        in_specs=[pl.BlockSpec((tr, LANES), lambda a, i: (i, 0)), pl.BlockSpec((None, LANES, c), lambda a, i: (a, 0, 0)), spec, spec, spec],
        out_specs=[spec] * 4, out_shape=[shape] * 4, compiler_params=_params(("parallel", "parallel")))(c_act_t, dmod, w, m, v)


def _place():
    x, y, c = lax.axis_index("x"), lax.axis_index("y"), lax.axis_index("c")
    return x, y, c


def _comm_call(body, name, ins, out_shapes, space, n_sems):
    n = len(ins)
    return pl.pallas_call(
        body, name=name, out_shape=out_shapes, in_specs=[pl.BlockSpec(memory_space=space)] * n,
        out_specs=[pl.BlockSpec(memory_space=space)] * n,
        scratch_shapes=[pltpu.SemaphoreType.DMA((n, n_sems)), pltpu.SemaphoreType.DMA((n, n_sems)), pltpu.SemaphoreType.DMA((n,))],
        compiler_params=pltpu.CompilerParams(vmem_limit_bytes=VMEM_LIMIT_BYTES))(*ins)


def all_gather(shards, name, in_vmem):
    n = len(shards)

    def body(*refs):
        x_refs, out_refs, (send_sems, recv_sems, local_sems) = refs[:n], refs[n:2 * n], refs[2 * n:]
        x, y, c = _place()
        me, sibling = (x, y, c), (x, y, 1 - c)
        chips = [(1 - x, y), (x, 1 - y), (1 - x, 1 - y)]

        def rows(a, px, py, pc):
            return out_refs[a].at[4 * px + 2 * py + pc]

        def copy(a, k, block, to, from_shard=False):
            return pltpu.make_async_remote_copy(
                src_ref=x_refs[a] if from_shard else rows(a, *block), dst_ref=rows(a, *block), send_sem=send_sems.at[a, k],
                recv_sem=recv_sems.at[a, k], device_id=to, device_id_type=MESH_IDS)

        arrays = range(n)
        mine = [pltpu.make_async_copy(x_refs[a], rows(a, *me), local_sems.at[a]) for a in arrays]
        first = [copy(a, 1 + j, me, (*chip, c), True) for j, chip in enumerate(chips) for a in arrays]
        first += [copy(a, 0, me, sibling, True) for a in arrays]
        for cp in mine + first:
            cp.start()
        passed = []
        for j, chip in enumerate(chips):
            for a in arrays:
                copy(a, 1 + j, (*chip, c), me).wait_recv()
                passed.append(copy(a, 4 + j, (*chip, c), sibling))
                passed[-1].start()
        for a in arrays:
            copy(a, 0, sibling, me).wait_recv()
            for j, chip in enumerate(chips):
                copy(a, 4 + j, (*chip, 1 - c), me).wait_recv()
        for cp in first + passed:
            cp.wait_send()
        for cp in mine:
            cp.wait()

    out_shapes = [jax.ShapeDtypeStruct((N_DEV,) + s.shape, s.dtype) for s in shards]
    return _comm_call(body, name, shards, out_shapes, pltpu.VMEM if in_vmem else pl.ANY, 7)


def all_gather_forwarding(shards, name):
    n = len(shards)

    def body(*refs):
        x_refs, out_refs, (send_sems, recv_sems, local_sems) = refs[:n], refs[n:2 * n], refs[2 * n:]
        x, y, c = _place()
        me = (x, y, c)

        def rows(a, dev, half):
            block = out_refs[a].at[4 * dev[0] + 2 * dev[1] + dev[2]]
            half_rows = shards[a].shape[0] // 2
            return block if half is None else block.at[pl.ds(half * half_rows, half_rows)]

        def copy(a, k, block, to, half=None, from_shard=False):
            return pltpu.make_async_remote_copy(
                src_ref=x_refs[a] if from_shard else rows(a, block, half), dst_ref=rows(a, block, half), send_sem=send_sems.at[a, k],
                recv_sem=recv_sems.at[a, k], device_id=to, device_id_type=MESH_IDS)

        def other_core(dev):
            return (dev[0], dev[1], 1 - c)

        sibling, x_nbr, y_nbr, diag = other_core(me), (1 - x, y, c), (x, 1 - y, c), (1 - x, 1 - y, c)
        arrays = range(n)
        mine = [pltpu.make_async_copy(x_refs[a], rows(a, me, None), local_sems.at[a]) for a in arrays]
        sent = [copy(a, k, me, to, from_shard=True) for k, to in ((1, x_nbr), (2, y_nbr), (0, sibling)) for a in arrays]
        for cp in mine + sent:
            cp.start()

        def on_arrival(k, block, half, passes):
            for a in arrays:
                copy(a, k, block, me, half).wait_recv()
                for k_out, to, half_out in passes:
                    sent.append(copy(a, k_out, block, to, half_out))
                    sent[-1].start()

        on_arrival(1, x_nbr, None, [(3, y_nbr, 0), (5, sibling, None)])
        on_arrival(2, y_nbr, None, [(4, x_nbr, 1), (6, sibling, None)])
        on_arrival(3, diag, 0, [(7, sibling, 0)])
        on_arrival(4, diag, 1, [(8, sibling, 1)])
        on_arrival(0, sibling, None, [])
        on_arrival(5, other_core(x_nbr), None, [])
        on_arrival(6, other_core(y_nbr), None, [])
        on_arrival(7, other_core(diag), 0, [])
        on_arrival(8, other_core(diag), 1, [])
        for cp in sent:
            cp.wait_send()
        for cp in mine:
            cp.wait()

    out_shapes = [jax.ShapeDtypeStruct((N_DEV,) + s.shape, s.dtype) for s in shards]
    return _comm_call(body, name, shards, out_shapes, pl.ANY, 9)


def scatter_to_sibling(slabs, name):
    n = len(slabs)

    def body(*refs):
        in_refs, out_refs, (send_sems, recv_sems, _) = refs[:n], refs[n:2 * n], refs[2 * n:]
        x, y, c = _place()
        copies = [pltpu.make_async_remote_copy(
            src_ref=in_refs[a].at[2 * chip + 1 - c], dst_ref=out_refs[a].at[chip], send_sem=send_sems.at[a, chip],
            recv_sem=recv_sems.at[a, chip], device_id=(x, y, 1 - c), device_id_type=MESH_IDS) for chip in range(4) for a in range(n)]
        for cp in copies:
            cp.start()
        for cp in copies:
            cp.wait_recv()
        for cp in copies:
            cp.wait_send()

    out_shapes = [jax.ShapeDtypeStruct((4,) + s.shape[1:], s.dtype) for s in slabs]
    return _comm_call(body, name, slabs, out_shapes, pl.ANY, 4)


def scatter_to_chips(slabs, name):
    n = len(slabs)

    def body(*refs):
        in_refs, out_refs, (send_sems, recv_sems, local_sems) = refs[:n], refs[n:2 * n], refs[2 * n:]
        x, y, c = _place()
        my_chip = 2 * x + y
        mine = [pltpu.make_async_copy(in_refs[a].at[my_chip], out_refs[a].at[my_chip], local_sems.at[a]) for a in range(n)]
        copies = [pltpu.make_async_remote_copy(
            src_ref=in_refs[a].at[2 * px + py], dst_ref=out_refs[a].at[my_chip], send_sem=send_sems.at[a, j], recv_sem=recv_sems.at[a, j],
            device_id=(px, py, c), device_id_type=MESH_IDS)
            for j, (px, py) in enumerate([(1 - x, y), (x, 1 - y), (1 - x, 1 - y)]) for a in range(n)]
        for cp in mine + copies:
            cp.start()
        for cp in copies:
            cp.wait_recv()
        for cp in copies:
            cp.wait_send()
        for cp in mine:
            cp.wait()

    return _comm_call(body, name, slabs, [jax.ShapeDtypeStruct(s.shape, s.dtype) for s in slabs], pl.ANY, 3)


def pair_sum(slabs, got, name):
    _, r, c = slabs.shape
    tr = _tile(r, 1024, 16)

    def body(a_ref, b_ref, o_ref):
        o_ref[...] = (a_ref[...].astype(F32) + b_ref[...].astype(F32)).astype(BF16)

    return pl.pallas_call(
        body, name=name, grid=(4, r // tr),
        in_specs=[pl.BlockSpec((None, tr, c), lambda s, i: (2 * s + lax.axis_index("c"), i, 0)), pl.BlockSpec((None, tr, c), lambda s, i: (s, i, 0))],
        out_specs=pl.BlockSpec((None, tr, c), lambda s, i: (s, i, 0)), out_shape=jax.ShapeDtypeStruct((4, r, c), BF16),
        compiler_params=_params(("parallel", "parallel")))(slabs, got)


BIG = ("w_in", "w_oa", "w_ob", "w_out", "w_ff1", "w_ff2")
SMALL = ("a_log", "dt_bias", "sinks", "dn_norm_w", "ln1_g", "ln1_b", "b_ff1", "b_ff2", "ln2_g", "ln2_b")


def _pack_rows(arrs, width, unit):
    flat = jnp.concatenate([a.reshape(-1) for a in arrs])
    rows = -(-flat.shape[0] // (width * unit)) * unit
    return jnp.pad(flat, (0, rows * width - flat.shape[0])).reshape(rows, width)


def _split_flat(flat, like):
    out, off = [], 0
    for a in like:
        n = 1
        for dim in a.shape:
            n *= dim
        out.append(flat[off:off + n].reshape(a.shape))
        off += n
    return out


ROW_SHARDED = ("w_oa", "w_ob", "w_out", "w_ff2")
W_IN_COL_TILE = 74


def _pack_shards(given):
    rows = jnp.concatenate([given[n].reshape(-1, D) for n in ROW_SHARDED]).astype(BF16)
    return [given["w_in"].astype(BF16).reshape(DEPTH * D, -1), rows, given["w_ff1"].astype(BF16).reshape(DEPTH * D, -1)]


def _unpack_weights(g_in, g_rows, g_ff1):
    w_in = g_in.reshape(N_DEV, DEPTH, D, -1)
    w_ff1 = g_ff1.reshape(N_DEV, DEPTH, D, -1)
    layers = []
    for l in range(DEPTH):
        full_in = w_in[:, l].transpose(1, 0, 2).reshape(D, -1)
        small = jnp.pad(full_in[:, IN_AT["small"]:IN_AT["ga"]], ((0, 0), (0, LANES + IN_AT["small"] - IN_AT["ga"])))
        groups = [jnp.concatenate([full_in[:, :QKV_W], small], axis=1)] + [full_in[:, IN_AT[g]:IN_AT[g] + D] for g in IN_GROUPS[1:]]
        lay, off = dict(w_in=tuple(groups), w_ff1=w_ff1[:, l].transpose(1, 0, 2).reshape(D, D_FF)), 0
        for n in ROW_SHARDED:
            per = (D_FF if n == "w_ff2" else D) // N_DEV
            lay[n] = g_rows[:, off + l * per:off + (l + 1) * per].reshape(per * N_DEV, D)
            off += DEPTH * per
        layers.append(lay)
    return layers


def _pack_grads(grads):
    def in_order(qkvs, dq, dk, dv, z, ga, gb):
        return jnp.concatenate([qkvs[:, :QKV_W], dq, dk, dv, z, qkvs[:, QKV_W:QKV_W + IN_AT["ga"] - IN_AT["small"]], ga, gb], axis=1)

    w_in = jnp.stack([in_order(*lay["w_in"]) for lay in grads])
    s_in = w_in.reshape(DEPTH, D, N_DEV, -1).transpose(2, 0, 1, 3).reshape(N_DEV, DEPTH * D, -1)
    rows = []
    for n in ROW_SHARDED:
        w = jnp.stack([lay[n] for lay in grads])
        rows.append(w.reshape(DEPTH, N_DEV, -1, D).transpose(1, 0, 2, 3).reshape(N_DEV, -1, D))
    w = jnp.stack([lay["w_ff1"] for lay in grads])
    s_ff1 = w.reshape(DEPTH, D, N_DEV, -1).transpose(2, 0, 1, 3).reshape(N_DEV, DEPTH * D, -1)
    return [s_in, jnp.concatenate(rows, axis=1), s_ff1]


def _layer(ops, x, u, mod, next_mod, sm, conv_w, w):
    s = x.shape[0]
    sh1, sc1, gt1, sh2, sc2, gt2 = (mod[:, i * D:(i + 1) * D] for i in range(6))
    qkvs, dq, dk, dv, z, ga, gb = ops["in_proj"](u, w["w_in"])
    q, k, v, small = qkvs[:, :D], qkvs[:, D:D + 256], qkvs[:, D + 256:QKV_W], qkvs[:, QKV_W:]
    qh = q.reshape(s, ATT_HEADS, ATT_HD).transpose(1, 0, 2)
    kh = k.reshape(s, ATT_KV, ATT_HD).transpose(1, 0, 2)
    vh = v.reshape(s, ATT_KV, ATT_HD).transpose(1, 0, 2)
    sink = jnp.broadcast_to(sm["sinks"].reshape(ATT_KV, ATT_GROUP, 1, 1), (ATT_KV, ATT_GROUP, WINDOW, 1)).reshape(ATT_KV, ATT_GROUP * WINDOW, 1)
    attn = ops["swa"](qh, kh, vh, sink).transpose(1, 0, 2).reshape(s, ATT_HEADS * ATT_HD)
    y_a = ops["mm_oa"](attn, w["w_oa"])
    qn = ops["prep_q"](dq, conv_w[:, :D])
    kn = ops["prep_k"](dk, conv_w[:, D:2 * D])
    vn = ops["prep_v"](dv, conv_w[:, 2 * D:])
    a_vec = jnp.pad(sm["a_log"], ((0, 0), (DN_HEADS, LANES - 2 * DN_HEADS)))
    b_vec = jnp.pad(sm["dt_bias"], ((0, 0), (DN_HEADS, LANES - 2 * DN_HEADS)))
    (gates,) = ops["gates"](small, a_vec, b_vec)
    o = ops["delta"](qn, kn, vn, gates)
    (og,) = ops["gdn_post"](o, z, sm["dn_norm_w"])
    y_b = ops["mm_ob"](og, w["w_ob"])
    (mix,) = ops["mix"](ga, gb, y_a, y_b)
    mixed = ops["mm_out"](mix, w["w_out"])
    x1, u2 = ops["ln1"](x, mixed, gt1, sm["ln1_g"], sm["ln1_b"], sc2, sh2)
    (h,) = ops["relu2"](ops["mm_ff1"](u2, w["w_ff1"]), sm["b_ff1"])
    f = ops["mm_ff2"](h, w["w_ff2"])
    if next_mod is None:
        return ops["ln2"](x1, f, gt2, sm["b_ff2"], sm["ln2_g"], sm["ln2_b"])[0], None
    return ops["ln2"](x1, f, gt2, sm["b_ff2"], sm["ln2_g"], sm["ln2_b"], next_mod[:, D:2 * D], next_mod[:, :D])


def _make_ops(l):
    t = f"l{l}_"
    last = l == DEPTH - 1
    return dict(
        in_proj=make_in_proj(t + "in_proj"), swa=make_swa(t + "swa"),
        mm_oa=make_mm(t + "mm_oa", BF16), mm_ob=make_mm(t + "mm_ob", BF16), mm_out=make_mm(t + "mm_out"),
        mm_ff1=make_mm(t + "mm_ff1", BF16), mm_ff2=make_mm(t + "mm_ff2"),
        prep_q=make_gdn_prep(t + "prep_q", "q"), prep_k=make_gdn_prep(t + "prep_k", "k"), prep_v=make_gdn_prep(t + "prep_v", "v"),
        gates=make_rowwise(_gates_fn, t + "gates", "t", "cc", "t"), delta=make_delta(t + "delta"),
        gdn_post=make_rowwise(_gdn_post_fn, t + "gdn_post", "ht", "s", "t", nc=DN_HEADS, tm=1024),
        mix=make_rowwise(_mix_fn, t + "mix", "tttt", "", "t"),
        ln1=make_rowwise(_ln1_fn, t + "ln1", "tt", "ccccc", "tt"),
        ln2=make_rowwise(_ln2_last_fn, t + "ln2", "tt", "cccc", "t") if last else make_rowwise(_ln2_fn, t + "ln2", "tt", "cccccc", "tt"),
        relu2=make_rowwise(_relu2_fn, t + "relu2", "t", "c", "t", nc=4, tm=512))


def kernel(x, c, w_ada, b_ada, w_in, conv_w, a_log, dt_bias, sinks, dn_norm_w, w_oa, w_ob, w_out, ln1_g, ln1_b, w_ff1, b_ff1, w_ff2, b_ff2, ln2_g, ln2_b, loss_target, m_w_ada, m_b_ada, m_w_in, m_conv_w, m_a_log, m_dt_bias, m_sinks, m_dn_norm_w, m_w_oa, m_w_ob, m_w_out, m_ln1_g, m_ln1_b, m_w_ff1, m_b_ff1, m_w_ff2, m_b_ff2, m_ln2_g, m_ln2_b, v_w_ada, v_b_ada, v_w_in, v_conv_w, v_a_log, v_dt_bias, v_sinks, v_dn_norm_w, v_w_oa, v_w_ob, v_w_out, v_ln1_g, v_ln1_b, v_w_ff1, v_b_ff1, v_w_ff2, v_b_ff2, v_ln2_g, v_ln2_b):
    given = dict(locals())
    me = 4 * lax.axis_index("x") + 2 * lax.axis_index("y") + lax.axis_index("c")
    conv_cols = conv_w.shape[2]

    gathered = all_gather([_pack_rows([c, conv_w], LANES, 8)], "gather_c_conv", True)[0].reshape(N_DEV, -1)
    c_all = gathered[:, :D]
    conv_full = gathered[:, D:D + DEPTH * CONV_K * conv_cols].reshape(N_DEV, DEPTH, CONV_K, conv_cols).transpose(1, 2, 0, 3).reshape(DEPTH, CONV_K, -1)

    b_cols = lax.dynamic_slice_in_dim(b_ada, me * ADA_COLS, ADA_COLS, axis=1)
    c_act_all, mod_cols = ada_project(c_all, w_ada, b_cols)
    mod_all = all_gather([mod_cols.reshape(-1, LANES)], "gather_mod", True)[0].reshape(N_DEV, DEPTH, N_DEV, ADA_COLS)
    mods = lax.dynamic_index_in_dim(mod_all, me, axis=2, keepdims=False).transpose(1, 0, 2).reshape(DEPTH, 6 * D)

    weights = _unpack_weights(*all_gather_forwarding(_pack_shards(given), "gather_weights"))

    small = {n: given[n] for n in SMALL}
    ops = [_make_ops(l) for l in range(DEPTH)]

    modulate0 = make_rowwise(_modulate_fn, "modulate0", "t", "cc", "t")

    def forward(x0, mods, small, conv_full, weights):
        (u,) = modulate0(x0, mods[:1, D:2 * D], mods[:1, :D])
        h = x0
        for l in range(DEPTH):
            next_mod = mods[l + 1:l + 2] if l + 1 < DEPTH else None
            h, u = _layer(ops[l], h, u, mods[l:l + 1], next_mod, {n: a[l:l + 1] for n, a in small.items()}, conv_full[l], weights[l])
        return h

    y, vjp = jax.vjp(forward, x[0], mods, small, conv_full, weights)
    loss_tile, dy = loss_head(y, loss_target[0])
    dx, d_mods, d_small, d_conv, d_weights = vjp(dy)
    loss = lax.psum(loss_tile[0, 0], AXES)

    slabs = _pack_grads(d_weights)
    from_sibling = scatter_to_sibling(slabs, "scatter_sibling")
    per_chip = [pair_sum(a, b, f"pair_sum_{i}") for i, (a, b) in enumerate(zip(slabs, from_sibling))]
    g_in, g_rows, g_ff1 = [sum_partials(p, f"sum_big_{i}") for i, p in enumerate(scatter_to_chips(per_chip, "scatter_chips"))]
    grad, off = dict(w_in=g_in.reshape(w_in.shape), w_ff1=g_ff1.reshape(w_ff1.shape)), 0
    for n in ROW_SHARDED:
        rows = given[n].shape[0] * given[n].shape[1]
        grad[n] = g_rows[off:off + rows].reshape(given[n].shape)
        off += rows
    delta, new_m, new_v = {}, {}, {}
    for n in BIG[1:]:
        delta[n], new_m[n], new_v[n] = adamw(given[n], grad[n], given["m_" + n], given["v_" + n], "adamw_" + n)
    cols_first = [a.transpose(2, 0, 1) for a in (w_in, grad["w_in"], m_w_in, v_w_in)]
    delta["w_in"], new_m["w_in"], new_v["w_in"] = (
        a.transpose(1, 2, 0) for a in adamw(*cols_first, "adamw_w_in", lead_tile=W_IN_COL_TILE))

    partial = [d_small[n] for n in SMALL] + [d_mods, d_conv]
    (parts,) = all_gather([_pack_rows(partial, LANES, 8)], "gather_small_grads", True)
    mods_at = sum(d_small[n].size for n in SMALL)
    d_mods_all = parts.reshape(N_DEV, -1)[:, mods_at:mods_at + DEPTH * 6 * D].reshape(N_DEV, DEPTH, 6 * D)
    total = _split_flat(sum_partials(parts, "sum_small").reshape(-1), partial)
    for n, g in zip(SMALL, total):
        grad[n] = g
    grad["b_ada"] = total[len(SMALL)]
    grad["conv_w"] = lax.dynamic_slice_in_dim(total[len(SMALL) + 1], me * conv_cols, conv_cols, axis=2)
    names = SMALL + ("b_ada", "conv_w")
    packed = [_pack_rows([src[p + n] for n in names], LANES, 8)[None] for src, p in ((given, ""), (grad, ""), (given, "m_"), (given, "v_"))]
    outs = adamw(*packed, "adamw_small")
    for res, o in zip((delta, new_m, new_v), outs):
        for n, a in zip(names, _split_flat(o.reshape(-1), [given[n] for n in names])):
            res[n] = a

    dmod_mine = lax.dynamic_slice_in_dim(d_mods_all, me * ADA_COLS, ADA_COLS, axis=2).transpose(1, 0, 2)
    pad = LANES - N_DEV
    grad["w_ada"], delta["w_ada"], new_m["w_ada"], new_v["w_ada"] = adamw_ada(
        jnp.pad(c_act_all.T, ((0, 0), (0, pad))), jnp.pad(dmod_mine, ((0, 0), (0, pad), (0, 0))), w_ada, m_w_ada, v_w_ada)

    order = ("w_ada", "b_ada", "w_in", "conv_w", "a_log", "dt_bias", "sinks", "dn_norm_w", "w_oa", "w_ob", "w_out", "ln1_g", "ln1_b",
             "w_ff1", "b_ff1", "w_ff2", "b_ff2", "ln2_g", "ln2_b")
    return (loss, dx[None], *[grad[n] for n in order], *[delta[n] for n in order], *[new_m[n] for n in order], *[new_v[n] for n in order])
```

```python
import functools

import jax
import jax.numpy as jnp
from jax import lax
from jax.experimental import pallas as pl
from jax.experimental.pallas import tpu as pltpu

F32 = jnp.float32
BF16 = jnp.bfloat16

D = 1024
DEPTH = 4
N_DEV = 8
ATT_HEADS, ATT_KV, ATT_GROUP, ATT_HD, WINDOW = 16, 4, 4, 64, 128
DN_HEADS, DN_HD, CONV_K, CHUNK = 8, 128, 4, 64
D_FF = 4096
ADA_COLS = 6 * D // N_DEV
ALPHA = (2 * DEPTH) ** 0.25
LN_EPS = 1e-5
RMS_EPS = 1e-6
ADAM_LR, ADAM_B1, ADAM_B2, ADAM_EPS, ADAM_WD, ADAM_STEP = 0.001, 0.9, 0.999, 1e-08, 0.01, 10
AXES = ("x", "y", "c")
MESH_IDS = pl.DeviceIdType.MESH
VMEM_LIMIT_BYTES = 48 * 1024 * 1024
LANES = 128
HIGHEST = lax.Precision.HIGHEST

IN_SPLITS = (("q", 1024), ("k", 256), ("v", 256), ("dq", 1024), ("dk", 1024), ("dv", 1024), ("z", 1024),
             ("small", 16), ("ga", 1024), ("gb", 1024))
IN_AT = {name: sum(w for _, w in IN_SPLITS[:i]) for i, (name, _) in enumerate(IN_SPLITS)}
IN_GROUPS = ("qkvs", "dq", "dk", "dv", "z", "ga", "gb")
QKV_W = 1536
QKVS_W = QKV_W + LANES


def _params(sem=None):
    return pltpu.CompilerParams(dimension_semantics=sem, vmem_limit_bytes=VMEM_LIMIT_BYTES)


def _tile(n, pref, unit):
    if n <= pref:
        return n
    t = (pref // unit) * unit
    while t > unit and n % t:
        t -= unit
    assert n % t == 0, (n, pref, unit)
    return t


def _dg(a, b, ca, cb):
    return lax.dot_general(a.astype(BF16), b.astype(BF16), (((ca,), (cb,)), ((), ())), preferred_element_type=F32)


@jax.custom_vjp
def bdot_nn(a, b):
    return _dg(a, b, 1, 0)


def _bdot_nn_fwd(a, b):
    return _dg(a, b, 1, 0), (a, b)


def _bdot_nn_bwd(res, g):
    a, b = res
    return _dg(g, b, 1, 1).astype(a.dtype), _dg(a, g, 0, 0).astype(b.dtype)


bdot_nn.defvjp(_bdot_nn_fwd, _bdot_nn_bwd)


@jax.custom_vjp
def bdot_nt(a, b):
    return _dg(a, b, 1, 1)


def _bdot_nt_fwd(a, b):
    return _dg(a, b, 1, 1), (a, b)


def _bdot_nt_bwd(res, g):
    a, b = res
    return _dg(g, b, 1, 0).astype(a.dtype), _dg(g, a, 0, 0).astype(b.dtype)


bdot_nt.defvjp(_bdot_nt_fwd, _bdot_nt_bwd)


@jax.custom_vjp
def bdot_tn(a, b):
    return _dg(a, b, 0, 0)


def _bdot_tn_fwd(a, b):
    return _dg(a, b, 0, 0), (a, b)


def _bdot_tn_bwd(res, g):
    a, b = res
    return _dg(b, g, 1, 1).astype(a.dtype), _dg(a, g, 1, 0).astype(b.dtype)


bdot_tn.defvjp(_bdot_tn_fwd, _bdot_tn_bwd)


def _hdg(a, b, ca, cb):
    a_hi, b_hi = a.astype(BF16), b.astype(BF16)
    a_lo, b_lo = (a - a_hi.astype(F32)).astype(BF16), (b - b_hi.astype(F32)).astype(BF16)

    def dot(x, y):
        return lax.dot_general(x, y, (((ca,), (cb,)), ((), ())), preferred_element_type=F32)

    return dot(a_hi, b_hi) + (dot(a_hi, b_lo) + dot(a_lo, b_hi))


@jax.custom_vjp
def hdot(a, b):
    return _hdg(a, b, 1, 0)


def _hdot_fwd(a, b):
    return _hdg(a, b, 1, 0), (a, b)


def _hdot_bwd(res, g):
    a, b = res
    return _hdg(g, b, 1, 1), _hdg(a, g, 0, 0)


hdot.defvjp(_hdot_fwd, _hdot_bwd)


def matmul(a, b, mode, out_dtype, name, acc=None):
    if mode == "nn":
        (m, k), (k2, n) = a.shape, b.shape
    elif mode == "nt":
        (m, k), (n, k2) = a.shape, b.shape
    else:
        (k, m), (k2, n) = a.shape, b.shape
    assert k == k2, (a.shape, b.shape, mode)
    def pick(n_, pref):
        t = _tile(n_, pref, LANES)
        return n_ if t < 2 * LANES and n_ <= 2048 else t

    tm, tn, tk = pick(m, 1024), pick(n, 1024), pick(k, 4096)
    nk = k // tk
    a_spec = pl.BlockSpec((tk, tm), lambda i, j, kk: (kk, i)) if mode == "tn" else pl.BlockSpec((tm, tk), lambda i, j, kk: (i, kk))
    b_spec = pl.BlockSpec((tn, tk), lambda i, j, kk: (j, kk)) if mode == "nt" else pl.BlockSpec((tk, tn), lambda i, j, kk: (kk, j))
    o_spec = pl.BlockSpec((tm, tn), lambda i, j, kk: (i, j))
    ca, cb = {"nn": (1, 0), "nt": (1, 1), "tn": (0, 0)}[mode]

    def body(*refs):
        a_ref, b_ref = refs[:2]
        c_ref = None if acc is None else refs[2]
        o_ref = refs[2 if acc is None else 3]
        part = _dg(a_ref[...], b_ref[...], ca, cb)
        if nk == 1:
            o_ref[...] = (part if c_ref is None else part + c_ref[...].astype(F32)).astype(out_dtype)
            return
        acc_ref = refs[-1]
        kk = pl.program_id(2)

        @pl.when(kk == 0)
        def _():
            acc_ref[...] = part if c_ref is None else part + c_ref[...].astype(F32)

        @pl.when(jnp.logical_and(kk > 0, kk < nk - 1))
        def _():
            acc_ref[...] += part

        @pl.when(kk == nk - 1)
        def _():
            o_ref[...] = (acc_ref[...] + part).astype(out_dtype)

    ins, in_specs = [a, b], [a_spec, b_spec]
    if acc is not None:
        ins.append(acc)
        in_specs.append(o_spec)
    return pl.pallas_call(
        body, name=name, grid=(m // tm, n // tn, nk), in_specs=in_specs, out_specs=o_spec,
        out_shape=jax.ShapeDtypeStruct((m, n), out_dtype), scratch_shapes=[pltpu.VMEM((tm, tn), F32)] if nk > 1 else [],
        compiler_params=_params(("parallel", "parallel", "arbitrary")))(*ins)


def make_mm(name, out_dtype=F32):
    @jax.custom_vjp
    def mm(a, w):
        return matmul(a, w, "nn", out_dtype, name + "_fwd")

    def fwd(a, w):
        return mm(a, w), (a, w)

    def bwd(res, g):
        a, w = res
        return matmul(g, w, "nt", a.dtype, name + "_da"), matmul(a, g, "tn", w.dtype, name + "_dw")

    mm.defvjp(fwd, bwd)
    return mm


def make_in_proj(name):
    @jax.custom_vjp
    def in_proj(u, ws):
        return tuple(matmul(u, w, "nn", BF16, f"{name}_fwd_{g}") for g, w in zip(IN_GROUPS, ws))

    def fwd(u, ws):
        return in_proj(u, ws), (u, ws)

    def bwd(res, gs):
        u, ws = res
        du = None
        for idx, (g, w, dy) in enumerate(zip(IN_GROUPS, ws, gs)):
            last = idx == len(ws) - 1
            du = matmul(dy, w, "nt", u.dtype if last else F32, f"{name}_du_{g}", acc=du)
        dws = tuple(matmul(u, dy, "tn", w.dtype, f"{name}_dw_{g}") for g, w, dy in zip(IN_GROUPS, ws, gs))
        return du, dws

    in_proj.defvjp(fwd, bwd)
    return in_proj


def make_rowwise(fn, name, tile_kinds, param_kinds, out_kinds, nc=1, tm=512):
    n_t, n_p, n_o = len(tile_kinds), len(param_kinds), len(out_kinds)

    def width(a, kind):
        if kind == "h":
            return a.shape[2]
        return a.shape[1] if kind == "s" else a.shape[1] // nc

    def spec(kind, w, rows):
        if kind == "t":
            return pl.BlockSpec((rows, w), lambda j, i: (i, j))
        if kind == "h":
            return pl.BlockSpec((None, rows, w), lambda j, i: (j, i, 0))
        if kind == "c":
            return pl.BlockSpec((1, w), lambda j, i: (0, j))
        return pl.BlockSpec((1, w), lambda j, i: (0, 0))

    def full_shape(kind, w, s):
        return (s, w * nc) if kind == "t" else (nc, s, w)

    def plan(tiles, params):
        s = tiles[0].shape[0] if tile_kinds[0] == "t" else tiles[0].shape[1]
        rows = min(tm, s)
        t_w = [width(a, kd) for a, kd in zip(tiles, tile_kinds)]
        p_w = [width(a, kd) for a, kd in zip(params, param_kinds)]
        t_s = [jax.ShapeDtypeStruct((rows, w), a.dtype) for a, w in zip(tiles, t_w)]
        p_s = [jax.ShapeDtypeStruct((1, w), a.dtype) for a, w in zip(params, p_w)]
        o_s = jax.eval_shape(fn, *t_s, *p_s)
        return s, rows, t_w, p_w, o_s

    def fwd_call(*args):
        tiles, params = args[:n_t], args[n_t:]
        s, rows, t_w, p_w, o_s = plan(tiles, params)

        def body(*refs):
            ins, outs = refs[:n_t + n_p], refs[n_t + n_p:]
            res = fn(*[r[...] for r in ins])
            for o_ref, val in zip(outs, res):
                o_ref[...] = val

        in_specs = [spec(kd, w, rows) for kd, w in zip(tile_kinds, t_w)] + [spec(kd, w, rows) for kd, w in zip(param_kinds, p_w)]
        return pl.pallas_call(
            body, name=name + "_fwd", grid=(nc, s // rows), in_specs=in_specs,
            out_specs=[spec(kd, o.shape[1], rows) for kd, o in zip(out_kinds, o_s)],
            out_shape=[jax.ShapeDtypeStruct(full_shape(kd, o.shape[1], s), o.dtype) for kd, o in zip(out_kinds, o_s)],
            compiler_params=_params(("parallel", "parallel")))(*args)

    def bwd_call(args, douts):
        tiles, params = args[:n_t], args[n_t:]
        s, rows, t_w, p_w, o_s = plan(tiles, params)

        def body(*refs):
            ins = refs[:n_t + n_p]
            dos = refs[n_t + n_p:n_t + n_p + n_o]
            dts = refs[n_t + n_p + n_o:n_t + n_p + n_o + n_t]
            dps = refs[n_t + n_p + n_o + n_t:]
            j, i = pl.program_id(0), pl.program_id(1)
            _, vjp = jax.vjp(lambda *a: tuple(fn(*a)), *[r[...] for r in ins])
            grads = vjp(tuple(r[...] for r in dos))
            for r, g in zip(dts, grads[:n_t]):
                r[...] = g.astype(r.dtype)
            for r, g, kd in zip(dps, grads[n_t:], param_kinds):
                first = (i == 0) if kd == "c" else jnp.logical_and(i == 0, j == 0)

                @pl.when(first)
                def _(r=r):
                    r[...] = jnp.zeros_like(r)

                r[...] += g.astype(F32)

        in_specs = ([spec(kd, w, rows) for kd, w in zip(tile_kinds, t_w)] + [spec(kd, w, rows) for kd, w in zip(param_kinds, p_w)]
                    + [spec(kd, o.shape[1], rows) for kd, o in zip(out_kinds, o_s)])
        out_specs = [spec(kd, w, rows) for kd, w in zip(tile_kinds, t_w)] + [spec(kd, w, rows) for kd, w in zip(param_kinds, p_w)]
        out_shape = [jax.ShapeDtypeStruct(a.shape, a.dtype) for a in tiles] + [jax.ShapeDtypeStruct(a.shape, F32) for a in params]
        return pl.pallas_call(
            body, name=name + "_bwd", grid=(nc, s // rows), in_specs=in_specs, out_specs=out_specs, out_shape=out_shape,
            compiler_params=_params(("arbitrary", "arbitrary")))(*args, *douts)

    @jax.custom_vjp
    def op(*args):
        return tuple(fwd_call(*args))

    def op_fwd(*args):
        return op(*args), args

    def op_bwd(args, douts):
        return tuple(bwd_call(args, douts))

    op.defvjp(op_fwd, op_bwd)
    return op


def _sigmoid(x):
    return 1.0 / (1.0 + jnp.exp(-x))


def _silu(x):
    return x * _sigmoid(x)


def _softplus(x):
    return jnp.maximum(x, 0.0) + jnp.log(1.0 + jnp.exp(-jnp.abs(x)))


def _layer_norm(h, g, b):
    mu = jnp.mean(h, axis=-1, keepdims=True)
    var = jnp.mean(jnp.square(h - mu), axis=-1, keepdims=True)
    return (h - mu) * lax.rsqrt(var + LN_EPS) * g + b


def _modulate_fn(x, sc, sh):
    return ((x * (1.0 + sc) + sh).astype(BF16),)


def _gates_fn(x, a_vec, b_vec):
    x = x.astype(F32)
    lane = lax.broadcasted_iota(jnp.int32, x.shape, 1)
    beta = _sigmoid(x)
    g = -jnp.exp(a_vec) * _softplus(x + b_vec)
    return (jnp.where(lane < DN_HEADS, beta, jnp.where(lane < 2 * DN_HEADS, g, 0.0)),)


def _gdn_post_fn(o, z, nw):
    o = o * lax.rsqrt(jnp.mean(jnp.square(o), axis=-1, keepdims=True) + RMS_EPS) * nw
    return ((o * _silu(z.astype(F32))).astype(BF16),)


def _mix_fn(ga, gb, ya, yb):
    ga, gb, ya, yb = (t.astype(F32) for t in (ga, gb, ya, yb))
    return ((_sigmoid(ga) * ya + _sigmoid(gb) * yb).astype(BF16),)


def _ln1_fn(x, mixed, gt, g, b, sc, sh):
    y = _layer_norm(ALPHA * x + (1.0 + gt) * mixed, g, b)
    return y, _modulate_fn(y, sc, sh)[0]


def _ln2_last_fn(x, f, gt, bf, g, b):
    return (_layer_norm(ALPHA * x + (1.0 + gt) * (f + bf), g, b),)


def _ln2_fn(x, f, gt, bf, g, b, sc, sh):
    (y,) = _ln2_last_fn(x, f, gt, bf, g, b)
    return y, _modulate_fn(y, sc, sh)[0]


def _relu2_fn(h, b):
    return (jnp.square(jnp.maximum(h.astype(F32) + b, 0.0)).astype(BF16),)


def _each(f, *lists):
    return [f(*xs) for xs in zip(*lists)]


def _swa_blocks(q4, kp, kc, vp, vc, sink, first):
    rows = ATT_GROUP * WINDOW
    qi = lax.broadcasted_iota(jnp.int32, (rows, 2 * WINDOW), 0) & (WINDOW - 1)
    si = lax.broadcasted_iota(jnp.int32, (rows, 2 * WINDOW), 1)
    diff = qi + WINDOW - si
    valid = (diff >= 0) & (diff < WINDOW) & (si >= jnp.where(first, WINDOW, 0))
    q = _each(lambda a: a.reshape(rows, ATT_HD), q4)
    k = _each(lambda a, b: jnp.concatenate([a, b], axis=0), kp, kc)
    v = _each(lambda a, b: jnp.concatenate([a, b], axis=0), vp, vc)
    s = _each(lambda q, k: jnp.where(valid, bdot_nt(q, k) * (ATT_HD ** -0.5), -jnp.inf), q, k)
    m = _each(lambda s, sink: lax.stop_gradient(jnp.maximum(jnp.max(s, axis=-1, keepdims=True), sink)), s, sink)
    p = _each(lambda s, m: jnp.exp(s - m), s, m)
    pn = _each(lambda p, sink, m: p / (jnp.sum(p, axis=-1, keepdims=True) + jnp.exp(sink - m)), p, sink, m)
    return _each(lambda pn, v: bdot_nn(pn, v).reshape(ATT_GROUP, WINDOW, ATT_HD).astype(BF16), pn, v)


def make_swa(name):
    heads = range(ATT_KV)

    def specs():
        q_spec = pl.BlockSpec((ATT_HEADS, WINDOW, ATT_HD), lambda n: (0, n, 0))
        cur = pl.BlockSpec((ATT_KV, WINDOW, ATT_HD), lambda n: (0, n, 0))
        prev = pl.BlockSpec((ATT_KV, WINDOW, ATT_HD), lambda n: (0, jnp.maximum(n - 1, 0), 0))
        sink = pl.BlockSpec((ATT_KV, ATT_GROUP * WINDOW, 1), lambda n: (0, 0, 0))
        return q_spec, cur, prev, sink

    def group(ref, h):
        return ref.at[pl.ds(h * ATT_GROUP, ATT_GROUP)]

    def load(q_ref, kp_ref, kc_ref, vp_ref, vc_ref, s_ref):
        return [[group(q_ref, h)[...] for h in heads]] + [[r[h] for h in heads] for r in (kp_ref, kc_ref, vp_ref, vc_ref, s_ref)]

    def fwd_call(q, k, v, sink):
        s = q.shape[1]
        q_spec, cur, prev, sink_spec = specs()

        def body(q_ref, kp_ref, kc_ref, vp_ref, vc_ref, s_ref, o_ref):
            o = _swa_blocks(*load(q_ref, kp_ref, kc_ref, vp_ref, vc_ref, s_ref), pl.program_id(0) == 0)
            for h in heads:
                group(o_ref, h)[...] = o[h]

        return pl.pallas_call(
            body, name=name + "_fwd", grid=(s // WINDOW,), in_specs=[q_spec, prev, cur, prev, cur, sink_spec],
            out_specs=q_spec, out_shape=jax.ShapeDtypeStruct(q.shape, BF16),
            compiler_params=_params(("parallel",)))(q, k, k, v, v, sink)

    def bwd_call(q, k, v, sink, do):
        s = q.shape[1]
        q_spec, cur, prev, sink_spec = specs()

        def body(q_ref, kp_ref, kc_ref, vp_ref, vc_ref, s_ref, do_ref, dq_ref, dkp_ref, dkc_ref, dvp_ref, dvc_ref, ds_ref):
            first = pl.program_id(0) == 0
            _, vjp = jax.vjp(lambda *a: _swa_blocks(*a, first), *load(q_ref, kp_ref, kc_ref, vp_ref, vc_ref, s_ref))
            dq, dkp, dkc, dvp, dvc, ds = vjp([group(do_ref, h)[...] for h in heads])

            @pl.when(first)
            def _():
                ds_ref[...] = jnp.zeros_like(ds_ref)

            for h in heads:
                group(dq_ref, h)[...] = dq[h]
                dkp_ref[h], dkc_ref[h], dvp_ref[h], dvc_ref[h] = dkp[h], dkc[h], dvp[h], dvc[h]
                ds_ref[h] += ds[h]

        kv = jax.ShapeDtypeStruct(k.shape, k.dtype)
        return pl.pallas_call(
            body, name=name + "_bwd", grid=(s // WINDOW,), in_specs=[q_spec, prev, cur, prev, cur, sink_spec, q_spec],
            out_specs=[q_spec, cur, cur, cur, cur, sink_spec],
            out_shape=[jax.ShapeDtypeStruct(q.shape, q.dtype), kv, kv, kv, kv, jax.ShapeDtypeStruct(sink.shape, F32)],
            compiler_params=_params(("arbitrary",)))(q, k, k, v, v, sink, do)

    @jax.custom_vjp
    def swa(q, k, v, sink):
        return fwd_call(q, k, v, sink)

    def fwd(q, k, v, sink):
        return swa(q, k, v, sink), (q, k, v, sink)

    def bwd(res, do):
        q, k, v, sink = res
        dq, dkp, dkc, dvp, dvc, ds = bwd_call(q, k, v, sink, do)

        def fold(cur, prev):
            shifted = jnp.concatenate([prev[:, WINDOW:], jnp.zeros_like(prev[:, :WINDOW])], axis=1)
            return (cur.astype(F32) + shifted.astype(F32)).astype(cur.dtype)

        return dq, fold(dkc, dkp), fold(dvc, dvp), ds

    swa.defvjp(fwd, bwd)
    return swa


def _shift_down(x, d, row):
    return x if d == 0 else jnp.where(row >= d, pltpu.roll(x, d, 0), 0.0)


def _shift_up(x, d, row):
    n = x.shape[0]
    return x if d == 0 else jnp.where(row < n - d, pltpu.roll(x, n - d, 0), 0.0)


def _conv_pre(x, w, row):
    return sum(w[j:j + 1, :] * _shift_down(x, CONV_K - 1 - j, row) for j in range(CONV_K))


def _prep_post(pre, kind):
    act = _silu(pre)
    if kind == "v":
        return act
    r = lax.rsqrt(jnp.sum(jnp.square(act), axis=-1, keepdims=True) + RMS_EPS)
    return act * r * (DN_HD ** -0.5 if kind == "q" else 1.0)


def make_gdn_prep(name, kind):
    def fwd_call(x, w):
        s = x.shape[0]

        def body(x_ref, w_ref, o_ref):
            row = lax.broadcasted_iota(jnp.int32, (s, DN_HD), 0)
            o_ref[...] = _prep_post(_conv_pre(x_ref[...].astype(F32), w_ref[...], row), kind)

        return pl.pallas_call(
            body, name=name + "_fwd", grid=(DN_HEADS,),
            in_specs=[pl.BlockSpec((s, DN_HD), lambda j: (0, j)), pl.BlockSpec((CONV_K, DN_HD), lambda j: (0, j))],
            out_specs=pl.BlockSpec((None, s, DN_HD), lambda j: (j, 0, 0)),
            out_shape=jax.ShapeDtypeStruct((DN_HEADS, s, DN_HD), F32), compiler_params=_params(("parallel",)))(x, w)

    def bwd_call(x, w, dy):
        s = x.shape[0]

        def body(x_ref, w_ref, dy_ref, dx_ref, dw_ref):
            row = lax.broadcasted_iota(jnp.int32, (s, DN_HD), 0)
            xv, wv = x_ref[...].astype(F32), w_ref[...]
            _, vjp = jax.vjp(lambda p: _prep_post(p, kind), _conv_pre(xv, wv, row))
            (dpre,) = vjp(dy_ref[...])
            dx_ref[...] = sum(wv[j:j + 1, :] * _shift_up(dpre, CONV_K - 1 - j, row) for j in range(CONV_K)).astype(dx_ref.dtype)
            for j in range(CONV_K):
                dw_ref[j:j + 1, :] = jnp.sum(dpre * _shift_down(xv, CONV_K - 1 - j, row), axis=0, keepdims=True)

        x_spec = pl.BlockSpec((s, DN_HD), lambda j: (0, j))
        w_spec = pl.BlockSpec((CONV_K, DN_HD), lambda j: (0, j))
        return pl.pallas_call(
            body, name=name + "_bwd", grid=(DN_HEADS,),
            in_specs=[x_spec, w_spec, pl.BlockSpec((None, s, DN_HD), lambda j: (j, 0, 0))], out_specs=[x_spec, w_spec],
            out_shape=[jax.ShapeDtypeStruct(x.shape, x.dtype), jax.ShapeDtypeStruct(w.shape, F32)],
            compiler_params=_params(("parallel",)))(x, w, dy)

    @jax.custom_vjp
    def prep(x, w):
        return fwd_call(x, w)

    def fwd(x, w):
        return prep(x, w), (x, w)

    def bwd(res, dy):
        return tuple(bwd_call(*res, dy))

    prep.defvjp(fwd, bwd)
    return prep


@jax.custom_vjp
def _saved_inverse(m, t):
    return t


def _saved_inverse_fwd(m, t):
    return t, t


def _saved_inverse_bwd(t, dt):
    return _hdg(_hdg(t, dt, 0, 0), t, 1, 1), jnp.zeros_like(t)


_saved_inverse.defvjp(_saved_inverse_fwd, _saved_inverse_bwd)


def _chunk_fn(q, k, v, beta, g, state, t_saved=None):
    c = CHUNK
    r = lax.broadcasted_iota(jnp.int32, (c, c), 0)
    cc = lax.broadcasted_iota(jnp.int32, (c, c), 1)
    eye = (r == cc).astype(F32)
    causal, strict = r >= cc, r > cc
    g_row = _each(lambda g: jnp.sum(g * eye, axis=0, keepdims=True), g)
    gc = _each(lambda g_row: jnp.sum(jnp.where(causal, g_row, 0.0), axis=1, keepdims=True), g_row)
    gc_row = _each(lambda gc: jnp.sum(gc * eye, axis=0, keepdims=True), gc)
    decay = _each(lambda gc, gc_row: jnp.exp(jnp.where(causal, gc - gc_row, -jnp.inf)), gc, gc_row)
    kb = _each(jnp.multiply, k, beta)
    vb = _each(jnp.multiply, v, beta)
    kk = _each(bdot_nt, kb, k)
    qk = _each(bdot_nt, q, k)
    m = _each(lambda kk, decay: -jnp.where(strict, kk * decay, 0.0), kk, decay)
    if t_saved is None:
        t, p = _each(lambda m: eye + m, m), m
        for _ in range(5):
            p = _each(hdot, p, p)
            t = _each(lambda t, p: t + hdot(t, p), t, p)
    else:
        t = _each(_saved_inverse, m, t_saved)
    eg = _each(jnp.exp, gc)
    u = _each(hdot, t, vb)
    w = _each(lambda t, kb, eg: hdot(t, kb * eg), t, kb, eg)
    ws = _each(bdot_nn, w, state)
    qs = _each(lambda q, eg, state: bdot_nn(q * eg, state), q, eg, state)
    v_new = _each(jnp.subtract, u, ws)
    o = _each(lambda qs, qk, decay, v_new: qs + bdot_nn(qk * decay, v_new), qs, qk, decay, v_new)
    g_last = _each(lambda g: jnp.sum(g, axis=0, keepdims=True), g)
    kv = _each(lambda k, g_last, gc, v_new: bdot_tn(k * jnp.exp(g_last - gc), v_new), k, g_last, gc, v_new)
    new_state = _each(lambda state, g_last, kv: state * jnp.exp(g_last) + kv, state, g_last, kv)
    return o, new_state, t


def _gate_columns(gates):
    lane = lax.broadcasted_iota(jnp.int32, gates.shape, 1)

    def column(at):
        return jnp.sum(jnp.where(lane == at, gates, 0.0), axis=1, keepdims=True)

    return [column(h) for h in range(DN_HEADS)], [column(DN_HEADS + h) for h in range(DN_HEADS)]


def make_delta(name):
    heads = range(DN_HEADS)

    def specs(n_chunks, reverse):
        def at(n):
            return n_chunks - 1 - n if reverse else n

        x_spec = pl.BlockSpec((DN_HEADS, CHUNK, DN_HD), lambda n: (0, at(n), 0))
        g_spec = pl.BlockSpec((CHUNK, LANES), lambda n: (at(n), 0))
        st_spec = pl.BlockSpec((DN_HEADS, None, DN_HD, DN_HD), lambda n: (0, at(n), 0, 0))
        t_spec = pl.BlockSpec((DN_HEADS, None, CHUNK, CHUNK), lambda n: (0, at(n), 0, 0))
        return x_spec, g_spec, st_spec, t_spec

    def per_head(ref):
        return [ref[i] for i in heads]

    def fwd_call(qh, kh, vh, gates):
        n_chunks = qh.shape[1] // CHUNK
        x_spec, g_spec, st_spec, t_spec = specs(n_chunks, False)

        def body(q_ref, k_ref, v_ref, g_ref, o_ref, st_ref, t_ref, state):
            @pl.when(pl.program_id(0) == 0)
            def _():
                state[...] = jnp.zeros_like(state)

            s_in = per_head(state)
            o, s_new, t = _chunk_fn(per_head(q_ref), per_head(k_ref), per_head(v_ref), *_gate_columns(g_ref[...]), s_in)
            for i in heads:
                st_ref[i], o_ref[i], state[i], t_ref[i] = s_in[i], o[i], s_new[i], t[i]

        return pl.pallas_call(
            body, name=name + "_fwd", grid=(n_chunks,), in_specs=[x_spec, x_spec, x_spec, g_spec], out_specs=[x_spec, st_spec, t_spec],
            out_shape=[jax.ShapeDtypeStruct(qh.shape, F32), jax.ShapeDtypeStruct((DN_HEADS, n_chunks, DN_HD, DN_HD), F32),
                       jax.ShapeDtypeStruct((DN_HEADS, n_chunks, CHUNK, CHUNK), F32)],
            scratch_shapes=[pltpu.VMEM((DN_HEADS, DN_HD, DN_HD), F32)], compiler_params=_params(("arbitrary",)))(qh, kh, vh, gates)

    def bwd_call(qh, kh, vh, gates, states, inverses, do):
        n_chunks = qh.shape[1] // CHUNK
        x_spec, g_spec, st_spec, t_spec = specs(n_chunks, True)

        def body(q_ref, k_ref, v_ref, g_ref, st_ref, t_ref, do_ref, dq_ref, dk_ref, dv_ref, dg_ref, dstate):
            @pl.when(pl.program_id(0) == 0)
            def _():
                dstate[...] = jnp.zeros_like(dstate)

            t_saved = per_head(t_ref)
            _, vjp = jax.vjp(lambda q, k, v, gates, state: _chunk_fn(q, k, v, *_gate_columns(gates), state, t_saved=t_saved)[:2],
                             per_head(q_ref), per_head(k_ref), per_head(v_ref), g_ref[...], per_head(st_ref))
            dq, dk, dv, dgates, ds = vjp((per_head(do_ref), per_head(dstate)))
            dg_ref[...] = dgates
            for i in heads:
                dq_ref[i], dk_ref[i], dv_ref[i], dstate[i] = dq[i], dk[i], dv[i], ds[i]

        big = jax.ShapeDtypeStruct(qh.shape, F32)
        return pl.pallas_call(
            body, name=name + "_bwd", grid=(n_chunks,), in_specs=[x_spec, x_spec, x_spec, g_spec, st_spec, t_spec, x_spec],
            out_specs=[x_spec, x_spec, x_spec, g_spec], out_shape=[big, big, big, jax.ShapeDtypeStruct(gates.shape, F32)],
            scratch_shapes=[pltpu.VMEM((DN_HEADS, DN_HD, DN_HD), F32)],
            compiler_params=_params(("arbitrary",)))(qh, kh, vh, gates, states, inverses, do)

    @jax.custom_vjp
    def delta(qh, kh, vh, gates):
        return fwd_call(qh, kh, vh, gates)[0]

    def fwd(qh, kh, vh, gates):
        o, states, inverses = fwd_call(qh, kh, vh, gates)
        return o, (qh, kh, vh, gates, states, inverses)

    def bwd(res, do):
        return tuple(bwd_call(*res, do))

    delta.defvjp(fwd, bwd)
    return delta


def loss_head(y, target):
    s, d = y.shape
    tm = min(256, s)

    def body(y_ref, t_ref, l_ref, dy_ref):
        err = y_ref[...] - t_ref[...]
        dy_ref[...] = err * (1.0 / d)

        @pl.when(pl.program_id(0) == 0)
        def _():
            l_ref[...] = jnp.zeros_like(l_ref)

        l_ref[...] += 0.5 * jnp.sum(jnp.mean(jnp.square(err), axis=-1, keepdims=True), axis=0, keepdims=True)

    spec = pl.BlockSpec((tm, d), lambda i: (i, 0))
    return pl.pallas_call(
        body, name="loss_head", grid=(s // tm,), in_specs=[spec, spec], out_specs=[pl.BlockSpec((8, LANES), lambda i: (0, 0)), spec],
        out_shape=[jax.ShapeDtypeStruct((8, LANES), F32), jax.ShapeDtypeStruct(y.shape, F32)], compiler_params=_params(("arbitrary",)))(y, target)


def ada_project(c_all, w_ada, b_cols):
    tn = 256
    cols = w_ada.shape[2]

    def body(c_ref, w_ref, b_ref, ca_ref, o_ref):
        c_act = _silu(c_ref[...])
        ca_ref[...] = c_act
        o_ref[...] = _dg(c_act, w_ref[...], 1, 0) + b_ref[...]

    return pl.pallas_call(
        body, name="ada_project", grid=(DEPTH, cols // tn),
        in_specs=[pl.BlockSpec((N_DEV, D), lambda l, j: (0, 0)), pl.BlockSpec((None, D, tn), lambda l, j: (l, 0, j)),
                  pl.BlockSpec((None, 1, tn), lambda l, j: (l, 0, j))],
        out_specs=[pl.BlockSpec((N_DEV, D), lambda l, j: (0, 0)), pl.BlockSpec((None, N_DEV, tn), lambda l, j: (l, 0, j))],
        out_shape=[jax.ShapeDtypeStruct((N_DEV, D), F32), jax.ShapeDtypeStruct((DEPTH, N_DEV, cols), F32)],
        compiler_params=_params(("arbitrary", "arbitrary")))(c_all, w_ada, b_cols.reshape(DEPTH, 1, cols))


def sum_partials(parts, name):
    n_parts, r, c = parts.shape
    tr = _tile(r, 512, 16 if parts.dtype == BF16 else 8)

    def body(p_ref, o_ref):
        total = p_ref[0].astype(F32)
        for part in range(1, n_parts):
            total = total + p_ref[part].astype(F32)
        o_ref[...] = total

    return pl.pallas_call(
        body, name=name, grid=(r // tr,), in_specs=[pl.BlockSpec((n_parts, tr, c), lambda i: (0, i, 0))],
        out_specs=pl.BlockSpec((tr, c), lambda i: (i, 0)), out_shape=jax.ShapeDtypeStruct((r, c), F32),
        compiler_params=_params(("parallel",)))(parts)


def _adamw(w, g, m, v):
    m = ADAM_B1 * m + (1.0 - ADAM_B1) * g
    v = ADAM_B2 * v + (1.0 - ADAM_B2) * jnp.square(g)
    m_hat = m / (1.0 - ADAM_B1 ** ADAM_STEP)
    v_hat = v / (1.0 - ADAM_B2 ** ADAM_STEP)
    return -ADAM_LR * (m_hat / (jnp.sqrt(v_hat) + ADAM_EPS) + ADAM_WD * w), m, v


def adamw(w, g, m, v, name, lead_tile=None):
    l, r, c = w.shape

    def body(w_ref, g_ref, m_ref, v_ref, d_ref, nm_ref, nv_ref):
        d_ref[...], nm_ref[...], nv_ref[...] = _adamw(w_ref[...], g_ref[...], m_ref[...], v_ref[...])

    if lead_tile is None:
        tr = _tile(r, 512, 8)
        spec, grid = pl.BlockSpec((None, tr, c), lambda a, i: (a, i, 0)), (l, r // tr)
    else:
        spec, grid = pl.BlockSpec((lead_tile, r, c), lambda a, i: (a, 0, 0)), (l // lead_tile, 1)
    shape = jax.ShapeDtypeStruct(w.shape, F32)
    return pl.pallas_call(body, name=name, grid=grid, in_specs=[spec] * 4, out_specs=[spec] * 3, out_shape=[shape] * 3,
                          compiler_params=_params(("parallel", "parallel")))(w, g, m, v)


def adamw_ada(c_act_t, dmod, w, m, v):
    l, r, c = w.shape
    tr = 256

    def body(c_ref, d_ref, w_ref, m_ref, v_ref, g_ref, dl_ref, nm_ref, nv_ref):
        g = hdot(c_ref[...], d_ref[...])
        g_ref[...] = g
        dl_ref[...], nm_ref[...], nv_ref[...] = _adamw(w_ref[...], g, m_ref[...], v_ref[...])

    spec = pl.BlockSpec((None, tr, c), lambda a, i: (a, i, 0))
    shape = jax.ShapeDtypeStruct(w.shape, F32)
    return pl.pallas_call(
        body, name="adamw_ada", grid=(l, r // tr),
        in_specs=[pl.BlockSpec((tr, LANES), lambda a, i: (i, 0)), pl.BlockSpec((None, LANES, c), lambda a, i: (a, 0, 0)), spec, spec, spec],
        out_specs=[spec] * 4, out_shape=[shape] * 4, compiler_params=_params(("parallel", "parallel")))(c_act_t, dmod, w, m, v)


def _place():
    x, y, c = lax.axis_index("x"), lax.axis_index("y"), lax.axis_index("c")
    return x, y, c


def _comm_call(body, name, ins, out_shapes, space, n_sems):
    n = len(ins)
    return pl.pallas_call(
        body, name=name, out_shape=out_shapes, in_specs=[pl.BlockSpec(memory_space=space)] * n,
        out_specs=[pl.BlockSpec(memory_space=space)] * n,
        scratch_shapes=[pltpu.SemaphoreType.DMA((n, n_sems)), pltpu.SemaphoreType.DMA((n, n_sems)), pltpu.SemaphoreType.DMA((n,))],
        compiler_params=pltpu.CompilerParams(vmem_limit_bytes=VMEM_LIMIT_BYTES))(*ins)


def all_gather(shards, name, in_vmem):
    n = len(shards)

    def body(*refs):
        x_refs, out_refs, (send_sems, recv_sems, local_sems) = refs[:n], refs[n:2 * n], refs[2 * n:]
        x, y, c = _place()
        me, sibling = (x, y, c), (x, y, 1 - c)
        chips = [(1 - x, y), (x, 1 - y), (1 - x, 1 - y)]

        def rows(a, px, py, pc):
            return out_refs[a].at[4 * px + 2 * py + pc]

        def copy(a, k, block, to, from_shard=False):
            return pltpu.make_async_remote_copy(
                src_ref=x_refs[a] if from_shard else rows(a, *block), dst_ref=rows(a, *block), send_sem=send_sems.at[a, k],
                recv_sem=recv_sems.at[a, k], device_id=to, device_id_type=MESH_IDS)

        arrays = range(n)
        mine = [pltpu.make_async_copy(x_refs[a], rows(a, *me), local_sems.at[a]) for a in arrays]
        first = [copy(a, 1 + j, me, (*chip, c), True) for j, chip in enumerate(chips) for a in arrays]
        first += [copy(a, 0, me, sibling, True) for a in arrays]
        for cp in mine + first:
            cp.start()
        passed = []
        for j, chip in enumerate(chips):
            for a in arrays:
                copy(a, 1 + j, (*chip, c), me).wait_recv()
                passed.append(copy(a, 4 + j, (*chip, c), sibling))
                passed[-1].start()
        for a in arrays:
            copy(a, 0, sibling, me).wait_recv()
            for j, chip in enumerate(chips):
                copy(a, 4 + j, (*chip, 1 - c), me).wait_recv()
        for cp in first + passed:
            cp.wait_send()
        for cp in mine:
            cp.wait()

    out_shapes = [jax.ShapeDtypeStruct((N_DEV,) + s.shape, s.dtype) for s in shards]
    return _comm_call(body, name, shards, out_shapes, pltpu.VMEM if in_vmem else pl.ANY, 7)


def all_gather_forwarding(shards, name):
    n = len(shards)

    def body(*refs):
        x_refs, out_refs, (send_sems, recv_sems, local_sems) = refs[:n], refs[n:2 * n], refs[2 * n:]
        x, y, c = _place()
        me = (x, y, c)

        def rows(a, dev, half):
            block = out_refs[a].at[4 * dev[0] + 2 * dev[1] + dev[2]]
            half_rows = shards[a].shape[0] // 2
            return block if half is None else block.at[pl.ds(half * half_rows, half_rows)]

        def copy(a, k, block, to, half=None, from_shard=False):
            return pltpu.make_async_remote_copy(
                src_ref=x_refs[a] if from_shard else rows(a, block, half), dst_ref=rows(a, block, half), send_sem=send_sems.at[a, k],
                recv_sem=recv_sems.at[a, k], device_id=to, device_id_type=MESH_IDS)

        def other_core(dev):
            return (dev[0], dev[1], 1 - c)

        sibling, x_nbr, y_nbr, diag = other_core(me), (1 - x, y, c), (x, 1 - y, c), (1 - x, 1 - y, c)
        arrays = range(n)
        mine = [pltpu.make_async_copy(x_refs[a], rows(a, me, None), local_sems.at[a]) for a in arrays]
        sent = [copy(a, k, me, to, from_shard=True) for k, to in ((1, x_nbr), (2, y_nbr), (0, sibling)) for a in arrays]
        for cp in mine + sent:
            cp.start()

        def on_arrival(k, block, half, passes):
            for a in arrays:
                copy(a, k, block, me, half).wait_recv()
                for k_out, to, half_out in passes:
                    sent.append(copy(a, k_out, block, to, half_out))
                    sent[-1].start()

        on_arrival(1, x_nbr, None, [(3, y_nbr, 0), (5, sibling, None)])
        on_arrival(2, y_nbr, None, [(4, x_nbr, 1), (6, sibling, None)])
        on_arrival(3, diag, 0, [(7, sibling, 0)])
        on_arrival(4, diag, 1, [(8, sibling, 1)])
        on_arrival(0, sibling, None, [])
        on_arrival(5, other_core(x_nbr), None, [])
        on_arrival(6, other_core(y_nbr), None, [])
        on_arrival(7, other_core(diag), 0, [])
        on_arrival(8, other_core(diag), 1, [])
        for cp in sent:
            cp.wait_send()
        for cp in mine:
            cp.wait()

    out_shapes = [jax.ShapeDtypeStruct((N_DEV,) + s.shape, s.dtype) for s in shards]
    return _comm_call(body, name, shards, out_shapes, pl.ANY, 9)


def scatter_to_sibling(slabs, name):
    n = len(slabs)

    def body(*refs):
        in_refs, out_refs, (send_sems, recv_sems, _) = refs[:n], refs[n:2 * n], refs[2 * n:]
        x, y, c = _place()
        copies = [pltpu.make_async_remote_copy(
            src_ref=in_refs[a].at[2 * chip + 1 - c], dst_ref=out_refs[a].at[chip], send_sem=send_sems.at[a, chip],
            recv_sem=recv_sems.at[a, chip], device_id=(x, y, 1 - c), device_id_type=MESH_IDS) for chip in range(4) for a in range(n)]
        for cp in copies:
            cp.start()
        for cp in copies:
            cp.wait_recv()
        for cp in copies:
            cp.wait_send()

    out_shapes = [jax.ShapeDtypeStruct((4,) + s.shape[1:], s.dtype) for s in slabs]
    return _comm_call(body, name, slabs, out_shapes, pl.ANY, 4)


def scatter_to_chips(slabs, name):
    n = len(slabs)

    def body(*refs):
        in_refs, out_refs, (send_sems, recv_sems, local_sems) = refs[:n], refs[n:2 * n], refs[2 * n:]
        x, y, c = _place()
        my_chip = 2 * x + y
        mine = [pltpu.make_async_copy(in_refs[a].at[my_chip], out_refs[a].at[my_chip], local_sems.at[a]) for a in range(n)]
        copies = [pltpu.make_async_remote_copy(
            src_ref=in_refs[a].at[2 * px + py], dst_ref=out_refs[a].at[my_chip], send_sem=send_sems.at[a, j], recv_sem=recv_sems.at[a, j],
            device_id=(px, py, c), device_id_type=MESH_IDS)
            for j, (px, py) in enumerate([(1 - x, y), (x, 1 - y), (1 - x, 1 - y)]) for a in range(n)]
        for cp in mine + copies:
            cp.start()
        for cp in copies:
            cp.wait_recv()
        for cp in copies:
            cp.wait_send()
        for cp in mine:
            cp.wait()

    return _comm_call(body, name, slabs, [jax.ShapeDtypeStruct(s.shape, s.dtype) for s in slabs], pl.ANY, 3)


def scatter_to_chips_carrying_adamw_ada(slabs, c_act_t, dmod, w, m, v, name):
    n = len(slabs)
    l, r, c = w.shape
    tr = 256
    n_rows = r // tr

    def body(*refs):
        in_refs, (c_ref, d_ref, w_ref, m_ref, v_ref) = refs[:n], refs[n:n + 5]
        out_refs, (g_ref, dl_ref, nm_ref, nv_ref) = refs[n + 5:2 * n + 5], refs[2 * n + 5:2 * n + 9]
        send_sems, recv_sems, local_sems = refs[2 * n + 9:]
        x, y, core = _place()
        my_chip = 2 * x + y
        mine = [pltpu.make_async_copy(in_refs[a].at[my_chip], out_refs[a].at[my_chip], local_sems.at[a]) for a in range(n)]
        copies = [pltpu.make_async_remote_copy(
            src_ref=in_refs[a].at[2 * px + py], dst_ref=out_refs[a].at[my_chip], send_sem=send_sems.at[a, j], recv_sem=recv_sems.at[a, j],
            device_id=(px, py, core), device_id_type=MESH_IDS)
            for j, (px, py) in enumerate([(1 - x, y), (x, 1 - y), (1 - x, 1 - y)]) for a in range(n)]
        step = pl.program_id(0) * n_rows + pl.program_id(1)

        @pl.when(step == 0)
        def _():
            for cp in mine + copies:
                cp.start()

        g = hdot(c_ref[...], d_ref[...])
        g_ref[...] = g
        dl_ref[...], nm_ref[...], nv_ref[...] = _adamw(w_ref[...], g, m_ref[...], v_ref[...])

        @pl.when(step == l * n_rows - 1)
        def _():
            for cp in copies:
                cp.wait_recv()
            for cp in copies:
                cp.wait_send()
            for cp in mine:
                cp.wait()

    any_spec = pl.BlockSpec(memory_space=pl.ANY)
    spec = pl.BlockSpec((None, tr, c), lambda a, i: (a, i, 0))
    shape = jax.ShapeDtypeStruct(w.shape, F32)
    outs = pl.pallas_call(
        body, name=name, grid=(l, n_rows),
        in_specs=[any_spec] * n + [pl.BlockSpec((tr, LANES), lambda a, i: (i, 0)), pl.BlockSpec((None, LANES, c), lambda a, i: (a, 0, 0)), spec, spec, spec],
        out_specs=[any_spec] * n + [spec] * 4,
        out_shape=[jax.ShapeDtypeStruct(s.shape, s.dtype) for s in slabs] + [shape] * 4,
        scratch_shapes=[pltpu.SemaphoreType.DMA((n, 3)), pltpu.SemaphoreType.DMA((n, 3)), pltpu.SemaphoreType.DMA((n,))],
        compiler_params=_params(("arbitrary", "arbitrary")))(*slabs, c_act_t, dmod, w, m, v)
    return outs[:n], outs[n:]


def pair_sum(slabs, got, name):
    _, r, c = slabs.shape
    tr = _tile(r, 1024, 16)

    def body(a_ref, b_ref, o_ref):
        o_ref[...] = (a_ref[...].astype(F32) + b_ref[...].astype(F32)).astype(BF16)

    return pl.pallas_call(
        body, name=name, grid=(4, r // tr),
        in_specs=[pl.BlockSpec((None, tr, c), lambda s, i: (2 * s + lax.axis_index("c"), i, 0)), pl.BlockSpec((None, tr, c), lambda s, i: (s, i, 0))],
        out_specs=pl.BlockSpec((None, tr, c), lambda s, i: (s, i, 0)), out_shape=jax.ShapeDtypeStruct((4, r, c), BF16),
        compiler_params=_params(("parallel", "parallel")))(slabs, got)


BIG = ("w_in", "w_oa", "w_ob", "w_out", "w_ff1", "w_ff2")
SMALL = ("a_log", "dt_bias", "sinks", "dn_norm_w", "ln1_g", "ln1_b", "b_ff1", "b_ff2", "ln2_g", "ln2_b")


def _pack_rows(arrs, width, unit):
    flat = jnp.concatenate([a.reshape(-1) for a in arrs])
    rows = -(-flat.shape[0] // (width * unit)) * unit
    return jnp.pad(flat, (0, rows * width - flat.shape[0])).reshape(rows, width)


def _split_flat(flat, like):
    out, off = [], 0
    for a in like:
        n = 1
        for dim in a.shape:
            n *= dim
        out.append(flat[off:off + n].reshape(a.shape))
        off += n
    return out


ROW_SHARDED = ("w_oa", "w_ob", "w_out", "w_ff2")
W_IN_COL_TILE = 74


def _pack_shards(given):
    rows = jnp.concatenate([given[n].reshape(-1, D) for n in ROW_SHARDED]).astype(BF16)
    return [given["w_in"].astype(BF16).reshape(DEPTH * D, -1), rows, given["w_ff1"].astype(BF16).reshape(DEPTH * D, -1)]


def _unpack_weights(g_in, g_rows, g_ff1):
    w_in = g_in.reshape(N_DEV, DEPTH, D, -1)
    w_ff1 = g_ff1.reshape(N_DEV, DEPTH, D, -1)
    layers = []
    for l in range(DEPTH):
        full_in = w_in[:, l].transpose(1, 0, 2).reshape(D, -1)
        small = jnp.pad(full_in[:, IN_AT["small"]:IN_AT["ga"]], ((0, 0), (0, LANES + IN_AT["small"] - IN_AT["ga"])))
        groups = [jnp.concatenate([full_in[:, :QKV_W], small], axis=1)] + [full_in[:, IN_AT[g]:IN_AT[g] + D] for g in IN_GROUPS[1:]]
        lay, off = dict(w_in=tuple(groups), w_ff1=w_ff1[:, l].transpose(1, 0, 2).reshape(D, D_FF)), 0
        for n in ROW_SHARDED:
            per = (D_FF if n == "w_ff2" else D) // N_DEV
            lay[n] = g_rows[:, off + l * per:off + (l + 1) * per].reshape(per * N_DEV, D)
            off += DEPTH * per
        layers.append(lay)
    return layers


def _pack_grads(grads):
    def in_order(qkvs, dq, dk, dv, z, ga, gb):
        return jnp.concatenate([qkvs[:, :QKV_W], dq, dk, dv, z, qkvs[:, QKV_W:QKV_W + IN_AT["ga"] - IN_AT["small"]], ga, gb], axis=1)

    w_in = jnp.stack([in_order(*lay["w_in"]) for lay in grads])
    s_in = w_in.reshape(DEPTH, D, N_DEV, -1).transpose(2, 0, 1, 3).reshape(N_DEV, DEPTH * D, -1)
    rows = []
    for n in ROW_SHARDED:
        w = jnp.stack([lay[n] for lay in grads])
        rows.append(w.reshape(DEPTH, N_DEV, -1, D).transpose(1, 0, 2, 3).reshape(N_DEV, -1, D))
    w = jnp.stack([lay["w_ff1"] for lay in grads])
    s_ff1 = w.reshape(DEPTH, D, N_DEV, -1).transpose(2, 0, 1, 3).reshape(N_DEV, DEPTH * D, -1)
    return [s_in, jnp.concatenate(rows, axis=1), s_ff1]


def _layer(ops, x, u, mod, next_mod, sm, conv_w, w):
    s = x.shape[0]
    sh1, sc1, gt1, sh2, sc2, gt2 = (mod[:, i * D:(i + 1) * D] for i in range(6))
    qkvs, dq, dk, dv, z, ga, gb = ops["in_proj"](u, w["w_in"])
    q, k, v, small = qkvs[:, :D], qkvs[:, D:D + 256], qkvs[:, D + 256:QKV_W], qkvs[:, QKV_W:]
    qh = q.reshape(s, ATT_HEADS, ATT_HD).transpose(1, 0, 2)
    kh = k.reshape(s, ATT_KV, ATT_HD).transpose(1, 0, 2)
    vh = v.reshape(s, ATT_KV, ATT_HD).transpose(1, 0, 2)
    sink = jnp.broadcast_to(sm["sinks"].reshape(ATT_KV, ATT_GROUP, 1, 1), (ATT_KV, ATT_GROUP, WINDOW, 1)).reshape(ATT_KV, ATT_GROUP * WINDOW, 1)
    attn = ops["swa"](qh, kh, vh, sink).transpose(1, 0, 2).reshape(s, ATT_HEADS * ATT_HD)
    y_a = ops["mm_oa"](attn, w["w_oa"])
    qn = ops["prep_q"](dq, conv_w[:, :D])
    kn = ops["prep_k"](dk, conv_w[:, D:2 * D])
    vn = ops["prep_v"](dv, conv_w[:, 2 * D:])
    a_vec = jnp.pad(sm["a_log"], ((0, 0), (DN_HEADS, LANES - 2 * DN_HEADS)))
    b_vec = jnp.pad(sm["dt_bias"], ((0, 0), (DN_HEADS, LANES - 2 * DN_HEADS)))
    (gates,) = ops["gates"](small, a_vec, b_vec)
    o = ops["delta"](qn, kn, vn, gates)
    (og,) = ops["gdn_post"](o, z, sm["dn_norm_w"])
    y_b = ops["mm_ob"](og, w["w_ob"])
    (mix,) = ops["mix"](ga, gb, y_a, y_b)
    mixed = ops["mm_out"](mix, w["w_out"])
    x1, u2 = ops["ln1"](x, mixed, gt1, sm["ln1_g"], sm["ln1_b"], sc2, sh2)
    (h,) = ops["relu2"](ops["mm_ff1"](u2, w["w_ff1"]), sm["b_ff1"])
    f = ops["mm_ff2"](h, w["w_ff2"])
    if next_mod is None:
        return ops["ln2"](x1, f, gt2, sm["b_ff2"], sm["ln2_g"], sm["ln2_b"])[0], None
    return ops["ln2"](x1, f, gt2, sm["b_ff2"], sm["ln2_g"], sm["ln2_b"], next_mod[:, D:2 * D], next_mod[:, :D])


def _make_ops(l):
    t = f"l{l}_"
    last = l == DEPTH - 1
    return dict(
        in_proj=make_in_proj(t + "in_proj"), swa=make_swa(t + "swa"),
        mm_oa=make_mm(t + "mm_oa", BF16), mm_ob=make_mm(t + "mm_ob", BF16), mm_out=make_mm(t + "mm_out"),
        mm_ff1=make_mm(t + "mm_ff1", BF16), mm_ff2=make_mm(t + "mm_ff2"),
        prep_q=make_gdn_prep(t + "prep_q", "q"), prep_k=make_gdn_prep(t + "prep_k", "k"), prep_v=make_gdn_prep(t + "prep_v", "v"),
        gates=make_rowwise(_gates_fn, t + "gates", "t", "cc", "t"), delta=make_delta(t + "delta"),
        gdn_post=make_rowwise(_gdn_post_fn, t + "gdn_post", "ht", "s", "t", nc=DN_HEADS, tm=1024),
        mix=make_rowwise(_mix_fn, t + "mix", "tttt", "", "t"),
        ln1=make_rowwise(_ln1_fn, t + "ln1", "tt", "ccccc", "tt"),
        ln2=make_rowwise(_ln2_last_fn, t + "ln2", "tt", "cccc", "t") if last else make_rowwise(_ln2_fn, t + "ln2", "tt", "cccccc", "tt"),
        relu2=make_rowwise(_relu2_fn, t + "relu2", "t", "c", "t", nc=4, tm=512))


def kernel(x, c, w_ada, b_ada, w_in, conv_w, a_log, dt_bias, sinks, dn_norm_w, w_oa, w_ob, w_out, ln1_g, ln1_b, w_ff1, b_ff1, w_ff2, b_ff2, ln2_g, ln2_b, loss_target, m_w_ada, m_b_ada, m_w_in, m_conv_w, m_a_log, m_dt_bias, m_sinks, m_dn_norm_w, m_w_oa, m_w_ob, m_w_out, m_ln1_g, m_ln1_b, m_w_ff1, m_b_ff1, m_w_ff2, m_b_ff2, m_ln2_g, m_ln2_b, v_w_ada, v_b_ada, v_w_in, v_conv_w, v_a_log, v_dt_bias, v_sinks, v_dn_norm_w, v_w_oa, v_w_ob, v_w_out, v_ln1_g, v_ln1_b, v_w_ff1, v_b_ff1, v_w_ff2, v_b_ff2, v_ln2_g, v_ln2_b):
    given = dict(locals())
    me = 4 * lax.axis_index("x") + 2 * lax.axis_index("y") + lax.axis_index("c")
    conv_cols = conv_w.shape[2]

    gathered = all_gather([_pack_rows([c, conv_w], LANES, 8)], "gather_c_conv", True)[0].reshape(N_DEV, -1)
    c_all = gathered[:, :D]
    conv_full = gathered[:, D:D + DEPTH * CONV_K * conv_cols].reshape(N_DEV, DEPTH, CONV_K, conv_cols).transpose(1, 2, 0, 3).reshape(DEPTH, CONV_K, -1)

    b_cols = lax.dynamic_slice_in_dim(b_ada, me * ADA_COLS, ADA_COLS, axis=1)
    c_act_all, mod_cols = ada_project(c_all, w_ada, b_cols)
    mod_all = all_gather([mod_cols.reshape(-1, LANES)], "gather_mod", True)[0].reshape(N_DEV, DEPTH, N_DEV, ADA_COLS)
    mods = lax.dynamic_index_in_dim(mod_all, me, axis=2, keepdims=False).transpose(1, 0, 2).reshape(DEPTH, 6 * D)

    weights = _unpack_weights(*all_gather_forwarding(_pack_shards(given), "gather_weights"))

    small = {n: given[n] for n in SMALL}
    ops = [_make_ops(l) for l in range(DEPTH)]

    modulate0 = make_rowwise(_modulate_fn, "modulate0", "t", "cc", "t")

    def forward(x0, mods, small, conv_full, weights):
        (u,) = modulate0(x0, mods[:1, D:2 * D], mods[:1, :D])
        h = x0
        for l in range(DEPTH):
            next_mod = mods[l + 1:l + 2] if l + 1 < DEPTH else None
            h, u = _layer(ops[l], h, u, mods[l:l + 1], next_mod, {n: a[l:l + 1] for n, a in small.items()}, conv_full[l], weights[l])
        return h

    y, vjp = jax.vjp(forward, x[0], mods, small, conv_full, weights)
    loss_tile, dy = loss_head(y, loss_target[0])
    dx, d_mods, d_small, d_conv, d_weights = vjp(dy)
    loss = lax.psum(loss_tile[0, 0], AXES)

    partial = [d_small[n] for n in SMALL] + [d_mods, d_conv]
    (parts,) = all_gather([_pack_rows(partial, LANES, 8)], "gather_small_grads", True)
    mods_at = sum(d_small[n].size for n in SMALL)
    d_mods_all = parts.reshape(N_DEV, -1)[:, mods_at:mods_at + DEPTH * 6 * D].reshape(N_DEV, DEPTH, 6 * D)
    dmod_mine = lax.dynamic_slice_in_dim(d_mods_all, me * ADA_COLS, ADA_COLS, axis=2).transpose(1, 0, 2)
    pad = LANES - N_DEV
    slabs = _pack_grads(d_weights)
    from_sibling = scatter_to_sibling(slabs, "scatter_sibling")
    per_chip = [pair_sum(a, b, f"pair_sum_{i}") for i, (a, b) in enumerate(zip(slabs, from_sibling))]
    received, ada = scatter_to_chips_carrying_adamw_ada(
        per_chip, jnp.pad(c_act_all.T, ((0, 0), (0, pad))), jnp.pad(dmod_mine, ((0, 0), (0, pad), (0, 0))), w_ada, m_w_ada, v_w_ada,
        "scatter_chips")
    g_in, g_rows, g_ff1 = [sum_partials(p, f"sum_big_{i}") for i, p in enumerate(received)]
    grad, off = dict(w_in=g_in.reshape(w_in.shape), w_ff1=g_ff1.reshape(w_ff1.shape)), 0
    for n in ROW_SHARDED:
        rows = given[n].shape[0] * given[n].shape[1]
        grad[n] = g_rows[off:off + rows].reshape(given[n].shape)
        off += rows
    delta, new_m, new_v = {}, {}, {}
    for n in BIG[1:]:
        delta[n], new_m[n], new_v[n] = adamw(given[n], grad[n], given["m_" + n], given["v_" + n], "adamw_" + n)
    cols_first = [a.transpose(2, 0, 1) for a in (w_in, grad["w_in"], m_w_in, v_w_in)]
    delta["w_in"], new_m["w_in"], new_v["w_in"] = (
        a.transpose(1, 2, 0) for a in adamw(*cols_first, "adamw_w_in", lead_tile=W_IN_COL_TILE))

    total = _split_flat(sum_partials(parts, "sum_small").reshape(-1), partial)
    for n, g in zip(SMALL, total):
        grad[n] = g
    grad["b_ada"] = total[len(SMALL)]
    grad["conv_w"] = lax.dynamic_slice_in_dim(total[len(SMALL) + 1], me * conv_cols, conv_cols, axis=2)
    names = SMALL + ("b_ada", "conv_w")
    packed = [_pack_rows([src[p + n] for n in names], LANES, 8)[None] for src, p in ((given, ""), (grad, ""), (given, "m_"), (given, "v_"))]
    outs = adamw(*packed, "adamw_small")
    for res, o in zip((delta, new_m, new_v), outs):
        for n, a in zip(names, _split_flat(o.reshape(-1), [given[n] for n in names])):
            res[n] = a

    grad["w_ada"], delta["w_ada"], new_m["w_ada"], new_v["w_ada"] = ada

    order = ("w_ada", "b_ada", "w_in", "conv_w", "a_log", "dt_bias", "sinks", "dn_norm_w", "w_oa", "w_ob", "w_out", "ln1_g", "ln1_b",
             "w_ff1", "b_ff1", "w_ff2", "b_ff2", "ln2_g", "ln2_b")
    return (loss, dx[None], *[grad[n] for n in order], *[delta[n] for n in order], *[new_m[n] for n in order], *[new_v[n] for n in order])
```
